```python
import jax, jax.numpy as jnp
from jax import lax
import numpy as np


D_MODEL = 1024
BATCH = 16
SEQ = 4096
DEPTH = 1
DEC_BATCH = 8
DEC_SEQ = 32
PAST_LEN = 2048

CHUNK = 64
A_HEADS = 8
A_KV_HEADS = 2
A_HEAD_DIM = 64
A_GROUP = A_HEADS // A_KV_HEADS
IDX_HEADS = 4
IDX_DIM = 64
IDX_TOPK = 256
Q_BLOCK = 128
ROPE_THETA = 10000.0
G_HEADS = 4
G_KEY_DIM = 128
G_VAL_DIM = 128
G_LOWRANK = 16
G_TAU = 16.0
P_HEADS = 8
P_NKEYS = 128
P_EXPERTS = P_NKEYS * P_NKEYS
P_QDIM = 256
P_HALF = P_QDIM // 2
P_TOPK = 16
P_BLOCK = 128
DN_ALPHA = (2.0 * DEPTH) ** 0.25
DN_BETA = (8.0 * DEPTH) ** -0.25
LN_EPS = 1e-5

W_AQ = A_HEADS * A_HEAD_DIM
W_AK = A_KV_HEADS * A_HEAD_DIM
W_AV = A_KV_HEADS * A_HEAD_DIM
W_IQ = IDX_HEADS * IDX_DIM
W_IK = IDX_DIM
W_IW = IDX_HEADS
W_GQ = G_HEADS * G_KEY_DIM
W_GK = G_HEADS * G_KEY_DIM
W_GV = G_HEADS * G_VAL_DIM
W_GF = G_LOWRANK
W_GR = G_HEADS * G_VAL_DIM
W_GATE = 2 * D_MODEL
IN_SIZES = (W_AQ, W_AK, W_AV, W_IQ, W_IK, W_IW, W_GQ, W_GK, W_GV, W_GF, W_GR, W_GATE)
IN_COLS = sum(IN_SIZES)

kernel_name = 'dsa_gla_peer_streaming_step'


def layer_norm(x, g, b):
    xf = x.astype(jnp.float32)
    mu = jnp.mean(xf, -1, keepdims=True)
    var = jnp.mean(jnp.square(xf - mu), -1, keepdims=True)
    return ((xf - mu) * lax.rsqrt(var + LN_EPS) * g + b).astype(x.dtype)


def rope(x, pos):
    half = x.shape[-1] // 2
    inv = ROPE_THETA ** (-jnp.arange(half, dtype=jnp.float32) / half)
    ang = pos.astype(jnp.float32)[:, None] * inv[None, :]
    cos = jnp.cos(ang)[None, :, None, :]
    sin = jnp.sin(ang)[None, :, None, :]
    x1 = x[..., :half].astype(jnp.float32)
    x2 = x[..., half:].astype(jnp.float32)
    return jnp.concatenate([x1 * cos - x2 * sin, x2 * cos + x1 * sin], -1).astype(x.dtype)


def mixer_inputs(h, pos, w_in, w_fa, b_fa):
    B, T = h.shape[0], h.shape[1]
    cuts = np.cumsum(IN_SIZES)[:-1].tolist()
    aq, ak, av, iq, ik, iw, gq, gk, gv, gf, gr, gate = jnp.split(h @ w_in, cuts, axis=-1)
    aq = rope(aq.reshape(B, T, A_HEADS, A_HEAD_DIM), pos)
    ak = rope(ak.reshape(B, T, A_KV_HEADS, A_HEAD_DIM), pos)
    av = av.reshape(B, T, A_KV_HEADS, A_HEAD_DIM)
    iq = rope(iq.reshape(B, T, IDX_HEADS, IDX_DIM), pos)
    ik = rope(ik.reshape(B, T, 1, IDX_DIM), pos)[:, :, 0]
    gq = gq.reshape(B, T, G_HEADS, G_KEY_DIM) * (G_KEY_DIM ** -0.5)
    gk = gk.reshape(B, T, G_HEADS, G_KEY_DIM)
    gv = gv.reshape(B, T, G_HEADS, G_VAL_DIM)
    glog = (jax.nn.log_sigmoid((gf @ w_fa + b_fa).astype(jnp.float32)) / G_TAU).reshape(B, T, G_HEADS, G_KEY_DIM)
    gate_a, gate_b = jnp.split(gate, 2, axis=-1)
    return aq, ak, av, iq, ik, iw, gq, gk, gv, glog, gr, gate_a, gate_b


def sparse_attention(aq, iq, iw, k_all, v_all, ik_all, limit):
    B, Tq = aq.shape[0], aq.shape[1]
    L = k_all.shape[1]
    topk = min(IDX_TOPK, L // 4)
    s = jnp.einsum('bqhd,bsd->bqhs', iq.astype(jnp.float32), ik_all.astype(jnp.float32)) * (IDX_DIM ** -0.5)
    score = jnp.einsum('bqhs,bqh->bqs', jax.nn.relu(s), iw.astype(jnp.float32) * (IDX_HEADS ** -0.5))
    admissible = jnp.arange(L)[None, :] < limit[:, None]
    score = jnp.where(admissible[None], score, -jnp.inf)
    sel_score, sel = lax.top_k(score, topk)
    valid = jnp.isfinite(sel_score)
    gather = jax.vmap(lambda t, i: t[i])
    kg = gather(k_all, sel)
    vg = gather(v_all, sel)
    q = aq.reshape(B, Tq, A_KV_HEADS, A_GROUP, A_HEAD_DIM).astype(jnp.float32)
    logits = jnp.einsum('bqngd,bqknd->bqngk', q, kg.astype(jnp.float32)) * (A_HEAD_DIM ** -0.5)
    logits = jnp.where(valid[:, :, None, None, :], logits, -jnp.inf)
    p = jax.nn.softmax(logits, axis=-1)
    o = jnp.einsum('bqngk,bqknd->bqngd', p, vg.astype(jnp.float32))
    return o.reshape(B, Tq, A_HEADS * A_HEAD_DIM).astype(aq.dtype)


def prompt_sparse_attention(aq, iq, iw, ak, av, ik):
    B, T = aq.shape[0], aq.shape[1]
    nb = T // Q_BLOCK
    def blocks(a):
        return jnp.moveaxis(a.reshape((B, nb, Q_BLOCK) + a.shape[2:]), 1, 0)
    limit = ((jnp.arange(T) // CHUNK) + 1) * CHUNK
    def one(args):
        q_b, iq_b, iw_b, lim_b = args
        return sparse_attention(q_b, iq_b, iw_b, ak, av, ik, lim_b)
    o = lax.map(one, (blocks(aq), blocks(iq), blocks(iw), limit.reshape(nb, Q_BLOCK)))
    return jnp.moveaxis(o, 0, 1).reshape(B, T, A_HEADS * A_HEAD_DIM)


def gla_chunk(state, q, k, v, g):
    C = q.shape[1]
    b = jnp.cumsum(g, axis=1)
    causal = jnp.tril(jnp.ones((C, C), dtype=bool))
    diff = b[:, :, None] - b[:, None, :]
    decay = jnp.exp(jnp.where(causal[None, :, :, None, None], diff, -jnp.inf))
    qf = q.astype(jnp.float32)
    kf = k.astype(jnp.float32)
    vf = v.astype(jnp.float32)
    attn = jnp.einsum('bthc,bshc,btshc->bhts', qf, kf, decay)
    o = jnp.einsum('bhts,bshv->bthv', attn, vf) + jnp.einsum('bthc,bhcv->bthv', qf * jnp.exp(b), state)
    b_last = b[:, -1]
    new_state = jnp.exp(b_last)[..., None] * state + jnp.einsum('bshc,bshv->bhcv', kf * jnp.exp(b_last[:, None] - b), vf)
    return new_state, o


def gla_prompt(q, k, v, g):
    B, T = q.shape[0], q.shape[1]
    nc = T // CHUNK
    def chunks(a):
        return jnp.moveaxis(a.reshape((B, nc, CHUNK) + a.shape[2:]), 1, 0)
    s0 = jnp.zeros((B, G_HEADS, G_KEY_DIM, G_VAL_DIM), jnp.float32)
    s_final, o = lax.scan(lambda s, xs: gla_chunk(s, xs[0], xs[1], xs[2], xs[3]), s0,
                          (chunks(q), chunks(k), chunks(v), chunks(g)))
    return s_final, jnp.moveaxis(o, 0, 1).reshape(B, T, G_HEADS, G_VAL_DIM)


def branch_merge(h, o_attn, o_gla, gr, gate_a, gate_b, g_norm, w_pa, w_pb, w_out):
    B, T = h.shape[0], h.shape[1]
    of = o_gla.astype(jnp.float32)
    of = of * lax.rsqrt(jnp.mean(jnp.square(of), -1, keepdims=True) + LN_EPS) * g_norm.reshape(G_HEADS, G_VAL_DIM)
    o_b = (of.reshape(B, T, G_HEADS * G_VAL_DIM) * jax.nn.silu(gr.astype(jnp.float32))).astype(h.dtype)
    ya = o_attn @ w_pa
    yb = o_b @ w_pb
    m = jax.nn.sigmoid(gate_a) * ya + jax.nn.sigmoid(gate_b) * yb
    return m @ w_out


def peer(h, w_pq, pk1, pk2, pu, pv):
    n = h.shape[0]
    q = (h @ w_pq).reshape(n, P_HEADS, 2, P_HALF).astype(jnp.float32)
    s1 = jnp.einsum('nhd,hkd->nhk', q[:, :, 0], pk1.astype(jnp.float32))
    s2 = jnp.einsum('nhd,hkd->nhk', q[:, :, 1], pk2.astype(jnp.float32))
    v1, i1 = lax.top_k(s1, P_TOPK)
    v2, i2 = lax.top_k(s2, P_TOPK)
    cand = (v1[..., :, None] + v2[..., None, :]).reshape(n, P_HEADS, P_TOPK * P_TOPK)
    cidx = (i1[..., :, None] * P_NKEYS + i2[..., None, :]).reshape(n, P_HEADS, P_TOPK * P_TOPK)
    sc, j = lax.top_k(cand, P_TOPK)
    e = jnp.take_along_axis(cidx, j, axis=-1)
    g = jax.nn.softmax(sc, axis=-1)
    ue = pu[e]
    ve = pv[e]
    a = jax.nn.gelu(jnp.einsum('nhkd,nd->nhk', ue, h).astype(jnp.float32), approximate=False)
    return jnp.einsum('nhk,nhkd->nd', (g * a).astype(h.dtype), ve)


def peer_blocked(h, w_pq, pk1, pk2, pu, pv):
    B, T, D = h.shape
    flat = h.reshape(-1, P_BLOCK, D)
    out = lax.map(lambda hb: peer(hb, w_pq, pk1, pk2, pu, pv), flat)
    return out.reshape(B, T, D)


def setup_inputs(seed: int = 0) -> dict:
    key = jax.random.key(seed)
    ks = jax.random.split(key, 24)
    def nrm(k, shape, scale):
        return jax.random.normal(k, shape, jnp.float32) * scale
    col_scale = jnp.concatenate([jnp.full((n,), DN_BETA if i in (2, 8) else 1.0, jnp.float32)
                                 for i, n in enumerate(IN_SIZES)])
    return {
        'x_prompt': nrm(ks[0], (BATCH, SEQ, D_MODEL), 1.0),
        'x_sample': nrm(ks[1], (DEC_BATCH, DEC_SEQ, D_MODEL), 1.0),
        'cache_k': nrm(ks[2], (DEPTH, DEC_BATCH, PAST_LEN, A_KV_HEADS, A_HEAD_DIM), 1.0),
        'cache_v': nrm(ks[3], (DEPTH, DEC_BATCH, PAST_LEN, A_KV_HEADS, A_HEAD_DIM), DN_BETA),
        'cache_idx_k': nrm(ks[4], (DEPTH, DEC_BATCH, PAST_LEN, IDX_DIM), 1.0),
        'state_gla': nrm(ks[5], (DEPTH, DEC_BATCH, G_HEADS, G_KEY_DIM, G_VAL_DIM), 1.0),
        'w_in': nrm(ks[6], (DEPTH, D_MODEL, IN_COLS), D_MODEL ** -0.5) * col_scale,
        'w_fa': nrm(ks[7], (DEPTH, G_LOWRANK, G_HEADS * G_KEY_DIM), G_LOWRANK ** -0.5),
        'b_fa': nrm(ks[8], (DEPTH, G_HEADS * G_KEY_DIM), 0.1),
        'g_gla_norm': 1.0 + nrm(ks[9], (DEPTH, G_HEADS * G_VAL_DIM), 0.02),
        'w_pa': nrm(ks[10], (DEPTH, A_HEADS * A_HEAD_DIM, D_MODEL), (A_HEADS * A_HEAD_DIM) ** -0.5 * DN_BETA),
        'w_pb': nrm(ks[11], (DEPTH, G_HEADS * G_VAL_DIM, D_MODEL), (G_HEADS * G_VAL_DIM) ** -0.5 * DN_BETA),
        'w_out': nrm(ks[12], (DEPTH, D_MODEL, D_MODEL), D_MODEL ** -0.5 * DN_BETA),
        'ln1_g': 1.0 + nrm(ks[13], (DEPTH, D_MODEL), 0.02),
        'ln1_b': nrm(ks[14], (DEPTH, D_MODEL), 0.02),
        'w_pq': nrm(ks[15], (DEPTH, D_MODEL, P_HEADS * P_QDIM), D_MODEL ** -0.5),
        'pk1': nrm(ks[16], (DEPTH, P_HEADS, P_NKEYS, P_HALF), P_HALF ** -0.5),
        'pk2': nrm(ks[17], (DEPTH, P_HEADS, P_NKEYS, P_HALF), P_HALF ** -0.5),
        'pu': nrm(ks[18], (DEPTH, P_EXPERTS, D_MODEL), D_MODEL ** -0.5),
        'pv': nrm(ks[19], (DEPTH, P_EXPERTS, D_MODEL), DN_BETA),
        'ln2_g': 1.0 + nrm(ks[20], (DEPTH, D_MODEL), 0.02),
        'ln2_b': nrm(ks[21], (DEPTH, D_MODEL), 0.02),
    }


def reference(x_prompt, x_sample, cache_k, cache_v, cache_idx_k, state_gla, w_in, w_fa, b_fa, g_gla_norm,
              w_pa, w_pb, w_out, ln1_g, ln1_b, w_pq, pk1, pk2, pu, pv, ln2_g, ln2_b):
    Tp = x_prompt.shape[1]
    Ts = x_sample.shape[1]
    P = cache_k.shape[2]
    pos_p = jnp.arange(Tp)
    pos_s = P + jnp.arange(Ts)
    lim_s = jnp.full((Ts,), P + Ts, jnp.int32)
    hp = x_prompt
    hs = x_sample
    kp_l, vp_l, ikp_l, sp_l = [], [], [], []
    ks_l, vs_l, iks_l, ss_l = [], [], [], []
    for l in range(DEPTH):
        aq, ak, av, iq, ik, iw, gq, gk, gv, glog, gr, ga, gb = mixer_inputs(hp, pos_p, w_in[l], w_fa[l], b_fa[l])
        o_a = prompt_sparse_attention(aq, iq, iw, ak, av, ik)
        s_fin, o_b = gla_prompt(gq, gk, gv, glog)
        mix = branch_merge(hp, o_a, o_b, gr, ga, gb, g_gla_norm[l], w_pa[l], w_pb[l], w_out[l])
        hp = layer_norm(DN_ALPHA * hp + mix, ln1_g[l], ln1_b[l])
        hp = layer_norm(DN_ALPHA * hp + peer_blocked(hp, w_pq[l], pk1[l], pk2[l], pu[l], pv[l]), ln2_g[l], ln2_b[l])
        kp_l.append(ak)
        vp_l.append(av)
        ikp_l.append(ik)
        sp_l.append(s_fin.astype(x_prompt.dtype))
        aq, ak, av, iq, ik, iw, gq, gk, gv, glog, gr, ga, gb = mixer_inputs(hs, pos_s, w_in[l], w_fa[l], b_fa[l])
        k_all = jnp.concatenate([cache_k[l], ak], axis=1)
        v_all = jnp.concatenate([cache_v[l], av], axis=1)
        ik_all = jnp.concatenate([cache_idx_k[l], ik], axis=1)
        o_a = sparse_attention(aq, iq, iw, k_all, v_all, ik_all, lim_s)
        s_new, o_b = gla_chunk(state_gla[l].astype(jnp.float32), gq, gk, gv, glog)
        mix = branch_merge(hs, o_a, o_b, gr, ga, gb, g_gla_norm[l], w_pa[l], w_pb[l], w_out[l])
        hs = layer_norm(DN_ALPHA * hs + mix, ln1_g[l], ln1_b[l])
        hs_flat = hs.reshape(-1, D_MODEL)
        hs = layer_norm(DN_ALPHA * hs + peer(hs_flat, w_pq[l], pk1[l], pk2[l], pu[l], pv[l]).reshape(hs.shape),
                        ln2_g[l], ln2_b[l])
        ks_l.append(ak)
        vs_l.append(av)
        iks_l.append(ik)
        ss_l.append(s_new.astype(x_sample.dtype))
    return (hp, hs, jnp.stack(kp_l), jnp.stack(vp_l), jnp.stack(ikp_l), jnp.stack(sp_l),
            jnp.stack(ks_l), jnp.stack(vs_l), jnp.stack(iks_l), jnp.stack(ss_l))
```

```python
import functools
import math

import numpy as np
import jax
import jax.numpy as jnp
from jax import lax
from jax.experimental import pallas as pl
from jax.experimental.pallas import tpu as pltpu

F32 = jnp.float32
BF16 = jnp.bfloat16
I32 = jnp.int32

LANES = 128
VMEM_LIMIT = 56 << 20

CHUNK = 64
A_HEADS = 8
A_KV_HEADS = 2
A_HEAD_DIM = 64
A_GROUP = A_HEADS // A_KV_HEADS
IDX_HEADS = 4
IDX_DIM = 64
IDX_TOPK = 256
ROPE_THETA = 10000.0
G_HEADS = 4
G_KEY_DIM = 128
G_VAL_DIM = 128
G_LOWRANK = 16
G_TAU = 16.0
P_HEADS = 8
P_NKEYS = 128
P_HALF = 128
P_TOPK = 16
LN_EPS = 1e-5

W_AQ = A_HEADS * A_HEAD_DIM
W_AK = A_KV_HEADS * A_HEAD_DIM
W_IQ = IDX_HEADS * IDX_DIM
W_G = G_HEADS * G_KEY_DIM
IN_SIZES = (W_AQ, W_AK, W_AK, W_IQ, IDX_DIM, IDX_HEADS, W_G, W_G, W_G, G_LOWRANK, W_G, None)

INT_MIN = -(2 ** 31)
NEG_INF_KEY = -2139095041
NEG_BIG = -1e30


def _dot(a, b):
    return jnp.dot(a, b, preferred_element_type=F32)


def _dot_nt(a, b):
    return lax.dot_general(a, b, (((1,), (1,)), ((), ())), preferred_element_type=F32)


def _dot_tn(a, b):
    return lax.dot_general(a, b, (((0,), (0,)), ((), ())), preferred_element_type=F32)


def _sort_key(x):
    bits = pltpu.bitcast(x, I32)
    key = jnp.where(bits < 0, bits ^ 0x7FFFFFFF, bits)
    return jnp.where(key == -1, 0, key)


def _params(sem):
    return pltpu.CompilerParams(dimension_semantics=sem, vmem_limit_bytes=VMEM_LIMIT)


_PG_AQ, _PG_K, _PG_V, _PG_IQ, _PG_IKW, _PG_GQ, _PG_GK, _PG_GV, _PG_GF, _PG_GR, _PG_GATE = range(11)


def _pack_layout(d_model):
    widths = [W_AQ, W_AK, W_AK, W_IQ, LANES, W_G, W_G, W_G, LANES, W_G, 2 * d_model]
    offs = np.concatenate([[0], np.cumsum(widths)]).tolist()
    return widths, offs


def _pack_w_in(w_in):
    d = w_in.shape[0]
    sizes = list(IN_SIZES[:-1]) + [2 * d]
    cuts = np.cumsum(sizes)[:-1].tolist()
    aq, ak, av, iq, ik, iw, gq, gk, gv, gf, gr, gate = jnp.split(w_in, cuts, axis=1)
    z = lambda n: jnp.zeros((d, n), w_in.dtype)
    ikw = jnp.concatenate([ik, iw, z(LANES - IDX_DIM - IDX_HEADS)], axis=1)
    gfp = jnp.concatenate([gf, z(LANES - G_LOWRANK)], axis=1)
    return jnp.concatenate([aq, ak, av, iq, ikw, gq, gk, gv, gfp, gr, gate], axis=1).astype(BF16)


def _rope_tables(pos):
    half = A_HEAD_DIM // 2
    inv = ROPE_THETA ** (-jnp.arange(half, dtype=F32) / half)
    ang = pos.astype(F32)[:, None] * inv[None, :]
    c, s = jnp.cos(ang), jnp.sin(ang)
    return jnp.concatenate([c, c, c, c], -1), jnp.concatenate([-s, s, -s, s], -1)


def _proj_kernel(x_ref, w_ref, wfa_ref, bfa_ref, cos_ref, sin_ref,
                 aq_ref, k_ref, v_ref, kb_ref, vb_ref, iq_ref, ikw_ref, ikb_ref,
                 gq_ref, gk_ref, gv_ref, glog_ref, gr_ref, gate_ref, *, offs, widths):
    xb = x_ref[...].astype(BF16)
    cos = cos_ref[...]
    sin = sin_ref[...]
    lane = lax.broadcasted_iota(I32, cos.shape, 1)
    first_half = (lane & (A_HEAD_DIM // 2)) == 0

    def proj(g):
        return _dot(xb, w_ref[:, offs[g]:offs[g] + widths[g]])

    def rope_slab(y):
        fwd = pltpu.roll(y, LANES - A_HEAD_DIM // 2, 1)
        bwd = pltpu.roll(y, A_HEAD_DIM // 2, 1)
        return y * cos + jnp.where(first_half, fwd, bwd) * sin

    def rope(y):
        return [rope_slab(y[:, s * LANES:(s + 1) * LANES]) for s in range(y.shape[1] // LANES)]

    for s, slab in enumerate(rope(proj(_PG_AQ))):
        aq_ref[:, s * LANES:(s + 1) * LANES] = (slab * (A_HEAD_DIM ** -0.5)).astype(BF16)
    k = rope(proj(_PG_K))[0]
    k_ref[...] = k
    kb_ref[...] = k.astype(BF16)
    v = proj(_PG_V)
    v_ref[...] = v
    vb_ref[...] = v.astype(BF16)
    for s, slab in enumerate(rope(proj(_PG_IQ))):
        iq_ref[:, s * LANES:(s + 1) * LANES] = (slab * (IDX_DIM ** -0.5)).astype(BF16)
    raw = proj(_PG_IKW)
    ikw = jnp.where(lane < IDX_DIM, rope_slab(raw), raw * (IDX_HEADS ** -0.5))
    ikw_ref[...] = ikw
    ikb_ref[...] = ikw.astype(BF16)
    gq_ref[...] = proj(_PG_GQ) * (G_KEY_DIM ** -0.5)
    gk_ref[...] = proj(_PG_GK)
    gv_ref[...] = proj(_PG_GV)
    z = _dot(proj(_PG_GF).astype(BF16), wfa_ref[...]) + bfa_ref[...]
    glog_ref[...] = (jnp.minimum(z, 0.0) - jnp.log1p(jnp.exp(-jnp.abs(z)))) * (1.0 / G_TAU)
    gr_ref[...] = proj(_PG_GR)
    gate_ref[...] = proj(_PG_GATE)


def _proj(x2, pos, w_in, w_fa, b_fa, tm):
    n, d = x2.shape
    t = pos.shape[0]
    widths, offs = _pack_layout(d)
    wp = _pack_w_in(w_in)
    wfa = jnp.concatenate([w_fa, jnp.zeros((LANES - G_LOWRANK, W_G), w_fa.dtype)], 0).astype(BF16)
    cos, sin = _rope_tables(pos)
    if tm > t:
        cos, sin = jnp.tile(cos, (tm // t, 1)), jnp.tile(sin, (tm // t, 1))
    nper = cos.shape[0] // tm
    row = lambda w: pl.BlockSpec((tm, w), lambda i: (i, 0))
    const = lambda a: pl.BlockSpec(a.shape, lambda i: (0, 0))
    tab = pl.BlockSpec((tm, LANES), lambda i: (i % nper, 0))
    outs = [(W_AQ, BF16), (LANES, F32), (LANES, F32), (LANES, BF16), (LANES, BF16), (W_IQ, BF16),
            (LANES, F32), (LANES, BF16), (W_G, F32), (W_G, F32), (W_G, F32), (W_G, F32), (W_G, F32),
            (2 * d, F32)]
    bfa = b_fa.reshape(1, W_G)
    return pl.pallas_call(
        functools.partial(_proj_kernel, offs=offs, widths=widths),
        grid=(n // tm,),
        in_specs=[row(d), const(wp), const(wfa), const(bfa), tab, tab],
        out_specs=[row(w) for w, _ in outs],
        out_shape=[jax.ShapeDtypeStruct((n, w), dt) for w, dt in outs],
        compiler_params=_params(("parallel",)),
        name="proj",
    )(x2, wp, wfa, bfa, cos, sin)


def _attn_kernel(nkt_ref, aq_ref, iq_ref, ikw_ref, lim_ref, kb_ref, vb_ref, ikb_ref, o_ref,
                 keys_ref, qs_ref, iqs_ref, m_ref, l_ref, acc_ref, *, qb, kt_w, topk, idx_bits):
    j = pl.program_id(1)
    nkt = nkt_ref[j]
    ncol = kt_w // LANES

    for h in range(A_HEADS):
        qs_ref[h // A_GROUP, (h % A_GROUP) * qb:(h % A_GROUP + 1) * qb, :] = \
            aq_ref[:, h * A_HEAD_DIM:(h + 1) * A_HEAD_DIM]
    for h in range(IDX_HEADS):
        iqs_ref[h * qb:(h + 1) * qb, :] = iq_ref[:, h * IDX_DIM:(h + 1) * IDX_DIM]

    lim = lim_ref[...]
    lane = lax.broadcasted_iota(I32, (qb, LANES), 1)

    def score_tile(kt, _):
        base = pl.multiple_of(kt * kt_w, kt_w)
        ik_t = ikb_ref[0, pl.ds(base, kt_w), :][:, :IDX_DIM]
        s = jnp.maximum(_dot_nt(iqs_ref[...], ik_t), 0.0)
        score = jnp.zeros((qb, kt_w), F32)
        for h in range(IDX_HEADS):
            w = ikw_ref[:, IDX_DIM + h:IDX_DIM + h + 1]
            score = score + s[h * qb:(h + 1) * qb, :] * w
        for c in range(ncol):
            idx = lane + (base + c * LANES)
            key = _sort_key(score[:, c * LANES:(c + 1) * LANES])
            keys_ref[:, pl.ds(pl.multiple_of(base + c * LANES, LANES), LANES)] = \
                jnp.where(idx < lim, key, NEG_INF_KEY)
        return 0

    lax.fori_loop(0, nkt, score_tile, 0)

    def count(pred):
        def body(kt, acc):
            base = kt * kt_w
            for c in range(ncol):
                off = pl.multiple_of(base + c * LANES, LANES)
                acc = acc + pred(keys_ref[:, pl.ds(off, LANES)], lane + off)
            return acc
        acc = lax.fori_loop(0, nkt, body, jnp.zeros((qb, LANES), F32))
        return jnp.sum(acc, axis=1, keepdims=True)

    def count_ge(t_col):
        tb = jnp.broadcast_to(t_col, (qb, LANES))
        return count(lambda kk, idx: jnp.where(kk >= tb, 1.0, 0.0))

    kf = float(topk)
    thr = jnp.where(count_ge(jnp.zeros((qb, 1), I32)) >= kf, 0, INT_MIN).astype(I32)

    def thr_bit(i, t):
        cand = t + jnp.left_shift(jnp.int32(1), 30 - i)
        return jnp.where(count_ge(cand) >= kf, cand, t)

    thr = lax.fori_loop(0, 31, thr_bit, thr)
    n_gt = count_ge(thr + 1)
    n_eq = count_ge(thr) - n_gt
    need = kf - n_gt
    finite = thr > NEG_INF_KEY
    excess = jnp.where(finite, jnp.where(n_eq > need, 1.0, 0.0), 0.0)
    cut_ref_default = jnp.where(finite, 2 ** 30, -1).astype(I32)

    thr_b = jnp.broadcast_to(thr, (qb, LANES))

    def count_eq_below(j_col):
        jb = jnp.broadcast_to(j_col, (qb, LANES))
        return count(lambda kk, idx: jnp.where(kk == thr_b, jnp.where(idx < jb, 1.0, 0.0), 0.0))

    def resolve_ties():
        def bit(i, jc):
            cand = jc + jnp.left_shift(jnp.int32(1), idx_bits - 1 - i)
            return jnp.where(count_eq_below(cand) <= need - 1.0, cand, jc)
        jc = lax.fori_loop(0, idx_bits, bit, jnp.zeros((qb, 1), I32))
        return jnp.where(finite, jc, -1)

    any_excess = jnp.max(excess) > 0.0
    cut = lax.cond(any_excess, resolve_ties, lambda: cut_ref_default)
    cut_b = jnp.broadcast_to(cut, (qb, LANES))

    m_ref[...] = jnp.full(m_ref.shape, NEG_BIG, F32)
    l_ref[...] = jnp.zeros(l_ref.shape, F32)
    acc_ref[...] = jnp.zeros(acc_ref.shape, F32)

    def attend(kt, _):
        base = pl.multiple_of(kt * kt_w, kt_w)
        k_t = kb_ref[0, pl.ds(base, kt_w), :]
        v_t = vb_ref[0, pl.ds(base, kt_w), :]
        bias = []
        for c in range(ncol):
            off = pl.multiple_of(base + c * LANES, LANES)
            kk = keys_ref[:, pl.ds(off, LANES)]
            tie = jnp.where(lane + off <= cut_b, 0.0, NEG_BIG)
            bias.append(jnp.where(kk > thr_b, 0.0, jnp.where(kk == thr_b, tie, NEG_BIG)))
        bias = jnp.concatenate(bias, axis=1)
        for n in range(A_KV_HEADS):
            logits = _dot_nt(qs_ref[n], k_t[:, n * A_HEAD_DIM:(n + 1) * A_HEAD_DIM])
            logits = (logits.reshape(A_GROUP, qb, kt_w) + bias[None]).reshape(A_GROUP * qb, kt_w)
            m_old = m_ref[n]
            m_new = jnp.maximum(m_old, jnp.max(logits, axis=1, keepdims=True))
            alpha = jnp.exp(m_old - m_new)
            p = jnp.exp(logits - m_new)
            l_ref[n] = alpha * l_ref[n] + jnp.sum(p, axis=1, keepdims=True)
            acc_ref[n] = alpha * acc_ref[n] + _dot(p.astype(BF16), v_t[:, n * A_HEAD_DIM:(n + 1) * A_HEAD_DIM])
            m_ref[n] = m_new
        return 0

    lax.fori_loop(0, nkt, attend, 0)

    for h in range(A_HEADS):
        n, g = h // A_GROUP, h % A_GROUP
        o = acc_ref[n, g * qb:(g + 1) * qb, :] / l_ref[n, g * qb:(g + 1) * qb, :]
        o_ref[:, h * A_HEAD_DIM:(h + 1) * A_HEAD_DIM] = o.astype(BF16)


def _attention(aq, iq, ikw, limits, kb, vb, ikb, nkt, qb, kt_w, topk):
    n = aq.shape[0]
    b, l_pad = kb.shape[0], kb.shape[1]
    nq = n // (b * qb)
    lim = jnp.broadcast_to(limits.astype(I32)[:, None], (limits.shape[0], LANES))
    idx_bits = max(1, int(math.ceil(math.log2(l_pad))))
    qrow = lambda w: pl.BlockSpec((qb, w), lambda bi, j, s: (bi * nq + j, 0))
    keys = lambda a: pl.BlockSpec((1,) + a.shape[1:], lambda bi, j, s: (bi, 0, 0))
    grid_spec = pltpu.PrefetchScalarGridSpec(
        num_scalar_prefetch=1,
        grid=(b, nq),
        in_specs=[qrow(W_AQ), qrow(W_IQ), qrow(LANES),
                  pl.BlockSpec((qb, LANES), lambda bi, j, s: (j, 0)),
                  keys(kb), keys(vb), keys(ikb)],
        out_specs=qrow(W_AQ),
        scratch_shapes=[pltpu.VMEM((qb, l_pad), I32),
                        pltpu.VMEM((A_KV_HEADS, A_GROUP * qb, A_HEAD_DIM), BF16),
                        pltpu.VMEM((IDX_HEADS * qb, IDX_DIM), BF16),
                        pltpu.VMEM((A_KV_HEADS, A_GROUP * qb, 1), F32),
                        pltpu.VMEM((A_KV_HEADS, A_GROUP * qb, 1), F32),
                        pltpu.VMEM((A_KV_HEADS, A_GROUP * qb, A_HEAD_DIM), F32)],
    )
    return pl.pallas_call(
        functools.partial(_attn_kernel, qb=qb, kt_w=kt_w, topk=topk, idx_bits=idx_bits),
        grid_spec=grid_spec,
        out_shape=jax.ShapeDtypeStruct((n, W_AQ), BF16),
        compiler_params=_params(("parallel", "arbitrary")),
        name="attn",
    )(nkt, aq, iq, ikw, lim, kb, vb, ikb)


def _gla_constants(c):
    nlev = int(math.log2(c))
    t = np.arange(c)
    mats = [(t[None, :] <= t[:, None])]
    masks = [np.eye(c, dtype=bool)]
    for lev in range(nlev):
        m = c >> (lev + 1)
        ref_row = (t // (2 * m)) * 2 * m + m
        mats.append(t[None, :] <= ref_row[:, None])
        upper = (t & m) != 0
        same = (t[:, None] // (2 * m)) == (t[None, :] // (2 * m))
        masks.append(same & upper[:, None] & ~upper[None, :])
    return (jnp.asarray(np.concatenate(mats, 0).astype(np.float32), BF16),
            jnp.asarray(np.stack(masks).astype(np.float32)), nlev)


def _gla_kernel(q_ref, k_ref, v_ref, g_ref, gr_ref, gn_ref, mst_ref, msk_ref, s0_ref,
                ob_ref, sfin_ref, st_ref, *, c, nlev):
    i = pl.program_id(1)
    hk = G_KEY_DIM

    @pl.when(i == 0)
    def _():
        for h in range(G_HEADS):
            st_ref[h] = s0_ref[0, h].T

    g = g_ref[...]
    g_hi = g.astype(BF16)
    r1 = g - g_hi.astype(F32)
    g_mid = r1.astype(BF16)
    g_lo = (r1 - g_mid.astype(F32)).astype(BF16)
    mst = mst_ref[...]
    bs = _dot(mst, g_hi) + _dot(mst, g_mid) + _dot(mst, g_lo)
    b = bs[0:c]
    q = q_ref[...]
    k = k_ref[...]
    vb = v_ref[...].astype(BF16)
    row = lax.broadcasted_iota(I32, q.shape, 0)
    hs = lambda a, h: a[:, h * hk:(h + 1) * hk]

    qb = q.astype(BF16)
    kb = k.astype(BF16)
    attn = [_dot_nt(hs(qb, h), hs(kb, h)) * msk_ref[0] for h in range(G_HEADS)]
    for lev in range(nlev):
        m = c >> (lev + 1)
        upper = (row & m) != 0
        d = b - bs[(lev + 1) * c:(lev + 2) * c]
        e = jnp.exp(jnp.where(upper, d, -d))
        qt = jnp.where(upper, q * e, 0.0).astype(BF16)
        kt = jnp.where(upper, 0.0, k * e).astype(BF16)
        mk = msk_ref[lev + 1]
        for h in range(G_HEADS):
            attn[h] = attn[h] + _dot_nt(hs(qt, h), hs(kt, h)) * mk

    qe = (q * jnp.exp(b)).astype(BF16)
    b_last = b[c - 1:c, :]
    khat = (k * jnp.exp(b_last - b)).astype(BF16)
    dec = jnp.exp(b_last)
    gr = gr_ref[...]
    gn = gn_ref[...]
    for h in range(G_HEADS):
        st = st_ref[h]
        o = _dot_nt(hs(qe, h), st.astype(BF16)) + _dot(attn[h].astype(BF16), hs(vb, h))
        st_ref[h] = st * hs(dec, h) + _dot_tn(hs(vb, h), hs(khat, h))
        ms = jnp.mean(o * o, axis=1, keepdims=True)
        grh = hs(gr, h)
        of = o * lax.rsqrt(ms + LN_EPS) * hs(gn, h) * (grh / (1.0 + jnp.exp(-grh)))
        ob_ref[:, h * hk:(h + 1) * hk] = of.astype(BF16)

    @pl.when(i == pl.num_programs(1) - 1)
    def _():
        for h in range(G_HEADS):
            sfin_ref[0, h] = st_ref[h].T


def _gla(gq, gk, gv, glog, gr, g_norm, s0, b, c):
    n = gq.shape[0]
    nc = n // (b * c)
    mst, msk, nlev = _gla_constants(c)
    row = pl.BlockSpec((c, W_G), lambda bi, i: (bi * nc + i, 0))
    const = lambda a: pl.BlockSpec(a.shape, lambda bi, i: (0,) * a.ndim)
    st_spec = pl.BlockSpec((1, G_HEADS, G_KEY_DIM, G_VAL_DIM), lambda bi, i: (bi, 0, 0, 0))
    gn = g_norm.reshape(1, W_G)
    return pl.pallas_call(
        functools.partial(_gla_kernel, c=c, nlev=nlev),
        grid=(b, nc),
        in_specs=[row, row, row, row, row, const(gn), const(mst), const(msk), st_spec],
        out_specs=[row, st_spec],
        out_shape=[jax.ShapeDtypeStruct((n, W_G), BF16),
                   jax.ShapeDtypeStruct((b, G_HEADS, G_KEY_DIM, G_VAL_DIM), F32)],
        scratch_shapes=[pltpu.VMEM((G_HEADS, G_VAL_DIM, G_KEY_DIM), F32)],
        compiler_params=_params(("parallel", "arbitrary")),
        name="gla",
    )(gq, gk, gv, glog, gr, gn, mst, msk, s0)


def _layer_norm(z, g, b):
    mu = jnp.mean(z, axis=1, keepdims=True)
    zc = z - mu
    var = jnp.mean(zc * zc, axis=1, keepdims=True)
    return zc * lax.rsqrt(var + LN_EPS) * g + b


def _sigmoid(x):
    return 1.0 / (1.0 + jnp.exp(-x))


def _merge_kernel(x_ref, oa_ref, ob_ref, gate_ref, wpa_ref, wpb_ref, wout_ref, g1_ref, b1_ref,
                  wpq_ref, pk1_ref, pk2_ref,
                  h1_ref, h1b_ref, s1_ref, s2_ref, s1t_ref, s2t_ref, *, alpha, d):
    ya = _dot(oa_ref[...], wpa_ref[...])
    yb = _dot(ob_ref[...], wpb_ref[...])
    m = _sigmoid(gate_ref[:, :d]) * ya + _sigmoid(gate_ref[:, d:]) * yb
    mix = _dot(m.astype(BF16), wout_ref[...])
    h1 = _layer_norm(alpha * x_ref[...] + mix, g1_ref[...], b1_ref[...])
    h1_ref[...] = h1
    h1b = h1.astype(BF16)
    h1b_ref[...] = h1b
    qp = _dot(h1b, wpq_ref[...])
    for h in range(P_HEADS):
        for half, (pk_ref, s_ref, st_ref) in enumerate(((pk1_ref, s1_ref, s1t_ref), (pk2_ref, s2_ref, s2t_ref))):
            c0 = (2 * h + half) * P_HALF
            s = _dot_nt(qp[:, c0:c0 + P_HALF].astype(BF16), pk_ref[h])
            s_ref[:, h * P_NKEYS:(h + 1) * P_NKEYS] = s
            st_ref[h * P_NKEYS:(h + 1) * P_NKEYS, :] = s.T


def _merge(x2, oa, ob, gate, w_pa, w_pb, w_out, ln_g, ln_b, w_pq, pk1, pk2, alpha, tm):
    n, d = x2.shape
    hk = P_HEADS * P_NKEYS
    row = lambda w: pl.BlockSpec((tm, w), lambda i: (i, 0))
    col = pl.BlockSpec((hk, tm), lambda i: (0, i))
    const = lambda a: pl.BlockSpec(a.shape, lambda i: (0,) * a.ndim)
    ws = [w_pa.astype(BF16), w_pb.astype(BF16), w_out.astype(BF16), ln_g.reshape(1, d), ln_b.reshape(1, d),
          w_pq.astype(BF16), pk1.astype(BF16), pk2.astype(BF16)]
    return pl.pallas_call(
        functools.partial(_merge_kernel, alpha=alpha, d=d),
        grid=(n // tm,),
        in_specs=[row(d), row(W_AQ), row(W_G), row(2 * d)] + [const(w) for w in ws],
        out_specs=[row(d), row(d), row(hk), row(hk), col, col],
        out_shape=[jax.ShapeDtypeStruct((n, d), F32), jax.ShapeDtypeStruct((n, d), BF16),
                   jax.ShapeDtypeStruct((n, hk), F32), jax.ShapeDtypeStruct((n, hk), F32),
                   jax.ShapeDtypeStruct((hk, n), F32), jax.ShapeDtypeStruct((hk, n), F32)],
        compiler_params=_params(("parallel",)),
        name="merge",
    )(x2, oa, ob, gate, *ws)


def _kth_largest_key(keys, kf, axis_rows):
    del axis_rows

    def cnt(t_row):
        return jnp.sum(jnp.where(keys >= t_row, 1.0, 0.0), axis=0, keepdims=True)

    tn = keys.shape[1]
    t0 = jnp.where(cnt(jnp.zeros((1, tn), I32)) >= kf, 0, INT_MIN).astype(I32)

    def bit(i, t):
        cand = t + jnp.left_shift(jnp.int32(1), 30 - i)
        return jnp.where(cnt(cand) >= kf, cand, t)

    return lax.fori_loop(0, 31, bit, t0)


def _select_kernel(s1t_ref, s2t_ref, tri_ref, e1_ref, e2_ref, tau_ref, *, tn):
    kf = float(P_TOPK)
    nk = P_NKEYS

    def head(h, _):
        r0 = pl.multiple_of(h * nk, nk)
        tops, maxes, svals = [], [], []
        for st_ref in (s1t_ref, s2t_ref):
            s = st_ref[pl.ds(r0, nk), :]
            keys = _sort_key(s)
            thr = _kth_largest_key(keys, kf, nk)
            chosen = jnp.where(keys >= thr, 1.0, 0.0)
            rank = _dot(tri_ref[...], chosen.astype(BF16)) * chosen
            tops.append(jnp.concatenate(
                [jnp.sum(jnp.where(rank == float(r + 1), s, 0.0), axis=0, keepdims=True) for r in range(P_TOPK)],
                axis=0))
            maxes.append(jnp.max(s, axis=0, keepdims=True))
            svals.append(s)
        v1, v2 = tops
        cand = jnp.concatenate([v1[a:a + 1, :] + v2 for a in range(P_TOPK)], axis=0)
        ckeys = _sort_key(cand)
        tkey = _kth_largest_key(ckeys, kf, P_TOPK * P_TOPK)
        picked = ckeys >= tkey
        tau = jnp.min(jnp.where(picked, cand, jnp.inf), axis=0, keepdims=True)
        cmax = maxes[0] + maxes[1]
        zsum = jnp.sum(jnp.where(picked, jnp.exp(cand - cmax), 0.0), axis=0, keepdims=True)
        e1 = jnp.exp(svals[0] - maxes[0]) / zsum
        e2 = jnp.exp(svals[1] - maxes[1])
        e1_ref[:, pl.ds(r0, nk)] = e1.T
        e2_ref[:, pl.ds(r0, nk)] = e2.T
        tau_ref[:, pl.ds(r0, nk)] = jnp.broadcast_to(tau, (nk, tn)).T
        return 0

    lax.fori_loop(0, P_HEADS, head, 0)


def _select(s1t, s2t, tn):
    hk, n = s1t.shape
    tri = jnp.asarray(np.tril(np.ones((P_NKEYS, P_NKEYS), np.float32)), BF16)
    col = pl.BlockSpec((hk, tn), lambda i: (0, i))
    row = pl.BlockSpec((tn, hk), lambda i: (i, 0))
    return pl.pallas_call(
        functools.partial(_select_kernel, tn=tn),
        grid=(n // tn,),
        in_specs=[col, col, pl.BlockSpec(tri.shape, lambda i: (0, 0))],
        out_specs=[row, row, row],
        out_shape=[jax.ShapeDtypeStruct((n, hk), F32)] * 3,
        compiler_params=_params(("parallel",)),
        name="select",
    )(s1t, s2t, tri)


I1_PER_STEP = LANES // P_HEADS
E_PER_STEP = I1_PER_STEP * P_NKEYS
ROW_BLOCK = 32


def _gelu(x):
    return 0.5 * x * (1.0 + lax.erf(x * (2.0 ** -0.5)))


def _peer_kernel(hb_ref, pu_ref, pv_ref, s1p_ref, e1p_ref, s2_ref, e2_ref, tau_ref, h1_ref, g2_ref, b2_ref,
                 y_ref, a_ref, gw_ref, acc_ref, *, tn, alpha):
    g = pl.program_id(1)

    @pl.when(g == 0)
    def _():
        acc_ref[...] = jnp.zeros(acc_ref.shape, F32)

    a_ref[...] = _dot_nt(hb_ref[...], pu_ref[...])

    def rows(r, _):
        r0 = pl.multiple_of(r * ROW_BLOCK, ROW_BLOCK)
        rs = pl.ds(r0, ROW_BLOCK)
        for i1 in range(I1_PER_STEP):
            w = jnp.zeros((ROW_BLOCK, LANES), F32)
            for h in range(P_HEADS):
                c = i1 * P_HEADS + h
                hl = slice(h * P_NKEYS, (h + 1) * P_NKEYS)
                total = s1p_ref[rs, c:c + 1] + s2_ref[rs, hl]
                w = w + jnp.where(total >= tau_ref[rs, hl], e1p_ref[rs, c:c + 1] * e2_ref[rs, hl], 0.0)
            el = slice(i1 * P_NKEYS, (i1 + 1) * P_NKEYS)
            gw_ref[rs, el] = (w * _gelu(a_ref[rs, el])).astype(BF16)
        return 0

    lax.fori_loop(0, tn // ROW_BLOCK, rows, 0)
    acc_ref[...] += _dot(gw_ref[...], pv_ref[...])

    @pl.when(g == pl.num_programs(1) - 1)
    def _():
        y_ref[...] = _layer_norm(alpha * h1_ref[...] + acc_ref[...], g2_ref[...], b2_ref[...])


def _group_heads_last(a):
    n = a.shape[0]
    ng = P_NKEYS // I1_PER_STEP
    return a.reshape(n, P_HEADS, ng, I1_PER_STEP).transpose(0, 2, 3, 1).reshape(n, P_HEADS * P_NKEYS)


def _peer(h1, h1b, s1, s2, e1, e2, tau, pu, pv, ln_g, ln_b, alpha, tn):
    n, d = h1.shape
    hk = P_HEADS * P_NKEYS
    ng = pu.shape[0] // E_PER_STEP
    s1p, e1p = _group_heads_last(s1), _group_heads_last(e1)
    row = lambda w: pl.BlockSpec((tn, w), lambda i, g: (i, 0))
    grp = pl.BlockSpec((tn, LANES), lambda i, g: (i, g))
    tab = pl.BlockSpec((E_PER_STEP, d), lambda i, g: (g, 0))
    const = pl.BlockSpec((1, d), lambda i, g: (0, 0))
    return pl.pallas_call(
        functools.partial(_peer_kernel, tn=tn, alpha=alpha),
        grid=(n // tn, ng),
        in_specs=[row(d), tab, tab, grp, grp, row(hk), row(hk), row(hk), row(d), const, const],
        out_specs=row(d),
        out_shape=jax.ShapeDtypeStruct((n, d), F32),
        scratch_shapes=[pltpu.VMEM((tn, E_PER_STEP), F32), pltpu.VMEM((tn, E_PER_STEP), BF16),
                        pltpu.VMEM((tn, d), F32)],
        compiler_params=_params(("parallel", "arbitrary")),
        name="peer",
    )(h1b, pu.astype(BF16), pv.astype(BF16), s1p, e1p, s2, e2, tau, h1, ln_g.reshape(1, d), ln_b.reshape(1, d))


def _pick_tile(n, pref):
    t = min(n, pref)
    assert n % t == 0
    return t


def _layer(x, pos, limits, past, s0, w, *, qb, kt_w, chunk, alpha):
    b, t, d = x.shape
    n = b * t
    x2 = x.reshape(n, d)
    tm = _pick_tile(n, 256)
    (aq, k32, v32, kb, vb, iq, ikw, ikb, gq, gk, gv, glog, gr, gate) = _proj(
        x2, pos, w["w_in"], w["w_fa"], w["b_fa"], tm)

    kb3, vb3, ikb3 = kb.reshape(b, t, LANES), vb.reshape(b, t, LANES), ikb.reshape(b, t, LANES)
    if past is not None:
        ck, cv, cik = past
        p = ck.shape[1]
        kb3 = jnp.concatenate([ck.reshape(b, p, LANES).astype(BF16), kb3], axis=1)
        vb3 = jnp.concatenate([cv.reshape(b, p, LANES).astype(BF16), vb3], axis=1)
        ikb3 = jnp.concatenate([cik.astype(BF16), ikb3[:, :, :IDX_DIM]], axis=1)
    l_all = kb3.shape[1]
    topk = min(IDX_TOPK, l_all // 4)
    l_pad = -(-l_all // kt_w) * kt_w
    if l_pad != l_all:
        padl = lambda a: jnp.pad(a, ((0, 0), (0, l_pad - l_all), (0, 0)))
        kb3, vb3, ikb3 = padl(kb3), padl(vb3), padl(ikb3)
    nq = t // qb
    lim_blk = np.asarray(limits).reshape(nq, qb).max(axis=1)
    nkt = jnp.asarray(np.minimum(-(-lim_blk // kt_w), l_pad // kt_w).astype(np.int32))
    o_a = _attention(aq, iq, ikw, jnp.asarray(limits, I32), kb3, vb3, ikb3, nkt, qb, kt_w, topk)

    o_b, s_fin = _gla(gq, gk, gv, glog, gr, w["g_gla_norm"], s0, b, chunk)

    h1, h1b, s1, s2, s1t, s2t = _merge(x2, o_a, o_b, gate, w["w_pa"], w["w_pb"], w["w_out"],
                                       w["ln1_g"], w["ln1_b"], w["w_pq"], w["pk1"], w["pk2"], alpha, tm)
    e1, e2, tau = _select(s1t, s2t, _pick_tile(n, 256))
    y = _peer(h1, h1b, s1, s2, e1, e2, tau, w["pu"], w["pv"], w["ln2_g"], w["ln2_b"], alpha, _pick_tile(n, 256))

    k_out = k32.reshape(b, t, A_KV_HEADS, A_HEAD_DIM)
    v_out = v32.reshape(b, t, A_KV_HEADS, A_HEAD_DIM)
    ik_out = ikw[:, :IDX_DIM].reshape(b, t, IDX_DIM)
    return y.reshape(b, t, d), k_out, v_out, ik_out, s_fin


def kernel(x_prompt, x_sample, cache_k, cache_v, cache_idx_k, state_gla, w_in, w_fa, b_fa, g_gla_norm,
           w_pa, w_pb, w_out, ln1_g, ln1_b, w_pq, pk1, pk2, pu, pv, ln2_g, ln2_b):
    depth = w_in.shape[0]
    alpha = (2.0 * depth) ** 0.25
    bp, tp, _ = x_prompt.shape
    bs, ts, _ = x_sample.shape
    past_len = cache_k.shape[2]
    pos_p = jnp.arange(tp)
    pos_s = past_len + jnp.arange(ts)
    lim_p = (np.arange(tp) // CHUNK + 1) * CHUNK
    lim_s = np.full((ts,), past_len + ts)
    names = ("w_in", "w_fa", "b_fa", "g_gla_norm", "w_pa", "w_pb", "w_out", "ln1_g", "ln1_b",
             "w_pq", "pk1", "pk2", "pu", "pv", "ln2_g", "ln2_b")
    stacked = (w_in, w_fa, b_fa, g_gla_norm, w_pa, w_pb, w_out, ln1_g, ln1_b, w_pq, pk1, pk2, pu, pv, ln2_g, ln2_b)
    hp, hs = x_prompt, x_sample
    outs_p, outs_s = [], []
    for l in range(depth):
        w = {nm: a[l] for nm, a in zip(names, stacked)}
        s0 = jnp.zeros((bp, G_HEADS, G_KEY_DIM, G_VAL_DIM), F32)
        hp, *rest = _layer(hp, pos_p, lim_p, None, s0, w, qb=128, kt_w=256, chunk=CHUNK, alpha=alpha)
        outs_p.append(rest)
        hs, *rest = _layer(hs, pos_s, lim_s, (cache_k[l], cache_v[l], cache_idx_k[l]), state_gla[l], w,
                           qb=ts, kt_w=256, chunk=ts, alpha=alpha)
        outs_s.append(rest)
    stack = lambda outs, i: jnp.stack([o[i] for o in outs])
    return (hp, hs, stack(outs_p, 0), stack(outs_p, 1), stack(outs_p, 2), stack(outs_p, 3),
            stack(outs_s, 0), stack(outs_s, 1), stack(outs_s, 2), stack(outs_s, 3))
```

```python
import functools
import math

import numpy as np
import jax
import jax.numpy as jnp
from jax import lax
from jax.experimental import pallas as pl
from jax.experimental.pallas import tpu as pltpu

F32 = jnp.float32
BF16 = jnp.bfloat16
I32 = jnp.int32

LANES = 128
SUBLANES = 8
VMEM_LIMIT = 56 << 20

CHUNK = 64
A_HEADS = 8
A_KV_HEADS = 2
A_HEAD_DIM = 64
A_GROUP = A_HEADS // A_KV_HEADS
IDX_HEADS = 4
IDX_DIM = 64
IDX_TOPK = 256
ROPE_THETA = 10000.0
G_HEADS = 4
G_KEY_DIM = 128
G_VAL_DIM = 128
G_LOWRANK = 16
G_TAU = 16.0
P_HEADS = 8
P_NKEYS = 128
P_HALF = 128
P_TOPK = 16
LN_EPS = 1e-5

W_AQ = A_HEADS * A_HEAD_DIM
W_AK = A_KV_HEADS * A_HEAD_DIM
W_IQ = IDX_HEADS * IDX_DIM
W_G = G_HEADS * G_KEY_DIM
IN_SIZES = (W_AQ, W_AK, W_AK, W_IQ, IDX_DIM, IDX_HEADS, W_G, W_G, W_G, G_LOWRANK, W_G, None)

TOKEN_TILE = 256
PEER_TOKEN_TILE = 512
Q_BLOCK = 128
KEY_TILE = 512
COUNT_TILES = 2

INT_MIN = -(2 ** 31)
NEG_INF_KEY = -2139095041
NEG_BIG = -1e30


def _dot(a, b):
    return jnp.dot(a, b, preferred_element_type=F32)


def _dot_nt(a, b):
    return lax.dot_general(a, b, (((1,), (1,)), ((), ())), preferred_element_type=F32)


def _dot_tn(a, b):
    return lax.dot_general(a, b, (((0,), (0,)), ((), ())), preferred_element_type=F32)


def _sort_key(x):
    bits = pltpu.bitcast(x, I32)
    key = jnp.where(bits < 0, bits ^ 0x7FFFFFFF, bits)
    return jnp.where(key == -1, 0, key)


def _params(sem):
    return pltpu.CompilerParams(dimension_semantics=sem, vmem_limit_bytes=VMEM_LIMIT)


_PG_AQ, _PG_K, _PG_V, _PG_IQ, _PG_IKW, _PG_GQ, _PG_GK, _PG_GV, _PG_GF, _PG_GR, _PG_GATE = range(11)


def _pack_layout(d_model):
    widths = [W_AQ, W_AK, W_AK, W_IQ, LANES, W_G, W_G, W_G, LANES, W_G, 2 * d_model]
    offs = np.concatenate([[0], np.cumsum(widths)]).tolist()
    return widths, offs


def _pack_w_in(w_in):
    d = w_in.shape[0]
    sizes = list(IN_SIZES[:-1]) + [2 * d]
    cuts = np.cumsum(sizes)[:-1].tolist()
    aq, ak, av, iq, ik, iw, gq, gk, gv, gf, gr, gate = jnp.split(w_in, cuts, axis=1)
    z = lambda n: jnp.zeros((d, n), w_in.dtype)
    ikw = jnp.concatenate([ik, iw, z(LANES - IDX_DIM - IDX_HEADS)], axis=1)
    gfp = jnp.concatenate([gf, z(LANES - G_LOWRANK)], axis=1)
    return jnp.concatenate([aq, ak, av, iq, ikw, gq, gk, gv, gfp, gr, gate], axis=1).astype(BF16)


def _rope_tables(pos):
    half = A_HEAD_DIM // 2
    inv = ROPE_THETA ** (-jnp.arange(half, dtype=F32) / half)
    ang = pos.astype(F32)[:, None] * inv[None, :]
    c, s = jnp.cos(ang), jnp.sin(ang)
    return jnp.concatenate([c, c, c, c], -1), jnp.concatenate([-s, s, -s, s], -1)


def _proj_kernel(x_ref, w_ref, wfa_ref, bfa_ref, cos_ref, sin_ref,
                 aq_ref, k_ref, v_ref, kb_ref, vb_ref, iq_ref, ikw_ref, ikb_ref,
                 gq_ref, gk_ref, gv_ref, glog_ref, gr_ref, gate_ref, *, offs, widths):
    xb = x_ref[...].astype(BF16)
    cos = cos_ref[...]
    sin = sin_ref[...]
    lane = lax.broadcasted_iota(I32, cos.shape, 1)
    first_half = (lane & (A_HEAD_DIM // 2)) == 0

    def proj(g):
        return _dot(xb, w_ref[:, offs[g]:offs[g] + widths[g]])

    def rope_slab(y):
        fwd = pltpu.roll(y, LANES - A_HEAD_DIM // 2, 1)
        bwd = pltpu.roll(y, A_HEAD_DIM // 2, 1)
        return y * cos + jnp.where(first_half, fwd, bwd) * sin

    def rope(y):
        return [rope_slab(y[:, s * LANES:(s + 1) * LANES]) for s in range(y.shape[1] // LANES)]

    for s, slab in enumerate(rope(proj(_PG_AQ))):
        aq_ref[:, s * LANES:(s + 1) * LANES] = (slab * (A_HEAD_DIM ** -0.5)).astype(BF16)
    k = rope(proj(_PG_K))[0]
    k_ref[...] = k
    kb_ref[...] = k.astype(BF16)
    v = proj(_PG_V)
    v_ref[...] = v
    vb_ref[...] = v.astype(BF16)
    for s, slab in enumerate(rope(proj(_PG_IQ))):
        iq_ref[:, s * LANES:(s + 1) * LANES] = (slab * (IDX_DIM ** -0.5)).astype(BF16)
    raw = proj(_PG_IKW)
    ikw = jnp.where(lane < IDX_DIM, rope_slab(raw), raw * (IDX_HEADS ** -0.5))
    ikw_ref[...] = ikw
    ikb_ref[...] = ikw.astype(BF16)
    gq_ref[...] = proj(_PG_GQ) * (G_KEY_DIM ** -0.5)
    gk_ref[...] = proj(_PG_GK)
    gv_ref[...] = proj(_PG_GV)
    z = _dot(proj(_PG_GF).astype(BF16), wfa_ref[...]) + bfa_ref[...]
    glog_ref[...] = (jnp.minimum(z, 0.0) - jnp.log1p(jnp.exp(-jnp.abs(z)))) * (1.0 / G_TAU)
    gr_ref[...] = proj(_PG_GR)
    gate_ref[...] = proj(_PG_GATE)


def _proj(x2, pos, w_in, w_fa, b_fa, tm):
    n, d = x2.shape
    t = pos.shape[0]
    widths, offs = _pack_layout(d)
    wp = _pack_w_in(w_in)
    wfa = jnp.concatenate([w_fa, jnp.zeros((LANES - G_LOWRANK, W_G), w_fa.dtype)], 0).astype(BF16)
    cos, sin = _rope_tables(pos)
    if tm > t:
        cos, sin = jnp.tile(cos, (tm // t, 1)), jnp.tile(sin, (tm // t, 1))
    nper = cos.shape[0] // tm
    row = lambda w: pl.BlockSpec((tm, w), lambda i: (i, 0))
    const = lambda a: pl.BlockSpec(a.shape, lambda i: (0, 0))
    tab = pl.BlockSpec((tm, LANES), lambda i: (i % nper, 0))
    outs = [(W_AQ, BF16), (LANES, F32), (LANES, F32), (LANES, BF16), (LANES, BF16), (W_IQ, BF16),
            (LANES, F32), (LANES, BF16), (W_G, F32), (W_G, F32), (W_G, F32), (W_G, F32), (W_G, F32),
            (2 * d, F32)]
    bfa = b_fa.reshape(1, W_G)
    return pl.pallas_call(
        functools.partial(_proj_kernel, offs=offs, widths=widths),
        grid=(n // tm,),
        in_specs=[row(d), const(wp), const(wfa), const(bfa), tab, tab],
        out_specs=[row(w) for w, _ in outs],
        out_shape=[jax.ShapeDtypeStruct((n, w), dt) for w, dt in outs],
        compiler_params=_params(("parallel",)),
        name="proj",
    )(x2, wp, wfa, bfa, cos, sin)


def _attn_kernel(nkt_ref, aq_ref, iq_ref, ikw_ref, lim_ref, kb_ref, vt_ref, ikb_ref, o_ref,
                 keys_ref, qs_ref, iqs_ref, acc_ref, *, kt_w, topk, idx_bits):
    qb = Q_BLOCK
    j = pl.program_id(1)
    nkt = nkt_ref[j]

    for h in range(A_HEADS):
        qs_ref[h // A_GROUP, (h % A_GROUP) * qb:(h % A_GROUP + 1) * qb, :] = \
            aq_ref[:, h * A_HEAD_DIM:(h + 1) * A_HEAD_DIM]
    for h in range(IDX_HEADS):
        iqs_ref[h * qb:(h + 1) * qb, :] = iq_ref[:, h * IDX_DIM:(h + 1) * IDX_DIM]

    ikw_t = ikw_ref[...].T
    iw_rows = [ikw_t[IDX_DIM + h:IDX_DIM + h + 1, :] for h in range(IDX_HEADS)]
    lim = lim_ref[0, 0:1, :]
    sub = lax.broadcasted_iota(I32, (kt_w, qb), 0)

    def score_tile(kt, _):
        base = pl.multiple_of(kt * kt_w, kt_w)
        ik_t = ikb_ref[0, pl.ds(base, kt_w), :][:, :IDX_DIM]
        s = jnp.maximum(_dot_nt(ik_t, iqs_ref[...]), 0.0)
        score = jnp.zeros((kt_w, qb), F32)
        for h in range(IDX_HEADS):
            score = score + s[:, h * qb:(h + 1) * qb] * iw_rows[h]
        keys_ref[pl.ds(base, kt_w), :] = jnp.where(sub + base < lim, _sort_key(score), NEG_INF_KEY)
        return 0

    lax.fori_loop(0, nkt, score_tile, 0)

    ct_w = COUNT_TILES * kt_w
    nct = (nkt + COUNT_TILES - 1) // COUNT_TILES

    def fill_tile(kt, _):
        keys_ref[pl.ds(pl.multiple_of(kt * kt_w, kt_w), kt_w), :] = jnp.full((kt_w, qb), NEG_INF_KEY, I32)
        return 0

    lax.fori_loop(nkt, nct * COUNT_TILES, fill_tile, 0)
    sub_c = lax.broadcasted_iota(I32, (ct_w, qb), 0)

    def count(pred):
        def body(ct, acc):
            base = pl.multiple_of(ct * ct_w, ct_w)
            hit = pred(keys_ref[pl.ds(base, ct_w), :], sub_c + base)
            return acc + jnp.sum(hit.reshape(ct_w // SUBLANES, SUBLANES, qb), axis=0)
        acc = lax.fori_loop(0, nct, body, jnp.zeros((SUBLANES, qb), F32))
        return jnp.sum(acc, axis=0, keepdims=True)

    def count_ge(t_row):
        return count(lambda kk, idx: jnp.where(kk >= t_row, 1.0, 0.0))

    kf = float(topk)
    thr = jnp.where(count_ge(jnp.zeros((1, qb), I32)) >= kf, 0, INT_MIN).astype(I32)

    def thr_bit(i, t):
        cand = t + jnp.left_shift(jnp.int32(1), 30 - i)
        return jnp.where(count_ge(cand) >= kf, cand, t)

    thr = lax.fori_loop(0, 31, thr_bit, thr)
    n_gt = count_ge(thr + 1)
    n_eq = count_ge(thr) - n_gt
    need = kf - n_gt
    finite = thr > NEG_INF_KEY
    excess = jnp.where(finite, jnp.where(n_eq > need, 1.0, 0.0), 0.0)

    def count_eq_below(j_row):
        return count(lambda kk, idx: jnp.where(kk == thr, jnp.where(idx < j_row, 1.0, 0.0), 0.0))

    def resolve_ties():
        def bit(i, jc):
            cand = jc + jnp.left_shift(jnp.int32(1), idx_bits - 1 - i)
            return jnp.where(count_eq_below(cand) <= need - 1.0, cand, jc)
        jc = lax.fori_loop(0, idx_bits, bit, jnp.zeros((1, qb), I32))
        return jnp.where(finite, jc, -1)

    cut = lax.cond(jnp.max(excess) > 0.0, resolve_ties,
                   lambda: jnp.where(finite, 2 ** 30, -1).astype(I32))

    acc_ref[...] = jnp.zeros(acc_ref.shape, F32)
    gq = A_GROUP * qb

    def attend(kt, carry):
        base = pl.multiple_of(kt * kt_w, kt_w)
        k_t = kb_ref[0, pl.ds(base, kt_w), :]
        kk = keys_ref[pl.ds(base, kt_w), :]
        tie = jnp.where(sub + base <= cut, 0.0, NEG_BIG)
        bias = jnp.where(kk > thr, 0.0, jnp.where(kk == thr, tie, NEG_BIG))
        bias = jnp.concatenate([bias] * A_GROUP, axis=1)
        out = []
        for n in range(A_KV_HEADS):
            m_old, l_old = carry[2 * n], carry[2 * n + 1]
            logits = _dot_nt(k_t[:, n * A_HEAD_DIM:(n + 1) * A_HEAD_DIM], qs_ref[n]) + bias
            m_new = jnp.maximum(m_old, jnp.max(logits, axis=0, keepdims=True))
            alpha = jnp.exp(m_old - m_new)
            p = jnp.exp(logits - m_new)
            l_new = alpha * l_old + jnp.sum(p, axis=0, keepdims=True)
            v_t = vt_ref[0, n * A_HEAD_DIM:(n + 1) * A_HEAD_DIM, pl.ds(base, kt_w)]
            acc_ref[n] = alpha * acc_ref[n] + _dot(v_t, p.astype(BF16))
            out += [m_new, l_new]
        return tuple(out)

    init = (jnp.full((1, gq), NEG_BIG, F32), jnp.zeros((1, gq), F32)) * A_KV_HEADS
    fin = lax.fori_loop(0, nkt, attend, init)

    o_t = jnp.concatenate([acc_ref[n] / fin[2 * n + 1] for n in range(A_KV_HEADS)], axis=0)
    for n in range(A_KV_HEADS):
        for g in range(A_GROUP):
            h = n * A_GROUP + g
            blk = o_t[n * A_HEAD_DIM:(n + 1) * A_HEAD_DIM, g * qb:(g + 1) * qb]
            o_ref[:, h * A_HEAD_DIM:(h + 1) * A_HEAD_DIM] = blk.T.astype(BF16)


def _attention(aq, iq, ikw, limits, kb, vb, ikb, b, topk):
    qb, kt_w = Q_BLOCK, KEY_TILE
    tq = aq.shape[0] // b
    tq_pad = -(-tq // qb) * qb
    limits = np.asarray(limits)
    if tq_pad != tq:
        padq = lambda a: jnp.pad(a.reshape(b, tq, -1), ((0, 0), (0, tq_pad - tq), (0, 0))).reshape(b * tq_pad, -1)
        aq, iq, ikw = padq(aq), padq(iq), padq(ikw)
        limits = np.concatenate([limits, np.full((tq_pad - tq,), limits[-1])])
    l_all = kb.shape[1]
    l_pad = -(-l_all // kt_w) * kt_w
    if l_pad != l_all:
        padl = lambda a: jnp.pad(a, ((0, 0), (0, l_pad - l_all), (0, 0)))
        kb, vb, ikb = padl(kb), padl(vb), padl(ikb)
    vt = jnp.swapaxes(vb, 1, 2)
    nq = tq_pad // qb
    lim_blk = limits.reshape(nq, qb)
    nkt = jnp.asarray(np.minimum(-(-lim_blk.max(axis=1) // kt_w), l_pad // kt_w).astype(np.int32))
    lim = jnp.asarray(np.broadcast_to(lim_blk[:, None, :], (nq, SUBLANES, qb)).astype(np.int32))
    idx_bits = max(1, int(math.ceil(math.log2(l_pad))))
    n = b * tq_pad
    qrow = lambda w: pl.BlockSpec((qb, w), lambda bi, j, s: (bi * nq + j, 0))
    keys = lambda a: pl.BlockSpec((1,) + a.shape[1:], lambda bi, j, s: (bi, 0, 0))
    gq = A_GROUP * qb
    grid_spec = pltpu.PrefetchScalarGridSpec(
        num_scalar_prefetch=1,
        grid=(b, nq),
        in_specs=[qrow(W_AQ), qrow(W_IQ), qrow(LANES),
                  pl.BlockSpec((1, SUBLANES, qb), lambda bi, j, s: (j, 0, 0)),
                  keys(kb), keys(vt), keys(ikb)],
        out_specs=qrow(W_AQ),
        scratch_shapes=[pltpu.VMEM((-(-l_pad // (COUNT_TILES * kt_w)) * COUNT_TILES * kt_w, qb), I32),
                        pltpu.VMEM((A_KV_HEADS, gq, A_HEAD_DIM), BF16),
                        pltpu.VMEM((IDX_HEADS * qb, IDX_DIM), BF16),
                        pltpu.VMEM((A_KV_HEADS, A_HEAD_DIM, gq), F32)],
    )
    o = pl.pallas_call(
        functools.partial(_attn_kernel, kt_w=kt_w, topk=topk, idx_bits=idx_bits),
        grid_spec=grid_spec,
        out_shape=jax.ShapeDtypeStruct((n, W_AQ), BF16),
        compiler_params=_params(("parallel", "arbitrary")),
        name="attn",
    )(nkt, aq, iq, ikw, lim, kb, vt, ikb)
    if tq_pad != tq:
        o = o.reshape(b, tq_pad, W_AQ)[:, :tq].reshape(b * tq, W_AQ)
    return o


def _gla_constants(c):
    nlev = int(math.log2(c))
    t = np.arange(c)
    mats = [(t[None, :] <= t[:, None])]
    masks = [np.eye(c, dtype=bool)]
    for lev in range(nlev):
        m = c >> (lev + 1)
        ref_row = (t // (2 * m)) * 2 * m + m
        mats.append(t[None, :] <= ref_row[:, None])
        upper = (t & m) != 0
        same = (t[:, None] // (2 * m)) == (t[None, :] // (2 * m))
        masks.append(same & upper[:, None] & ~upper[None, :])
    return (jnp.asarray(np.concatenate(mats, 0).astype(np.float32), BF16),
            jnp.asarray(np.stack(masks).astype(np.float32)), nlev)


def _gla_kernel(q_ref, k_ref, v_ref, g_ref, gr_ref, gn_ref, mst_ref, msk_ref, s0_ref,
                ob_ref, sfin_ref, st_ref, *, c, nlev):
    i = pl.program_id(1)
    hk = G_KEY_DIM

    @pl.when(i == 0)
    def _():
        for h in range(G_HEADS):
            st_ref[h] = s0_ref[0, h].T

    g = g_ref[...]
    g_hi = g.astype(BF16)
    r1 = g - g_hi.astype(F32)
    g_mid = r1.astype(BF16)
    g_lo = (r1 - g_mid.astype(F32)).astype(BF16)
    mst = mst_ref[...]
    bs = _dot(mst, g_hi) + _dot(mst, g_mid) + _dot(mst, g_lo)
    b = bs[0:c]
    q = q_ref[...]
    k = k_ref[...]
    vb = v_ref[...].astype(BF16)
    row = lax.broadcasted_iota(I32, q.shape, 0)
    hs = lambda a, h: a[:, h * hk:(h + 1) * hk]

    qb = q.astype(BF16)
    kb = k.astype(BF16)
    attn = [_dot_nt(hs(qb, h), hs(kb, h)) * msk_ref[0] for h in range(G_HEADS)]
    for lev in range(nlev):
        m = c >> (lev + 1)
        upper = (row & m) != 0
        d = b - bs[(lev + 1) * c:(lev + 2) * c]
        e = jnp.exp(jnp.where(upper, d, -d))
        qt = jnp.where(upper, q * e, 0.0).astype(BF16)
        kt = jnp.where(upper, 0.0, k * e).astype(BF16)
        mk = msk_ref[lev + 1]
        for h in range(G_HEADS):
            attn[h] = attn[h] + _dot_nt(hs(qt, h), hs(kt, h)) * mk

    qe = (q * jnp.exp(b)).astype(BF16)
    b_last = b[c - 1:c, :]
    khat = (k * jnp.exp(b_last - b)).astype(BF16)
    dec = jnp.exp(b_last)
    gr = gr_ref[...]
    gn = gn_ref[...]
    for h in range(G_HEADS):
        st = st_ref[h]
        o = _dot_nt(hs(qe, h), st.astype(BF16)) + _dot(attn[h].astype(BF16), hs(vb, h))
        st_ref[h] = st * hs(dec, h) + _dot_tn(hs(vb, h), hs(khat, h))
        ms = jnp.mean(o * o, axis=1, keepdims=True)
        grh = hs(gr, h)
        of = o * lax.rsqrt(ms + LN_EPS) * hs(gn, h) * (grh / (1.0 + jnp.exp(-grh)))
        ob_ref[:, h * hk:(h + 1) * hk] = of.astype(BF16)

    @pl.when(i == pl.num_programs(1) - 1)
    def _():
        for h in range(G_HEADS):
            sfin_ref[0, h] = st_ref[h].T


def _gla(gq, gk, gv, glog, gr, g_norm, s0, b, c):
    n = gq.shape[0]
    nc = n // (b * c)
    mst, msk, nlev = _gla_constants(c)
    row = pl.BlockSpec((c, W_G), lambda bi, i: (bi * nc + i, 0))
    const = lambda a: pl.BlockSpec(a.shape, lambda bi, i: (0,) * a.ndim)
    st_spec = pl.BlockSpec((1, G_HEADS, G_KEY_DIM, G_VAL_DIM), lambda bi, i: (bi, 0, 0, 0))
    gn = g_norm.reshape(1, W_G)
    return pl.pallas_call(
        functools.partial(_gla_kernel, c=c, nlev=nlev),
        grid=(b, nc),
        in_specs=[row, row, row, row, row, const(gn), const(mst), const(msk), st_spec],
        out_specs=[row, st_spec],
        out_shape=[jax.ShapeDtypeStruct((n, W_G), BF16),
                   jax.ShapeDtypeStruct((b, G_HEADS, G_KEY_DIM, G_VAL_DIM), F32)],
        scratch_shapes=[pltpu.VMEM((G_HEADS, G_VAL_DIM, G_KEY_DIM), F32)],
        compiler_params=_params(("parallel", "arbitrary")),
        name="gla",
    )(gq, gk, gv, glog, gr, gn, mst, msk, s0)


def _layer_norm(z, g, b):
    mu = jnp.mean(z, axis=1, keepdims=True)
    zc = z - mu
    var = jnp.mean(zc * zc, axis=1, keepdims=True)
    return zc * lax.rsqrt(var + LN_EPS) * g + b


def _sigmoid(x):
    return 1.0 / (1.0 + jnp.exp(-x))


def _merge_kernel(x_ref, oa_ref, ob_ref, gate_ref, wpa_ref, wpb_ref, wout_ref, g1_ref, b1_ref,
                  wpq_ref, pk1_ref, pk2_ref, h1_ref, h1b_ref, s1t_ref, s2t_ref, *, alpha, d):
    ya = _dot(oa_ref[...], wpa_ref[...])
    yb = _dot(ob_ref[...], wpb_ref[...])
    m = _sigmoid(gate_ref[:, :d]) * ya + _sigmoid(gate_ref[:, d:]) * yb
    mix = _dot(m.astype(BF16), wout_ref[...])
    h1 = _layer_norm(alpha * x_ref[...] + mix, g1_ref[...], b1_ref[...])
    h1_ref[...] = h1
    h1b = h1.astype(BF16)
    h1b_ref[...] = h1b
    qp = _dot(h1b, wpq_ref[...])
    for h in range(P_HEADS):
        for half, (pk_ref, st_ref) in enumerate(((pk1_ref, s1t_ref), (pk2_ref, s2t_ref))):
            c0 = (2 * h + half) * P_HALF
            st_ref[h * P_NKEYS:(h + 1) * P_NKEYS, :] = _dot_nt(pk_ref[h], qp[:, c0:c0 + P_HALF].astype(BF16))


def _merge(x2, oa, ob, gate, w_pa, w_pb, w_out, ln_g, ln_b, w_pq, pk1, pk2, alpha, tm):
    n, d = x2.shape
    hk = P_HEADS * P_NKEYS
    row = lambda w: pl.BlockSpec((tm, w), lambda i: (i, 0))
    col = pl.BlockSpec((hk, tm), lambda i: (0, i))
    const = lambda a: pl.BlockSpec(a.shape, lambda i: (0,) * a.ndim)
    ws = [w_pa.astype(BF16), w_pb.astype(BF16), w_out.astype(BF16), ln_g.reshape(1, d), ln_b.reshape(1, d),
          w_pq.astype(BF16), pk1.astype(BF16), pk2.astype(BF16)]
    return pl.pallas_call(
        functools.partial(_merge_kernel, alpha=alpha, d=d),
        grid=(n // tm,),
        in_specs=[row(d), row(W_AQ), row(W_G), row(2 * d)] + [const(w) for w in ws],
        out_specs=[row(d), row(d), col, col],
        out_shape=[jax.ShapeDtypeStruct((n, d), F32), jax.ShapeDtypeStruct((n, d), BF16),
                   jax.ShapeDtypeStruct((hk, n), F32), jax.ShapeDtypeStruct((hk, n), F32)],
        compiler_params=_params(("parallel",)),
        name="merge",
    )(x2, oa, ob, gate, *ws)


def _top_desc(s, count):
    tops = []
    for r in range(count):
        m = jnp.max(s, axis=0, keepdims=True)
        tops.append(m)
        if r + 1 < count:
            s = jnp.where(s == m, -jnp.inf, s)
    return tops


def _select_kernel(s1t_ref, s2t_ref, th_ref, e1_ref, e2_ref):
    nk = P_NKEYS

    def head(h, _):
        r0 = pl.multiple_of(h * nk, nk)
        rows = pl.ds(r0, nk)
        s1 = s1t_ref[rows, :]
        s2 = s2t_ref[rows, :]
        v1 = _top_desc(s1, P_TOPK)
        v2 = _top_desc(s2, P_TOPK)
        pairs = [(a, b) for a in range(P_TOPK) for b in range(P_TOPK // (a + 1))]
        fill = [jnp.full_like(v1[0], -jnp.inf)] * (-len(pairs) % SUBLANES)
        cand = jnp.concatenate([v1[a] + v2[b] for a, b in pairs] + fill, axis=0)
        work, seen = cand, jnp.zeros_like(v1[0])
        tau = jnp.full_like(v1[0], -jnp.inf)
        for _ in range(P_TOPK):
            m = jnp.max(work, axis=0, keepdims=True)
            hit = work == m
            seen = seen + jnp.sum(jnp.where(hit, 1.0, 0.0), axis=0, keepdims=True)
            tau = jnp.maximum(tau, jnp.where(seen >= float(P_TOPK), m, -jnp.inf))
            work = jnp.where(hit, -jnp.inf, work)
        cmax = v1[0] + v2[0]
        zsum = jnp.sum(jnp.where(cand >= tau, jnp.exp(cand - cmax), 0.0), axis=0, keepdims=True)
        v2all = jnp.concatenate(v2, axis=0)
        th = jnp.full(s1.shape, jnp.inf, F32)
        for a in range(P_TOPK):
            th_a = jnp.min(jnp.where(v1[a] + v2all >= tau, v2all, jnp.inf), axis=0, keepdims=True)
            th = jnp.where(s1 == v1[a], th_a, th)
        th_ref[rows, :] = th
        e1_ref[rows, :] = jnp.exp(s1 - v1[0]) / zsum
        e2_ref[rows, :] = jnp.exp(s2 - v2[0])
        return 0

    lax.fori_loop(0, P_HEADS, head, 0)


def _select(s1t, s2t, tn):
    hk, n = s1t.shape
    col = pl.BlockSpec((hk, tn), lambda i: (0, i))
    return pl.pallas_call(
        _select_kernel,
        grid=(n // tn,),
        in_specs=[col, col],
        out_specs=[col, col, col],
        out_shape=[jax.ShapeDtypeStruct((hk, n), F32)] * 3,
        compiler_params=_params(("parallel",)),
        name="select",
    )(s1t, s2t)


I1_PER_STEP = 8
E_PER_STEP = I1_PER_STEP * P_NKEYS
I2_BLOCK = 16


def _gelu(x):
    return 0.5 * x * (1.0 + lax.erf(x * (2.0 ** -0.5)))


def _peer_kernel(hb_ref, pu_ref, pvt_ref, th_ref, e1_ref, s2_ref, e2_ref, h1_ref, g2_ref, b2_ref,
                 y_ref, a_ref, gw_ref, acc_ref, *, tn, alpha):
    g = pl.program_id(1)

    @pl.when(g == 0)
    def _():
        acc_ref[...] = jnp.zeros(acc_ref.shape, F32)

    a_ref[...] = _dot_nt(pu_ref[...], hb_ref[...])

    assert I1_PER_STEP == SUBLANES
    for lt in range(tn // LANES):
        ls = slice(lt * LANES, (lt + 1) * LANES)
        grp = lambda ref, h: ref[pl.ds(pl.multiple_of(h * P_NKEYS + g * I1_PER_STEP, SUBLANES), SUBLANES), ls]
        th8 = [grp(th_ref, h) for h in range(P_HEADS)]
        e18 = [grp(e1_ref, h) for h in range(P_HEADS)]
        for j in range(I1_PER_STEP):
            th = [jnp.broadcast_to(th8[h][j:j + 1, :], (I2_BLOCK, LANES)) for h in range(P_HEADS)]
            e1 = [jnp.broadcast_to(e18[h][j:j + 1, :], (I2_BLOCK, LANES)) for h in range(P_HEADS)]
            for i2b in range(P_NKEYS // I2_BLOCK):
                w = jnp.zeros((I2_BLOCK, LANES), F32)
                for h in range(P_HEADS):
                    rows = slice(h * P_NKEYS + i2b * I2_BLOCK, h * P_NKEYS + (i2b + 1) * I2_BLOCK)
                    w = w + jnp.where(s2_ref[rows, ls] >= th[h], e2_ref[rows, ls] * e1[h], 0.0)
                arow = slice(j * P_NKEYS + i2b * I2_BLOCK, j * P_NKEYS + (i2b + 1) * I2_BLOCK)
                gw_ref[arow, ls] = (w * _gelu(a_ref[arow, ls])).astype(BF16)

    acc_ref[...] += _dot(pvt_ref[...], gw_ref[...])

    @pl.when(g == pl.num_programs(1) - 1)
    def _():
        y_ref[...] = _layer_norm(alpha * h1_ref[...] + acc_ref[...].T, g2_ref[...], b2_ref[...])


def _peer(h1, h1b, th, e1, s2t, e2, pu, pv, ln_g, ln_b, alpha, tn):
    n, d = h1.shape
    hk = P_HEADS * P_NKEYS
    ng = pu.shape[0] // E_PER_STEP
    row = lambda w: pl.BlockSpec((tn, w), lambda i, g: (i, 0))
    col = pl.BlockSpec((hk, tn), lambda i, g: (0, i))
    const = pl.BlockSpec((1, d), lambda i, g: (0, 0))
    return pl.pallas_call(
        functools.partial(_peer_kernel, tn=tn, alpha=alpha),
        grid=(n // tn, ng),
        in_specs=[row(d), pl.BlockSpec((E_PER_STEP, d), lambda i, g: (g, 0)),
                  pl.BlockSpec((d, E_PER_STEP), lambda i, g: (0, g)),
                  col, col, col, col, row(d), const, const],
        out_specs=row(d),
        out_shape=jax.ShapeDtypeStruct((n, d), F32),
        scratch_shapes=[pltpu.VMEM((E_PER_STEP, tn), F32), pltpu.VMEM((E_PER_STEP, tn), BF16),
                        pltpu.VMEM((d, tn), F32)],
        compiler_params=_params(("parallel", "arbitrary")),
        name="peer",
    )(h1b, pu.astype(BF16), pv.astype(BF16).T, th, e1, s2t, e2, h1, ln_g.reshape(1, d), ln_b.reshape(1, d))


def _pick_tile(n, pref):
    t = min(n, pref)
    assert n % t == 0
    return t


def _layer(x, pos, limits, past, s0, w, *, chunk, alpha):
    b, t, d = x.shape
    n = b * t
    x2 = x.reshape(n, d)
    tm = _pick_tile(n, TOKEN_TILE)
    (aq, k32, v32, kb, vb, iq, ikw, ikb, gq, gk, gv, glog, gr, gate) = _proj(
        x2, pos, w["w_in"], w["w_fa"], w["b_fa"], tm)

    kb3, vb3, ikb3 = kb.reshape(b, t, LANES), vb.reshape(b, t, LANES), ikb.reshape(b, t, LANES)
    if past is not None:
        ck, cv, cik = past
        p = ck.shape[1]
        kb3 = jnp.concatenate([ck.reshape(b, p, LANES).astype(BF16), kb3], axis=1)
        vb3 = jnp.concatenate([cv.reshape(b, p, LANES).astype(BF16), vb3], axis=1)
        ikb3 = jnp.concatenate([cik.astype(BF16), ikb3[:, :, :IDX_DIM]], axis=1)
    topk = min(IDX_TOPK, kb3.shape[1] // 4)
    o_a = _attention(aq, iq, ikw, limits, kb3, vb3, ikb3, b, topk)

    o_b, s_fin = _gla(gq, gk, gv, glog, gr, w["g_gla_norm"], s0, b, chunk)

    h1, h1b, s1t, s2t = _merge(x2, o_a, o_b, gate, w["w_pa"], w["w_pb"], w["w_out"],
                               w["ln1_g"], w["ln1_b"], w["w_pq"], w["pk1"], w["pk2"], alpha, tm)
    th, e1, e2 = _select(s1t, s2t, tm)
    y = _peer(h1, h1b, th, e1, s2t, e2, w["pu"], w["pv"], w["ln2_g"], w["ln2_b"], alpha,
              _pick_tile(n, PEER_TOKEN_TILE))

    k_out = k32.reshape(b, t, A_KV_HEADS, A_HEAD_DIM)
    v_out = v32.reshape(b, t, A_KV_HEADS, A_HEAD_DIM)
    ik_out = ikw[:, :IDX_DIM].reshape(b, t, IDX_DIM)
    return y.reshape(b, t, d), k_out, v_out, ik_out, s_fin


def kernel(x_prompt, x_sample, cache_k, cache_v, cache_idx_k, state_gla, w_in, w_fa, b_fa, g_gla_norm,
           w_pa, w_pb, w_out, ln1_g, ln1_b, w_pq, pk1, pk2, pu, pv, ln2_g, ln2_b):
    depth = w_in.shape[0]
    alpha = (2.0 * depth) ** 0.25
    bp, tp, _ = x_prompt.shape
    bs, ts, _ = x_sample.shape
    past_len = cache_k.shape[2]
    pos_p = jnp.arange(tp)
    pos_s = past_len + jnp.arange(ts)
    lim_p = (np.arange(tp) // CHUNK + 1) * CHUNK
    lim_s = np.full((ts,), past_len + ts)
    names = ("w_in", "w_fa", "b_fa", "g_gla_norm", "w_pa", "w_pb", "w_out", "ln1_g", "ln1_b",
             "w_pq", "pk1", "pk2", "pu", "pv", "ln2_g", "ln2_b")
    stacked = (w_in, w_fa, b_fa, g_gla_norm, w_pa, w_pb, w_out, ln1_g, ln1_b, w_pq, pk1, pk2, pu, pv, ln2_g, ln2_b)
    hp, hs = x_prompt, x_sample
    outs_p, outs_s = [], []
    for l in range(depth):
        w = {nm: a[l] for nm, a in zip(names, stacked)}
        s0 = jnp.zeros((bp, G_HEADS, G_KEY_DIM, G_VAL_DIM), F32)
        hp, *rest = _layer(hp, pos_p, lim_p, None, s0, w, chunk=CHUNK, alpha=alpha)
        outs_p.append(rest)
        hs, *rest = _layer(hs, pos_s, lim_s, (cache_k[l], cache_v[l], cache_idx_k[l]), state_gla[l], w,
                           chunk=ts, alpha=alpha)
        outs_s.append(rest)
    stack = lambda outs, i: jnp.stack([o[i] for o in outs])
    return (hp, hs, stack(outs_p, 0), stack(outs_p, 1), stack(outs_p, 2), stack(outs_p, 3),
            stack(outs_s, 0), stack(outs_s, 1), stack(outs_s, 2), stack(outs_s, 3))
```

```python
import functools
import math

import numpy as np
import jax
import jax.numpy as jnp
from jax import lax
from jax.experimental import pallas as pl
from jax.experimental.pallas import tpu as pltpu

F32 = jnp.float32
BF16 = jnp.bfloat16
I32 = jnp.int32

LANES = 128
SUBLANES = 8
VMEM_LIMIT = 56 << 20

CHUNK = 64
A_HEADS = 8
A_KV_HEADS = 2
A_HEAD_DIM = 64
A_GROUP = A_HEADS // A_KV_HEADS
IDX_HEADS = 4
IDX_DIM = 64
IDX_TOPK = 256
ROPE_THETA = 10000.0
G_HEADS = 4
G_KEY_DIM = 128
G_VAL_DIM = 128
G_LOWRANK = 16
G_TAU = 16.0
P_HEADS = 8
P_NKEYS = 128
P_HALF = 128
P_TOPK = 16
LN_EPS = 1e-5

W_AQ = A_HEADS * A_HEAD_DIM
W_AK = A_KV_HEADS * A_HEAD_DIM
W_IQ = IDX_HEADS * IDX_DIM
W_G = G_HEADS * G_KEY_DIM
IN_SIZES = (W_AQ, W_AK, W_AK, W_IQ, IDX_DIM, IDX_HEADS, W_G, W_G, W_G, G_LOWRANK, W_G, None)

TOKEN_TILE = 256
PEER_TOKEN_TILE = 512
Q_BLOCK = 128
KEY_TILE = 512
COUNT_TILES = 1

INT_MIN = -(2 ** 31)
NEG_INF_KEY = -2139095041
NEG_BIG = -1e30


def _dot(a, b):
    return jnp.dot(a, b, preferred_element_type=F32)


def _dot_nt(a, b):
    return lax.dot_general(a, b, (((1,), (1,)), ((), ())), preferred_element_type=F32)


def _dot_tn(a, b):
    return lax.dot_general(a, b, (((0,), (0,)), ((), ())), preferred_element_type=F32)


def _sort_key(x):
    bits = pltpu.bitcast(x, I32)
    key = jnp.where(bits < 0, bits ^ 0x7FFFFFFF, bits)
    return jnp.where(key == -1, 0, key)


def _fold_rows(x, op):
    x = x.reshape(x.shape[0] // SUBLANES, SUBLANES, x.shape[1])
    while x.shape[0] > 1:
        half = x.shape[0] // 2
        folded = op(x[:half], x[half:2 * half])
        x = folded if x.shape[0] == 2 * half else jnp.concatenate([folded, x[2 * half:]], axis=0)
    return x[0]


def _params(sem):
    return pltpu.CompilerParams(dimension_semantics=sem, vmem_limit_bytes=VMEM_LIMIT)


_PG_AQ, _PG_K, _PG_V, _PG_IQ, _PG_IKW, _PG_GQ, _PG_GK, _PG_GV, _PG_GF, _PG_GR, _PG_GATE = range(11)


def _pack_layout(d_model):
    widths = [W_AQ, W_AK, W_AK, W_IQ, LANES, W_G, W_G, W_G, LANES, W_G, 2 * d_model]
    offs = np.concatenate([[0], np.cumsum(widths)]).tolist()
    return widths, offs


def _pack_w_in(w_in):
    d = w_in.shape[0]
    sizes = list(IN_SIZES[:-1]) + [2 * d]
    cuts = np.cumsum(sizes)[:-1].tolist()
    aq, ak, av, iq, ik, iw, gq, gk, gv, gf, gr, gate = jnp.split(w_in, cuts, axis=1)
    z = lambda n: jnp.zeros((d, n), w_in.dtype)
    ikw = jnp.concatenate([ik, iw, z(LANES - IDX_DIM - IDX_HEADS)], axis=1)
    gfp = jnp.concatenate([gf, z(LANES - G_LOWRANK)], axis=1)
    return jnp.concatenate([aq, ak, av, iq, ikw, gq, gk, gv, gfp, gr, gate], axis=1).astype(BF16)


def _rope_tables(pos):
    half = A_HEAD_DIM // 2
    inv = ROPE_THETA ** (-jnp.arange(half, dtype=F32) / half)
    ang = pos.astype(F32)[:, None] * inv[None, :]
    c, s = jnp.cos(ang), jnp.sin(ang)
    return jnp.concatenate([c, c, c, c], -1), jnp.concatenate([-s, s, -s, s], -1)


def _proj_kernel(x_ref, w_ref, wfa_ref, bfa_ref, cos_ref, sin_ref,
                 aq_ref, k_ref, v_ref, kb_ref, vb_ref, iq_ref, ikw_ref, ikb_ref,
                 gq_ref, gk_ref, gv_ref, glog_ref, gr_ref, gate_ref, *, offs, widths):
    xb = x_ref[...].astype(BF16)
    cos = cos_ref[...]
    sin = sin_ref[...]
    lane = lax.broadcasted_iota(I32, cos.shape, 1)
    first_half = (lane & (A_HEAD_DIM // 2)) == 0

    def proj(g):
        return _dot(xb, w_ref[:, offs[g]:offs[g] + widths[g]])

    def rope_slab(y):
        fwd = pltpu.roll(y, LANES - A_HEAD_DIM // 2, 1)
        bwd = pltpu.roll(y, A_HEAD_DIM // 2, 1)
        return y * cos + jnp.where(first_half, fwd, bwd) * sin

    def rope(y):
        return [rope_slab(y[:, s * LANES:(s + 1) * LANES]) for s in range(y.shape[1] // LANES)]

    for s, slab in enumerate(rope(proj(_PG_AQ))):
        aq_ref[:, s * LANES:(s + 1) * LANES] = (slab * (A_HEAD_DIM ** -0.5)).astype(BF16)
    k = rope(proj(_PG_K))[0]
    k_ref[...] = k
    kb_ref[...] = k.astype(BF16)
    v = proj(_PG_V)
    v_ref[...] = v
    vb_ref[...] = v.astype(BF16)
    for s, slab in enumerate(rope(proj(_PG_IQ))):
        iq_ref[:, s * LANES:(s + 1) * LANES] = (slab * (IDX_DIM ** -0.5)).astype(BF16)
    raw = proj(_PG_IKW)
    ikw = jnp.where(lane < IDX_DIM, rope_slab(raw), raw * (IDX_HEADS ** -0.5))
    ikw_ref[...] = ikw
    ikb_ref[...] = ikw.astype(BF16)
    gq_ref[...] = proj(_PG_GQ) * (G_KEY_DIM ** -0.5)
    gk_ref[...] = proj(_PG_GK)
    gv_ref[...] = proj(_PG_GV)
    z = _dot(proj(_PG_GF).astype(BF16), wfa_ref[...]) + bfa_ref[...]
    glog_ref[...] = (jnp.minimum(z, 0.0) - jnp.log1p(jnp.exp(-jnp.abs(z)))) * (1.0 / G_TAU)
    gr_ref[...] = proj(_PG_GR)
    gate_ref[...] = proj(_PG_GATE)


def _proj(x2, pos, w_in, w_fa, b_fa, tm):
    n, d = x2.shape
    t = pos.shape[0]
    widths, offs = _pack_layout(d)
    wp = _pack_w_in(w_in)
    wfa = jnp.concatenate([w_fa, jnp.zeros((LANES - G_LOWRANK, W_G), w_fa.dtype)], 0).astype(BF16)
    cos, sin = _rope_tables(pos)
    if tm > t:
        cos, sin = jnp.tile(cos, (tm // t, 1)), jnp.tile(sin, (tm // t, 1))
    nper = cos.shape[0] // tm
    row = lambda w: pl.BlockSpec((tm, w), lambda i: (i, 0))
    const = lambda a: pl.BlockSpec(a.shape, lambda i: (0, 0))
    tab = pl.BlockSpec((tm, LANES), lambda i: (i % nper, 0))
    outs = [(W_AQ, BF16), (LANES, F32), (LANES, F32), (LANES, BF16), (LANES, BF16), (W_IQ, BF16),
            (LANES, F32), (LANES, BF16), (W_G, F32), (W_G, F32), (W_G, F32), (W_G, F32), (W_G, F32),
            (2 * d, F32)]
    bfa = b_fa.reshape(1, W_G)
    return pl.pallas_call(
        functools.partial(_proj_kernel, offs=offs, widths=widths),
        grid=(n // tm,),
        in_specs=[row(d), const(wp), const(wfa), const(bfa), tab, tab],
        out_specs=[row(w) for w, _ in outs],
        out_shape=[jax.ShapeDtypeStruct((n, w), dt) for w, dt in outs],
        compiler_params=_params(("parallel",)),
        name="proj",
    )(x2, wp, wfa, bfa, cos, sin)


def _attn_kernel(nkt_ref, aq_ref, iq_ref, ikw_ref, lim_ref, kb_ref, vt_ref, ikb_ref, o_ref,
                 keys_ref, qs_ref, iqs_ref, acc_ref, *, kt_w, topk, idx_bits):
    qb = Q_BLOCK
    j = pl.program_id(1)
    nkt = nkt_ref[j]

    for h in range(A_HEADS):
        qs_ref[h // A_GROUP, (h % A_GROUP) * qb:(h % A_GROUP + 1) * qb, :] = \
            aq_ref[:, h * A_HEAD_DIM:(h + 1) * A_HEAD_DIM]
    for h in range(IDX_HEADS):
        iqs_ref[h * qb:(h + 1) * qb, :] = iq_ref[:, h * IDX_DIM:(h + 1) * IDX_DIM]

    ikw_t = ikw_ref[...].T
    iw_rows = [ikw_t[IDX_DIM + h:IDX_DIM + h + 1, :] for h in range(IDX_HEADS)]
    lim = lim_ref[0, 0:1, :]
    sub = lax.broadcasted_iota(I32, (kt_w, qb), 0)

    def score_tile(kt, _):
        base = pl.multiple_of(kt * kt_w, kt_w)
        ik_t = ikb_ref[0, pl.ds(base, kt_w), :][:, :IDX_DIM]
        s = jnp.maximum(_dot_nt(ik_t, iqs_ref[...]), 0.0)
        score = jnp.zeros((kt_w, qb), F32)
        for h in range(IDX_HEADS):
            score = score + s[:, h * qb:(h + 1) * qb] * iw_rows[h]
        keys_ref[pl.ds(base, kt_w), :] = jnp.where(sub + base < lim, _sort_key(score), NEG_INF_KEY)
        return 0

    lax.fori_loop(0, nkt, score_tile, 0)

    ct_w = COUNT_TILES * kt_w
    nct = (nkt + COUNT_TILES - 1) // COUNT_TILES

    def fill_tile(kt, _):
        keys_ref[pl.ds(pl.multiple_of(kt * kt_w, kt_w), kt_w), :] = jnp.full((kt_w, qb), NEG_INF_KEY, I32)
        return 0

    lax.fori_loop(nkt, nct * COUNT_TILES, fill_tile, 0)
    sub_c = lax.broadcasted_iota(I32, (ct_w, qb), 0)

    def count(pred):
        def body(ct, acc):
            base = pl.multiple_of(ct * ct_w, ct_w)
            hit = pred(keys_ref[pl.ds(base, ct_w), :], sub_c + base)
            return acc + _fold_rows(hit, jnp.add)
        acc = lax.fori_loop(0, nct, body, jnp.zeros((SUBLANES, qb), F32))
        return jnp.sum(acc, axis=0, keepdims=True)

    def count_ge(t_row):
        return count(lambda kk, idx: jnp.where(kk >= t_row, 1.0, 0.0))

    kf = float(topk)
    thr = jnp.where(count_ge(jnp.zeros((1, qb), I32)) >= kf, 0, INT_MIN).astype(I32)

    def thr_bit(i, t):
        cand = t + jnp.left_shift(jnp.int32(1), 30 - i)
        return jnp.where(count_ge(cand) >= kf, cand, t)

    thr = lax.fori_loop(0, 31, thr_bit, thr)
    n_gt = count_ge(thr + 1)
    n_eq = count_ge(thr) - n_gt
    need = kf - n_gt
    finite = thr > NEG_INF_KEY
    excess = jnp.where(finite, jnp.where(n_eq > need, 1.0, 0.0), 0.0)

    def count_eq_below(j_row):
        return count(lambda kk, idx: jnp.where(kk == thr, jnp.where(idx < j_row, 1.0, 0.0), 0.0))

    def resolve_ties():
        def bit(i, jc):
            cand = jc + jnp.left_shift(jnp.int32(1), idx_bits - 1 - i)
            return jnp.where(count_eq_below(cand) <= need - 1.0, cand, jc)
        jc = lax.fori_loop(0, idx_bits, bit, jnp.zeros((1, qb), I32))
        return jnp.where(finite, jc, -1)

    cut = lax.cond(jnp.max(excess) > 0.0, resolve_ties,
                   lambda: jnp.where(finite, 2 ** 30, -1).astype(I32))

    acc_ref[...] = jnp.zeros(acc_ref.shape, F32)
    gq = A_GROUP * qb

    def attend(kt, carry):
        base = pl.multiple_of(kt * kt_w, kt_w)
        k_t = kb_ref[0, pl.ds(base, kt_w), :]
        kk = keys_ref[pl.ds(base, kt_w), :]
        tie = jnp.where(sub + base <= cut, 0.0, NEG_BIG)
        bias = jnp.where(kk > thr, 0.0, jnp.where(kk == thr, tie, NEG_BIG))
        bias = jnp.concatenate([bias] * A_GROUP, axis=1)
        out = []
        for n in range(A_KV_HEADS):
            m_old, l_old = carry[2 * n], carry[2 * n + 1]
            logits = _dot_nt(k_t[:, n * A_HEAD_DIM:(n + 1) * A_HEAD_DIM], qs_ref[n]) + bias
            m_new = jnp.maximum(m_old, jnp.max(_fold_rows(logits, jnp.maximum), axis=0, keepdims=True))
            alpha = jnp.exp(m_old - m_new)
            p = jnp.exp(logits - m_new)
            l_new = alpha * l_old + jnp.sum(_fold_rows(p, jnp.add), axis=0, keepdims=True)
            v_t = vt_ref[0, n * A_HEAD_DIM:(n + 1) * A_HEAD_DIM, pl.ds(base, kt_w)]
            acc_ref[n] = alpha * acc_ref[n] + _dot(v_t, p.astype(BF16))
            out += [m_new, l_new]
        return tuple(out)

    init = (jnp.full((1, gq), NEG_BIG, F32), jnp.zeros((1, gq), F32)) * A_KV_HEADS
    fin = lax.fori_loop(0, nkt, attend, init)

    o_t = jnp.concatenate([acc_ref[n] / fin[2 * n + 1] for n in range(A_KV_HEADS)], axis=0)
    for n in range(A_KV_HEADS):
        for g in range(A_GROUP):
            h = n * A_GROUP + g
            blk = o_t[n * A_HEAD_DIM:(n + 1) * A_HEAD_DIM, g * qb:(g + 1) * qb]
            o_ref[:, h * A_HEAD_DIM:(h + 1) * A_HEAD_DIM] = blk.T.astype(BF16)


def _attention(aq, iq, ikw, limits, kb, vb, ikb, b, topk):
    qb, kt_w = Q_BLOCK, KEY_TILE
    tq = aq.shape[0] // b
    tq_pad = -(-tq // qb) * qb
    limits = np.asarray(limits)
    if tq_pad != tq:
        padq = lambda a: jnp.pad(a.reshape(b, tq, -1), ((0, 0), (0, tq_pad - tq), (0, 0))).reshape(b * tq_pad, -1)
        aq, iq, ikw = padq(aq), padq(iq), padq(ikw)
        limits = np.concatenate([limits, np.full((tq_pad - tq,), limits[-1])])
    l_all = kb.shape[1]
    l_pad = -(-l_all // kt_w) * kt_w
    if l_pad != l_all:
        padl = lambda a: jnp.pad(a, ((0, 0), (0, l_pad - l_all), (0, 0)))
        kb, vb, ikb = padl(kb), padl(vb), padl(ikb)
    vt = jnp.swapaxes(vb, 1, 2)
    nq = tq_pad // qb
    lim_blk = limits.reshape(nq, qb)
    nkt = jnp.asarray(np.minimum(-(-lim_blk.max(axis=1) // kt_w), l_pad // kt_w).astype(np.int32))
    lim = jnp.asarray(np.broadcast_to(lim_blk[:, None, :], (nq, SUBLANES, qb)).astype(np.int32))
    idx_bits = max(1, int(math.ceil(math.log2(l_pad))))
    n = b * tq_pad
    qrow = lambda w: pl.BlockSpec((qb, w), lambda bi, j, s: (bi * nq + j, 0))
    keys = lambda a: pl.BlockSpec((1,) + a.shape[1:], lambda bi, j, s: (bi, 0, 0))
    gq = A_GROUP * qb
    grid_spec = pltpu.PrefetchScalarGridSpec(
        num_scalar_prefetch=1,
        grid=(b, nq),
        in_specs=[qrow(W_AQ), qrow(W_IQ), qrow(LANES),
                  pl.BlockSpec((1, SUBLANES, qb), lambda bi, j, s: (j, 0, 0)),
                  keys(kb), keys(vt), keys(ikb)],
        out_specs=qrow(W_AQ),
        scratch_shapes=[pltpu.VMEM((-(-l_pad // (COUNT_TILES * kt_w)) * COUNT_TILES * kt_w, qb), I32),
                        pltpu.VMEM((A_KV_HEADS, gq, A_HEAD_DIM), BF16),
                        pltpu.VMEM((IDX_HEADS * qb, IDX_DIM), BF16),
                        pltpu.VMEM((A_KV_HEADS, A_HEAD_DIM, gq), F32)],
    )
    o = pl.pallas_call(
        functools.partial(_attn_kernel, kt_w=kt_w, topk=topk, idx_bits=idx_bits),
        grid_spec=grid_spec,
        out_shape=jax.ShapeDtypeStruct((n, W_AQ), BF16),
        compiler_params=_params(("parallel", "arbitrary")),
        name="attn",
    )(nkt, aq, iq, ikw, lim, kb, vt, ikb)
    if tq_pad != tq:
        o = o.reshape(b, tq_pad, W_AQ)[:, :tq].reshape(b * tq, W_AQ)
    return o


def _gla_constants(c):
    nlev = int(math.log2(c))
    t = np.arange(c)
    mats = [(t[None, :] <= t[:, None])]
    masks = [np.eye(c, dtype=bool)]
    for lev in range(nlev):
        m = c >> (lev + 1)
        ref_row = (t // (2 * m)) * 2 * m + m
        mats.append(t[None, :] <= ref_row[:, None])
        upper = (t & m) != 0
        same = (t[:, None] // (2 * m)) == (t[None, :] // (2 * m))
        masks.append(same & upper[:, None] & ~upper[None, :])
    return (jnp.asarray(np.concatenate(mats, 0).astype(np.float32), BF16),
            jnp.asarray(np.stack(masks).astype(np.float32)), nlev)


def _gla_kernel(q_ref, k_ref, v_ref, g_ref, gr_ref, gn_ref, mst_ref, msk_ref, s0_ref,
                ob_ref, sfin_ref, st_ref, *, c, nlev):
    i = pl.program_id(1)
    hk = G_KEY_DIM

    @pl.when(i == 0)
    def _():
        for h in range(G_HEADS):
            st_ref[h] = s0_ref[0, h].T

    g = g_ref[...]
    g_hi = g.astype(BF16)
    r1 = g - g_hi.astype(F32)
    g_mid = r1.astype(BF16)
    g_lo = (r1 - g_mid.astype(F32)).astype(BF16)
    mst = mst_ref[...]
    bs = _dot(mst, g_hi) + _dot(mst, g_mid) + _dot(mst, g_lo)
    b = bs[0:c]
    q = q_ref[...]
    k = k_ref[...]
    vb = v_ref[...].astype(BF16)
    row = lax.broadcasted_iota(I32, q.shape, 0)
    hs = lambda a, h: a[:, h * hk:(h + 1) * hk]

    qb = q.astype(BF16)
    kb = k.astype(BF16)
    attn = [_dot_nt(hs(qb, h), hs(kb, h)) * msk_ref[0] for h in range(G_HEADS)]
    for lev in range(nlev):
        m = c >> (lev + 1)
        upper = (row & m) != 0
        d = b - bs[(lev + 1) * c:(lev + 2) * c]
        e = jnp.exp(jnp.where(upper, d, -d))
        qt = jnp.where(upper, q * e, 0.0).astype(BF16)
        kt = jnp.where(upper, 0.0, k * e).astype(BF16)
        mk = msk_ref[lev + 1]
        for h in range(G_HEADS):
            attn[h] = attn[h] + _dot_nt(hs(qt, h), hs(kt, h)) * mk

    qe = (q * jnp.exp(b)).astype(BF16)
    b_last = b[c - 1:c, :]
    khat = (k * jnp.exp(b_last - b)).astype(BF16)
    dec = jnp.exp(b_last)
    gr = gr_ref[...]
    gn = gn_ref[...]
    for h in range(G_HEADS):
        st = st_ref[h]
        o = _dot_nt(hs(qe, h), st.astype(BF16)) + _dot(attn[h].astype(BF16), hs(vb, h))
        st_ref[h] = st * hs(dec, h) + _dot_tn(hs(vb, h), hs(khat, h))
        ms = jnp.mean(o * o, axis=1, keepdims=True)
        grh = hs(gr, h)
        of = o * lax.rsqrt(ms + LN_EPS) * hs(gn, h) * (grh / (1.0 + jnp.exp(-grh)))
        ob_ref[:, h * hk:(h + 1) * hk] = of.astype(BF16)

    @pl.when(i == pl.num_programs(1) - 1)
    def _():
        for h in range(G_HEADS):
            sfin_ref[0, h] = st_ref[h].T


def _gla(gq, gk, gv, glog, gr, g_norm, s0, b, c):
    n = gq.shape[0]
    nc = n // (b * c)
    mst, msk, nlev = _gla_constants(c)
    row = pl.BlockSpec((c, W_G), lambda bi, i: (bi * nc + i, 0))
    const = lambda a: pl.BlockSpec(a.shape, lambda bi, i: (0,) * a.ndim)
    st_spec = pl.BlockSpec((1, G_HEADS, G_KEY_DIM, G_VAL_DIM), lambda bi, i: (bi, 0, 0, 0))
    gn = g_norm.reshape(1, W_G)
    return pl.pallas_call(
        functools.partial(_gla_kernel, c=c, nlev=nlev),
        grid=(b, nc),
        in_specs=[row, row, row, row, row, const(gn), const(mst), const(msk), st_spec],
        out_specs=[row, st_spec],
        out_shape=[jax.ShapeDtypeStruct((n, W_G), BF16),
                   jax.ShapeDtypeStruct((b, G_HEADS, G_KEY_DIM, G_VAL_DIM), F32)],
        scratch_shapes=[pltpu.VMEM((G_HEADS, G_VAL_DIM, G_KEY_DIM), F32)],
        compiler_params=_params(("parallel", "arbitrary")),
        name="gla",
    )(gq, gk, gv, glog, gr, gn, mst, msk, s0)


def _layer_norm(z, g, b):
    mu = jnp.mean(z, axis=1, keepdims=True)
    zc = z - mu
    var = jnp.mean(zc * zc, axis=1, keepdims=True)
    return zc * lax.rsqrt(var + LN_EPS) * g + b


def _sigmoid(x):
    return 1.0 / (1.0 + jnp.exp(-x))


def _merge_kernel(x_ref, oa_ref, ob_ref, gate_ref, wpa_ref, wpb_ref, wout_ref, g1_ref, b1_ref,
                  wpq_ref, pk1_ref, pk2_ref, h1_ref, h1b_ref, s1t_ref, s2t_ref, *, alpha, d):
    ya = _dot(oa_ref[...], wpa_ref[...])
    yb = _dot(ob_ref[...], wpb_ref[...])
    m = _sigmoid(gate_ref[:, :d]) * ya + _sigmoid(gate_ref[:, d:]) * yb
    mix = _dot(m.astype(BF16), wout_ref[...])
    h1 = _layer_norm(alpha * x_ref[...] + mix, g1_ref[...], b1_ref[...])
    h1_ref[...] = h1
    h1b = h1.astype(BF16)
    h1b_ref[...] = h1b
    qp = _dot(h1b, wpq_ref[...])
    for h in range(P_HEADS):
        for half, (pk_ref, st_ref) in enumerate(((pk1_ref, s1t_ref), (pk2_ref, s2t_ref))):
            c0 = (2 * h + half) * P_HALF
            st_ref[h * P_NKEYS:(h + 1) * P_NKEYS, :] = _dot_nt(pk_ref[h], qp[:, c0:c0 + P_HALF].astype(BF16))


def _merge(x2, oa, ob, gate, w_pa, w_pb, w_out, ln_g, ln_b, w_pq, pk1, pk2, alpha, tm):
    n, d = x2.shape
    hk = P_HEADS * P_NKEYS
    row = lambda w: pl.BlockSpec((tm, w), lambda i: (i, 0))
    col = pl.BlockSpec((hk, tm), lambda i: (0, i))
    const = lambda a: pl.BlockSpec(a.shape, lambda i: (0,) * a.ndim)
    ws = [w_pa.astype(BF16), w_pb.astype(BF16), w_out.astype(BF16), ln_g.reshape(1, d), ln_b.reshape(1, d),
          w_pq.astype(BF16), pk1.astype(BF16), pk2.astype(BF16)]
    return pl.pallas_call(
        functools.partial(_merge_kernel, alpha=alpha, d=d),
        grid=(n // tm,),
        in_specs=[row(d), row(W_AQ), row(W_G), row(2 * d)] + [const(w) for w in ws],
        out_specs=[row(d), row(d), col, col],
        out_shape=[jax.ShapeDtypeStruct((n, d), F32), jax.ShapeDtypeStruct((n, d), BF16),
                   jax.ShapeDtypeStruct((hk, n), F32), jax.ShapeDtypeStruct((hk, n), F32)],
        compiler_params=_params(("parallel",)),
        name="merge",
    )(x2, oa, ob, gate, *ws)


def _top_desc(s, count):
    tops = []
    for r in range(count):
        m = jnp.max(_fold_rows(s, jnp.maximum), axis=0, keepdims=True)
        tops.append(m)
        if r + 1 < count:
            s = jnp.where(s == m, -jnp.inf, s)
    return tops


def _select_kernel(s1t_ref, s2t_ref, cnt_ref, e1_ref, rank_ref, e2_ref):
    nk = P_NKEYS

    def head(h, _):
        r0 = pl.multiple_of(h * nk, nk)
        rows = pl.ds(r0, nk)
        s1 = s1t_ref[rows, :]
        s2 = s2t_ref[rows, :]
        v1 = _top_desc(s1, P_TOPK)
        v2 = _top_desc(s2, P_TOPK)
        pairs = [(a, b) for a in range(P_TOPK) for b in range(P_TOPK // (a + 1))]
        fill = [jnp.full_like(v1[0], -jnp.inf)] * (-len(pairs) % SUBLANES)
        cand = jnp.concatenate([v1[a] + v2[b] for a, b in pairs] + fill, axis=0)
        work, seen = cand, jnp.zeros_like(v1[0])
        tau = jnp.full_like(v1[0], -jnp.inf)
        for _ in range(P_TOPK):
            m = jnp.max(work, axis=0, keepdims=True)
            hit = work == m
            seen = seen + jnp.sum(jnp.where(hit, 1.0, 0.0), axis=0, keepdims=True)
            tau = jnp.maximum(tau, jnp.where(seen >= float(P_TOPK), m, -jnp.inf))
            work = jnp.where(hit, -jnp.inf, work)
        cmax = v1[0] + v2[0]
        zsum = jnp.sum(jnp.where(cand >= tau, jnp.exp(cand - cmax), 0.0), axis=0, keepdims=True)
        v2all = jnp.concatenate(v2, axis=0)
        cnt = jnp.zeros(s1.shape, F32)
        rank = jnp.full(s2.shape, float(P_TOPK), F32)
        for a in range(P_TOPK):
            cnt_a = jnp.sum(jnp.where(v1[a] + v2all >= tau, 1.0, 0.0), axis=0, keepdims=True)
            cnt = jnp.where(s1 == v1[a], cnt_a, cnt)
            rank = jnp.where(s2 == v2[a], float(a), rank)
        cnt_ref[rows, :] = cnt
        rank_ref[rows, :] = rank
        e1_ref[rows, :] = jnp.exp(s1 - v1[0]) / zsum
        e2_ref[rows, :] = jnp.exp(s2 - v2[0])
        return 0

    lax.fori_loop(0, P_HEADS, head, 0)


def _select(s1t, s2t, tn):
    hk, n = s1t.shape
    col = pl.BlockSpec((hk, tn), lambda i: (0, i))
    return pl.pallas_call(
        _select_kernel,
        grid=(n // tn,),
        in_specs=[col, col],
        out_specs=[col, col, col, col],
        out_shape=[jax.ShapeDtypeStruct((hk, n), F32)] * 4,
        compiler_params=_params(("parallel",)),
        name="select",
    )(s1t, s2t)


I1_PER_STEP = 8
E_PER_STEP = I1_PER_STEP * P_NKEYS
I2_BLOCK = 16


def _gelu(x):
    return 0.5 * x * (1.0 + lax.erf(x * (2.0 ** -0.5)))


def _peer_kernel(hb_ref, pu_ref, pvt_ref, cnt_ref, e1_ref, rank_ref, e2_ref, h1_ref, g2_ref, b2_ref,
                 y_ref, a_ref, gw_ref, acc_ref, rank_s, e2_s, *, tn, alpha):
    g = pl.program_id(1)

    @pl.when(g == 0)
    def _():
        acc_ref[...] = jnp.zeros(acc_ref.shape, F32)
        rank_s[...] = rank_ref[...].astype(BF16)
        e2_s[...] = e2_ref[...].astype(BF16)

    a_ref[...] = _dot_nt(pu_ref[...], hb_ref[...])

    assert I1_PER_STEP == SUBLANES
    for lt in range(tn // LANES):
        ls = slice(lt * LANES, (lt + 1) * LANES)
        grp = lambda ref, h: ref[pl.ds(pl.multiple_of(h * P_NKEYS + g * I1_PER_STEP, SUBLANES), SUBLANES), ls]
        cnt8 = [grp(cnt_ref, h) for h in range(P_HEADS)]
        e18 = [grp(e1_ref, h) for h in range(P_HEADS)]
        for j in range(I1_PER_STEP):
            bcast = lambda a8: jnp.broadcast_to(a8[j:j + 1, :], (I2_BLOCK, LANES)).astype(BF16)
            cnt = [bcast(cnt8[h]) for h in range(P_HEADS)]
            e1 = [bcast(e18[h]) for h in range(P_HEADS)]
            for i2b in range(P_NKEYS // I2_BLOCK):
                w = jnp.zeros((I2_BLOCK, LANES), BF16)
                for h in range(P_HEADS):
                    rows = slice(h * P_NKEYS + i2b * I2_BLOCK, h * P_NKEYS + (i2b + 1) * I2_BLOCK)
                    w = w + jnp.where(rank_s[rows, ls] < cnt[h], e2_s[rows, ls] * e1[h], jnp.zeros((), BF16))
                arow = slice(j * P_NKEYS + i2b * I2_BLOCK, j * P_NKEYS + (i2b + 1) * I2_BLOCK)
                gw_ref[arow, ls] = (w.astype(F32) * _gelu(a_ref[arow, ls])).astype(BF16)

    acc_ref[...] += _dot(pvt_ref[...], gw_ref[...])

    @pl.when(g == pl.num_programs(1) - 1)
    def _():
        y_ref[...] = _layer_norm(alpha * h1_ref[...] + acc_ref[...].T, g2_ref[...], b2_ref[...])


def _peer(h1, h1b, cnt, e1, rank, e2, pu, pv, ln_g, ln_b, alpha, tn):
    n, d = h1.shape
    hk = P_HEADS * P_NKEYS
    ng = pu.shape[0] // E_PER_STEP
    row = lambda w: pl.BlockSpec((tn, w), lambda i, g: (i, 0))
    col = pl.BlockSpec((hk, tn), lambda i, g: (0, i))
    const = pl.BlockSpec((1, d), lambda i, g: (0, 0))
    return pl.pallas_call(
        functools.partial(_peer_kernel, tn=tn, alpha=alpha),
        grid=(n // tn, ng),
        in_specs=[row(d), pl.BlockSpec((E_PER_STEP, d), lambda i, g: (g, 0)),
                  pl.BlockSpec((d, E_PER_STEP), lambda i, g: (0, g)),
                  col, col, col, col, row(d), const, const],
        out_specs=row(d),
        out_shape=jax.ShapeDtypeStruct((n, d), F32),
        scratch_shapes=[pltpu.VMEM((E_PER_STEP, tn), F32), pltpu.VMEM((E_PER_STEP, tn), BF16),
                        pltpu.VMEM((d, tn), F32), pltpu.VMEM((hk, tn), BF16), pltpu.VMEM((hk, tn), BF16)],
        compiler_params=_params(("parallel", "arbitrary")),
        name="peer",
    )(h1b, pu.astype(BF16), pv.astype(BF16).T, cnt, e1, rank, e2, h1, ln_g.reshape(1, d), ln_b.reshape(1, d))


def _pick_tile(n, pref):
    t = min(n, pref)
    assert n % t == 0
    return t


def _layer(x, pos, limits, past, s0, w, *, chunk, alpha):
    b, t, d = x.shape
    n = b * t
    x2 = x.reshape(n, d)
    tm = _pick_tile(n, TOKEN_TILE)
    (aq, k32, v32, kb, vb, iq, ikw, ikb, gq, gk, gv, glog, gr, gate) = _proj(
        x2, pos, w["w_in"], w["w_fa"], w["b_fa"], tm)

    kb3, vb3, ikb3 = kb.reshape(b, t, LANES), vb.reshape(b, t, LANES), ikb.reshape(b, t, LANES)
    if past is not None:
        ck, cv, cik = past
        p = ck.shape[1]
        kb3 = jnp.concatenate([ck.reshape(b, p, LANES).astype(BF16), kb3], axis=1)
        vb3 = jnp.concatenate([cv.reshape(b, p, LANES).astype(BF16), vb3], axis=1)
        ikb3 = jnp.concatenate([cik.astype(BF16), ikb3[:, :, :IDX_DIM]], axis=1)
    topk = min(IDX_TOPK, kb3.shape[1] // 4)
    o_a = _attention(aq, iq, ikw, limits, kb3, vb3, ikb3, b, topk)

    o_b, s_fin = _gla(gq, gk, gv, glog, gr, w["g_gla_norm"], s0, b, chunk)

    h1, h1b, s1t, s2t = _merge(x2, o_a, o_b, gate, w["w_pa"], w["w_pb"], w["w_out"],
                               w["ln1_g"], w["ln1_b"], w["w_pq"], w["pk1"], w["pk2"], alpha, tm)
    cnt, e1, rank, e2 = _select(s1t, s2t, tm)
    y = _peer(h1, h1b, cnt, e1, rank, e2, w["pu"], w["pv"], w["ln2_g"], w["ln2_b"], alpha,
              _pick_tile(n, PEER_TOKEN_TILE))

    k_out = k32.reshape(b, t, A_KV_HEADS, A_HEAD_DIM)
    v_out = v32.reshape(b, t, A_KV_HEADS, A_HEAD_DIM)
    ik_out = ikw[:, :IDX_DIM].reshape(b, t, IDX_DIM)
    return y.reshape(b, t, d), k_out, v_out, ik_out, s_fin


def kernel(x_prompt, x_sample, cache_k, cache_v, cache_idx_k, state_gla, w_in, w_fa, b_fa, g_gla_norm,
           w_pa, w_pb, w_out, ln1_g, ln1_b, w_pq, pk1, pk2, pu, pv, ln2_g, ln2_b):
    depth = w_in.shape[0]
    alpha = (2.0 * depth) ** 0.25
    bp, tp, _ = x_prompt.shape
    bs, ts, _ = x_sample.shape
    past_len = cache_k.shape[2]
    pos_p = jnp.arange(tp)
    pos_s = past_len + jnp.arange(ts)
    lim_p = (np.arange(tp) // CHUNK + 1) * CHUNK
    lim_s = np.full((ts,), past_len + ts)
    names = ("w_in", "w_fa", "b_fa", "g_gla_norm", "w_pa", "w_pb", "w_out", "ln1_g", "ln1_b",
             "w_pq", "pk1", "pk2", "pu", "pv", "ln2_g", "ln2_b")
    stacked = (w_in, w_fa, b_fa, g_gla_norm, w_pa, w_pb, w_out, ln1_g, ln1_b, w_pq, pk1, pk2, pu, pv, ln2_g, ln2_b)
    hp, hs = x_prompt, x_sample
    outs_p, outs_s = [], []
    for l in range(depth):
        w = {nm: a[l] for nm, a in zip(names, stacked)}
        s0 = jnp.zeros((bp, G_HEADS, G_KEY_DIM, G_VAL_DIM), F32)
        hp, *rest = _layer(hp, pos_p, lim_p, None, s0, w, chunk=CHUNK, alpha=alpha)
        outs_p.append(rest)
        hs, *rest = _layer(hs, pos_s, lim_s, (cache_k[l], cache_v[l], cache_idx_k[l]), state_gla[l], w,
                           chunk=ts, alpha=alpha)
        outs_s.append(rest)
    stack = lambda outs, i: jnp.stack([o[i] for o in outs])
    return (hp, hs, stack(outs_p, 0), stack(outs_p, 1), stack(outs_p, 2), stack(outs_p, 3),
            stack(outs_s, 0), stack(outs_s, 1), stack(outs_s, 2), stack(outs_s, 3))
```

```python
import functools
import math

import numpy as np
import jax
import jax.numpy as jnp
from jax import lax
from jax.experimental import pallas as pl
from jax.experimental.pallas import tpu as pltpu

F32 = jnp.float32
BF16 = jnp.bfloat16
I32 = jnp.int32

LANES = 128
SUBLANES = 8
VMEM_LIMIT = 56 << 20

CHUNK = 64
A_HEADS = 8
A_KV_HEADS = 2
A_HEAD_DIM = 64
A_GROUP = A_HEADS // A_KV_HEADS
IDX_HEADS = 4
IDX_DIM = 64
IDX_TOPK = 256
ROPE_THETA = 10000.0
G_HEADS = 4
G_KEY_DIM = 128
G_VAL_DIM = 128
G_LOWRANK = 16
G_TAU = 16.0
P_HEADS = 8
P_NKEYS = 128
P_HALF = 128
P_TOPK = 16
LN_EPS = 1e-5

W_AQ = A_HEADS * A_HEAD_DIM
W_AK = A_KV_HEADS * A_HEAD_DIM
W_IQ = IDX_HEADS * IDX_DIM
W_G = G_HEADS * G_KEY_DIM
IN_SIZES = (W_AQ, W_AK, W_AK, W_IQ, IDX_DIM, IDX_HEADS, W_G, W_G, W_G, G_LOWRANK, W_G, None)

TOKEN_TILE = 256
PEER_TOKEN_TILE = 512
Q_BLOCK = 128
KEY_TILE = 512
COUNT_TILES = 1

INT_MIN = -(2 ** 31)
NEG_INF_KEY = -2139095041
NEG_BIG = -1e30


def _dot(a, b):
    return jnp.dot(a, b, preferred_element_type=F32)


def _dot_nt(a, b):
    return lax.dot_general(a, b, (((1,), (1,)), ((), ())), preferred_element_type=F32)


def _dot_tn(a, b):
    return lax.dot_general(a, b, (((0,), (0,)), ((), ())), preferred_element_type=F32)


def _sort_key(x):
    bits = pltpu.bitcast(x, I32)
    key = jnp.where(bits < 0, bits ^ 0x7FFFFFFF, bits)
    return jnp.where(key == -1, 0, key)


def _fold_rows(x, op):
    x = x.reshape(x.shape[0] // SUBLANES, SUBLANES, x.shape[1])
    while x.shape[0] > 1:
        half = x.shape[0] // 2
        folded = op(x[:half], x[half:2 * half])
        x = folded if x.shape[0] == 2 * half else jnp.concatenate([folded, x[2 * half:]], axis=0)
    return x[0]


def _params(sem):
    return pltpu.CompilerParams(dimension_semantics=sem, vmem_limit_bytes=VMEM_LIMIT)


_PG_AQ, _PG_K, _PG_V, _PG_IQ, _PG_IKW, _PG_GQ, _PG_GK, _PG_GV, _PG_GF, _PG_GR, _PG_GATE = range(11)


def _pack_layout(d_model):
    widths = [W_AQ, W_AK, W_AK, W_IQ, LANES, W_G, W_G, W_G, LANES, W_G, 2 * d_model]
    offs = np.concatenate([[0], np.cumsum(widths)]).tolist()
    return widths, offs


def _pack_w_in(w_in):
    d = w_in.shape[0]
    sizes = list(IN_SIZES[:-1]) + [2 * d]
    cuts = np.cumsum(sizes)[:-1].tolist()
    aq, ak, av, iq, ik, iw, gq, gk, gv, gf, gr, gate = jnp.split(w_in, cuts, axis=1)
    z = lambda n: jnp.zeros((d, n), w_in.dtype)
    ikw = jnp.concatenate([ik, iw, z(LANES - IDX_DIM - IDX_HEADS)], axis=1)
    gfp = jnp.concatenate([gf, z(LANES - G_LOWRANK)], axis=1)
    return jnp.concatenate([aq, ak, av, iq, ikw, gq, gk, gv, gfp, gr, gate], axis=1).astype(BF16)


def _rope_tables(pos):
    half = A_HEAD_DIM // 2
    inv = ROPE_THETA ** (-jnp.arange(half, dtype=F32) / half)
    ang = pos.astype(F32)[:, None] * inv[None, :]
    c, s = jnp.cos(ang), jnp.sin(ang)
    return jnp.concatenate([c, c, c, c], -1), jnp.concatenate([-s, s, -s, s], -1)


def _proj_kernel(x_ref, w_ref, wfa_ref, bfa_ref, cos_ref, sin_ref,
                 aq_ref, k_ref, v_ref, kb_ref, vb_ref, iq_ref, ikw_ref, ikb_ref,
                 gq_ref, gk_ref, gv_ref, glog_ref, gr_ref, gate_ref, *, offs, widths):
    xb = x_ref[...].astype(BF16)
    cos = cos_ref[...]
    sin = sin_ref[...]
    lane = lax.broadcasted_iota(I32, cos.shape, 1)
    first_half = (lane & (A_HEAD_DIM // 2)) == 0

    def proj(g):
        return _dot(xb, w_ref[:, offs[g]:offs[g] + widths[g]])

    def rope_slab(y):
        fwd = pltpu.roll(y, LANES - A_HEAD_DIM // 2, 1)
        bwd = pltpu.roll(y, A_HEAD_DIM // 2, 1)
        return y * cos + jnp.where(first_half, fwd, bwd) * sin

    def rope(y):
        return [rope_slab(y[:, s * LANES:(s + 1) * LANES]) for s in range(y.shape[1] // LANES)]

    for s, slab in enumerate(rope(proj(_PG_AQ))):
        aq_ref[:, s * LANES:(s + 1) * LANES] = (slab * (A_HEAD_DIM ** -0.5)).astype(BF16)
    k = rope(proj(_PG_K))[0]
    k_ref[...] = k
    kb_ref[...] = k.astype(BF16)
    v = proj(_PG_V)
    v_ref[...] = v
    vb_ref[...] = v.astype(BF16)
    for s, slab in enumerate(rope(proj(_PG_IQ))):
        iq_ref[:, s * LANES:(s + 1) * LANES] = (slab * (IDX_DIM ** -0.5)).astype(BF16)
    raw = proj(_PG_IKW)
    ikw = jnp.where(lane < IDX_DIM, rope_slab(raw), raw * (IDX_HEADS ** -0.5))
    ikw_ref[...] = ikw
    ikb_ref[...] = ikw.astype(BF16)
    gq_ref[...] = proj(_PG_GQ) * (G_KEY_DIM ** -0.5)
    gk_ref[...] = proj(_PG_GK)
    gv_ref[...] = proj(_PG_GV)
    z = _dot(proj(_PG_GF).astype(BF16), wfa_ref[...]) + bfa_ref[...]
    glog_ref[...] = (jnp.minimum(z, 0.0) - jnp.log1p(jnp.exp(-jnp.abs(z)))) * (1.0 / G_TAU)
    gr_ref[...] = proj(_PG_GR)
    gate_ref[...] = proj(_PG_GATE)


def _proj(x2, pos, w_in, w_fa, b_fa, tm):
    n, d = x2.shape
    t = pos.shape[0]
    widths, offs = _pack_layout(d)
    wp = _pack_w_in(w_in)
    wfa = jnp.concatenate([w_fa, jnp.zeros((LANES - G_LOWRANK, W_G), w_fa.dtype)], 0).astype(BF16)
    cos, sin = _rope_tables(pos)
    if tm > t:
        cos, sin = jnp.tile(cos, (tm // t, 1)), jnp.tile(sin, (tm // t, 1))
    nper = cos.shape[0] // tm
    row = lambda w: pl.BlockSpec((tm, w), lambda i: (i, 0))
    const = lambda a: pl.BlockSpec(a.shape, lambda i: (0, 0))
    tab = pl.BlockSpec((tm, LANES), lambda i: (i % nper, 0))
    outs = [(W_AQ, BF16), (LANES, F32), (LANES, F32), (LANES, BF16), (LANES, BF16), (W_IQ, BF16),
            (LANES, F32), (LANES, BF16), (W_G, F32), (W_G, F32), (W_G, F32), (W_G, F32), (W_G, F32),
            (2 * d, F32)]
    bfa = b_fa.reshape(1, W_G)
    return pl.pallas_call(
        functools.partial(_proj_kernel, offs=offs, widths=widths),
        grid=(n // tm,),
        in_specs=[row(d), const(wp), const(wfa), const(bfa), tab, tab],
        out_specs=[row(w) for w, _ in outs],
        out_shape=[jax.ShapeDtypeStruct((n, w), dt) for w, dt in outs],
        compiler_params=_params(("parallel",)),
        name="proj",
    )(x2, wp, wfa, bfa, cos, sin)


def _attn_kernel(nkt_ref, aq_ref, iq_ref, ikw_ref, lim_ref, kb_ref, vt_ref, ikb_ref, o_ref,
                 keys_ref, qs_ref, iqs_ref, acc_ref, *, kt_w, topk, idx_bits):
    qb = Q_BLOCK
    j = pl.program_id(1)
    nkt = nkt_ref[j]

    for h in range(A_HEADS):
        qs_ref[h // A_GROUP, (h % A_GROUP) * qb:(h % A_GROUP + 1) * qb, :] = \
            aq_ref[:, h * A_HEAD_DIM:(h + 1) * A_HEAD_DIM]
    for h in range(IDX_HEADS):
        iqs_ref[h * qb:(h + 1) * qb, :] = iq_ref[:, h * IDX_DIM:(h + 1) * IDX_DIM]

    ikw_t = ikw_ref[...].T
    iw_rows = [ikw_t[IDX_DIM + h:IDX_DIM + h + 1, :] for h in range(IDX_HEADS)]
    lim = lim_ref[0, 0:1, :]
    sub = lax.broadcasted_iota(I32, (kt_w, qb), 0)

    def score_tile(kt, _):
        base = pl.multiple_of(kt * kt_w, kt_w)
        ik_t = ikb_ref[0, pl.ds(base, kt_w), :][:, :IDX_DIM]
        s = jnp.maximum(_dot_nt(ik_t, iqs_ref[...]), 0.0)
        score = jnp.zeros((kt_w, qb), F32)
        for h in range(IDX_HEADS):
            score = score + s[:, h * qb:(h + 1) * qb] * iw_rows[h]
        keys_ref[pl.ds(base, kt_w), :] = jnp.where(sub + base < lim, _sort_key(score), NEG_INF_KEY)
        return 0

    lax.fori_loop(0, nkt, score_tile, 0)

    ct_w = COUNT_TILES * kt_w
    nct = (nkt + COUNT_TILES - 1) // COUNT_TILES

    def fill_tile(kt, _):
        keys_ref[pl.ds(pl.multiple_of(kt * kt_w, kt_w), kt_w), :] = jnp.full((kt_w, qb), NEG_INF_KEY, I32)
        return 0

    lax.fori_loop(nkt, nct * COUNT_TILES, fill_tile, 0)
    sub_c = lax.broadcasted_iota(I32, (ct_w, qb), 0)

    def count(pred):
        def body(ct, acc):
            base = pl.multiple_of(ct * ct_w, ct_w)
            hit = pred(keys_ref[pl.ds(base, ct_w), :], sub_c + base)
            return acc + _fold_rows(hit, jnp.add)
        acc = lax.fori_loop(0, nct, body, jnp.zeros((SUBLANES, qb), F32))
        return jnp.sum(acc, axis=0, keepdims=True)

    def count_ge(t_row):
        return count(lambda kk, idx: jnp.where(kk >= t_row, 1.0, 0.0))

    kf = float(topk)
    thr = jnp.where(count_ge(jnp.zeros((1, qb), I32)) >= kf, 0, INT_MIN).astype(I32)

    def thr_bit(i, t):
        cand = t + jnp.left_shift(jnp.int32(1), 30 - i)
        return jnp.where(count_ge(cand) >= kf, cand, t)

    thr = lax.fori_loop(0, 31, thr_bit, thr)
    n_gt = count_ge(thr + 1)
    n_eq = count_ge(thr) - n_gt
    need = kf - n_gt
    finite = thr > NEG_INF_KEY
    excess = jnp.where(finite, jnp.where(n_eq > need, 1.0, 0.0), 0.0)

    def count_eq_below(j_row):
        return count(lambda kk, idx: jnp.where(kk == thr, jnp.where(idx < j_row, 1.0, 0.0), 0.0))

    def resolve_ties():
        def bit(i, jc):
            cand = jc + jnp.left_shift(jnp.int32(1), idx_bits - 1 - i)
            return jnp.where(count_eq_below(cand) <= need - 1.0, cand, jc)
        jc = lax.fori_loop(0, idx_bits, bit, jnp.zeros((1, qb), I32))
        return jnp.where(finite, jc, -1)

    cut = lax.cond(jnp.max(excess) > 0.0, resolve_ties,
                   lambda: jnp.where(finite, 2 ** 30, -1).astype(I32))

    gq = A_GROUP * qb

    def logits_of(kt, n, bias):
        base = pl.multiple_of(kt * kt_w, kt_w)
        k_t = kb_ref[0, pl.ds(base, kt_w), n * A_HEAD_DIM:(n + 1) * A_HEAD_DIM]
        return _dot_nt(k_t, qs_ref[n]) + jnp.concatenate([bias] * A_GROUP, axis=1)

    def max_tile(kt, mx):
        rows = pl.ds(pl.multiple_of(kt * kt_w, kt_w), kt_w)
        kk = keys_ref[rows, :]
        tie = jnp.where(sub + kt * kt_w <= cut, 0.0, NEG_BIG)
        bias = jnp.where(kk > thr, 0.0, jnp.where(kk == thr, tie, NEG_BIG))
        keys_ref[rows, :] = pltpu.bitcast(bias, I32)
        return tuple(jnp.maximum(mx[n], _fold_rows(logits_of(kt, n, bias), jnp.maximum))
                     for n in range(A_KV_HEADS))

    mx = lax.fori_loop(0, nkt, max_tile, (jnp.full((SUBLANES, gq), NEG_BIG, F32),) * A_KV_HEADS)
    m_row = [jnp.max(mx[n], axis=0, keepdims=True) for n in range(A_KV_HEADS)]
    acc_ref[...] = jnp.zeros(acc_ref.shape, F32)

    def attend(kt, ls):
        base = pl.multiple_of(kt * kt_w, kt_w)
        bias = pltpu.bitcast(keys_ref[pl.ds(base, kt_w), :], F32)
        out = []
        for n in range(A_KV_HEADS):
            p = jnp.exp(logits_of(kt, n, bias) - m_row[n])
            out.append(ls[n] + _fold_rows(p, jnp.add))
            v_t = vt_ref[0, n * A_HEAD_DIM:(n + 1) * A_HEAD_DIM, pl.ds(base, kt_w)]
            acc_ref[n] += _dot(v_t, p.astype(BF16))
        return tuple(out)

    ls = lax.fori_loop(0, nkt, attend, (jnp.zeros((SUBLANES, gq), F32),) * A_KV_HEADS)
    l_row = [jnp.sum(ls[n], axis=0, keepdims=True) for n in range(A_KV_HEADS)]

    o_t = jnp.concatenate([acc_ref[n] / l_row[n] for n in range(A_KV_HEADS)], axis=0)
    for n in range(A_KV_HEADS):
        for g in range(A_GROUP):
            h = n * A_GROUP + g
            blk = o_t[n * A_HEAD_DIM:(n + 1) * A_HEAD_DIM, g * qb:(g + 1) * qb]
            o_ref[:, h * A_HEAD_DIM:(h + 1) * A_HEAD_DIM] = blk.T.astype(BF16)


def _attention(aq, iq, ikw, limits, kb, vb, ikb, b, topk):
    qb, kt_w = Q_BLOCK, KEY_TILE
    tq = aq.shape[0] // b
    tq_pad = -(-tq // qb) * qb
    limits = np.asarray(limits)
    if tq_pad != tq:
        padq = lambda a: jnp.pad(a.reshape(b, tq, -1), ((0, 0), (0, tq_pad - tq), (0, 0))).reshape(b * tq_pad, -1)
        aq, iq, ikw = padq(aq), padq(iq), padq(ikw)
        limits = np.concatenate([limits, np.full((tq_pad - tq,), limits[-1])])
    l_all = kb.shape[1]
    l_pad = -(-l_all // kt_w) * kt_w
    if l_pad != l_all:
        padl = lambda a: jnp.pad(a, ((0, 0), (0, l_pad - l_all), (0, 0)))
        kb, vb, ikb = padl(kb), padl(vb), padl(ikb)
    vt = jnp.swapaxes(vb, 1, 2)
    nq = tq_pad // qb
    lim_blk = limits.reshape(nq, qb)
    nkt = jnp.asarray(np.minimum(-(-lim_blk.max(axis=1) // kt_w), l_pad // kt_w).astype(np.int32))
    lim = jnp.asarray(np.broadcast_to(lim_blk[:, None, :], (nq, SUBLANES, qb)).astype(np.int32))
    idx_bits = max(1, int(math.ceil(math.log2(l_pad))))
    n = b * tq_pad
    qrow = lambda w: pl.BlockSpec((qb, w), lambda bi, j, s: (bi * nq + j, 0))
    keys = lambda a: pl.BlockSpec((1,) + a.shape[1:], lambda bi, j, s: (bi, 0, 0))
    gq = A_GROUP * qb
    grid_spec = pltpu.PrefetchScalarGridSpec(
        num_scalar_prefetch=1,
        grid=(b, nq),
        in_specs=[qrow(W_AQ), qrow(W_IQ), qrow(LANES),
                  pl.BlockSpec((1, SUBLANES, qb), lambda bi, j, s: (j, 0, 0)),
                  keys(kb), keys(vt), keys(ikb)],
        out_specs=qrow(W_AQ),
        scratch_shapes=[pltpu.VMEM((-(-l_pad // (COUNT_TILES * kt_w)) * COUNT_TILES * kt_w, qb), I32),
                        pltpu.VMEM((A_KV_HEADS, gq, A_HEAD_DIM), BF16),
                        pltpu.VMEM((IDX_HEADS * qb, IDX_DIM), BF16),
                        pltpu.VMEM((A_KV_HEADS, A_HEAD_DIM, gq), F32)],
    )
    o = pl.pallas_call(
        functools.partial(_attn_kernel, kt_w=kt_w, topk=topk, idx_bits=idx_bits),
        grid_spec=grid_spec,
        out_shape=jax.ShapeDtypeStruct((n, W_AQ), BF16),
        compiler_params=_params(("parallel", "arbitrary")),
        name="attn",
    )(nkt, aq, iq, ikw, lim, kb, vt, ikb)
    if tq_pad != tq:
        o = o.reshape(b, tq_pad, W_AQ)[:, :tq].reshape(b * tq, W_AQ)
    return o


def _gla_constants(c):
    nlev = int(math.log2(c))
    t = np.arange(c)
    mats = [(t[None, :] <= t[:, None])]
    masks = [np.eye(c, dtype=bool)]
    for lev in range(nlev):
        m = c >> (lev + 1)
        ref_row = (t // (2 * m)) * 2 * m + m
        mats.append(t[None, :] <= ref_row[:, None])
        upper = (t & m) != 0
        same = (t[:, None] // (2 * m)) == (t[None, :] // (2 * m))
        masks.append(same & upper[:, None] & ~upper[None, :])
    return (jnp.asarray(np.concatenate(mats, 0).astype(np.float32), BF16),
            jnp.asarray(np.stack(masks).astype(np.float32)), nlev)


def _gla_kernel(q_ref, k_ref, v_ref, g_ref, gr_ref, gn_ref, mst_ref, msk_ref, s0_ref,
                ob_ref, sfin_ref, st_ref, *, c, nlev):
    i = pl.program_id(1)
    hk = G_KEY_DIM

    @pl.when(i == 0)
    def _():
        for h in range(G_HEADS):
            st_ref[h] = s0_ref[0, h].T

    g = g_ref[...]
    g_hi = g.astype(BF16)
    r1 = g - g_hi.astype(F32)
    g_mid = r1.astype(BF16)
    g_lo = (r1 - g_mid.astype(F32)).astype(BF16)
    mst = mst_ref[...]
    bs = _dot(mst, g_hi) + _dot(mst, g_mid) + _dot(mst, g_lo)
    b = bs[0:c]
    q = q_ref[...]
    k = k_ref[...]
    vb = v_ref[...].astype(BF16)
    row = lax.broadcasted_iota(I32, q.shape, 0)
    hs = lambda a, h: a[:, h * hk:(h + 1) * hk]

    qb = q.astype(BF16)
    kb = k.astype(BF16)
    attn = [_dot_nt(hs(qb, h), hs(kb, h)) * msk_ref[0] for h in range(G_HEADS)]
    for lev in range(nlev):
        m = c >> (lev + 1)
        upper = (row & m) != 0
        d = b - bs[(lev + 1) * c:(lev + 2) * c]
        e = jnp.exp(jnp.where(upper, d, -d))
        qt = jnp.where(upper, q * e, 0.0).astype(BF16)
        kt = jnp.where(upper, 0.0, k * e).astype(BF16)
        mk = msk_ref[lev + 1]
        for h in range(G_HEADS):
            attn[h] = attn[h] + _dot_nt(hs(qt, h), hs(kt, h)) * mk

    qe = (q * jnp.exp(b)).astype(BF16)
    b_last = b[c - 1:c, :]
    khat = (k * jnp.exp(b_last - b)).astype(BF16)
    dec = jnp.exp(b_last)
    gr = gr_ref[...]
    gn = gn_ref[...]
    for h in range(G_HEADS):
        st = st_ref[h]
        o = _dot_nt(hs(qe, h), st.astype(BF16)) + _dot(attn[h].astype(BF16), hs(vb, h))
        st_ref[h] = st * hs(dec, h) + _dot_tn(hs(vb, h), hs(khat, h))
        ms = jnp.mean(o * o, axis=1, keepdims=True)
        grh = hs(gr, h)
        of = o * lax.rsqrt(ms + LN_EPS) * hs(gn, h) * (grh / (1.0 + jnp.exp(-grh)))
        ob_ref[:, h * hk:(h + 1) * hk] = of.astype(BF16)

    @pl.when(i == pl.num_programs(1) - 1)
    def _():
        for h in range(G_HEADS):
            sfin_ref[0, h] = st_ref[h].T


def _gla(gq, gk, gv, glog, gr, g_norm, s0, b, c):
    n = gq.shape[0]
    nc = n // (b * c)
    mst, msk, nlev = _gla_constants(c)
    row = pl.BlockSpec((c, W_G), lambda bi, i: (bi * nc + i, 0))
    const = lambda a: pl.BlockSpec(a.shape, lambda bi, i: (0,) * a.ndim)
    st_spec = pl.BlockSpec((1, G_HEADS, G_KEY_DIM, G_VAL_DIM), lambda bi, i: (bi, 0, 0, 0))
    gn = g_norm.reshape(1, W_G)
    return pl.pallas_call(
        functools.partial(_gla_kernel, c=c, nlev=nlev),
        grid=(b, nc),
        in_specs=[row, row, row, row, row, const(gn), const(mst), const(msk), st_spec],
        out_specs=[row, st_spec],
        out_shape=[jax.ShapeDtypeStruct((n, W_G), BF16),
                   jax.ShapeDtypeStruct((b, G_HEADS, G_KEY_DIM, G_VAL_DIM), F32)],
        scratch_shapes=[pltpu.VMEM((G_HEADS, G_VAL_DIM, G_KEY_DIM), F32)],
        compiler_params=_params(("parallel", "arbitrary")),
        name="gla",
    )(gq, gk, gv, glog, gr, gn, mst, msk, s0)


def _layer_norm(z, g, b):
    mu = jnp.mean(z, axis=1, keepdims=True)
    zc = z - mu
    var = jnp.mean(zc * zc, axis=1, keepdims=True)
    return zc * lax.rsqrt(var + LN_EPS) * g + b


def _sigmoid(x):
    return 1.0 / (1.0 + jnp.exp(-x))


def _merge_kernel(x_ref, oa_ref, ob_ref, gate_ref, wpa_ref, wpb_ref, wout_ref, g1_ref, b1_ref,
                  wpq_ref, pk1_ref, pk2_ref, h1_ref, h1b_ref, s1t_ref, s2t_ref, *, alpha, d):
    ya = _dot(oa_ref[...], wpa_ref[...])
    yb = _dot(ob_ref[...], wpb_ref[...])
    m = _sigmoid(gate_ref[:, :d]) * ya + _sigmoid(gate_ref[:, d:]) * yb
    mix = _dot(m.astype(BF16), wout_ref[...])
    h1 = _layer_norm(alpha * x_ref[...] + mix, g1_ref[...], b1_ref[...])
    h1_ref[...] = h1
    h1b = h1.astype(BF16)
    h1b_ref[...] = h1b
    qp = _dot(h1b, wpq_ref[...])
    for h in range(P_HEADS):
        for half, (pk_ref, st_ref) in enumerate(((pk1_ref, s1t_ref), (pk2_ref, s2t_ref))):
            c0 = (2 * h + half) * P_HALF
            st_ref[h * P_NKEYS:(h + 1) * P_NKEYS, :] = _dot_nt(pk_ref[h], qp[:, c0:c0 + P_HALF].astype(BF16))


def _merge(x2, oa, ob, gate, w_pa, w_pb, w_out, ln_g, ln_b, w_pq, pk1, pk2, alpha, tm):
    n, d = x2.shape
    hk = P_HEADS * P_NKEYS
    row = lambda w: pl.BlockSpec((tm, w), lambda i: (i, 0))
    col = pl.BlockSpec((hk, tm), lambda i: (0, i))
    const = lambda a: pl.BlockSpec(a.shape, lambda i: (0,) * a.ndim)
    ws = [w_pa.astype(BF16), w_pb.astype(BF16), w_out.astype(BF16), ln_g.reshape(1, d), ln_b.reshape(1, d),
          w_pq.astype(BF16), pk1.astype(BF16), pk2.astype(BF16)]
    return pl.pallas_call(
        functools.partial(_merge_kernel, alpha=alpha, d=d),
        grid=(n // tm,),
        in_specs=[row(d), row(W_AQ), row(W_G), row(2 * d)] + [const(w) for w in ws],
        out_specs=[row(d), row(d), col, col],
        out_shape=[jax.ShapeDtypeStruct((n, d), F32), jax.ShapeDtypeStruct((n, d), BF16),
                   jax.ShapeDtypeStruct((hk, n), F32), jax.ShapeDtypeStruct((hk, n), F32)],
        compiler_params=_params(("parallel",)),
        name="merge",
    )(x2, oa, ob, gate, *ws)


def _top_desc(s, count):
    tops = []
    for r in range(count):
        m = jnp.max(_fold_rows(s, jnp.maximum), axis=0, keepdims=True)
        tops.append(m)
        if r + 1 < count:
            s = jnp.where(s == m, -jnp.inf, s)
    return tops


def _select_kernel(s1t_ref, s2t_ref, cnt_ref, e1_ref, rank_ref, e2_ref):
    nk = P_NKEYS

    def head(h, _):
        r0 = pl.multiple_of(h * nk, nk)
        rows = pl.ds(r0, nk)
        s1 = s1t_ref[rows, :]
        s2 = s2t_ref[rows, :]
        v1 = _top_desc(s1, P_TOPK)
        v2 = _top_desc(s2, P_TOPK)
        pairs = [(a, b) for a in range(P_TOPK) for b in range(P_TOPK // (a + 1))]
        fill = [jnp.full_like(v1[0], -jnp.inf)] * (-len(pairs) % SUBLANES)
        cand = jnp.concatenate([v1[a] + v2[b] for a, b in pairs] + fill, axis=0)
        work, seen = cand, jnp.zeros_like(v1[0])
        tau = jnp.full_like(v1[0], -jnp.inf)
        for _ in range(P_TOPK):
            m = jnp.max(work, axis=0, keepdims=True)
            hit = work == m
            seen = seen + jnp.sum(jnp.where(hit, 1.0, 0.0), axis=0, keepdims=True)
            tau = jnp.maximum(tau, jnp.where(seen >= float(P_TOPK), m, -jnp.inf))
            work = jnp.where(hit, -jnp.inf, work)
        cmax = v1[0] + v2[0]
        zsum = jnp.sum(jnp.where(cand >= tau, jnp.exp(cand - cmax), 0.0), axis=0, keepdims=True)
        v2all = jnp.concatenate(v2, axis=0)
        cnt = jnp.zeros(s1.shape, F32)
        rank = jnp.full(s2.shape, float(P_TOPK), F32)
        for a in range(P_TOPK):
            cnt_a = jnp.sum(jnp.where(v1[a] + v2all >= tau, 1.0, 0.0), axis=0, keepdims=True)
            cnt = jnp.where(s1 == v1[a], cnt_a, cnt)
            rank = jnp.where(s2 == v2[a], float(a), rank)
        cnt_ref[rows, :] = cnt
        rank_ref[rows, :] = rank
        e1_ref[rows, :] = jnp.exp(s1 - v1[0]) / zsum
        e2_ref[rows, :] = jnp.exp(s2 - v2[0])
        return 0

    lax.fori_loop(0, P_HEADS, head, 0)


def _select(s1t, s2t, tn):
    hk, n = s1t.shape
    col = pl.BlockSpec((hk, tn), lambda i: (0, i))
    return pl.pallas_call(
        _select_kernel,
        grid=(n // tn,),
        in_specs=[col, col],
        out_specs=[col, col, col, col],
        out_shape=[jax.ShapeDtypeStruct((hk, n), F32)] * 4,
        compiler_params=_params(("parallel",)),
        name="select",
    )(s1t, s2t)


I1_PER_STEP = 8
E_PER_STEP = I1_PER_STEP * P_NKEYS
I2_BLOCK = 16


def _gelu(x):
    return 0.5 * x * (1.0 + lax.erf(x * (2.0 ** -0.5)))


def _peer_kernel(hb_ref, pu_ref, pvt_ref, cnt_ref, e1_ref, rank_ref, e2_ref, h1_ref, g2_ref, b2_ref,
                 y_ref, a_ref, gw_ref, acc_ref, rank_s, e2_s, *, tn, alpha):
    g = pl.program_id(1)

    @pl.when(g == 0)
    def _():
        acc_ref[...] = jnp.zeros(acc_ref.shape, F32)
        rank_s[...] = rank_ref[...].astype(BF16)
        e2_s[...] = e2_ref[...].astype(BF16)

    a_ref[...] = _dot_nt(pu_ref[...], hb_ref[...])

    assert I1_PER_STEP == SUBLANES
    for lt in range(tn // LANES):
        ls = slice(lt * LANES, (lt + 1) * LANES)
        grp = lambda ref, h: ref[pl.ds(pl.multiple_of(h * P_NKEYS + g * I1_PER_STEP, SUBLANES), SUBLANES), ls]
        cnt8 = [grp(cnt_ref, h) for h in range(P_HEADS)]
        e18 = [grp(e1_ref, h) for h in range(P_HEADS)]
        for j in range(I1_PER_STEP):
            bcast = lambda a8: jnp.broadcast_to(a8[j:j + 1, :], (I2_BLOCK, LANES)).astype(BF16)
            cnt = [bcast(cnt8[h]) for h in range(P_HEADS)]
            e1 = [bcast(e18[h]) for h in range(P_HEADS)]
            for i2b in range(P_NKEYS // I2_BLOCK):
                w = jnp.zeros((I2_BLOCK, LANES), BF16)
                for h in range(P_HEADS):
                    rows = slice(h * P_NKEYS + i2b * I2_BLOCK, h * P_NKEYS + (i2b + 1) * I2_BLOCK)
                    w = w + jnp.where(rank_s[rows, ls] < cnt[h], e2_s[rows, ls] * e1[h], jnp.zeros((), BF16))
                arow = slice(j * P_NKEYS + i2b * I2_BLOCK, j * P_NKEYS + (i2b + 1) * I2_BLOCK)
                gw_ref[arow, ls] = (w.astype(F32) * _gelu(a_ref[arow, ls])).astype(BF16)

    acc_ref[...] += _dot(pvt_ref[0], gw_ref[...])

    @pl.when(g == pl.num_programs(1) - 1)
    def _():
        y_ref[...] = _layer_norm(alpha * h1_ref[...] + acc_ref[...].T, g2_ref[...], b2_ref[...])


def _peer(h1, h1b, cnt, e1, rank, e2, pu, pv, ln_g, ln_b, alpha, tn):
    n, d = h1.shape
    hk = P_HEADS * P_NKEYS
    ng = pu.shape[0] // E_PER_STEP
    row = lambda w: pl.BlockSpec((tn, w), lambda i, g: (i, 0))
    col = pl.BlockSpec((hk, tn), lambda i, g: (0, i))
    const = pl.BlockSpec((1, d), lambda i, g: (0, 0))
    pvt = pv.astype(BF16).reshape(ng, E_PER_STEP, d).transpose(0, 2, 1)
    return pl.pallas_call(
        functools.partial(_peer_kernel, tn=tn, alpha=alpha),
        grid=(n // tn, ng),
        in_specs=[row(d), pl.BlockSpec((E_PER_STEP, d), lambda i, g: (g, 0)),
                  pl.BlockSpec((1, d, E_PER_STEP), lambda i, g: (g, 0, 0)),
                  col, col, col, col, row(d), const, const],
        out_specs=row(d),
        out_shape=jax.ShapeDtypeStruct((n, d), F32),
        scratch_shapes=[pltpu.VMEM((E_PER_STEP, tn), F32), pltpu.VMEM((E_PER_STEP, tn), BF16),
                        pltpu.VMEM((d, tn), F32), pltpu.VMEM((hk, tn), BF16), pltpu.VMEM((hk, tn), BF16)],
        compiler_params=_params(("parallel", "arbitrary")),
        name="peer",
    )(h1b, pu.astype(BF16), pvt, cnt, e1, rank, e2, h1, ln_g.reshape(1, d), ln_b.reshape(1, d))


def _pick_tile(n, pref):
    t = min(n, pref)
    assert n % t == 0
    return t


def _layer(x, pos, limits, past, s0, w, *, chunk, alpha):
    b, t, d = x.shape
    n = b * t
    x2 = x.reshape(n, d)
    tm = _pick_tile(n, TOKEN_TILE)
    (aq, k32, v32, kb, vb, iq, ikw, ikb, gq, gk, gv, glog, gr, gate) = _proj(
        x2, pos, w["w_in"], w["w_fa"], w["b_fa"], tm)

    kb3, vb3, ikb3 = kb.reshape(b, t, LANES), vb.reshape(b, t, LANES), ikb.reshape(b, t, LANES)
    if past is not None:
        ck, cv, cik = past
        p = ck.shape[1]
        kb3 = jnp.concatenate([ck.reshape(b, p, LANES).astype(BF16), kb3], axis=1)
        vb3 = jnp.concatenate([cv.reshape(b, p, LANES).astype(BF16), vb3], axis=1)
        ikb3 = jnp.concatenate([cik.astype(BF16), ikb3[:, :, :IDX_DIM]], axis=1)
    topk = min(IDX_TOPK, kb3.shape[1] // 4)
    o_a = _attention(aq, iq, ikw, limits, kb3, vb3, ikb3, b, topk)

    o_b, s_fin = _gla(gq, gk, gv, glog, gr, w["g_gla_norm"], s0, b, chunk)

    h1, h1b, s1t, s2t = _merge(x2, o_a, o_b, gate, w["w_pa"], w["w_pb"], w["w_out"],
                               w["ln1_g"], w["ln1_b"], w["w_pq"], w["pk1"], w["pk2"], alpha, tm)
    cnt, e1, rank, e2 = _select(s1t, s2t, tm)
    y = _peer(h1, h1b, cnt, e1, rank, e2, w["pu"], w["pv"], w["ln2_g"], w["ln2_b"], alpha,
              _pick_tile(n, PEER_TOKEN_TILE))

    k_out = k32.reshape(b, t, A_KV_HEADS, A_HEAD_DIM)
    v_out = v32.reshape(b, t, A_KV_HEADS, A_HEAD_DIM)
    ik_out = ikw[:, :IDX_DIM].reshape(b, t, IDX_DIM)
    return y.reshape(b, t, d), k_out, v_out, ik_out, s_fin


def kernel(x_prompt, x_sample, cache_k, cache_v, cache_idx_k, state_gla, w_in, w_fa, b_fa, g_gla_norm,
           w_pa, w_pb, w_out, ln1_g, ln1_b, w_pq, pk1, pk2, pu, pv, ln2_g, ln2_b):
    depth = w_in.shape[0]
    alpha = (2.0 * depth) ** 0.25
    bp, tp, _ = x_prompt.shape
    bs, ts, _ = x_sample.shape
    past_len = cache_k.shape[2]
    pos_p = jnp.arange(tp)
    pos_s = past_len + jnp.arange(ts)
    lim_p = (np.arange(tp) // CHUNK + 1) * CHUNK
    lim_s = np.full((ts,), past_len + ts)
    names = ("w_in", "w_fa", "b_fa", "g_gla_norm", "w_pa", "w_pb", "w_out", "ln1_g", "ln1_b",
             "w_pq", "pk1", "pk2", "pu", "pv", "ln2_g", "ln2_b")
    stacked = (w_in, w_fa, b_fa, g_gla_norm, w_pa, w_pb, w_out, ln1_g, ln1_b, w_pq, pk1, pk2, pu, pv, ln2_g, ln2_b)
    hp, hs = x_prompt, x_sample
    outs_p, outs_s = [], []
    for l in range(depth):
        w = {nm: a[l] for nm, a in zip(names, stacked)}
        s0 = jnp.zeros((bp, G_HEADS, G_KEY_DIM, G_VAL_DIM), F32)
        hp, *rest = _layer(hp, pos_p, lim_p, None, s0, w, chunk=CHUNK, alpha=alpha)
        outs_p.append(rest)
        hs, *rest = _layer(hs, pos_s, lim_s, (cache_k[l], cache_v[l], cache_idx_k[l]), state_gla[l], w,
                           chunk=ts, alpha=alpha)
        outs_s.append(rest)
    stack = lambda outs, i: jnp.stack([o[i] for o in outs])
    return (hp, hs, stack(outs_p, 0), stack(outs_p, 1), stack(outs_p, 2), stack(outs_p, 3),
            stack(outs_s, 0), stack(outs_s, 1), stack(outs_s, 2), stack(outs_s, 3))
```

```python
import functools
import math

import numpy as np
import jax
import jax.numpy as jnp
from jax import lax
from jax.experimental import pallas as pl
from jax.experimental.pallas import tpu as pltpu

F32 = jnp.float32
BF16 = jnp.bfloat16
I32 = jnp.int32

LANES = 128
SUBLANES = 8
VMEM_LIMIT = 56 << 20

CHUNK = 64
A_HEADS = 8
A_KV_HEADS = 2
A_HEAD_DIM = 64
A_GROUP = A_HEADS // A_KV_HEADS
IDX_HEADS = 4
IDX_DIM = 64
IDX_TOPK = 256
ROPE_THETA = 10000.0
G_HEADS = 4
G_KEY_DIM = 128
G_VAL_DIM = 128
G_LOWRANK = 16
G_TAU = 16.0
P_HEADS = 8
P_NKEYS = 128
P_HALF = 128
P_TOPK = 16
LN_EPS = 1e-5

W_AQ = A_HEADS * A_HEAD_DIM
W_AK = A_KV_HEADS * A_HEAD_DIM
W_IQ = IDX_HEADS * IDX_DIM
W_G = G_HEADS * G_KEY_DIM
IN_SIZES = (W_AQ, W_AK, W_AK, W_IQ, IDX_DIM, IDX_HEADS, W_G, W_G, W_G, G_LOWRANK, W_G, None)

TOKEN_TILE = 256
PEER_TOKEN_TILE = 512
Q_BLOCK = 128
KEY_TILE = 512

INT_MIN = -(2 ** 31)
NEG_INF_KEY = -2139095041
NEG_BIG = -1e30


def _dot(a, b):
    return jnp.dot(a, b, preferred_element_type=F32)


def _dot_nt(a, b):
    return lax.dot_general(a, b, (((1,), (1,)), ((), ())), preferred_element_type=F32)


def _dot_tn(a, b):
    return lax.dot_general(a, b, (((0,), (0,)), ((), ())), preferred_element_type=F32)


def _sort_key(x):
    bits = pltpu.bitcast(x, I32)
    key = jnp.where(bits < 0, bits ^ 0x7FFFFFFF, bits)
    return jnp.where(key == -1, 0, key)


def _fold_rows(x, op):
    x = x.reshape(x.shape[0] // SUBLANES, SUBLANES, x.shape[1])
    while x.shape[0] > 1:
        half = x.shape[0] // 2
        folded = op(x[:half], x[half:2 * half])
        x = folded if x.shape[0] == 2 * half else jnp.concatenate([folded, x[2 * half:]], axis=0)
    return x[0]


def _params(sem):
    return pltpu.CompilerParams(dimension_semantics=sem, vmem_limit_bytes=VMEM_LIMIT)


_PG_AQ, _PG_K, _PG_V, _PG_IQ, _PG_IKW, _PG_GQ, _PG_GK, _PG_GV, _PG_GF, _PG_GR, _PG_GATE = range(11)


def _pack_layout(d_model):
    widths = [W_AQ, W_AK, W_AK, W_IQ, LANES, W_G, W_G, W_G, LANES, W_G, 2 * d_model]
    offs = np.concatenate([[0], np.cumsum(widths)]).tolist()
    return widths, offs


def _pack_w_in(w_in):
    d = w_in.shape[0]
    sizes = list(IN_SIZES[:-1]) + [2 * d]
    cuts = np.cumsum(sizes)[:-1].tolist()
    aq, ak, av, iq, ik, iw, gq, gk, gv, gf, gr, gate = jnp.split(w_in, cuts, axis=1)
    z = lambda n: jnp.zeros((d, n), w_in.dtype)
    ikw = jnp.concatenate([ik, iw, z(LANES - IDX_DIM - IDX_HEADS)], axis=1)
    gfp = jnp.concatenate([gf, z(LANES - G_LOWRANK)], axis=1)
    return jnp.concatenate([aq, ak, av, iq, ikw, gq, gk, gv, gfp, gr, gate], axis=1).astype(BF16)


def _rope_tables(pos):
    half = A_HEAD_DIM // 2
    inv = ROPE_THETA ** (-jnp.arange(half, dtype=F32) / half)
    ang = pos.astype(F32)[:, None] * inv[None, :]
    c, s = jnp.cos(ang), jnp.sin(ang)
    return jnp.concatenate([c, c, c, c], -1), jnp.concatenate([-s, s, -s, s], -1)


def _proj_kernel(x_ref, w_ref, wfa_ref, bfa_ref, cos_ref, sin_ref,
                 aq_ref, k_ref, v_ref, kb_ref, vb_ref, iq_ref, ikw_ref, ikb_ref,
                 gq_ref, gk_ref, gv_ref, glog_ref, gr_ref, gate_ref, *, offs, widths):
    xb = x_ref[...].astype(BF16)
    cos = cos_ref[...]
    sin = sin_ref[...]
    lane = lax.broadcasted_iota(I32, cos.shape, 1)
    first_half = (lane & (A_HEAD_DIM // 2)) == 0

    def proj(g):
        return _dot(xb, w_ref[:, offs[g]:offs[g] + widths[g]])

    def rope_slab(y):
        fwd = pltpu.roll(y, LANES - A_HEAD_DIM // 2, 1)
        bwd = pltpu.roll(y, A_HEAD_DIM // 2, 1)
        return y * cos + jnp.where(first_half, fwd, bwd) * sin

    def rope(y):
        return [rope_slab(y[:, s * LANES:(s + 1) * LANES]) for s in range(y.shape[1] // LANES)]

    for s, slab in enumerate(rope(proj(_PG_AQ))):
        aq_ref[:, s * LANES:(s + 1) * LANES] = (slab * (A_HEAD_DIM ** -0.5)).astype(BF16)
    k = rope(proj(_PG_K))[0]
    k_ref[...] = k
    kb_ref[...] = k.astype(BF16)
    v = proj(_PG_V)
    v_ref[...] = v
    vb_ref[...] = v.astype(BF16)
    for s, slab in enumerate(rope(proj(_PG_IQ))):
        iq_ref[:, s * LANES:(s + 1) * LANES] = (slab * (IDX_DIM ** -0.5)).astype(BF16)
    raw = proj(_PG_IKW)
    ikw = jnp.where(lane < IDX_DIM, rope_slab(raw), raw * (IDX_HEADS ** -0.5))
    ikw_ref[...] = ikw
    ikb_ref[...] = ikw.astype(BF16)
    gq_ref[...] = proj(_PG_GQ) * (G_KEY_DIM ** -0.5)
    gk_ref[...] = proj(_PG_GK)
    gv_ref[...] = proj(_PG_GV)
    z = _dot(proj(_PG_GF).astype(BF16), wfa_ref[...]) + bfa_ref[...]
    glog_ref[...] = (jnp.minimum(z, 0.0) - jnp.log1p(jnp.exp(-jnp.abs(z)))) * (1.0 / G_TAU)
    gr_ref[...] = proj(_PG_GR)
    gate_ref[...] = proj(_PG_GATE)


def _proj(x2, pos, w_in, w_fa, b_fa, tm):
    n, d = x2.shape
    t = pos.shape[0]
    widths, offs = _pack_layout(d)
    wp = _pack_w_in(w_in)
    wfa = jnp.concatenate([w_fa, jnp.zeros((LANES - G_LOWRANK, W_G), w_fa.dtype)], 0).astype(BF16)
    cos, sin = _rope_tables(pos)
    if tm > t:
        cos, sin = jnp.tile(cos, (tm // t, 1)), jnp.tile(sin, (tm // t, 1))
    nper = cos.shape[0] // tm
    row = lambda w: pl.BlockSpec((tm, w), lambda i: (i, 0))
    const = lambda a: pl.BlockSpec(a.shape, lambda i: (0, 0))
    tab = pl.BlockSpec((tm, LANES), lambda i: (i % nper, 0))
    outs = [(W_AQ, BF16), (LANES, F32), (LANES, F32), (LANES, BF16), (LANES, BF16), (W_IQ, BF16),
            (LANES, F32), (LANES, BF16), (W_G, F32), (W_G, F32), (W_G, F32), (W_G, F32), (W_G, F32),
            (2 * d, F32)]
    bfa = b_fa.reshape(1, W_G)
    return pl.pallas_call(
        functools.partial(_proj_kernel, offs=offs, widths=widths),
        grid=(n // tm,),
        in_specs=[row(d), const(wp), const(wfa), const(bfa), tab, tab],
        out_specs=[row(w) for w, _ in outs],
        out_shape=[jax.ShapeDtypeStruct((n, w), dt) for w, dt in outs],
        compiler_params=_params(("parallel",)),
        name="proj",
    )(x2, wp, wfa, bfa, cos, sin)


def _attn_kernel(aq_ref, iq_ref, ikw_ref, lim_ref, kb_ref, vt_ref, ikb_ref, o_ref,
                 keys_ref, qs_ref, iqs_ref, acc_ref, *, kt_w, nkt, topk, idx_bits):
    qb = Q_BLOCK

    for h in range(A_HEADS):
        qs_ref[h // A_GROUP, (h % A_GROUP) * qb:(h % A_GROUP + 1) * qb, :] = \
            aq_ref[:, h * A_HEAD_DIM:(h + 1) * A_HEAD_DIM]
    for h in range(IDX_HEADS):
        iqs_ref[h * qb:(h + 1) * qb, :] = iq_ref[:, h * IDX_DIM:(h + 1) * IDX_DIM]

    ikw_t = ikw_ref[...].T
    iw_rows = [ikw_t[IDX_DIM + h:IDX_DIM + h + 1, :] for h in range(IDX_HEADS)]
    lim = lim_ref[0, 0:1, :]
    sub = lax.broadcasted_iota(I32, (kt_w, qb), 0)

    def score_tile(kt, _):
        base = pl.multiple_of(kt * kt_w, kt_w)
        ik_t = ikb_ref[0, pl.ds(base, kt_w), :][:, :IDX_DIM]
        s = jnp.maximum(_dot_nt(ik_t, iqs_ref[...]), 0.0)
        score = jnp.zeros((kt_w, qb), F32)
        for h in range(IDX_HEADS):
            score = score + s[:, h * qb:(h + 1) * qb] * iw_rows[h]
        keys_ref[pl.ds(base, kt_w), :] = jnp.where(sub + base < lim, _sort_key(score), NEG_INF_KEY)
        return 0

    lax.fori_loop(0, nkt, score_tile, 0, unroll=min(nkt, 2))

    def count(pred):
        acc = jnp.zeros((SUBLANES, qb), F32)
        for kt in range(nkt):
            acc = acc + _fold_rows(pred(keys_ref[kt * kt_w:(kt + 1) * kt_w, :], sub + kt * kt_w), jnp.add)
        return jnp.sum(acc, axis=0, keepdims=True)

    def count_ge(t_row):
        return count(lambda kk, idx: jnp.where(kk >= t_row, 1.0, 0.0))

    kf = float(topk)
    thr = jnp.where(count_ge(jnp.zeros((1, qb), I32)) >= kf, 0, INT_MIN).astype(I32)

    def thr_bit(i, t):
        cand = t + jnp.left_shift(jnp.int32(1), 30 - i)
        return jnp.where(count_ge(cand) >= kf, cand, t)

    thr = lax.fori_loop(0, 31, thr_bit, thr)
    n_gt = count_ge(thr + 1)
    n_eq = count_ge(thr) - n_gt
    need = kf - n_gt
    finite = thr > NEG_INF_KEY
    excess = jnp.where(finite, jnp.where(n_eq > need, 1.0, 0.0), 0.0)

    def count_eq_below(j_row):
        return count(lambda kk, idx: jnp.where(kk == thr, jnp.where(idx < j_row, 1.0, 0.0), 0.0))

    def resolve_ties():
        def bit(i, jc):
            cand = jc + jnp.left_shift(jnp.int32(1), idx_bits - 1 - i)
            return jnp.where(count_eq_below(cand) <= need - 1.0, cand, jc)
        jc = lax.fori_loop(0, idx_bits, bit, jnp.zeros((1, qb), I32))
        return jnp.where(finite, jc, -1)

    cut = lax.cond(jnp.max(excess) > 0.0, resolve_ties,
                   lambda: jnp.where(finite, 2 ** 30, -1).astype(I32))

    acc_ref[...] = jnp.zeros(acc_ref.shape, F32)
    gq = A_GROUP * qb

    def attend(kt, carry):
        base = pl.multiple_of(kt * kt_w, kt_w)
        k_t = kb_ref[0, pl.ds(base, kt_w), :]
        kk = keys_ref[pl.ds(base, kt_w), :]
        tie = jnp.where(sub + base <= cut, 0.0, NEG_BIG)
        bias = jnp.where(kk > thr, 0.0, jnp.where(kk == thr, tie, NEG_BIG))
        bias = jnp.concatenate([bias] * A_GROUP, axis=1)
        out = []
        for n in range(A_KV_HEADS):
            m_old, l_old = carry[2 * n], carry[2 * n + 1]
            logits = _dot_nt(k_t[:, n * A_HEAD_DIM:(n + 1) * A_HEAD_DIM], qs_ref[n]) + bias
            m_new = jnp.maximum(m_old, jnp.max(_fold_rows(logits, jnp.maximum), axis=0, keepdims=True))
            alpha = jnp.exp(m_old - m_new)
            p = jnp.exp(logits - m_new)
            l_new = alpha * l_old + jnp.sum(_fold_rows(p, jnp.add), axis=0, keepdims=True)
            v_t = vt_ref[0, n * A_HEAD_DIM:(n + 1) * A_HEAD_DIM, pl.ds(base, kt_w)]
            acc_ref[n] = alpha * acc_ref[n] + _dot(v_t, p.astype(BF16))
            out += [m_new, l_new]
        return tuple(out)

    init = (jnp.full((1, gq), NEG_BIG, F32), jnp.zeros((1, gq), F32)) * A_KV_HEADS
    fin = lax.fori_loop(0, nkt, attend, init, unroll=min(nkt, 2))
    l_row = [fin[2 * n + 1] for n in range(A_KV_HEADS)]

    o_t = jnp.concatenate([acc_ref[n] / l_row[n] for n in range(A_KV_HEADS)], axis=0)
    for n in range(A_KV_HEADS):
        for g in range(A_GROUP):
            h = n * A_GROUP + g
            blk = o_t[n * A_HEAD_DIM:(n + 1) * A_HEAD_DIM, g * qb:(g + 1) * qb]
            o_ref[:, h * A_HEAD_DIM:(h + 1) * A_HEAD_DIM] = blk.T.astype(BF16)


def _attention(aq, iq, ikw, limits, kb, vb, ikb, b, topk):
    qb, kt_w = Q_BLOCK, KEY_TILE
    tq = aq.shape[0] // b
    tq_pad = -(-tq // qb) * qb
    limits = np.asarray(limits)
    if tq_pad != tq:
        padq = lambda a: jnp.pad(a.reshape(b, tq, -1), ((0, 0), (0, tq_pad - tq), (0, 0))).reshape(b * tq_pad, -1)
        aq, iq, ikw = padq(aq), padq(iq), padq(ikw)
        limits = np.concatenate([limits, np.full((tq_pad - tq,), limits[-1])])
    l_all = kb.shape[1]
    l_pad = -(-l_all // kt_w) * kt_w
    if l_pad != l_all:
        padl = lambda a: jnp.pad(a, ((0, 0), (0, l_pad - l_all), (0, 0)))
        kb, vb, ikb = padl(kb), padl(vb), padl(ikb)
    vt = jnp.swapaxes(vb, 1, 2)
    nq = tq_pad // qb
    lim_blk = limits.reshape(nq, qb)
    nkt = np.minimum(-(-lim_blk.max(axis=1) // kt_w), l_pad // kt_w)
    lim = jnp.asarray(np.broadcast_to(lim_blk[:, None, :], (nq, SUBLANES, qb)).astype(np.int32))
    idx_bits = max(1, int(math.ceil(math.log2(l_pad))))
    keys = lambda a: pl.BlockSpec((1,) + a.shape[1:], lambda bi, j: (bi, 0, 0))
    gq = A_GROUP * qb
    runs, j0 = [], 0
    for j in range(1, nq + 1):
        if j == nq or nkt[j] != nkt[j0]:
            runs.append((j0, j - j0, int(nkt[j0])))
            j0 = j
    outs = []
    for j0, nj, n_tiles in runs:
        qrow = lambda w, j0=j0: pl.BlockSpec((qb, w), lambda bi, j: (bi * nq + j0 + j, 0))
        outs.append(pl.pallas_call(
            functools.partial(_attn_kernel, kt_w=kt_w, nkt=n_tiles, topk=topk, idx_bits=idx_bits),
            grid=(b, nj),
            in_specs=[qrow(W_AQ), qrow(W_IQ), qrow(LANES),
                      pl.BlockSpec((1, SUBLANES, qb), lambda bi, j, j0=j0: (j0 + j, 0, 0)),
                      keys(kb), keys(vt), keys(ikb)],
            out_specs=pl.BlockSpec((qb, W_AQ), lambda bi, j, nj=nj: (bi * nj + j, 0)),
            out_shape=jax.ShapeDtypeStruct((b * nj * qb, W_AQ), BF16),
            scratch_shapes=[pltpu.VMEM((n_tiles * kt_w, qb), I32),
                            pltpu.VMEM((A_KV_HEADS, gq, A_HEAD_DIM), BF16),
                            pltpu.VMEM((IDX_HEADS * qb, IDX_DIM), BF16),
                            pltpu.VMEM((A_KV_HEADS, A_HEAD_DIM, gq), F32)],
            compiler_params=_params(("parallel", "arbitrary")),
            name="attn",
        )(aq, iq, ikw, lim, kb, vt, ikb).reshape(b, nj * qb, W_AQ))
    o = outs[0] if len(outs) == 1 else jnp.concatenate(outs, axis=1)
    return o[:, :tq].reshape(b * tq, W_AQ)


def _gla_constants(c):
    nlev = int(math.log2(c))
    t = np.arange(c)
    mats = [(t[None, :] <= t[:, None])]
    masks = [np.eye(c, dtype=bool)]
    for lev in range(nlev):
        m = c >> (lev + 1)
        ref_row = (t // (2 * m)) * 2 * m + m
        mats.append(t[None, :] <= ref_row[:, None])
        upper = (t & m) != 0
        same = (t[:, None] // (2 * m)) == (t[None, :] // (2 * m))
        masks.append(same & upper[:, None] & ~upper[None, :])
    return (jnp.asarray(np.concatenate(mats, 0).astype(np.float32), BF16),
            jnp.asarray(np.stack(masks).astype(np.float32)), nlev)


def _gla_kernel(q_ref, k_ref, v_ref, g_ref, gr_ref, gn_ref, mst_ref, msk_ref, s0_ref,
                ob_ref, sfin_ref, st_ref, *, c, nlev):
    i = pl.program_id(1)
    hk = G_KEY_DIM

    @pl.when(i == 0)
    def _():
        for h in range(G_HEADS):
            st_ref[h] = s0_ref[0, h].T

    g = g_ref[...]
    g_hi = g.astype(BF16)
    r1 = g - g_hi.astype(F32)
    g_mid = r1.astype(BF16)
    g_lo = (r1 - g_mid.astype(F32)).astype(BF16)
    mst = mst_ref[...]
    bs = _dot(mst, g_hi) + _dot(mst, g_mid) + _dot(mst, g_lo)
    b = bs[0:c]
    q = q_ref[...]
    k = k_ref[...]
    vb = v_ref[...].astype(BF16)
    row = lax.broadcasted_iota(I32, q.shape, 0)
    hs = lambda a, h: a[:, h * hk:(h + 1) * hk]

    qb = q.astype(BF16)
    kb = k.astype(BF16)
    attn = [_dot_nt(hs(qb, h), hs(kb, h)) * msk_ref[0] for h in range(G_HEADS)]
    for lev in range(nlev):
        m = c >> (lev + 1)
        upper = (row & m) != 0
        d = b - bs[(lev + 1) * c:(lev + 2) * c]
        e = jnp.exp(jnp.where(upper, d, -d))
        qt = jnp.where(upper, q * e, 0.0).astype(BF16)
        kt = jnp.where(upper, 0.0, k * e).astype(BF16)
        mk = msk_ref[lev + 1]
        for h in range(G_HEADS):
            attn[h] = attn[h] + _dot_nt(hs(qt, h), hs(kt, h)) * mk

    qe = (q * jnp.exp(b)).astype(BF16)
    b_last = b[c - 1:c, :]
    khat = (k * jnp.exp(b_last - b)).astype(BF16)
    dec = jnp.exp(b_last)
    gr = gr_ref[...]
    gn = gn_ref[...]
    for h in range(G_HEADS):
        st = st_ref[h]
        o = _dot_nt(hs(qe, h), st.astype(BF16)) + _dot(attn[h].astype(BF16), hs(vb, h))
        st_ref[h] = st * hs(dec, h) + _dot_tn(hs(vb, h), hs(khat, h))
        ms = jnp.mean(o * o, axis=1, keepdims=True)
        grh = hs(gr, h)
        of = o * lax.rsqrt(ms + LN_EPS) * hs(gn, h) * (grh / (1.0 + jnp.exp(-grh)))
        ob_ref[:, h * hk:(h + 1) * hk] = of.astype(BF16)

    @pl.when(i == pl.num_programs(1) - 1)
    def _():
        for h in range(G_HEADS):
            sfin_ref[0, h] = st_ref[h].T


def _gla(gq, gk, gv, glog, gr, g_norm, s0, b, c):
    n = gq.shape[0]
    nc = n // (b * c)
    mst, msk, nlev = _gla_constants(c)
    row = pl.BlockSpec((c, W_G), lambda bi, i: (bi * nc + i, 0))
    const = lambda a: pl.BlockSpec(a.shape, lambda bi, i: (0,) * a.ndim)
    st_spec = pl.BlockSpec((1, G_HEADS, G_KEY_DIM, G_VAL_DIM), lambda bi, i: (bi, 0, 0, 0))
    gn = g_norm.reshape(1, W_G)
    return pl.pallas_call(
        functools.partial(_gla_kernel, c=c, nlev=nlev),
        grid=(b, nc),
        in_specs=[row, row, row, row, row, const(gn), const(mst), const(msk), st_spec],
        out_specs=[row, st_spec],
        out_shape=[jax.ShapeDtypeStruct((n, W_G), BF16),
                   jax.ShapeDtypeStruct((b, G_HEADS, G_KEY_DIM, G_VAL_DIM), F32)],
        scratch_shapes=[pltpu.VMEM((G_HEADS, G_VAL_DIM, G_KEY_DIM), F32)],
        compiler_params=_params(("parallel", "arbitrary")),
        name="gla",
    )(gq, gk, gv, glog, gr, gn, mst, msk, s0)


def _layer_norm(z, g, b):
    mu = jnp.mean(z, axis=1, keepdims=True)
    zc = z - mu
    var = jnp.mean(zc * zc, axis=1, keepdims=True)
    return zc * lax.rsqrt(var + LN_EPS) * g + b


def _sigmoid(x):
    return 1.0 / (1.0 + jnp.exp(-x))


def _merge_kernel(x_ref, oa_ref, ob_ref, gate_ref, wpa_ref, wpb_ref, wout_ref, g1_ref, b1_ref,
                  wpq_ref, pk1_ref, pk2_ref, h1_ref, h1b_ref, s1t_ref, s2t_ref, *, alpha, d):
    ya = _dot(oa_ref[...], wpa_ref[...])
    yb = _dot(ob_ref[...], wpb_ref[...])
    m = _sigmoid(gate_ref[:, :d]) * ya + _sigmoid(gate_ref[:, d:]) * yb
    mix = _dot(m.astype(BF16), wout_ref[...])
    h1 = _layer_norm(alpha * x_ref[...] + mix, g1_ref[...], b1_ref[...])
    h1_ref[...] = h1
    h1b = h1.astype(BF16)
    h1b_ref[...] = h1b
    qp = _dot(h1b, wpq_ref[...])
    for h in range(P_HEADS):
        for half, (pk_ref, st_ref) in enumerate(((pk1_ref, s1t_ref), (pk2_ref, s2t_ref))):
            c0 = (2 * h + half) * P_HALF
            st_ref[h * P_NKEYS:(h + 1) * P_NKEYS, :] = _dot_nt(pk_ref[h], qp[:, c0:c0 + P_HALF].astype(BF16))


def _merge(x2, oa, ob, gate, w_pa, w_pb, w_out, ln_g, ln_b, w_pq, pk1, pk2, alpha, tm):
    n, d = x2.shape
    hk = P_HEADS * P_NKEYS
    row = lambda w: pl.BlockSpec((tm, w), lambda i: (i, 0))
    col = pl.BlockSpec((hk, tm), lambda i: (0, i))
    const = lambda a: pl.BlockSpec(a.shape, lambda i: (0,) * a.ndim)
    ws = [w_pa.astype(BF16), w_pb.astype(BF16), w_out.astype(BF16), ln_g.reshape(1, d), ln_b.reshape(1, d),
          w_pq.astype(BF16), pk1.astype(BF16), pk2.astype(BF16)]
    return pl.pallas_call(
        functools.partial(_merge_kernel, alpha=alpha, d=d),
        grid=(n // tm,),
        in_specs=[row(d), row(W_AQ), row(W_G), row(2 * d)] + [const(w) for w in ws],
        out_specs=[row(d), row(d), col, col],
        out_shape=[jax.ShapeDtypeStruct((n, d), F32), jax.ShapeDtypeStruct((n, d), BF16),
                   jax.ShapeDtypeStruct((hk, n), F32), jax.ShapeDtypeStruct((hk, n), F32)],
        compiler_params=_params(("parallel",)),
        name="merge",
    )(x2, oa, ob, gate, *ws)


def _top_desc(s, count):
    tops = []
    for r in range(count):
        m = jnp.max(_fold_rows(s, jnp.maximum), axis=0, keepdims=True)
        tops.append(m)
        if r + 1 < count:
            s = jnp.where(s == m, -jnp.inf, s)
    return tops


def _select_kernel(s1t_ref, s2t_ref, cnt_ref, e1_ref, rank_ref, e2_ref):
    nk = P_NKEYS

    def head(h, _):
        r0 = pl.multiple_of(h * nk, nk)
        rows = pl.ds(r0, nk)
        s1 = s1t_ref[rows, :]
        s2 = s2t_ref[rows, :]
        v1 = _top_desc(s1, P_TOPK)
        v2 = _top_desc(s2, P_TOPK)
        pairs = [(a, b) for a in range(P_TOPK) for b in range(P_TOPK // (a + 1))]
        fill = [jnp.full_like(v1[0], -jnp.inf)] * (-len(pairs) % SUBLANES)
        cand = jnp.concatenate([v1[a] + v2[b] for a, b in pairs] + fill, axis=0)
        work, seen = cand, jnp.zeros_like(v1[0])
        tau = jnp.full_like(v1[0], -jnp.inf)
        for _ in range(P_TOPK):
            m = jnp.max(work, axis=0, keepdims=True)
            hit = work == m
            seen = seen + jnp.sum(jnp.where(hit, 1.0, 0.0), axis=0, keepdims=True)
            tau = jnp.maximum(tau, jnp.where(seen >= float(P_TOPK), m, -jnp.inf))
            work = jnp.where(hit, -jnp.inf, work)
        cmax = v1[0] + v2[0]
        zsum = jnp.sum(jnp.where(cand >= tau, jnp.exp(cand - cmax), 0.0), axis=0, keepdims=True)
        v2all = jnp.concatenate(v2, axis=0)
        cnt = jnp.zeros(s1.shape, F32)
        rank = jnp.full(s2.shape, float(P_TOPK), F32)
        for a in range(P_TOPK):
            cnt_a = jnp.sum(jnp.where(v1[a] + v2all >= tau, 1.0, 0.0), axis=0, keepdims=True)
            cnt = jnp.where(s1 == v1[a], cnt_a, cnt)
            rank = jnp.where(s2 == v2[a], float(a), rank)
        cnt_ref[rows, :] = cnt
        rank_ref[rows, :] = rank
        e1_ref[rows, :] = jnp.exp(s1 - v1[0]) / zsum
        e2_ref[rows, :] = jnp.exp(s2 - v2[0])
        return 0

    lax.fori_loop(0, P_HEADS, head, 0)


def _select(s1t, s2t, tn):
    hk, n = s1t.shape
    col = pl.BlockSpec((hk, tn), lambda i: (0, i))
    return pl.pallas_call(
        _select_kernel,
        grid=(n // tn,),
        in_specs=[col, col],
        out_specs=[col, col, col, col],
        out_shape=[jax.ShapeDtypeStruct((hk, n), F32)] * 4,
        compiler_params=_params(("parallel",)),
        name="select",
    )(s1t, s2t)


I1_PER_STEP = 8
E_PER_STEP = I1_PER_STEP * P_NKEYS
I2_BLOCK = 16


def _gelu(x):
    return 0.5 * x * (1.0 + lax.erf(x * (2.0 ** -0.5)))


def _peer_kernel(hb_ref, pu_ref, pvt_ref, cnt_ref, e1_ref, rank_ref, e2_ref, h1_ref, g2_ref, b2_ref,
                 y_ref, a_ref, gw_ref, acc_ref, rank_s, e2_s, *, tn, alpha):
    g = pl.program_id(1)

    @pl.when(g == 0)
    def _():
        acc_ref[...] = jnp.zeros(acc_ref.shape, F32)
        rank_s[...] = rank_ref[...].astype(BF16)
        e2_s[...] = e2_ref[...].astype(BF16)

    a_ref[...] = _dot_nt(pu_ref[...], hb_ref[...])

    assert I1_PER_STEP == SUBLANES
    for lt in range(tn // LANES):
        ls = slice(lt * LANES, (lt + 1) * LANES)
        grp = lambda ref, h: ref[pl.ds(pl.multiple_of(h * P_NKEYS + g * I1_PER_STEP, SUBLANES), SUBLANES), ls]
        cnt8 = [grp(cnt_ref, h) for h in range(P_HEADS)]
        e18 = [grp(e1_ref, h) for h in range(P_HEADS)]
        for j in range(I1_PER_STEP):
            bcast = lambda a8: jnp.broadcast_to(a8[j:j + 1, :], (I2_BLOCK, LANES)).astype(BF16)
            cnt = [bcast(cnt8[h]) for h in range(P_HEADS)]
            e1 = [bcast(e18[h]) for h in range(P_HEADS)]
            for i2b in range(P_NKEYS // I2_BLOCK):
                w = jnp.zeros((I2_BLOCK, LANES), BF16)
                for h in range(P_HEADS):
                    rows = slice(h * P_NKEYS + i2b * I2_BLOCK, h * P_NKEYS + (i2b + 1) * I2_BLOCK)
                    w = w + jnp.where(rank_s[rows, ls] < cnt[h], e2_s[rows, ls] * e1[h], jnp.zeros((), BF16))
                arow = slice(j * P_NKEYS + i2b * I2_BLOCK, j * P_NKEYS + (i2b + 1) * I2_BLOCK)
                gw_ref[arow, ls] = (w.astype(F32) * _gelu(a_ref[arow, ls])).astype(BF16)

    acc_ref[...] += _dot(pvt_ref[0], gw_ref[...])

    @pl.when(g == pl.num_programs(1) - 1)
    def _():
        y_ref[...] = _layer_norm(alpha * h1_ref[...] + acc_ref[...].T, g2_ref[...], b2_ref[...])


def _peer(h1, h1b, cnt, e1, rank, e2, pu, pv, ln_g, ln_b, alpha, tn):
    n, d = h1.shape
    hk = P_HEADS * P_NKEYS
    ng = pu.shape[0] // E_PER_STEP
    row = lambda w: pl.BlockSpec((tn, w), lambda i, g: (i, 0))
    col = pl.BlockSpec((hk, tn), lambda i, g: (0, i))
    const = pl.BlockSpec((1, d), lambda i, g: (0, 0))
    pvt = pv.astype(BF16).reshape(ng, E_PER_STEP, d).transpose(0, 2, 1)
    return pl.pallas_call(
        functools.partial(_peer_kernel, tn=tn, alpha=alpha),
        grid=(n // tn, ng),
        in_specs=[row(d), pl.BlockSpec((E_PER_STEP, d), lambda i, g: (g, 0)),
                  pl.BlockSpec((1, d, E_PER_STEP), lambda i, g: (g, 0, 0)),
                  col, col, col, col, row(d), const, const],
        out_specs=row(d),
        out_shape=jax.ShapeDtypeStruct((n, d), F32),
        scratch_shapes=[pltpu.VMEM((E_PER_STEP, tn), F32), pltpu.VMEM((E_PER_STEP, tn), BF16),
                        pltpu.VMEM((d, tn), F32), pltpu.VMEM((hk, tn), BF16), pltpu.VMEM((hk, tn), BF16)],
        compiler_params=_params(("parallel", "arbitrary")),
        name="peer",
    )(h1b, pu.astype(BF16), pvt, cnt, e1, rank, e2, h1, ln_g.reshape(1, d), ln_b.reshape(1, d))


def _pick_tile(n, pref):
    t = min(n, pref)
    assert n % t == 0
    return t


def _layer(x, pos, limits, past, s0, w, *, chunk, alpha):
    b, t, d = x.shape
    n = b * t
    x2 = x.reshape(n, d)
    tm = _pick_tile(n, TOKEN_TILE)
    (aq, k32, v32, kb, vb, iq, ikw, ikb, gq, gk, gv, glog, gr, gate) = _proj(
        x2, pos, w["w_in"], w["w_fa"], w["b_fa"], tm)

    kb3, vb3, ikb3 = kb.reshape(b, t, LANES), vb.reshape(b, t, LANES), ikb.reshape(b, t, LANES)
    if past is not None:
        ck, cv, cik = past
        p = ck.shape[1]
        kb3 = jnp.concatenate([ck.reshape(b, p, LANES).astype(BF16), kb3], axis=1)
        vb3 = jnp.concatenate([cv.reshape(b, p, LANES).astype(BF16), vb3], axis=1)
        ikb3 = jnp.concatenate([cik.astype(BF16), ikb3[:, :, :IDX_DIM]], axis=1)
    topk = min(IDX_TOPK, kb3.shape[1] // 4)
    o_a = _attention(aq, iq, ikw, limits, kb3, vb3, ikb3, b, topk)

    o_b, s_fin = _gla(gq, gk, gv, glog, gr, w["g_gla_norm"], s0, b, chunk)

    h1, h1b, s1t, s2t = _merge(x2, o_a, o_b, gate, w["w_pa"], w["w_pb"], w["w_out"],
                               w["ln1_g"], w["ln1_b"], w["w_pq"], w["pk1"], w["pk2"], alpha, tm)
    cnt, e1, rank, e2 = _select(s1t, s2t, tm)
    y = _peer(h1, h1b, cnt, e1, rank, e2, w["pu"], w["pv"], w["ln2_g"], w["ln2_b"], alpha,
              _pick_tile(n, PEER_TOKEN_TILE))

    k_out = k32.reshape(b, t, A_KV_HEADS, A_HEAD_DIM)
    v_out = v32.reshape(b, t, A_KV_HEADS, A_HEAD_DIM)
    ik_out = ikw[:, :IDX_DIM].reshape(b, t, IDX_DIM)
    return y.reshape(b, t, d), k_out, v_out, ik_out, s_fin


def kernel(x_prompt, x_sample, cache_k, cache_v, cache_idx_k, state_gla, w_in, w_fa, b_fa, g_gla_norm,
           w_pa, w_pb, w_out, ln1_g, ln1_b, w_pq, pk1, pk2, pu, pv, ln2_g, ln2_b):
    depth = w_in.shape[0]
    alpha = (2.0 * depth) ** 0.25
    bp, tp, _ = x_prompt.shape
    bs, ts, _ = x_sample.shape
    past_len = cache_k.shape[2]
    pos_p = jnp.arange(tp)
    pos_s = past_len + jnp.arange(ts)
    lim_p = (np.arange(tp) // CHUNK + 1) * CHUNK
    lim_s = np.full((ts,), past_len + ts)
    names = ("w_in", "w_fa", "b_fa", "g_gla_norm", "w_pa", "w_pb", "w_out", "ln1_g", "ln1_b",
             "w_pq", "pk1", "pk2", "pu", "pv", "ln2_g", "ln2_b")
    stacked = (w_in, w_fa, b_fa, g_gla_norm, w_pa, w_pb, w_out, ln1_g, ln1_b, w_pq, pk1, pk2, pu, pv, ln2_g, ln2_b)
    hp, hs = x_prompt, x_sample
    outs_p, outs_s = [], []
    for l in range(depth):
        w = {nm: a[l] for nm, a in zip(names, stacked)}
        s0 = jnp.zeros((bp, G_HEADS, G_KEY_DIM, G_VAL_DIM), F32)
        hp, *rest = _layer(hp, pos_p, lim_p, None, s0, w, chunk=CHUNK, alpha=alpha)
        outs_p.append(rest)
        hs, *rest = _layer(hs, pos_s, lim_s, (cache_k[l], cache_v[l], cache_idx_k[l]), state_gla[l], w,
                           chunk=ts, alpha=alpha)
        outs_s.append(rest)
    stack = lambda outs, i: jnp.stack([o[i] for o in outs])
    return (hp, hs, stack(outs_p, 0), stack(outs_p, 1), stack(outs_p, 2), stack(outs_p, 3),
            stack(outs_s, 0), stack(outs_s, 1), stack(outs_s, 2), stack(outs_s, 3))
```

```python
import functools
import math

import numpy as np
import jax
import jax.numpy as jnp
from jax import lax
from jax.experimental import pallas as pl
from jax.experimental.pallas import tpu as pltpu

F32 = jnp.float32
BF16 = jnp.bfloat16
I32 = jnp.int32

LANES = 128
SUBLANES = 8
VMEM_LIMIT = 56 << 20

CHUNK = 64
A_HEADS = 8
A_KV_HEADS = 2
A_HEAD_DIM = 64
A_GROUP = A_HEADS // A_KV_HEADS
IDX_HEADS = 4
IDX_DIM = 64
IDX_TOPK = 256
ROPE_THETA = 10000.0
G_HEADS = 4
G_KEY_DIM = 128
G_VAL_DIM = 128
G_LOWRANK = 16
G_TAU = 16.0
P_HEADS = 8
P_NKEYS = 128
P_HALF = 128
P_TOPK = 16
LN_EPS = 1e-5

W_AQ = A_HEADS * A_HEAD_DIM
W_AK = A_KV_HEADS * A_HEAD_DIM
W_IQ = IDX_HEADS * IDX_DIM
W_G = G_HEADS * G_KEY_DIM
IN_SIZES = (W_AQ, W_AK, W_AK, W_IQ, IDX_DIM, IDX_HEADS, W_G, W_G, W_G, G_LOWRANK, W_G, None)

TOKEN_TILE = 256
PEER_TOKEN_TILE = 512
Q_BLOCK = 128
KEY_TILE = 512

INT_MIN = -(2 ** 31)
NEG_INF_KEY = -2139095041
NEG_BIG = -1e30


def _dot(a, b):
    return jnp.dot(a, b, preferred_element_type=F32)


def _dot_nt(a, b):
    return lax.dot_general(a, b, (((1,), (1,)), ((), ())), preferred_element_type=F32)


def _dot_tn(a, b):
    return lax.dot_general(a, b, (((0,), (0,)), ((), ())), preferred_element_type=F32)


def _sort_key(x):
    bits = pltpu.bitcast(x, I32)
    key = jnp.where(bits < 0, bits ^ 0x7FFFFFFF, bits)
    return jnp.where(key == -1, 0, key)


def _fold_rows(x, op):
    x = x.reshape(x.shape[0] // SUBLANES, SUBLANES, x.shape[1])
    while x.shape[0] > 1:
        half = x.shape[0] // 2
        folded = op(x[:half], x[half:2 * half])
        x = folded if x.shape[0] == 2 * half else jnp.concatenate([folded, x[2 * half:]], axis=0)
    return x[0]


def _params(sem):
    return pltpu.CompilerParams(dimension_semantics=sem, vmem_limit_bytes=VMEM_LIMIT)


_PG_AQ, _PG_K, _PG_V, _PG_IQ, _PG_IKW, _PG_GQ, _PG_GK, _PG_GV, _PG_GF, _PG_GR, _PG_GATE = range(11)


def _pack_layout(d_model):
    widths = [W_AQ, W_AK, W_AK, W_IQ, LANES, W_G, W_G, W_G, LANES, W_G, 2 * d_model]
    offs = np.concatenate([[0], np.cumsum(widths)]).tolist()
    return widths, offs


def _pack_w_in(w_in):
    d = w_in.shape[0]
    sizes = list(IN_SIZES[:-1]) + [2 * d]
    cuts = np.cumsum(sizes)[:-1].tolist()
    aq, ak, av, iq, ik, iw, gq, gk, gv, gf, gr, gate = jnp.split(w_in, cuts, axis=1)
    z = lambda n: jnp.zeros((d, n), w_in.dtype)
    ikw = jnp.concatenate([ik, iw, z(LANES - IDX_DIM - IDX_HEADS)], axis=1)
    gfp = jnp.concatenate([gf, z(LANES - G_LOWRANK)], axis=1)
    return jnp.concatenate([aq, ak, av, iq, ikw, gq, gk, gv, gfp, gr, gate], axis=1).astype(BF16)


def _rope_tables(pos):
    half = A_HEAD_DIM // 2
    inv = ROPE_THETA ** (-jnp.arange(half, dtype=F32) / half)
    ang = pos.astype(F32)[:, None] * inv[None, :]
    c, s = jnp.cos(ang), jnp.sin(ang)
    return jnp.concatenate([c, c, c, c], -1), jnp.concatenate([-s, s, -s, s], -1)


def _proj_kernel(x_ref, w_ref, wfa_ref, bfa_ref, cos_ref, sin_ref,
                 aq_ref, k_ref, v_ref, kb_ref, vb_ref, iq_ref, ikw_ref, ikb_ref,
                 gq_ref, gk_ref, gv_ref, glog_ref, gr_ref, gate_ref, *, offs, widths):
    xb = x_ref[...].astype(BF16)
    cos = cos_ref[...]
    sin = sin_ref[...]
    lane = lax.broadcasted_iota(I32, cos.shape, 1)
    first_half = (lane & (A_HEAD_DIM // 2)) == 0

    def proj(g):
        return _dot(xb, w_ref[:, offs[g]:offs[g] + widths[g]])

    def rope_slab(y):
        fwd = pltpu.roll(y, LANES - A_HEAD_DIM // 2, 1)
        bwd = pltpu.roll(y, A_HEAD_DIM // 2, 1)
        return y * cos + jnp.where(first_half, fwd, bwd) * sin

    def rope(y):
        return [rope_slab(y[:, s * LANES:(s + 1) * LANES]) for s in range(y.shape[1] // LANES)]

    for s, slab in enumerate(rope(proj(_PG_AQ))):
        aq_ref[:, s * LANES:(s + 1) * LANES] = (slab * (A_HEAD_DIM ** -0.5)).astype(BF16)
    k = rope(proj(_PG_K))[0]
    k_ref[...] = k
    kb_ref[...] = k.astype(BF16)
    v = proj(_PG_V)
    v_ref[...] = v
    vb_ref[...] = v.astype(BF16)
    for s, slab in enumerate(rope(proj(_PG_IQ))):
        iq_ref[:, s * LANES:(s + 1) * LANES] = (slab * (IDX_DIM ** -0.5)).astype(BF16)
    raw = proj(_PG_IKW)
    ikw = jnp.where(lane < IDX_DIM, rope_slab(raw), raw * (IDX_HEADS ** -0.5))
    ikw_ref[...] = ikw
    ikb_ref[...] = ikw.astype(BF16)
    gq_ref[...] = proj(_PG_GQ) * (G_KEY_DIM ** -0.5)
    gk_ref[...] = proj(_PG_GK)
    gv_ref[...] = proj(_PG_GV)
    z = _dot(proj(_PG_GF).astype(BF16), wfa_ref[...]) + bfa_ref[...]
    glog_ref[...] = (jnp.minimum(z, 0.0) - jnp.log1p(jnp.exp(-jnp.abs(z)))) * (1.0 / G_TAU)
    gr_ref[...] = proj(_PG_GR)
    gate_ref[...] = proj(_PG_GATE)


def _proj(x2, pos, w_in, w_fa, b_fa, tm):
    n, d = x2.shape
    t = pos.shape[0]
    widths, offs = _pack_layout(d)
    wp = _pack_w_in(w_in)
    wfa = jnp.concatenate([w_fa, jnp.zeros((LANES - G_LOWRANK, W_G), w_fa.dtype)], 0).astype(BF16)
    cos, sin = _rope_tables(pos)
    if tm > t:
        cos, sin = jnp.tile(cos, (tm // t, 1)), jnp.tile(sin, (tm // t, 1))
    nper = cos.shape[0] // tm
    row = lambda w: pl.BlockSpec((tm, w), lambda i: (i, 0))
    const = lambda a: pl.BlockSpec(a.shape, lambda i: (0, 0))
    tab = pl.BlockSpec((tm, LANES), lambda i: (i % nper, 0))
    outs = [(W_AQ, BF16), (LANES, F32), (LANES, F32), (LANES, BF16), (LANES, BF16), (W_IQ, BF16),
            (LANES, F32), (LANES, BF16), (W_G, F32), (W_G, F32), (W_G, F32), (W_G, F32), (W_G, F32),
            (2 * d, F32)]
    bfa = b_fa.reshape(1, W_G)
    return pl.pallas_call(
        functools.partial(_proj_kernel, offs=offs, widths=widths),
        grid=(n // tm,),
        in_specs=[row(d), const(wp), const(wfa), const(bfa), tab, tab],
        out_specs=[row(w) for w, _ in outs],
        out_shape=[jax.ShapeDtypeStruct((n, w), dt) for w, dt in outs],
        compiler_params=_params(("parallel",)),
        name="proj",
    )(x2, wp, wfa, bfa, cos, sin)


def _attn_kernel(aq_ref, iq_ref, ikw_ref, lim_ref, kb_ref, vt_ref, ikb_ref, o_ref,
                 keys_ref, qs_ref, iqs_ref, acc_ref, *, kt_w, nkt, topk, idx_bits):
    qb = Q_BLOCK

    for h in range(A_HEADS):
        qs_ref[h // A_GROUP, (h % A_GROUP) * qb:(h % A_GROUP + 1) * qb, :] = \
            aq_ref[:, h * A_HEAD_DIM:(h + 1) * A_HEAD_DIM]
    for h in range(IDX_HEADS):
        iqs_ref[h * qb:(h + 1) * qb, :] = iq_ref[:, h * IDX_DIM:(h + 1) * IDX_DIM]

    ikw_t = ikw_ref[...].T
    iw_rows = [ikw_t[IDX_DIM + h:IDX_DIM + h + 1, :] for h in range(IDX_HEADS)]
    lim = lim_ref[0, 0:1, :]
    sub = lax.broadcasted_iota(I32, (kt_w, qb), 0)

    def score_tile(kt, _):
        base = pl.multiple_of(kt * kt_w, kt_w)
        ik_t = ikb_ref[0, pl.ds(base, kt_w), :][:, :IDX_DIM]
        s = jnp.maximum(_dot_nt(ik_t, iqs_ref[...]), 0.0)
        score = jnp.zeros((kt_w, qb), F32)
        for h in range(IDX_HEADS):
            score = score + s[:, h * qb:(h + 1) * qb] * iw_rows[h]
        keys_ref[pl.ds(base, kt_w), :] = jnp.where(sub + base < lim, _sort_key(score), NEG_INF_KEY)
        return 0

    lax.fori_loop(0, nkt, score_tile, 0, unroll=min(nkt, 2))

    def count(pred):
        acc = jnp.zeros((SUBLANES, qb), F32)
        for kt in range(nkt):
            acc = acc + _fold_rows(pred(keys_ref[kt * kt_w:(kt + 1) * kt_w, :], sub + kt * kt_w), jnp.add)
        return jnp.sum(acc, axis=0, keepdims=True)

    def count_ge(t_row):
        return count(lambda kk, idx: jnp.where(kk >= t_row, 1.0, 0.0))

    kf = float(topk)
    thr = jnp.where(count_ge(jnp.zeros((1, qb), I32)) >= kf, 0, INT_MIN).astype(I32)

    def thr_bit(i, t):
        cand = t + jnp.left_shift(jnp.int32(1), 30 - i)
        return jnp.where(count_ge(cand) >= kf, cand, t)

    thr = lax.fori_loop(0, 31, thr_bit, thr)
    n_gt = count_ge(thr + 1)
    n_eq = count_ge(thr) - n_gt
    need = kf - n_gt
    finite = thr > NEG_INF_KEY
    excess = jnp.where(finite, jnp.where(n_eq > need, 1.0, 0.0), 0.0)

    def count_eq_below(j_row):
        return count(lambda kk, idx: jnp.where(kk == thr, jnp.where(idx < j_row, 1.0, 0.0), 0.0))

    def resolve_ties():
        def bit(i, jc):
            cand = jc + jnp.left_shift(jnp.int32(1), idx_bits - 1 - i)
            return jnp.where(count_eq_below(cand) <= need - 1.0, cand, jc)
        jc = lax.fori_loop(0, idx_bits, bit, jnp.zeros((1, qb), I32))
        return jnp.where(finite, jc, -1)

    cut = lax.cond(jnp.max(excess) > 0.0, resolve_ties,
                   lambda: jnp.where(finite, 2 ** 30, -1).astype(I32))

    acc_ref[...] = jnp.zeros(acc_ref.shape, F32)
    gq = A_GROUP * qb

    def attend(kt, carry):
        base = pl.multiple_of(kt * kt_w, kt_w)
        k_t = kb_ref[0, pl.ds(base, kt_w), :]
        kk = keys_ref[pl.ds(base, kt_w), :]
        tie = jnp.where(sub + base <= cut, 0.0, NEG_BIG)
        bias = jnp.where(kk > thr, 0.0, jnp.where(kk == thr, tie, NEG_BIG))
        bias = jnp.concatenate([bias] * A_GROUP, axis=1)
        out = []
        for n in range(A_KV_HEADS):
            m_old, l_old = carry[2 * n], carry[2 * n + 1]
            logits = _dot_nt(k_t[:, n * A_HEAD_DIM:(n + 1) * A_HEAD_DIM], qs_ref[n]) + bias
            m_new = jnp.maximum(m_old, jnp.max(_fold_rows(logits, jnp.maximum), axis=0, keepdims=True))
            alpha = jnp.exp(m_old - m_new)
            p = jnp.exp(logits - m_new)
            l_new = alpha * l_old + jnp.sum(_fold_rows(p, jnp.add), axis=0, keepdims=True)
            v_t = vt_ref[0, n * A_HEAD_DIM:(n + 1) * A_HEAD_DIM, pl.ds(base, kt_w)]
            acc_ref[n] = alpha * acc_ref[n] + _dot(v_t, p.astype(BF16))
            out += [m_new, l_new]
        return tuple(out)

    init = (jnp.full((1, gq), NEG_BIG, F32), jnp.zeros((1, gq), F32)) * A_KV_HEADS
    fin = lax.fori_loop(0, nkt, attend, init, unroll=min(nkt, 2))
    l_row = [fin[2 * n + 1] for n in range(A_KV_HEADS)]

    o_t = jnp.concatenate([acc_ref[n] / l_row[n] for n in range(A_KV_HEADS)], axis=0)
    for n in range(A_KV_HEADS):
        for g in range(A_GROUP):
            h = n * A_GROUP + g
            blk = o_t[n * A_HEAD_DIM:(n + 1) * A_HEAD_DIM, g * qb:(g + 1) * qb]
            o_ref[:, h * A_HEAD_DIM:(h + 1) * A_HEAD_DIM] = blk.T.astype(BF16)


def _attention(aq, iq, ikw, limits, kb, vb, ikb, b, topk):
    qb, kt_w = Q_BLOCK, KEY_TILE
    tq = aq.shape[0] // b
    tq_pad = -(-tq // qb) * qb
    limits = np.asarray(limits)
    if tq_pad != tq:
        padq = lambda a: jnp.pad(a.reshape(b, tq, -1), ((0, 0), (0, tq_pad - tq), (0, 0))).reshape(b * tq_pad, -1)
        aq, iq, ikw = padq(aq), padq(iq), padq(ikw)
        limits = np.concatenate([limits, np.full((tq_pad - tq,), limits[-1])])
    l_all = kb.shape[1]
    l_pad = -(-l_all // kt_w) * kt_w
    if l_pad != l_all:
        padl = lambda a: jnp.pad(a, ((0, 0), (0, l_pad - l_all), (0, 0)))
        kb, vb, ikb = padl(kb), padl(vb), padl(ikb)
    vt = jnp.swapaxes(vb, 1, 2)
    nq = tq_pad // qb
    lim_blk = limits.reshape(nq, qb)
    nkt = np.minimum(-(-lim_blk.max(axis=1) // kt_w), l_pad // kt_w)
    lim = jnp.asarray(np.broadcast_to(lim_blk[:, None, :], (nq, SUBLANES, qb)).astype(np.int32))
    idx_bits = max(1, int(math.ceil(math.log2(l_pad))))
    keys = lambda a: pl.BlockSpec((1,) + a.shape[1:], lambda bi, j: (bi, 0, 0))
    gq = A_GROUP * qb
    runs, j0 = [], 0
    for j in range(1, nq + 1):
        if j == nq or nkt[j] != nkt[j0]:
            runs.append((j0, j - j0, int(nkt[j0])))
            j0 = j
    outs = []
    for j0, nj, n_tiles in runs:
        qrow = lambda w, j0=j0: pl.BlockSpec((qb, w), lambda bi, j: (bi * nq + j0 + j, 0))
        outs.append(pl.pallas_call(
            functools.partial(_attn_kernel, kt_w=kt_w, nkt=n_tiles, topk=topk, idx_bits=idx_bits),
            grid=(b, nj),
            in_specs=[qrow(W_AQ), qrow(W_IQ), qrow(LANES),
                      pl.BlockSpec((1, SUBLANES, qb), lambda bi, j, j0=j0: (j0 + j, 0, 0)),
                      keys(kb), keys(vt), keys(ikb)],
            out_specs=pl.BlockSpec((qb, W_AQ), lambda bi, j, nj=nj: (bi * nj + j, 0)),
            out_shape=jax.ShapeDtypeStruct((b * nj * qb, W_AQ), BF16),
            scratch_shapes=[pltpu.VMEM((n_tiles * kt_w, qb), I32),
                            pltpu.VMEM((A_KV_HEADS, gq, A_HEAD_DIM), BF16),
                            pltpu.VMEM((IDX_HEADS * qb, IDX_DIM), BF16),
                            pltpu.VMEM((A_KV_HEADS, A_HEAD_DIM, gq), F32)],
            compiler_params=_params(("parallel", "arbitrary")),
            name="attn",
        )(aq, iq, ikw, lim, kb, vt, ikb).reshape(b, nj * qb, W_AQ))
    o = outs[0] if len(outs) == 1 else jnp.concatenate(outs, axis=1)
    return o[:, :tq].reshape(b * tq, W_AQ)


def _gla_constants(c):
    nlev = int(math.log2(c))
    t = np.arange(c)
    mats = [(t[None, :] <= t[:, None])]
    masks = [np.eye(c, dtype=bool)]
    for lev in range(nlev):
        m = c >> (lev + 1)
        ref_row = (t // (2 * m)) * 2 * m + m
        mats.append(t[None, :] <= ref_row[:, None])
        upper = (t & m) != 0
        same = (t[:, None] // (2 * m)) == (t[None, :] // (2 * m))
        masks.append(same & upper[:, None] & ~upper[None, :])
    return (jnp.asarray(np.concatenate(mats, 0).astype(np.float32), BF16),
            jnp.asarray(np.stack(masks).astype(np.float32)), nlev)


def _gla_kernel(q_ref, k_ref, v_ref, g_ref, gr_ref, gn_ref, mst_ref, msk_ref, s0_ref,
                ob_ref, sfin_ref, st_ref, *, c, nlev):
    i = pl.program_id(1)
    hk = G_KEY_DIM

    @pl.when(i == 0)
    def _():
        for h in range(G_HEADS):
            st_ref[h] = s0_ref[0, h].T

    g = g_ref[...]
    g_hi = g.astype(BF16)
    r1 = g - g_hi.astype(F32)
    g_mid = r1.astype(BF16)
    g_lo = (r1 - g_mid.astype(F32)).astype(BF16)
    mst = mst_ref[...]
    bs = _dot(mst, g_hi) + _dot(mst, g_mid) + _dot(mst, g_lo)
    b = bs[0:c]
    q = q_ref[...]
    k = k_ref[...]
    vb = v_ref[...].astype(BF16)
    row = lax.broadcasted_iota(I32, q.shape, 0)
    hs = lambda a, h: a[:, h * hk:(h + 1) * hk]

    qb = q.astype(BF16)
    kb = k.astype(BF16)
    attn = [_dot_nt(hs(qb, h), hs(kb, h)) * msk_ref[0] for h in range(G_HEADS)]
    for lev in range(nlev):
        m = c >> (lev + 1)
        upper = (row & m) != 0
        d = b - bs[(lev + 1) * c:(lev + 2) * c]
        e = jnp.exp(jnp.where(upper, d, -d))
        qt = jnp.where(upper, q * e, 0.0).astype(BF16)
        kt = jnp.where(upper, 0.0, k * e).astype(BF16)
        mk = msk_ref[lev + 1]
        for h in range(G_HEADS):
            attn[h] = attn[h] + _dot_nt(hs(qt, h), hs(kt, h)) * mk

    qe = (q * jnp.exp(b)).astype(BF16)
    b_last = b[c - 1:c, :]
    khat = (k * jnp.exp(b_last - b)).astype(BF16)
    dec = jnp.exp(b_last)
    gr = gr_ref[...]
    gn = gn_ref[...]
    for h in range(G_HEADS):
        st = st_ref[h]
        o = _dot_nt(hs(qe, h), st.astype(BF16)) + _dot(attn[h].astype(BF16), hs(vb, h))
        st_ref[h] = st * hs(dec, h) + _dot_tn(hs(vb, h), hs(khat, h))
        ms = jnp.mean(o * o, axis=1, keepdims=True)
        grh = hs(gr, h)
        of = o * lax.rsqrt(ms + LN_EPS) * hs(gn, h) * (grh / (1.0 + jnp.exp(-grh)))
        ob_ref[:, h * hk:(h + 1) * hk] = of.astype(BF16)

    @pl.when(i == pl.num_programs(1) - 1)
    def _():
        for h in range(G_HEADS):
            sfin_ref[0, h] = st_ref[h].T


def _gla(gq, gk, gv, glog, gr, g_norm, s0, b, c):
    n = gq.shape[0]
    nc = n // (b * c)
    mst, msk, nlev = _gla_constants(c)
    row = pl.BlockSpec((c, W_G), lambda bi, i: (bi * nc + i, 0))
    const = lambda a: pl.BlockSpec(a.shape, lambda bi, i: (0,) * a.ndim)
    st_spec = pl.BlockSpec((1, G_HEADS, G_KEY_DIM, G_VAL_DIM), lambda bi, i: (bi, 0, 0, 0))
    gn = g_norm.reshape(1, W_G)
    return pl.pallas_call(
        functools.partial(_gla_kernel, c=c, nlev=nlev),
        grid=(b, nc),
        in_specs=[row, row, row, row, row, const(gn), const(mst), const(msk), st_spec],
        out_specs=[row, st_spec],
        out_shape=[jax.ShapeDtypeStruct((n, W_G), BF16),
                   jax.ShapeDtypeStruct((b, G_HEADS, G_KEY_DIM, G_VAL_DIM), F32)],
        scratch_shapes=[pltpu.VMEM((G_HEADS, G_VAL_DIM, G_KEY_DIM), F32)],
        compiler_params=_params(("parallel", "arbitrary")),
        name="gla",
    )(gq, gk, gv, glog, gr, gn, mst, msk, s0)


def _layer_norm(z, g, b):
    mu = jnp.mean(z, axis=1, keepdims=True)
    zc = z - mu
    var = jnp.mean(zc * zc, axis=1, keepdims=True)
    return zc * lax.rsqrt(var + LN_EPS) * g + b


def _sigmoid(x):
    return 1.0 / (1.0 + jnp.exp(-x))


def _merge_kernel(x_ref, oa_ref, ob_ref, gate_ref, wpa_ref, wpb_ref, wout_ref, g1_ref, b1_ref,
                  wpq_ref, pk1_ref, pk2_ref, h1_ref, h1b_ref, s1t_ref, s2t_ref, *, alpha, d):
    ya = _dot(oa_ref[...], wpa_ref[...])
    yb = _dot(ob_ref[...], wpb_ref[...])
    m = _sigmoid(gate_ref[:, :d]) * ya + _sigmoid(gate_ref[:, d:]) * yb
    mix = _dot(m.astype(BF16), wout_ref[...])
    h1 = _layer_norm(alpha * x_ref[...] + mix, g1_ref[...], b1_ref[...])
    h1_ref[...] = h1
    h1b = h1.astype(BF16)
    h1b_ref[...] = h1b
    qp = _dot(h1b, wpq_ref[...])
    for h in range(P_HEADS):
        for half, (pk_ref, st_ref) in enumerate(((pk1_ref, s1t_ref), (pk2_ref, s2t_ref))):
            c0 = (2 * h + half) * P_HALF
            st_ref[h * P_NKEYS:(h + 1) * P_NKEYS, :] = _dot_nt(pk_ref[h], qp[:, c0:c0 + P_HALF].astype(BF16))


def _merge(x2, oa, ob, gate, w_pa, w_pb, w_out, ln_g, ln_b, w_pq, pk1, pk2, alpha, tm):
    n, d = x2.shape
    hk = P_HEADS * P_NKEYS
    row = lambda w: pl.BlockSpec((tm, w), lambda i: (i, 0))
    col = pl.BlockSpec((hk, tm), lambda i: (0, i))
    const = lambda a: pl.BlockSpec(a.shape, lambda i: (0,) * a.ndim)
    ws = [w_pa.astype(BF16), w_pb.astype(BF16), w_out.astype(BF16), ln_g.reshape(1, d), ln_b.reshape(1, d),
          w_pq.astype(BF16), pk1.astype(BF16), pk2.astype(BF16)]
    return pl.pallas_call(
        functools.partial(_merge_kernel, alpha=alpha, d=d),
        grid=(n // tm,),
        in_specs=[row(d), row(W_AQ), row(W_G), row(2 * d)] + [const(w) for w in ws],
        out_specs=[row(d), row(d), col, col],
        out_shape=[jax.ShapeDtypeStruct((n, d), F32), jax.ShapeDtypeStruct((n, d), BF16),
                   jax.ShapeDtypeStruct((hk, n), F32), jax.ShapeDtypeStruct((hk, n), F32)],
        compiler_params=_params(("parallel",)),
        name="merge",
    )(x2, oa, ob, gate, *ws)


def _top_desc(s, count):
    tops = []
    for r in range(count):
        m = jnp.max(_fold_rows(s, jnp.maximum), axis=0, keepdims=True)
        tops.append(m)
        if r + 1 < count:
            s = jnp.where(s == m, -jnp.inf, s)
    return tops


def _select_kernel(s1t_ref, s2t_ref, cnt_ref, e1_ref, rank_ref, e2_ref):
    nk = P_NKEYS

    def head(h, _):
        r0 = pl.multiple_of(h * nk, nk)
        rows = pl.ds(r0, nk)
        s1 = s1t_ref[rows, :]
        s2 = s2t_ref[rows, :]
        v1 = _top_desc(s1, P_TOPK)
        v2 = _top_desc(s2, P_TOPK)
        pairs = [(a, b) for a in range(P_TOPK) for b in range(P_TOPK // (a + 1))]
        fill = [jnp.full_like(v1[0], -jnp.inf)] * (-len(pairs) % SUBLANES)
        cand = jnp.concatenate([v1[a] + v2[b] for a, b in pairs] + fill, axis=0)
        work, seen = cand, jnp.zeros_like(v1[0])
        tau = jnp.full_like(v1[0], -jnp.inf)
        for _ in range(P_TOPK):
            m = jnp.max(work, axis=0, keepdims=True)
            hit = work == m
            seen = seen + jnp.sum(jnp.where(hit, 1.0, 0.0), axis=0, keepdims=True)
            tau = jnp.maximum(tau, jnp.where(seen >= float(P_TOPK), m, -jnp.inf))
            work = jnp.where(hit, -jnp.inf, work)
        cmax = v1[0] + v2[0]
        zsum = jnp.sum(jnp.where(cand >= tau, jnp.exp(cand - cmax), 0.0), axis=0, keepdims=True)
        v2all = jnp.concatenate(v2, axis=0)
        cnt = jnp.zeros(s1.shape, F32)
        rank = jnp.full(s2.shape, float(P_TOPK), F32)
        for a in range(P_TOPK):
            cnt_a = jnp.sum(jnp.where(v1[a] + v2all >= tau, 1.0, 0.0), axis=0, keepdims=True)
            cnt = jnp.where(s1 == v1[a], cnt_a, cnt)
            rank = jnp.where(s2 == v2[a], float(a), rank)
        cnt_ref[rows, :] = cnt
        rank_ref[rows, :] = rank
        e1_ref[rows, :] = jnp.exp(s1 - v1[0]) / zsum
        e2_ref[rows, :] = jnp.exp(s2 - v2[0])
        return 0

    lax.fori_loop(0, P_HEADS, head, 0)


def _select(s1t, s2t, tn):
    hk, n = s1t.shape
    col = pl.BlockSpec((hk, tn), lambda i: (0, i))
    return pl.pallas_call(
        _select_kernel,
        grid=(n // tn,),
        in_specs=[col, col],
        out_specs=[col, col, col, col],
        out_shape=[jax.ShapeDtypeStruct((hk, n), F32)] * 4,
        compiler_params=_params(("parallel",)),
        name="select",
    )(s1t, s2t)


I1_PER_STEP = 8
E_PER_STEP = I1_PER_STEP * P_NKEYS
I2_BLOCK = 16


def _gelu(x):
    return 0.5 * x * (1.0 + lax.erf(x * (2.0 ** -0.5)))


def _peer_steps(t, n_steps, ng):
    item = lambda d: jnp.clip(t - d, 0, n_steps - 1)
    return [(item(d) // ng, item(d) % ng) for d in range(3)]


def _peer_kernel(hb_ref, pu_ref, pvt_ref, cnt_ref, e1_ref, rank_ref, e2_ref, h1_ref, g2_ref, b2_ref,
                 y_ref, a_ref, gw_ref, acc_ref, rank_s, e2_s, *, tn, alpha, n_steps, ng):
    t = pl.program_id(0)
    (_, _), (_, g), (_, g_out) = _peer_steps(t, n_steps, ng)
    slot = t % 2
    prev = 1 - slot

    @pl.when(t == 0)
    def _():
        a_ref[...] = jnp.zeros(a_ref.shape, F32)
        gw_ref[...] = jnp.zeros(gw_ref.shape, BF16)

    @pl.when(g == 0)
    def _():
        rank_s[...] = rank_ref[...].astype(BF16)
        e2_s[...] = e2_ref[...].astype(BF16)

    a_ref[slot] = _dot_nt(pu_ref[...], hb_ref[...])

    assert I1_PER_STEP == SUBLANES
    for lt in range(tn // LANES):
        ls = slice(lt * LANES, (lt + 1) * LANES)
        grp = lambda ref, h: ref[pl.ds(pl.multiple_of(h * P_NKEYS + g * I1_PER_STEP, SUBLANES), SUBLANES), ls]
        cnt8 = [grp(cnt_ref, h) for h in range(P_HEADS)]
        e18 = [grp(e1_ref, h) for h in range(P_HEADS)]
        for j in range(I1_PER_STEP):
            bcast = lambda a8: jnp.broadcast_to(a8[j:j + 1, :], (I2_BLOCK, LANES)).astype(BF16)
            cnt = [bcast(cnt8[h]) for h in range(P_HEADS)]
            e1 = [bcast(e18[h]) for h in range(P_HEADS)]
            for i2b in range(P_NKEYS // I2_BLOCK):
                w = jnp.zeros((I2_BLOCK, LANES), BF16)
                for h in range(P_HEADS):
                    rows = slice(h * P_NKEYS + i2b * I2_BLOCK, h * P_NKEYS + (i2b + 1) * I2_BLOCK)
                    w = w + jnp.where(rank_s[rows, ls] < cnt[h], e2_s[rows, ls] * e1[h], jnp.zeros((), BF16))
                arow = slice(j * P_NKEYS + i2b * I2_BLOCK, j * P_NKEYS + (i2b + 1) * I2_BLOCK)
                gw_ref[prev, arow, ls] = (w.astype(F32) * _gelu(a_ref[prev, arow, ls])).astype(BF16)

    part = _dot(pvt_ref[0], gw_ref[slot])
    acc_ref[...] = jnp.where(g_out == 0, part, acc_ref[...] + part)

    @pl.when(jnp.logical_and(g_out == ng - 1, t >= 2))
    def _():
        y_ref[...] = _layer_norm(alpha * h1_ref[...] + acc_ref[...].T, g2_ref[...], b2_ref[...])


def _peer(h1, h1b, cnt, e1, rank, e2, pu, pv, ln_g, ln_b, alpha, tn):
    n, d = h1.shape
    hk = P_HEADS * P_NKEYS
    ng = pu.shape[0] // E_PER_STEP
    n_steps = (n // tn) * ng
    stage = lambda k, f: (lambda t: f(*_peer_steps(t, n_steps, ng)[k]))
    const = pl.BlockSpec((1, d), lambda t: (0, 0))
    routing = pl.BlockSpec((hk, tn), stage(1, lambda i, g: (0, i)))
    pvt = pv.astype(BF16).reshape(ng, E_PER_STEP, d).transpose(0, 2, 1)
    return pl.pallas_call(
        functools.partial(_peer_kernel, tn=tn, alpha=alpha, n_steps=n_steps, ng=ng),
        grid=(n_steps + 2,),
        in_specs=[pl.BlockSpec((tn, d), stage(0, lambda i, g: (i, 0))),
                  pl.BlockSpec((E_PER_STEP, d), stage(0, lambda i, g: (g, 0))),
                  pl.BlockSpec((1, d, E_PER_STEP), stage(2, lambda i, g: (g, 0, 0))),
                  routing, routing, routing, routing,
                  pl.BlockSpec((tn, d), stage(2, lambda i, g: (i, 0))), const, const],
        out_specs=pl.BlockSpec((tn, d), stage(2, lambda i, g: (i, 0))),
        out_shape=jax.ShapeDtypeStruct((n, d), F32),
        scratch_shapes=[pltpu.VMEM((2, E_PER_STEP, tn), F32), pltpu.VMEM((2, E_PER_STEP, tn), BF16),
                        pltpu.VMEM((d, tn), F32), pltpu.VMEM((hk, tn), BF16), pltpu.VMEM((hk, tn), BF16)],
        compiler_params=_params(("arbitrary",)),
        name="peer",
    )(h1b, pu.astype(BF16), pvt, cnt, e1, rank, e2, h1, ln_g.reshape(1, d), ln_b.reshape(1, d))


def _pick_tile(n, pref):
    t = min(n, pref)
    assert n % t == 0
    return t


def _layer(x, pos, limits, past, s0, w, *, chunk, alpha):
    b, t, d = x.shape
    n = b * t
    x2 = x.reshape(n, d)
    tm = _pick_tile(n, TOKEN_TILE)
    (aq, k32, v32, kb, vb, iq, ikw, ikb, gq, gk, gv, glog, gr, gate) = _proj(
        x2, pos, w["w_in"], w["w_fa"], w["b_fa"], tm)

    kb3, vb3, ikb3 = kb.reshape(b, t, LANES), vb.reshape(b, t, LANES), ikb.reshape(b, t, LANES)
    if past is not None:
        ck, cv, cik = past
        p = ck.shape[1]
        kb3 = jnp.concatenate([ck.reshape(b, p, LANES).astype(BF16), kb3], axis=1)
        vb3 = jnp.concatenate([cv.reshape(b, p, LANES).astype(BF16), vb3], axis=1)
        ikb3 = jnp.concatenate([cik.astype(BF16), ikb3[:, :, :IDX_DIM]], axis=1)
    topk = min(IDX_TOPK, kb3.shape[1] // 4)
    o_a = _attention(aq, iq, ikw, limits, kb3, vb3, ikb3, b, topk)

    o_b, s_fin = _gla(gq, gk, gv, glog, gr, w["g_gla_norm"], s0, b, chunk)

    h1, h1b, s1t, s2t = _merge(x2, o_a, o_b, gate, w["w_pa"], w["w_pb"], w["w_out"],
                               w["ln1_g"], w["ln1_b"], w["w_pq"], w["pk1"], w["pk2"], alpha, tm)
    cnt, e1, rank, e2 = _select(s1t, s2t, tm)
    y = _peer(h1, h1b, cnt, e1, rank, e2, w["pu"], w["pv"], w["ln2_g"], w["ln2_b"], alpha,
              _pick_tile(n, PEER_TOKEN_TILE))

    k_out = k32.reshape(b, t, A_KV_HEADS, A_HEAD_DIM)
    v_out = v32.reshape(b, t, A_KV_HEADS, A_HEAD_DIM)
    ik_out = ikw[:, :IDX_DIM].reshape(b, t, IDX_DIM)
    return y.reshape(b, t, d), k_out, v_out, ik_out, s_fin


def kernel(x_prompt, x_sample, cache_k, cache_v, cache_idx_k, state_gla, w_in, w_fa, b_fa, g_gla_norm,
           w_pa, w_pb, w_out, ln1_g, ln1_b, w_pq, pk1, pk2, pu, pv, ln2_g, ln2_b):
    depth = w_in.shape[0]
    alpha = (2.0 * depth) ** 0.25
    bp, tp, _ = x_prompt.shape
    bs, ts, _ = x_sample.shape
    past_len = cache_k.shape[2]
    pos_p = jnp.arange(tp)
    pos_s = past_len + jnp.arange(ts)
    lim_p = (np.arange(tp) // CHUNK + 1) * CHUNK
    lim_s = np.full((ts,), past_len + ts)
    names = ("w_in", "w_fa", "b_fa", "g_gla_norm", "w_pa", "w_pb", "w_out", "ln1_g", "ln1_b",
             "w_pq", "pk1", "pk2", "pu", "pv", "ln2_g", "ln2_b")
    stacked = (w_in, w_fa, b_fa, g_gla_norm, w_pa, w_pb, w_out, ln1_g, ln1_b, w_pq, pk1, pk2, pu, pv, ln2_g, ln2_b)
    hp, hs = x_prompt, x_sample
    outs_p, outs_s = [], []
    for l in range(depth):
        w = {nm: a[l] for nm, a in zip(names, stacked)}
        s0 = jnp.zeros((bp, G_HEADS, G_KEY_DIM, G_VAL_DIM), F32)
        hp, *rest = _layer(hp, pos_p, lim_p, None, s0, w, chunk=CHUNK, alpha=alpha)
        outs_p.append(rest)
        hs, *rest = _layer(hs, pos_s, lim_s, (cache_k[l], cache_v[l], cache_idx_k[l]), state_gla[l], w,
                           chunk=ts, alpha=alpha)
        outs_s.append(rest)
    stack = lambda outs, i: jnp.stack([o[i] for o in outs])
    return (hp, hs, stack(outs_p, 0), stack(outs_p, 1), stack(outs_p, 2), stack(outs_p, 3),
            stack(outs_s, 0), stack(outs_s, 1), stack(outs_s, 2), stack(outs_s, 3))
```

```python
import functools
import math

import numpy as np
import jax
import jax.numpy as jnp
from jax import lax
from jax.experimental import pallas as pl
from jax.experimental.pallas import tpu as pltpu

F32 = jnp.float32
BF16 = jnp.bfloat16
I32 = jnp.int32

LANES = 128
SUBLANES = 8
VMEM_LIMIT = 56 << 20

CHUNK = 64
A_HEADS = 8
A_KV_HEADS = 2
A_HEAD_DIM = 64
A_GROUP = A_HEADS // A_KV_HEADS
IDX_HEADS = 4
IDX_DIM = 64
IDX_TOPK = 256
ROPE_THETA = 10000.0
G_HEADS = 4
G_KEY_DIM = 128
G_VAL_DIM = 128
G_LOWRANK = 16
G_TAU = 16.0
P_HEADS = 8
P_NKEYS = 128
P_HALF = 128
P_TOPK = 16
LN_EPS = 1e-5

W_AQ = A_HEADS * A_HEAD_DIM
W_AK = A_KV_HEADS * A_HEAD_DIM
W_IQ = IDX_HEADS * IDX_DIM
W_G = G_HEADS * G_KEY_DIM
IN_SIZES = (W_AQ, W_AK, W_AK, W_IQ, IDX_DIM, IDX_HEADS, W_G, W_G, W_G, G_LOWRANK, W_G, None)

TOKEN_TILE = 256
PEER_TOKEN_TILE = 512
Q_BLOCK = 128
KEY_TILE = 512
GLA_SEQS = 4

INT_MIN = -(2 ** 31)
NEG_INF_KEY = -2139095041
NEG_BIG = -1e30


def _dot(a, b):
    return jnp.dot(a, b, preferred_element_type=F32)


def _dot_nt(a, b):
    return lax.dot_general(a, b, (((1,), (1,)), ((), ())), preferred_element_type=F32)


def _dot_tn(a, b):
    return lax.dot_general(a, b, (((0,), (0,)), ((), ())), preferred_element_type=F32)


def _sort_key(x):
    bits = pltpu.bitcast(x, I32)
    key = jnp.where(bits < 0, bits ^ 0x7FFFFFFF, bits)
    return jnp.where(key == -1, 0, key)


def _fold_rows(x, op):
    x = x.reshape(x.shape[0] // SUBLANES, SUBLANES, x.shape[1])
    while x.shape[0] > 1:
        half = x.shape[0] // 2
        folded = op(x[:half], x[half:2 * half])
        x = folded if x.shape[0] == 2 * half else jnp.concatenate([folded, x[2 * half:]], axis=0)
    return x[0]


def _params(sem):
    return pltpu.CompilerParams(dimension_semantics=sem, vmem_limit_bytes=VMEM_LIMIT)


_PG_AQ, _PG_K, _PG_V, _PG_IQ, _PG_IKW, _PG_GQ, _PG_GK, _PG_GV, _PG_GF, _PG_GR, _PG_GATE = range(11)


def _pack_layout(d_model):
    widths = [W_AQ, W_AK, W_AK, W_IQ, LANES, W_G, W_G, W_G, LANES, W_G, 2 * d_model]
    offs = np.concatenate([[0], np.cumsum(widths)]).tolist()
    return widths, offs


def _pack_w_in(w_in):
    d = w_in.shape[0]
    sizes = list(IN_SIZES[:-1]) + [2 * d]
    cuts = np.cumsum(sizes)[:-1].tolist()
    aq, ak, av, iq, ik, iw, gq, gk, gv, gf, gr, gate = jnp.split(w_in, cuts, axis=1)
    z = lambda n: jnp.zeros((d, n), w_in.dtype)
    ikw = jnp.concatenate([ik, iw, z(LANES - IDX_DIM - IDX_HEADS)], axis=1)
    gfp = jnp.concatenate([gf, z(LANES - G_LOWRANK)], axis=1)
    return jnp.concatenate([aq, ak, av, iq, ikw, gq, gk, gv, gfp, gr, gate], axis=1).astype(BF16)


def _rope_tables(pos):
    half = A_HEAD_DIM // 2
    inv = ROPE_THETA ** (-jnp.arange(half, dtype=F32) / half)
    ang = pos.astype(F32)[:, None] * inv[None, :]
    c, s = jnp.cos(ang), jnp.sin(ang)
    return jnp.concatenate([c, c, c, c], -1), jnp.concatenate([-s, s, -s, s], -1)


def _proj_kernel(x_ref, w_ref, wfa_ref, bfa_ref, cos_ref, sin_ref,
                 aq_ref, k_ref, v_ref, kb_ref, vb_ref, iq_ref, ikw_ref, ikb_ref,
                 gq_ref, gk_ref, gv_ref, glog_ref, gr_ref, gate_ref, *, offs, widths):
    xb = x_ref[...].astype(BF16)
    cos = cos_ref[...]
    sin = sin_ref[...]
    lane = lax.broadcasted_iota(I32, cos.shape, 1)
    first_half = (lane & (A_HEAD_DIM // 2)) == 0

    def proj(g):
        return _dot(xb, w_ref[:, offs[g]:offs[g] + widths[g]])

    def rope_slab(y):
        fwd = pltpu.roll(y, LANES - A_HEAD_DIM // 2, 1)
        bwd = pltpu.roll(y, A_HEAD_DIM // 2, 1)
        return y * cos + jnp.where(first_half, fwd, bwd) * sin

    def rope(y):
        return [rope_slab(y[:, s * LANES:(s + 1) * LANES]) for s in range(y.shape[1] // LANES)]

    for s, slab in enumerate(rope(proj(_PG_AQ))):
        aq_ref[:, s * LANES:(s + 1) * LANES] = (slab * (A_HEAD_DIM ** -0.5)).astype(BF16)
    k = rope(proj(_PG_K))[0]
    k_ref[...] = k
    kb_ref[...] = k.astype(BF16)
    v = proj(_PG_V)
    v_ref[...] = v
    vb_ref[...] = v.astype(BF16)
    for s, slab in enumerate(rope(proj(_PG_IQ))):
        iq_ref[:, s * LANES:(s + 1) * LANES] = (slab * (IDX_DIM ** -0.5)).astype(BF16)
    raw = proj(_PG_IKW)
    ikw = jnp.where(lane < IDX_DIM, rope_slab(raw), raw * (IDX_HEADS ** -0.5))
    ikw_ref[...] = ikw
    ikb_ref[...] = ikw.astype(BF16)
    gq_ref[...] = proj(_PG_GQ) * (G_KEY_DIM ** -0.5)
    gk_ref[...] = proj(_PG_GK)
    gv_ref[...] = proj(_PG_GV)
    z = _dot(proj(_PG_GF).astype(BF16), wfa_ref[...]) + bfa_ref[...]
    glog_ref[...] = (jnp.minimum(z, 0.0) - jnp.log1p(jnp.exp(-jnp.abs(z)))) * (1.0 / G_TAU)
    gr_ref[...] = proj(_PG_GR)
    gate_ref[...] = proj(_PG_GATE)


def _proj(x2, pos, w_in, w_fa, b_fa, tm):
    n, d = x2.shape
    t = pos.shape[0]
    widths, offs = _pack_layout(d)
    wp = _pack_w_in(w_in)
    wfa = jnp.concatenate([w_fa, jnp.zeros((LANES - G_LOWRANK, W_G), w_fa.dtype)], 0).astype(BF16)
    cos, sin = _rope_tables(pos)
    if tm > t:
        cos, sin = jnp.tile(cos, (tm // t, 1)), jnp.tile(sin, (tm // t, 1))
    nper = cos.shape[0] // tm
    row = lambda w: pl.BlockSpec((tm, w), lambda i: (i, 0))
    const = lambda a: pl.BlockSpec(a.shape, lambda i: (0, 0))
    tab = pl.BlockSpec((tm, LANES), lambda i: (i % nper, 0))
    outs = [(W_AQ, BF16), (LANES, F32), (LANES, F32), (LANES, BF16), (LANES, BF16), (W_IQ, BF16),
            (LANES, F32), (LANES, BF16), (W_G, F32), (W_G, F32), (W_G, F32), (W_G, F32), (W_G, F32),
            (2 * d, F32)]
    bfa = b_fa.reshape(1, W_G)
    return pl.pallas_call(
        functools.partial(_proj_kernel, offs=offs, widths=widths),
        grid=(n // tm,),
        in_specs=[row(d), const(wp), const(wfa), const(bfa), tab, tab],
        out_specs=[row(w) for w, _ in outs],
        out_shape=[jax.ShapeDtypeStruct((n, w), dt) for w, dt in outs],
        compiler_params=_params(("parallel",)),
        name="proj",
    )(x2, wp, wfa, bfa, cos, sin)


def _attn_kernel(aq_ref, iq_ref, ikw_ref, lim_ref, kb_ref, vt_ref, ikb_ref, o_ref,
                 keys_ref, qs_ref, iqs_ref, acc_ref, *, kt_w, nkt, topk, idx_bits):
    qb = Q_BLOCK

    for h in range(A_HEADS):
        qs_ref[h // A_GROUP, (h % A_GROUP) * qb:(h % A_GROUP + 1) * qb, :] = \
            aq_ref[:, h * A_HEAD_DIM:(h + 1) * A_HEAD_DIM]
    for h in range(IDX_HEADS):
        iqs_ref[h * qb:(h + 1) * qb, :] = iq_ref[:, h * IDX_DIM:(h + 1) * IDX_DIM]

    ikw_t = ikw_ref[...].T
    iw_rows = [ikw_t[IDX_DIM + h:IDX_DIM + h + 1, :] for h in range(IDX_HEADS)]
    lim = lim_ref[0, 0:1, :]
    sub = lax.broadcasted_iota(I32, (kt_w, qb), 0)

    def score_tile(kt, _):
        base = pl.multiple_of(kt * kt_w, kt_w)
        ik_t = ikb_ref[0, pl.ds(base, kt_w), :][:, :IDX_DIM]
        s = jnp.maximum(_dot_nt(ik_t, iqs_ref[...]), 0.0)
        score = jnp.zeros((kt_w, qb), F32)
        for h in range(IDX_HEADS):
            score = score + s[:, h * qb:(h + 1) * qb] * iw_rows[h]
        keys_ref[pl.ds(base, kt_w), :] = jnp.where(sub + base < lim, _sort_key(score), NEG_INF_KEY)
        return 0

    lax.fori_loop(0, nkt, score_tile, 0, unroll=min(nkt, 2))

    def count(pred):
        acc = jnp.zeros((SUBLANES, qb), F32)
        for kt in range(nkt):
            acc = acc + _fold_rows(pred(keys_ref[kt * kt_w:(kt + 1) * kt_w, :], sub + kt * kt_w), jnp.add)
        return jnp.sum(acc, axis=0, keepdims=True)

    def count_ge(t_row):
        return count(lambda kk, idx: jnp.where(kk >= t_row, 1.0, 0.0))

    kf = float(topk)
    thr = jnp.where(count_ge(jnp.zeros((1, qb), I32)) >= kf, 0, INT_MIN).astype(I32)

    def thr_bit(i, t):
        cand = t + jnp.left_shift(jnp.int32(1), 30 - i)
        return jnp.where(count_ge(cand) >= kf, cand, t)

    thr = lax.fori_loop(0, 31, thr_bit, thr)
    n_gt = count_ge(thr + 1)
    n_eq = count_ge(thr) - n_gt
    need = kf - n_gt
    finite = thr > NEG_INF_KEY
    excess = jnp.where(finite, jnp.where(n_eq > need, 1.0, 0.0), 0.0)

    def count_eq_below(j_row):
        return count(lambda kk, idx: jnp.where(kk == thr, jnp.where(idx < j_row, 1.0, 0.0), 0.0))

    def resolve_ties():
        def bit(i, jc):
            cand = jc + jnp.left_shift(jnp.int32(1), idx_bits - 1 - i)
            return jnp.where(count_eq_below(cand) <= need - 1.0, cand, jc)
        jc = lax.fori_loop(0, idx_bits, bit, jnp.zeros((1, qb), I32))
        return jnp.where(finite, jc, -1)

    cut = lax.cond(jnp.max(excess) > 0.0, resolve_ties,
                   lambda: jnp.where(finite, 2 ** 30, -1).astype(I32))

    acc_ref[...] = jnp.zeros(acc_ref.shape, F32)
    gq = A_GROUP * qb

    def attend(kt, carry):
        base = pl.multiple_of(kt * kt_w, kt_w)
        k_t = kb_ref[0, pl.ds(base, kt_w), :]
        kk = keys_ref[pl.ds(base, kt_w), :]
        tie = jnp.where(sub + base <= cut, 0.0, NEG_BIG)
        bias = jnp.where(kk > thr, 0.0, jnp.where(kk == thr, tie, NEG_BIG))
        bias = jnp.concatenate([bias] * A_GROUP, axis=1)
        out = []
        for n in range(A_KV_HEADS):
            m_old, l_old = carry[2 * n], carry[2 * n + 1]
            logits = _dot_nt(k_t[:, n * A_HEAD_DIM:(n + 1) * A_HEAD_DIM], qs_ref[n]) + bias
            m_new = jnp.maximum(m_old, jnp.max(_fold_rows(logits, jnp.maximum), axis=0, keepdims=True))
            alpha = jnp.exp(m_old - m_new)
            p = jnp.exp(logits - m_new)
            l_new = alpha * l_old + jnp.sum(_fold_rows(p, jnp.add), axis=0, keepdims=True)
            v_t = vt_ref[0, n * A_HEAD_DIM:(n + 1) * A_HEAD_DIM, pl.ds(base, kt_w)]
            acc_ref[n] = alpha * acc_ref[n] + _dot(v_t, p.astype(BF16))
            out += [m_new, l_new]
        return tuple(out)

    init = (jnp.full((1, gq), NEG_BIG, F32), jnp.zeros((1, gq), F32)) * A_KV_HEADS
    fin = lax.fori_loop(0, nkt, attend, init, unroll=min(nkt, 2))
    l_row = [fin[2 * n + 1] for n in range(A_KV_HEADS)]

    o_t = jnp.concatenate([acc_ref[n] / l_row[n] for n in range(A_KV_HEADS)], axis=0)
    for n in range(A_KV_HEADS):
        for g in range(A_GROUP):
            h = n * A_GROUP + g
            blk = o_t[n * A_HEAD_DIM:(n + 1) * A_HEAD_DIM, g * qb:(g + 1) * qb]
            o_ref[:, h * A_HEAD_DIM:(h + 1) * A_HEAD_DIM] = blk.T.astype(BF16)


def _attention(aq, iq, ikw, limits, kb, vb, ikb, b, topk):
    qb, kt_w = Q_BLOCK, KEY_TILE
    tq = aq.shape[0] // b
    tq_pad = -(-tq // qb) * qb
    limits = np.asarray(limits)
    if tq_pad != tq:
        padq = lambda a: jnp.pad(a.reshape(b, tq, -1), ((0, 0), (0, tq_pad - tq), (0, 0))).reshape(b * tq_pad, -1)
        aq, iq, ikw = padq(aq), padq(iq), padq(ikw)
        limits = np.concatenate([limits, np.full((tq_pad - tq,), limits[-1])])
    l_all = kb.shape[1]
    l_pad = -(-l_all // kt_w) * kt_w
    if l_pad != l_all:
        padl = lambda a: jnp.pad(a, ((0, 0), (0, l_pad - l_all), (0, 0)))
        kb, vb, ikb = padl(kb), padl(vb), padl(ikb)
    vt = jnp.swapaxes(vb, 1, 2)
    nq = tq_pad // qb
    lim_blk = limits.reshape(nq, qb)
    nkt = np.minimum(-(-lim_blk.max(axis=1) // kt_w), l_pad // kt_w)
    lim = jnp.asarray(np.broadcast_to(lim_blk[:, None, :], (nq, SUBLANES, qb)).astype(np.int32))
    idx_bits = max(1, int(math.ceil(math.log2(l_pad))))
    keys = lambda a: pl.BlockSpec((1,) + a.shape[1:], lambda bi, j: (bi, 0, 0))
    gq = A_GROUP * qb
    runs, j0 = [], 0
    for j in range(1, nq + 1):
        if j == nq or nkt[j] != nkt[j0]:
            runs.append((j0, j - j0, int(nkt[j0])))
            j0 = j
    outs = []
    for j0, nj, n_tiles in runs:
        qrow = lambda w, j0=j0: pl.BlockSpec((qb, w), lambda bi, j: (bi * nq + j0 + j, 0))
        outs.append(pl.pallas_call(
            functools.partial(_attn_kernel, kt_w=kt_w, nkt=n_tiles, topk=topk, idx_bits=idx_bits),
            grid=(b, nj),
            in_specs=[qrow(W_AQ), qrow(W_IQ), qrow(LANES),
                      pl.BlockSpec((1, SUBLANES, qb), lambda bi, j, j0=j0: (j0 + j, 0, 0)),
                      keys(kb), keys(vt), keys(ikb)],
            out_specs=pl.BlockSpec((qb, W_AQ), lambda bi, j, nj=nj: (bi * nj + j, 0)),
            out_shape=jax.ShapeDtypeStruct((b * nj * qb, W_AQ), BF16),
            scratch_shapes=[pltpu.VMEM((n_tiles * kt_w, qb), I32),
                            pltpu.VMEM((A_KV_HEADS, gq, A_HEAD_DIM), BF16),
                            pltpu.VMEM((IDX_HEADS * qb, IDX_DIM), BF16),
                            pltpu.VMEM((A_KV_HEADS, A_HEAD_DIM, gq), F32)],
            compiler_params=_params(("parallel", "arbitrary")),
            name="attn",
        )(aq, iq, ikw, lim, kb, vt, ikb).reshape(b, nj * qb, W_AQ))
    o = outs[0] if len(outs) == 1 else jnp.concatenate(outs, axis=1)
    return o[:, :tq].reshape(b * tq, W_AQ)


def _gla_constants(c):
    nlev = int(math.log2(c))
    t = np.arange(c)
    mats = [(t[None, :] <= t[:, None])]
    masks = [np.eye(c, dtype=bool)]
    for lev in range(nlev):
        m = c >> (lev + 1)
        ref_row = (t // (2 * m)) * 2 * m + m
        mats.append(t[None, :] <= ref_row[:, None])
        upper = (t & m) != 0
        same = (t[:, None] // (2 * m)) == (t[None, :] // (2 * m))
        masks.append(same & upper[:, None] & ~upper[None, :])
    return (jnp.asarray(np.concatenate(mats, 0).astype(np.float32), BF16),
            jnp.asarray(np.stack(masks).astype(np.float32)), nlev)


def _gla_kernel(q_ref, k_ref, v_ref, g_ref, gr_ref, gn_ref, mst_ref, msk_ref, s0_ref,
                ob_ref, sfin_ref, st_ref, *, c, nlev, nb):
    i = pl.program_id(1)
    hk = G_KEY_DIM

    @pl.when(i == 0)
    def _():
        for s in range(nb):
            for h in range(G_HEADS):
                st_ref[s, h] = s0_ref[s, h].T

    mst = mst_ref[...]
    gn = gn_ref[...]
    hs = lambda a, h: a[:, h * hk:(h + 1) * hk]
    for s in range(nb):
        g = g_ref[s]
        g_hi = g.astype(BF16)
        r1 = g - g_hi.astype(F32)
        g_mid = r1.astype(BF16)
        g_lo = (r1 - g_mid.astype(F32)).astype(BF16)
        bs = _dot(mst, g_hi) + _dot(mst, g_mid) + _dot(mst, g_lo)
        b = bs[0:c]
        q = q_ref[s]
        k = k_ref[s]
        vb = v_ref[s].astype(BF16)
        row = lax.broadcasted_iota(I32, q.shape, 0)

        qb = q.astype(BF16)
        kb = k.astype(BF16)
        attn = [_dot_nt(hs(qb, h), hs(kb, h)) * msk_ref[0] for h in range(G_HEADS)]
        for lev in range(nlev):
            m = c >> (lev + 1)
            upper = (row & m) != 0
            d = b - bs[(lev + 1) * c:(lev + 2) * c]
            e = jnp.exp(jnp.where(upper, d, -d))
            qt = jnp.where(upper, q * e, 0.0).astype(BF16)
            kt = jnp.where(upper, 0.0, k * e).astype(BF16)
            mk = msk_ref[lev + 1]
            for h in range(G_HEADS):
                attn[h] = attn[h] + _dot_nt(hs(qt, h), hs(kt, h)) * mk

        qe = (q * jnp.exp(b)).astype(BF16)
        b_last = b[c - 1:c, :]
        khat = (k * jnp.exp(b_last - b)).astype(BF16)
        dec = jnp.exp(b_last)
        gr = gr_ref[s]
        for h in range(G_HEADS):
            st = st_ref[s, h]
            o = _dot_nt(hs(qe, h), st.astype(BF16)) + _dot(attn[h].astype(BF16), hs(vb, h))
            st_ref[s, h] = st * hs(dec, h) + _dot_tn(hs(vb, h), hs(khat, h))
            ms = jnp.mean(o * o, axis=1, keepdims=True)
            grh = hs(gr, h)
            of = o * lax.rsqrt(ms + LN_EPS) * hs(gn, h) * (grh / (1.0 + jnp.exp(-grh)))
            ob_ref[s, :, h * hk:(h + 1) * hk] = of.astype(BF16)

    @pl.when(i == pl.num_programs(1) - 1)
    def _():
        for s in range(nb):
            for h in range(G_HEADS):
                sfin_ref[s, h] = st_ref[s, h].T


def _gla(gq, gk, gv, glog, gr, g_norm, s0, b, c):
    n = gq.shape[0]
    t = n // b
    nc = t // c
    nb = GLA_SEQS if b % GLA_SEQS == 0 else 1
    mst, msk, nlev = _gla_constants(c)
    seq = lambda a: a.reshape(b, t, W_G)
    row = pl.BlockSpec((nb, c, W_G), lambda bi, i: (bi, i, 0))
    const = lambda a: pl.BlockSpec(a.shape, lambda bi, i: (0,) * a.ndim)
    st_spec = pl.BlockSpec((nb, G_HEADS, G_KEY_DIM, G_VAL_DIM), lambda bi, i: (bi, 0, 0, 0))
    gn = g_norm.reshape(1, W_G)
    ob, s_fin = pl.pallas_call(
        functools.partial(_gla_kernel, c=c, nlev=nlev, nb=nb),
        grid=(b // nb, nc),
        in_specs=[row, row, row, row, row, const(gn), const(mst), const(msk), st_spec],
        out_specs=[row, st_spec],
        out_shape=[jax.ShapeDtypeStruct((b, t, W_G), BF16),
                   jax.ShapeDtypeStruct((b, G_HEADS, G_KEY_DIM, G_VAL_DIM), F32)],
        scratch_shapes=[pltpu.VMEM((nb, G_HEADS, G_VAL_DIM, G_KEY_DIM), F32)],
        compiler_params=_params(("parallel", "arbitrary")),
        name="gla",
    )(seq(gq), seq(gk), seq(gv), seq(glog), seq(gr), gn, mst, msk, s0)
    return ob.reshape(n, W_G), s_fin


def _layer_norm(z, g, b):
    mu = jnp.mean(z, axis=1, keepdims=True)
    zc = z - mu
    var = jnp.mean(zc * zc, axis=1, keepdims=True)
    return zc * lax.rsqrt(var + LN_EPS) * g + b


def _sigmoid(x):
    return 1.0 / (1.0 + jnp.exp(-x))


def _merge_kernel(x_ref, oa_ref, ob_ref, gate_ref, wpa_ref, wpb_ref, wout_ref, g1_ref, b1_ref,
                  wpq_ref, pk1_ref, pk2_ref, h1_ref, h1b_ref, s1t_ref, s2t_ref, *, alpha, d):
    ya = _dot(oa_ref[...], wpa_ref[...])
    yb = _dot(ob_ref[...], wpb_ref[...])
    m = _sigmoid(gate_ref[:, :d]) * ya + _sigmoid(gate_ref[:, d:]) * yb
    mix = _dot(m.astype(BF16), wout_ref[...])
    h1 = _layer_norm(alpha * x_ref[...] + mix, g1_ref[...], b1_ref[...])
    h1_ref[...] = h1
    h1b = h1.astype(BF16)
    h1b_ref[...] = h1b
    qp = _dot(h1b, wpq_ref[...])
    for h in range(P_HEADS):
        for half, (pk_ref, st_ref) in enumerate(((pk1_ref, s1t_ref), (pk2_ref, s2t_ref))):
            c0 = (2 * h + half) * P_HALF
            st_ref[h * P_NKEYS:(h + 1) * P_NKEYS, :] = _dot_nt(pk_ref[h], qp[:, c0:c0 + P_HALF].astype(BF16))


def _merge(x2, oa, ob, gate, w_pa, w_pb, w_out, ln_g, ln_b, w_pq, pk1, pk2, alpha, tm):
    n, d = x2.shape
    hk = P_HEADS * P_NKEYS
    row = lambda w: pl.BlockSpec((tm, w), lambda i: (i, 0))
    col = pl.BlockSpec((hk, tm), lambda i: (0, i))
    const = lambda a: pl.BlockSpec(a.shape, lambda i: (0,) * a.ndim)
    ws = [w_pa.astype(BF16), w_pb.astype(BF16), w_out.astype(BF16), ln_g.reshape(1, d), ln_b.reshape(1, d),
          w_pq.astype(BF16), pk1.astype(BF16), pk2.astype(BF16)]
    return pl.pallas_call(
        functools.partial(_merge_kernel, alpha=alpha, d=d),
        grid=(n // tm,),
        in_specs=[row(d), row(W_AQ), row(W_G), row(2 * d)] + [const(w) for w in ws],
        out_specs=[row(d), row(d), col, col],
        out_shape=[jax.ShapeDtypeStruct((n, d), F32), jax.ShapeDtypeStruct((n, d), BF16),
                   jax.ShapeDtypeStruct((hk, n), F32), jax.ShapeDtypeStruct((hk, n), F32)],
        compiler_params=_params(("parallel",)),
        name="merge",
    )(x2, oa, ob, gate, *ws)


def _top_desc(s, count):
    tops = []
    for r in range(count):
        m = jnp.max(_fold_rows(s, jnp.maximum), axis=0, keepdims=True)
        tops.append(m)
        if r + 1 < count:
            s = jnp.where(s == m, -jnp.inf, s)
    return tops


def _select_kernel(s1t_ref, s2t_ref, cnt_ref, e1_ref, rank_ref, e2_ref):
    nk = P_NKEYS

    def head(h, _):
        r0 = pl.multiple_of(h * nk, nk)
        rows = pl.ds(r0, nk)
        s1 = s1t_ref[rows, :]
        s2 = s2t_ref[rows, :]
        v1 = _top_desc(s1, P_TOPK)
        v2 = _top_desc(s2, P_TOPK)
        pairs = [(a, b) for a in range(P_TOPK) for b in range(P_TOPK // (a + 1))]
        fill = [jnp.full_like(v1[0], -jnp.inf)] * (-len(pairs) % SUBLANES)
        cand = jnp.concatenate([v1[a] + v2[b] for a, b in pairs] + fill, axis=0)
        work, seen = cand, jnp.zeros_like(v1[0])
        tau = jnp.full_like(v1[0], -jnp.inf)
        for _ in range(P_TOPK):
            m = jnp.max(work, axis=0, keepdims=True)
            hit = work == m
            seen = seen + jnp.sum(jnp.where(hit, 1.0, 0.0), axis=0, keepdims=True)
            tau = jnp.maximum(tau, jnp.where(seen >= float(P_TOPK), m, -jnp.inf))
            work = jnp.where(hit, -jnp.inf, work)
        cmax = v1[0] + v2[0]
        zsum = jnp.sum(jnp.where(cand >= tau, jnp.exp(cand - cmax), 0.0), axis=0, keepdims=True)
        v2all = jnp.concatenate(v2, axis=0)
        cnt = jnp.zeros(s1.shape, F32)
        rank = jnp.full(s2.shape, float(P_TOPK), F32)
        for a in range(P_TOPK):
            cnt_a = jnp.sum(jnp.where(v1[a] + v2all >= tau, 1.0, 0.0), axis=0, keepdims=True)
            cnt = jnp.where(s1 == v1[a], cnt_a, cnt)
            rank = jnp.where(s2 == v2[a], float(a), rank)
        cnt_ref[rows, :] = cnt
        rank_ref[rows, :] = rank
        e1_ref[rows, :] = jnp.exp(s1 - v1[0]) / zsum
        e2_ref[rows, :] = jnp.exp(s2 - v2[0])
        return 0

    lax.fori_loop(0, P_HEADS, head, 0)


def _select(s1t, s2t, tn):
    hk, n = s1t.shape
    col = pl.BlockSpec((hk, tn), lambda i: (0, i))
    return pl.pallas_call(
        _select_kernel,
        grid=(n // tn,),
        in_specs=[col, col],
        out_specs=[col, col, col, col],
        out_shape=[jax.ShapeDtypeStruct((hk, n), F32)] * 4,
        compiler_params=_params(("parallel",)),
        name="select",
    )(s1t, s2t)


I1_PER_STEP = 8
E_PER_STEP = I1_PER_STEP * P_NKEYS
I2_BLOCK = 16


def _gelu(x):
    return 0.5 * x * (1.0 + lax.erf(x * (2.0 ** -0.5)))


def _peer_steps(t, n_steps, ng):
    item = lambda d: jnp.clip(t - d, 0, n_steps - 1)
    return [(item(d) // ng, item(d) % ng) for d in range(3)]


def _peer_kernel(hb_ref, pu_ref, pvt_ref, cnt_ref, e1_ref, rank_ref, e2_ref, h1_ref, g2_ref, b2_ref,
                 y_ref, a_ref, gw_ref, acc_ref, rank_s, e2_s, *, tn, alpha, n_steps, ng):
    t = pl.program_id(0)
    (_, _), (_, g), (_, g_out) = _peer_steps(t, n_steps, ng)
    slot = t % 2
    prev = 1 - slot

    @pl.when(t == 0)
    def _():
        a_ref[...] = jnp.zeros(a_ref.shape, F32)
        gw_ref[...] = jnp.zeros(gw_ref.shape, BF16)

    @pl.when(g == 0)
    def _():
        rank_s[...] = rank_ref[...].astype(BF16)
        e2_s[...] = e2_ref[...].astype(BF16)

    a_ref[slot] = _dot_nt(pu_ref[...], hb_ref[...])

    assert I1_PER_STEP == SUBLANES
    for lt in range(tn // LANES):
        ls = slice(lt * LANES, (lt + 1) * LANES)
        grp = lambda ref, h: ref[pl.ds(pl.multiple_of(h * P_NKEYS + g * I1_PER_STEP, SUBLANES), SUBLANES), ls]
        cnt8 = [grp(cnt_ref, h) for h in range(P_HEADS)]
        e18 = [grp(e1_ref, h) for h in range(P_HEADS)]
        for j in range(I1_PER_STEP):
            bcast = lambda a8: jnp.broadcast_to(a8[j:j + 1, :], (I2_BLOCK, LANES)).astype(BF16)
            cnt = [bcast(cnt8[h]) for h in range(P_HEADS)]
            e1 = [bcast(e18[h]) for h in range(P_HEADS)]
            for i2b in range(P_NKEYS // I2_BLOCK):
                w = jnp.zeros((I2_BLOCK, LANES), BF16)
                for h in range(P_HEADS):
                    rows = slice(h * P_NKEYS + i2b * I2_BLOCK, h * P_NKEYS + (i2b + 1) * I2_BLOCK)
                    w = w + jnp.where(rank_s[rows, ls] < cnt[h], e2_s[rows, ls] * e1[h], jnp.zeros((), BF16))
                arow = slice(j * P_NKEYS + i2b * I2_BLOCK, j * P_NKEYS + (i2b + 1) * I2_BLOCK)
                gw_ref[prev, arow, ls] = (w.astype(F32) * _gelu(a_ref[prev, arow, ls])).astype(BF16)

    part = _dot(pvt_ref[0], gw_ref[slot])
    acc_ref[...] = jnp.where(g_out == 0, part, acc_ref[...] + part)

    @pl.when(jnp.logical_and(g_out == ng - 1, t >= 2))
    def _():
        y_ref[...] = _layer_norm(alpha * h1_ref[...] + acc_ref[...].T, g2_ref[...], b2_ref[...])


def _peer(h1, h1b, cnt, e1, rank, e2, pu, pv, ln_g, ln_b, alpha, tn):
    n, d = h1.shape
    hk = P_HEADS * P_NKEYS
    ng = pu.shape[0] // E_PER_STEP
    n_steps = (n // tn) * ng
    stage = lambda k, f: (lambda t: f(*_peer_steps(t, n_steps, ng)[k]))
    const = pl.BlockSpec((1, d), lambda t: (0, 0))
    routing = pl.BlockSpec((hk, tn), stage(1, lambda i, g: (0, i)))
    pvt = pv.astype(BF16).reshape(ng, E_PER_STEP, d).transpose(0, 2, 1)
    return pl.pallas_call(
        functools.partial(_peer_kernel, tn=tn, alpha=alpha, n_steps=n_steps, ng=ng),
        grid=(n_steps + 2,),
        in_specs=[pl.BlockSpec((tn, d), stage(0, lambda i, g: (i, 0))),
                  pl.BlockSpec((E_PER_STEP, d), stage(0, lambda i, g: (g, 0))),
                  pl.BlockSpec((1, d, E_PER_STEP), stage(2, lambda i, g: (g, 0, 0))),
                  routing, routing, routing, routing,
                  pl.BlockSpec((tn, d), stage(2, lambda i, g: (i, 0))), const, const],
        out_specs=pl.BlockSpec((tn, d), stage(2, lambda i, g: (i, 0))),
        out_shape=jax.ShapeDtypeStruct((n, d), F32),
        scratch_shapes=[pltpu.VMEM((2, E_PER_STEP, tn), F32), pltpu.VMEM((2, E_PER_STEP, tn), BF16),
                        pltpu.VMEM((d, tn), F32), pltpu.VMEM((hk, tn), BF16), pltpu.VMEM((hk, tn), BF16)],
        compiler_params=_params(("arbitrary",)),
        name="peer",
    )(h1b, pu.astype(BF16), pvt, cnt, e1, rank, e2, h1, ln_g.reshape(1, d), ln_b.reshape(1, d))


def _pick_tile(n, pref):
    t = min(n, pref)
    assert n % t == 0
    return t


def _layer(x, pos, limits, past, s0, w, *, chunk, alpha):
    b, t, d = x.shape
    n = b * t
    x2 = x.reshape(n, d)
    tm = _pick_tile(n, TOKEN_TILE)
    (aq, k32, v32, kb, vb, iq, ikw, ikb, gq, gk, gv, glog, gr, gate) = _proj(
        x2, pos, w["w_in"], w["w_fa"], w["b_fa"], tm)

    kb3, vb3, ikb3 = kb.reshape(b, t, LANES), vb.reshape(b, t, LANES), ikb.reshape(b, t, LANES)
    if past is not None:
        ck, cv, cik = past
        p = ck.shape[1]
        kb3 = jnp.concatenate([ck.reshape(b, p, LANES).astype(BF16), kb3], axis=1)
        vb3 = jnp.concatenate([cv.reshape(b, p, LANES).astype(BF16), vb3], axis=1)
        ikb3 = jnp.concatenate([cik.astype(BF16), ikb3[:, :, :IDX_DIM]], axis=1)
    topk = min(IDX_TOPK, kb3.shape[1] // 4)
    o_a = _attention(aq, iq, ikw, limits, kb3, vb3, ikb3, b, topk)

    o_b, s_fin = _gla(gq, gk, gv, glog, gr, w["g_gla_norm"], s0, b, chunk)

    h1, h1b, s1t, s2t = _merge(x2, o_a, o_b, gate, w["w_pa"], w["w_pb"], w["w_out"],
                               w["ln1_g"], w["ln1_b"], w["w_pq"], w["pk1"], w["pk2"], alpha, tm)
    cnt, e1, rank, e2 = _select(s1t, s2t, tm)
    y = _peer(h1, h1b, cnt, e1, rank, e2, w["pu"], w["pv"], w["ln2_g"], w["ln2_b"], alpha,
              _pick_tile(n, PEER_TOKEN_TILE))

    k_out = k32.reshape(b, t, A_KV_HEADS, A_HEAD_DIM)
    v_out = v32.reshape(b, t, A_KV_HEADS, A_HEAD_DIM)
    ik_out = ikw[:, :IDX_DIM].reshape(b, t, IDX_DIM)
    return y.reshape(b, t, d), k_out, v_out, ik_out, s_fin


def kernel(x_prompt, x_sample, cache_k, cache_v, cache_idx_k, state_gla, w_in, w_fa, b_fa, g_gla_norm,
           w_pa, w_pb, w_out, ln1_g, ln1_b, w_pq, pk1, pk2, pu, pv, ln2_g, ln2_b):
    depth = w_in.shape[0]
    alpha = (2.0 * depth) ** 0.25
    bp, tp, _ = x_prompt.shape
    bs, ts, _ = x_sample.shape
    past_len = cache_k.shape[2]
    pos_p = jnp.arange(tp)
    pos_s = past_len + jnp.arange(ts)
    lim_p = (np.arange(tp) // CHUNK + 1) * CHUNK
    lim_s = np.full((ts,), past_len + ts)
    names = ("w_in", "w_fa", "b_fa", "g_gla_norm", "w_pa", "w_pb", "w_out", "ln1_g", "ln1_b",
             "w_pq", "pk1", "pk2", "pu", "pv", "ln2_g", "ln2_b")
    stacked = (w_in, w_fa, b_fa, g_gla_norm, w_pa, w_pb, w_out, ln1_g, ln1_b, w_pq, pk1, pk2, pu, pv, ln2_g, ln2_b)
    hp, hs = x_prompt, x_sample
    outs_p, outs_s = [], []
    for l in range(depth):
        w = {nm: a[l] for nm, a in zip(names, stacked)}
        s0 = jnp.zeros((bp, G_HEADS, G_KEY_DIM, G_VAL_DIM), F32)
        hp, *rest = _layer(hp, pos_p, lim_p, None, s0, w, chunk=CHUNK, alpha=alpha)
        outs_p.append(rest)
        hs, *rest = _layer(hs, pos_s, lim_s, (cache_k[l], cache_v[l], cache_idx_k[l]), state_gla[l], w,
                           chunk=ts, alpha=alpha)
        outs_s.append(rest)
    stack = lambda outs, i: jnp.stack([o[i] for o in outs])
    return (hp, hs, stack(outs_p, 0), stack(outs_p, 1), stack(outs_p, 2), stack(outs_p, 3),
            stack(outs_s, 0), stack(outs_s, 1), stack(outs_s, 2), stack(outs_s, 3))
```

```python
import functools
import math

import numpy as np
import jax
import jax.numpy as jnp
from jax import lax
from jax.experimental import pallas as pl
from jax.experimental.pallas import tpu as pltpu

F32 = jnp.float32
BF16 = jnp.bfloat16
I32 = jnp.int32

LANES = 128
SUBLANES = 8
VMEM_LIMIT = 56 << 20

CHUNK = 64
A_HEADS = 8
A_KV_HEADS = 2
A_HEAD_DIM = 64
A_GROUP = A_HEADS // A_KV_HEADS
IDX_HEADS = 4
IDX_DIM = 64
IDX_TOPK = 256
ROPE_THETA = 10000.0
G_HEADS = 4
G_KEY_DIM = 128
G_VAL_DIM = 128
G_LOWRANK = 16
G_TAU = 16.0
P_HEADS = 8
P_NKEYS = 128
P_HALF = 128
P_TOPK = 16
LN_EPS = 1e-5

W_AQ = A_HEADS * A_HEAD_DIM
W_AK = A_KV_HEADS * A_HEAD_DIM
W_IQ = IDX_HEADS * IDX_DIM
W_G = G_HEADS * G_KEY_DIM
IN_SIZES = (W_AQ, W_AK, W_AK, W_IQ, IDX_DIM, IDX_HEADS, W_G, W_G, W_G, G_LOWRANK, W_G, None)

TOKEN_TILE = 256
PEER_TOKEN_TILE = 512
Q_BLOCK = 128
KEY_TILE = 512
GLA_SEQS = 4

INT_MIN = -(2 ** 31)
NEG_INF_KEY = -2139095041
NEG_BIG = -1e30


def _dot(a, b):
    return jnp.dot(a, b, preferred_element_type=F32)


def _dot_nt(a, b):
    return lax.dot_general(a, b, (((1,), (1,)), ((), ())), preferred_element_type=F32)


def _dot_tn(a, b):
    return lax.dot_general(a, b, (((0,), (0,)), ((), ())), preferred_element_type=F32)


def _sort_key(x):
    bits = pltpu.bitcast(x, I32)
    key = jnp.where(bits < 0, bits ^ 0x7FFFFFFF, bits)
    return jnp.where(key == -1, 0, key)


def _fold_rows(x, op):
    x = x.reshape(x.shape[0] // SUBLANES, SUBLANES, x.shape[1])
    while x.shape[0] > 1:
        half = x.shape[0] // 2
        folded = op(x[:half], x[half:2 * half])
        x = folded if x.shape[0] == 2 * half else jnp.concatenate([folded, x[2 * half:]], axis=0)
    return x[0]


def _params(sem):
    return pltpu.CompilerParams(dimension_semantics=sem, vmem_limit_bytes=VMEM_LIMIT)


_PG_AQ, _PG_K, _PG_V, _PG_IQ, _PG_IKW, _PG_GQ, _PG_GK, _PG_GV, _PG_GF, _PG_GR, _PG_GATE = range(11)


def _pack_layout(d_model):
    widths = [W_AQ, W_AK, W_AK, W_IQ, LANES, W_G, W_G, W_G, LANES, W_G, 2 * d_model]
    offs = np.concatenate([[0], np.cumsum(widths)]).tolist()
    return widths, offs


def _pack_w_in(w_in):
    d = w_in.shape[0]
    sizes = list(IN_SIZES[:-1]) + [2 * d]
    cuts = np.cumsum(sizes)[:-1].tolist()
    aq, ak, av, iq, ik, iw, gq, gk, gv, gf, gr, gate = jnp.split(w_in, cuts, axis=1)
    z = lambda n: jnp.zeros((d, n), w_in.dtype)
    ikw = jnp.concatenate([ik, iw, z(LANES - IDX_DIM - IDX_HEADS)], axis=1)
    gfp = jnp.concatenate([gf, z(LANES - G_LOWRANK)], axis=1)
    return jnp.concatenate([aq, ak, av, iq, ikw, gq, gk, gv, gfp, gr, gate], axis=1).astype(BF16)


def _rope_tables(pos):
    half = A_HEAD_DIM // 2
    inv = ROPE_THETA ** (-jnp.arange(half, dtype=F32) / half)
    ang = pos.astype(F32)[:, None] * inv[None, :]
    c, s = jnp.cos(ang), jnp.sin(ang)
    return jnp.concatenate([c, c, c, c], -1), jnp.concatenate([-s, s, -s, s], -1)


def _proj_kernel(x_ref, w_ref, wfa_ref, bfa_ref, cos_ref, sin_ref,
                 aq_ref, k_ref, v_ref, kb_ref, vb_ref, iq_ref, ikw_ref, ikb_ref,
                 gq_ref, gk_ref, gv_ref, glog_ref, gr_ref, gate_ref, *, offs, widths):
    xb = x_ref[...].astype(BF16)
    cos = cos_ref[...]
    sin = sin_ref[...]
    lane = lax.broadcasted_iota(I32, cos.shape, 1)
    first_half = (lane & (A_HEAD_DIM // 2)) == 0

    def proj(g):
        return _dot(xb, w_ref[:, offs[g]:offs[g] + widths[g]])

    def rope_slab(y):
        fwd = pltpu.roll(y, LANES - A_HEAD_DIM // 2, 1)
        bwd = pltpu.roll(y, A_HEAD_DIM // 2, 1)
        return y * cos + jnp.where(first_half, fwd, bwd) * sin

    def rope(y):
        return [rope_slab(y[:, s * LANES:(s + 1) * LANES]) for s in range(y.shape[1] // LANES)]

    for s, slab in enumerate(rope(proj(_PG_AQ))):
        aq_ref[:, s * LANES:(s + 1) * LANES] = (slab * (A_HEAD_DIM ** -0.5)).astype(BF16)
    k = rope(proj(_PG_K))[0]
    k_ref[...] = k
    kb_ref[...] = k.astype(BF16)
    v = proj(_PG_V)
    v_ref[...] = v
    vb_ref[...] = v.astype(BF16)
    for s, slab in enumerate(rope(proj(_PG_IQ))):
        iq_ref[:, s * LANES:(s + 1) * LANES] = (slab * (IDX_DIM ** -0.5)).astype(BF16)
    raw = proj(_PG_IKW)
    ikw = jnp.where(lane < IDX_DIM, rope_slab(raw), raw * (IDX_HEADS ** -0.5))
    ikw_ref[...] = ikw
    ikb_ref[...] = ikw.astype(BF16)
    gq_ref[...] = proj(_PG_GQ) * (G_KEY_DIM ** -0.5)
    gk_ref[...] = proj(_PG_GK)
    gv_ref[...] = proj(_PG_GV)
    z = _dot(proj(_PG_GF).astype(BF16), wfa_ref[...]) + bfa_ref[...]
    glog_ref[...] = (jnp.minimum(z, 0.0) - jnp.log1p(jnp.exp(-jnp.abs(z)))) * (1.0 / G_TAU)
    gr_ref[...] = proj(_PG_GR)
    gate_ref[...] = proj(_PG_GATE)


def _proj(x2, pos, w_in, w_fa, b_fa, tm):
    n, d = x2.shape
    t = pos.shape[0]
    widths, offs = _pack_layout(d)
    wp = _pack_w_in(w_in)
    wfa = jnp.concatenate([w_fa, jnp.zeros((LANES - G_LOWRANK, W_G), w_fa.dtype)], 0).astype(BF16)
    cos, sin = _rope_tables(pos)
    if tm > t:
        cos, sin = jnp.tile(cos, (tm // t, 1)), jnp.tile(sin, (tm // t, 1))
    nper = cos.shape[0] // tm
    row = lambda w: pl.BlockSpec((tm, w), lambda i: (i, 0))
    const = lambda a: pl.BlockSpec(a.shape, lambda i: (0, 0))
    tab = pl.BlockSpec((tm, LANES), lambda i: (i % nper, 0))
    outs = [(W_AQ, BF16), (LANES, F32), (LANES, F32), (LANES, BF16), (LANES, BF16), (W_IQ, BF16),
            (LANES, F32), (LANES, BF16), (W_G, F32), (W_G, F32), (W_G, F32), (W_G, F32), (W_G, F32),
            (2 * d, F32)]
    bfa = b_fa.reshape(1, W_G)
    return pl.pallas_call(
        functools.partial(_proj_kernel, offs=offs, widths=widths),
        grid=(n // tm,),
        in_specs=[row(d), const(wp), const(wfa), const(bfa), tab, tab],
        out_specs=[row(w) for w, _ in outs],
        out_shape=[jax.ShapeDtypeStruct((n, w), dt) for w, dt in outs],
        compiler_params=_params(("parallel",)),
        name="proj",
    )(x2, wp, wfa, bfa, cos, sin)


def _attn_kernel(aq_ref, iq_ref, ikw_ref, lim_ref, kb_ref, vt_ref, ikb_ref, o_ref,
                 keys_ref, qs_ref, iqs_ref, acc_ref, *, kt_w, nkt, topk, idx_bits):
    qb = Q_BLOCK

    aq_t = aq_ref[...].astype(F32).T
    for h in range(A_HEADS):
        qs_ref[h // A_GROUP, :, (h % A_GROUP) * qb:(h % A_GROUP + 1) * qb] = \
            aq_t[h * A_HEAD_DIM:(h + 1) * A_HEAD_DIM, :].astype(BF16)
    iq_t = iq_ref[...].astype(F32).T
    for h in range(IDX_HEADS):
        iqs_ref[:, h * qb:(h + 1) * qb] = iq_t[h * IDX_DIM:(h + 1) * IDX_DIM, :].astype(BF16)

    ikw_t = ikw_ref[...].T
    iw_rows = [ikw_t[IDX_DIM + h:IDX_DIM + h + 1, :] for h in range(IDX_HEADS)]
    lim = lim_ref[0, 0:1, :]
    sub = lax.broadcasted_iota(I32, (kt_w, qb), 0)

    def score_tile(kt, _):
        base = pl.multiple_of(kt * kt_w, kt_w)
        ik_t = ikb_ref[0, pl.ds(base, kt_w), :][:, :IDX_DIM]
        s = jnp.maximum(_dot(ik_t, iqs_ref[...]), 0.0)
        score = jnp.zeros((kt_w, qb), F32)
        for h in range(IDX_HEADS):
            score = score + s[:, h * qb:(h + 1) * qb] * iw_rows[h]
        keys_ref[pl.ds(base, kt_w), :] = jnp.where(sub + base < lim, _sort_key(score), NEG_INF_KEY)
        return 0

    lax.fori_loop(0, nkt, score_tile, 0, unroll=min(nkt, 2))

    def count(pred):
        acc = jnp.zeros((SUBLANES, qb), F32)
        for kt in range(nkt):
            acc = acc + _fold_rows(pred(keys_ref[kt * kt_w:(kt + 1) * kt_w, :], sub + kt * kt_w), jnp.add)
        return jnp.sum(acc, axis=0, keepdims=True)

    def count_ge(t_row):
        return count(lambda kk, idx: jnp.where(kk >= t_row, 1.0, 0.0))

    kf = float(topk)
    thr = jnp.where(count_ge(jnp.zeros((1, qb), I32)) >= kf, 0, INT_MIN).astype(I32)

    def thr_bit(i, t):
        cand = t + jnp.left_shift(jnp.int32(1), 30 - i)
        return jnp.where(count_ge(cand) >= kf, cand, t)

    thr = lax.fori_loop(0, 31, thr_bit, thr)
    n_gt = count_ge(thr + 1)
    n_eq = count_ge(thr) - n_gt
    need = kf - n_gt
    finite = thr > NEG_INF_KEY
    excess = jnp.where(finite, jnp.where(n_eq > need, 1.0, 0.0), 0.0)

    def count_eq_below(j_row):
        return count(lambda kk, idx: jnp.where(kk == thr, jnp.where(idx < j_row, 1.0, 0.0), 0.0))

    def resolve_ties():
        def bit(i, jc):
            cand = jc + jnp.left_shift(jnp.int32(1), idx_bits - 1 - i)
            return jnp.where(count_eq_below(cand) <= need - 1.0, cand, jc)
        jc = lax.fori_loop(0, idx_bits, bit, jnp.zeros((1, qb), I32))
        return jnp.where(finite, jc, -1)

    cut = lax.cond(jnp.max(excess) > 0.0, resolve_ties,
                   lambda: jnp.where(finite, 2 ** 30, -1).astype(I32))

    acc_ref[...] = jnp.zeros(acc_ref.shape, F32)
    gq = A_GROUP * qb

    def attend(kt, carry):
        base = pl.multiple_of(kt * kt_w, kt_w)
        k_t = kb_ref[0, pl.ds(base, kt_w), :]
        kk = keys_ref[pl.ds(base, kt_w), :]
        tie = jnp.where(sub + base <= cut, 0.0, NEG_BIG)
        bias = jnp.where(kk > thr, 0.0, jnp.where(kk == thr, tie, NEG_BIG))
        bias = jnp.concatenate([bias] * A_GROUP, axis=1)
        out = []
        for n in range(A_KV_HEADS):
            m_old, l_old = carry[2 * n], carry[2 * n + 1]
            logits = _dot(k_t[:, n * A_HEAD_DIM:(n + 1) * A_HEAD_DIM], qs_ref[n]) + bias
            m_new = jnp.maximum(m_old, jnp.max(_fold_rows(logits, jnp.maximum), axis=0, keepdims=True))
            alpha = jnp.exp(m_old - m_new)
            p = jnp.exp(logits - m_new)
            l_new = alpha * l_old + jnp.sum(_fold_rows(p, jnp.add), axis=0, keepdims=True)
            v_t = vt_ref[0, n * A_HEAD_DIM:(n + 1) * A_HEAD_DIM, pl.ds(base, kt_w)]
            acc_ref[n] = alpha * acc_ref[n] + _dot(v_t, p.astype(BF16))
            out += [m_new, l_new]
        return tuple(out)

    init = (jnp.full((1, gq), NEG_BIG, F32), jnp.zeros((1, gq), F32)) * A_KV_HEADS
    fin = lax.fori_loop(0, nkt, attend, init, unroll=min(nkt, 2))
    l_row = [fin[2 * n + 1] for n in range(A_KV_HEADS)]

    o_t = jnp.concatenate([acc_ref[n] / l_row[n] for n in range(A_KV_HEADS)], axis=0)
    for n in range(A_KV_HEADS):
        for g in range(A_GROUP):
            h = n * A_GROUP + g
            blk = o_t[n * A_HEAD_DIM:(n + 1) * A_HEAD_DIM, g * qb:(g + 1) * qb]
            o_ref[:, h * A_HEAD_DIM:(h + 1) * A_HEAD_DIM] = blk.T.astype(BF16)


def _attention(aq, iq, ikw, limits, kb, vb, ikb, b, topk):
    qb, kt_w = Q_BLOCK, KEY_TILE
    tq = aq.shape[0] // b
    tq_pad = -(-tq // qb) * qb
    limits = np.asarray(limits)
    if tq_pad != tq:
        padq = lambda a: jnp.pad(a.reshape(b, tq, -1), ((0, 0), (0, tq_pad - tq), (0, 0))).reshape(b * tq_pad, -1)
        aq, iq, ikw = padq(aq), padq(iq), padq(ikw)
        limits = np.concatenate([limits, np.full((tq_pad - tq,), limits[-1])])
    l_all = kb.shape[1]
    l_pad = -(-l_all // kt_w) * kt_w
    if l_pad != l_all:
        padl = lambda a: jnp.pad(a, ((0, 0), (0, l_pad - l_all), (0, 0)))
        kb, vb, ikb = padl(kb), padl(vb), padl(ikb)
    vt = jnp.swapaxes(vb, 1, 2)
    nq = tq_pad // qb
    lim_blk = limits.reshape(nq, qb)
    nkt = np.minimum(-(-lim_blk.max(axis=1) // kt_w), l_pad // kt_w)
    lim = jnp.asarray(np.broadcast_to(lim_blk[:, None, :], (nq, SUBLANES, qb)).astype(np.int32))
    idx_bits = max(1, int(math.ceil(math.log2(l_pad))))
    keys = lambda a: pl.BlockSpec((1,) + a.shape[1:], lambda bi, j: (bi, 0, 0))
    gq = A_GROUP * qb
    runs, j0 = [], 0
    for j in range(1, nq + 1):
        if j == nq or nkt[j] != nkt[j0]:
            runs.append((j0, j - j0, int(nkt[j0])))
            j0 = j
    outs = []
    for j0, nj, n_tiles in runs:
        qrow = lambda w, j0=j0: pl.BlockSpec((qb, w), lambda bi, j: (bi * nq + j0 + j, 0))
        outs.append(pl.pallas_call(
            functools.partial(_attn_kernel, kt_w=kt_w, nkt=n_tiles, topk=topk, idx_bits=idx_bits),
            grid=(b, nj),
            in_specs=[qrow(W_AQ), qrow(W_IQ), qrow(LANES),
                      pl.BlockSpec((1, SUBLANES, qb), lambda bi, j, j0=j0: (j0 + j, 0, 0)),
                      keys(kb), keys(vt), keys(ikb)],
            out_specs=pl.BlockSpec((qb, W_AQ), lambda bi, j, nj=nj: (bi * nj + j, 0)),
            out_shape=jax.ShapeDtypeStruct((b * nj * qb, W_AQ), BF16),
            scratch_shapes=[pltpu.VMEM((n_tiles * kt_w, qb), I32),
                            pltpu.VMEM((A_KV_HEADS, A_HEAD_DIM, gq), BF16),
                            pltpu.VMEM((IDX_DIM, IDX_HEADS * qb), BF16),
                            pltpu.VMEM((A_KV_HEADS, A_HEAD_DIM, gq), F32)],
            compiler_params=_params(("parallel", "arbitrary")),
            name="attn",
        )(aq, iq, ikw, lim, kb, vt, ikb).reshape(b, nj * qb, W_AQ))
    o = outs[0] if len(outs) == 1 else jnp.concatenate(outs, axis=1)
    return o[:, :tq].reshape(b * tq, W_AQ)


def _gla_constants(c):
    nlev = int(math.log2(c))
    t = np.arange(c)
    mats = [(t[None, :] <= t[:, None])]
    masks = [np.eye(c, dtype=bool)]
    for lev in range(nlev):
        m = c >> (lev + 1)
        ref_row = (t // (2 * m)) * 2 * m + m
        mats.append(t[None, :] <= ref_row[:, None])
        upper = (t & m) != 0
        same = (t[:, None] // (2 * m)) == (t[None, :] // (2 * m))
        masks.append(same & upper[:, None] & ~upper[None, :])
    return (jnp.asarray(np.concatenate(mats, 0).astype(np.float32), BF16),
            jnp.asarray(np.stack(masks).astype(np.float32)), nlev)


def _gla_kernel(q_ref, k_ref, v_ref, g_ref, gr_ref, gn_ref, mst_ref, msk_ref, s0_ref,
                ob_ref, sfin_ref, st_ref, *, c, nlev, nb):
    i = pl.program_id(1)
    hk = G_KEY_DIM

    @pl.when(i == 0)
    def _():
        for s in range(nb):
            for h in range(G_HEADS):
                st_ref[s, h] = s0_ref[s, h].T

    mst = mst_ref[...]
    gn = gn_ref[...]
    hs = lambda a, h: a[:, h * hk:(h + 1) * hk]
    for s in range(nb):
        g = g_ref[s]
        g_hi = g.astype(BF16)
        r1 = g - g_hi.astype(F32)
        g_mid = r1.astype(BF16)
        g_lo = (r1 - g_mid.astype(F32)).astype(BF16)
        bs = _dot(mst, g_hi) + _dot(mst, g_mid) + _dot(mst, g_lo)
        b = bs[0:c]
        q = q_ref[s]
        k = k_ref[s]
        vb = v_ref[s].astype(BF16)
        row = lax.broadcasted_iota(I32, q.shape, 0)

        qb = q.astype(BF16)
        kb = k.astype(BF16)
        attn = [_dot_nt(hs(qb, h), hs(kb, h)) * msk_ref[0] for h in range(G_HEADS)]
        for lev in range(nlev):
            m = c >> (lev + 1)
            upper = (row & m) != 0
            d = b - bs[(lev + 1) * c:(lev + 2) * c]
            e = jnp.exp(jnp.where(upper, d, -d))
            qt = jnp.where(upper, q * e, 0.0).astype(BF16)
            kt = jnp.where(upper, 0.0, k * e).astype(BF16)
            mk = msk_ref[lev + 1]
            for h in range(G_HEADS):
                attn[h] = attn[h] + _dot_nt(hs(qt, h), hs(kt, h)) * mk

        qe = (q * jnp.exp(b)).astype(BF16)
        b_last = b[c - 1:c, :]
        khat = (k * jnp.exp(b_last - b)).astype(BF16)
        dec = jnp.exp(b_last)
        gr = gr_ref[s]
        for h in range(G_HEADS):
            st = st_ref[s, h]
            o = _dot_nt(hs(qe, h), st.astype(BF16)) + _dot(attn[h].astype(BF16), hs(vb, h))
            st_ref[s, h] = st * hs(dec, h) + _dot_tn(hs(vb, h), hs(khat, h))
            ms = jnp.mean(o * o, axis=1, keepdims=True)
            grh = hs(gr, h)
            of = o * lax.rsqrt(ms + LN_EPS) * hs(gn, h) * (grh / (1.0 + jnp.exp(-grh)))
            ob_ref[s, :, h * hk:(h + 1) * hk] = of.astype(BF16)

    @pl.when(i == pl.num_programs(1) - 1)
    def _():
        for s in range(nb):
            for h in range(G_HEADS):
                sfin_ref[s, h] = st_ref[s, h].T


def _gla(gq, gk, gv, glog, gr, g_norm, s0, b, c):
    n = gq.shape[0]
    t = n // b
    nc = t // c
    nb = GLA_SEQS if b % GLA_SEQS == 0 else 1
    mst, msk, nlev = _gla_constants(c)
    seq = lambda a: a.reshape(b, t, W_G)
    row = pl.BlockSpec((nb, c, W_G), lambda bi, i: (bi, i, 0))
    const = lambda a: pl.BlockSpec(a.shape, lambda bi, i: (0,) * a.ndim)
    st_spec = pl.BlockSpec((nb, G_HEADS, G_KEY_DIM, G_VAL_DIM), lambda bi, i: (bi, 0, 0, 0))
    gn = g_norm.reshape(1, W_G)
    ob, s_fin = pl.pallas_call(
        functools.partial(_gla_kernel, c=c, nlev=nlev, nb=nb),
        grid=(b // nb, nc),
        in_specs=[row, row, row, row, row, const(gn), const(mst), const(msk), st_spec],
        out_specs=[row, st_spec],
        out_shape=[jax.ShapeDtypeStruct((b, t, W_G), BF16),
                   jax.ShapeDtypeStruct((b, G_HEADS, G_KEY_DIM, G_VAL_DIM), F32)],
        scratch_shapes=[pltpu.VMEM((nb, G_HEADS, G_VAL_DIM, G_KEY_DIM), F32)],
        compiler_params=_params(("parallel", "arbitrary")),
        name="gla",
    )(seq(gq), seq(gk), seq(gv), seq(glog), seq(gr), gn, mst, msk, s0)
    return ob.reshape(n, W_G), s_fin


def _layer_norm(z, g, b):
    mu = jnp.mean(z, axis=1, keepdims=True)
    zc = z - mu
    var = jnp.mean(zc * zc, axis=1, keepdims=True)
    return zc * lax.rsqrt(var + LN_EPS) * g + b


def _sigmoid(x):
    return 1.0 / (1.0 + jnp.exp(-x))


def _merge_kernel(x_ref, oa_ref, ob_ref, gate_ref, wpa_ref, wpb_ref, wout_ref, g1_ref, b1_ref,
                  wpq_ref, pk1_ref, pk2_ref, h1_ref, h1b_ref, s1t_ref, s2t_ref, *, alpha, d):
    ya = _dot(oa_ref[...], wpa_ref[...])
    yb = _dot(ob_ref[...], wpb_ref[...])
    m = _sigmoid(gate_ref[:, :d]) * ya + _sigmoid(gate_ref[:, d:]) * yb
    mix = _dot(m.astype(BF16), wout_ref[...])
    h1 = _layer_norm(alpha * x_ref[...] + mix, g1_ref[...], b1_ref[...])
    h1_ref[...] = h1
    h1t = h1.T.astype(BF16)
    h1b_ref[...] = h1t
    qpt = _dot(wpq_ref[...], h1t)
    for h in range(P_HEADS):
        for half, (pk_ref, st_ref) in enumerate(((pk1_ref, s1t_ref), (pk2_ref, s2t_ref))):
            c0 = (2 * h + half) * P_HALF
            st_ref[h * P_NKEYS:(h + 1) * P_NKEYS, :] = _dot(pk_ref[h], qpt[c0:c0 + P_HALF, :].astype(BF16))


def _merge(x2, oa, ob, gate, w_pa, w_pb, w_out, ln_g, ln_b, w_pq, pk1, pk2, alpha, tm):
    n, d = x2.shape
    hk = P_HEADS * P_NKEYS
    row = lambda w: pl.BlockSpec((tm, w), lambda i: (i, 0))
    col = pl.BlockSpec((hk, tm), lambda i: (0, i))
    const = lambda a: pl.BlockSpec(a.shape, lambda i: (0,) * a.ndim)
    ws = [w_pa.astype(BF16), w_pb.astype(BF16), w_out.astype(BF16), ln_g.reshape(1, d), ln_b.reshape(1, d),
          w_pq.T.astype(BF16), pk1.astype(BF16), pk2.astype(BF16)]
    return pl.pallas_call(
        functools.partial(_merge_kernel, alpha=alpha, d=d),
        grid=(n // tm,),
        in_specs=[row(d), row(W_AQ), row(W_G), row(2 * d)] + [const(w) for w in ws],
        out_specs=[row(d), pl.BlockSpec((d, tm), lambda i: (0, i)), col, col],
        out_shape=[jax.ShapeDtypeStruct((n, d), F32), jax.ShapeDtypeStruct((d, n), BF16),
                   jax.ShapeDtypeStruct((hk, n), F32), jax.ShapeDtypeStruct((hk, n), F32)],
        compiler_params=_params(("parallel",)),
        name="merge",
    )(x2, oa, ob, gate, *ws)


def _top_desc(s, count):
    tops = []
    for r in range(count):
        m = jnp.max(_fold_rows(s, jnp.maximum), axis=0, keepdims=True)
        tops.append(m)
        if r + 1 < count:
            s = jnp.where(s == m, -jnp.inf, s)
    return tops


def _select_kernel(s1t_ref, s2t_ref, cnt_ref, e1_ref, rank_ref, e2_ref):
    nk = P_NKEYS

    def head(h, _):
        r0 = pl.multiple_of(h * nk, nk)
        rows = pl.ds(r0, nk)
        s1 = s1t_ref[rows, :]
        s2 = s2t_ref[rows, :]
        v1 = _top_desc(s1, P_TOPK)
        v2 = _top_desc(s2, P_TOPK)
        pairs = [(a, b) for a in range(P_TOPK) for b in range(P_TOPK // (a + 1))]
        fill = [jnp.full_like(v1[0], -jnp.inf)] * (-len(pairs) % SUBLANES)
        cand = jnp.concatenate([v1[a] + v2[b] for a, b in pairs] + fill, axis=0)
        work, seen = cand, jnp.zeros_like(v1[0])
        tau = jnp.full_like(v1[0], -jnp.inf)
        for _ in range(P_TOPK):
            m = jnp.max(work, axis=0, keepdims=True)
            hit = work == m
            seen = seen + jnp.sum(jnp.where(hit, 1.0, 0.0), axis=0, keepdims=True)
            tau = jnp.maximum(tau, jnp.where(seen >= float(P_TOPK), m, -jnp.inf))
            work = jnp.where(hit, -jnp.inf, work)
        cmax = v1[0] + v2[0]
        zsum = jnp.sum(jnp.where(cand >= tau, jnp.exp(cand - cmax), 0.0), axis=0, keepdims=True)
        v2all = jnp.concatenate(v2, axis=0)
        cnt = jnp.zeros(s1.shape, F32)
        rank = jnp.full(s2.shape, float(P_TOPK), F32)
        for a in range(P_TOPK):
            cnt_a = jnp.sum(jnp.where(v1[a] + v2all >= tau, 1.0, 0.0), axis=0, keepdims=True)
            cnt = jnp.where(s1 == v1[a], cnt_a, cnt)
            rank = jnp.where(s2 == v2[a], float(a), rank)
        cnt_ref[rows, :] = cnt
        rank_ref[rows, :] = rank
        e1_ref[rows, :] = jnp.exp(s1 - v1[0]) / zsum
        e2_ref[rows, :] = jnp.exp(s2 - v2[0])
        return 0

    lax.fori_loop(0, P_HEADS, head, 0)


def _select(s1t, s2t, tn):
    hk, n = s1t.shape
    col = pl.BlockSpec((hk, tn), lambda i: (0, i))
    return pl.pallas_call(
        _select_kernel,
        grid=(n // tn,),
        in_specs=[col, col],
        out_specs=[col, col, col, col],
        out_shape=[jax.ShapeDtypeStruct((hk, n), F32)] * 4,
        compiler_params=_params(("parallel",)),
        name="select",
    )(s1t, s2t)


I1_PER_STEP = 8
E_PER_STEP = I1_PER_STEP * P_NKEYS
I2_BLOCK = 16


def _gelu(x):
    return 0.5 * x * (1.0 + lax.erf(x * (2.0 ** -0.5)))


def _peer_steps(t, n_steps, ng):
    item = lambda d: jnp.clip(t - d, 0, n_steps - 1)
    return [(item(d) // ng, item(d) % ng) for d in range(3)]


def _peer_kernel(hb_ref, pu_ref, pvt_ref, cnt_ref, e1_ref, rank_ref, e2_ref, h1_ref, g2_ref, b2_ref,
                 y_ref, a_ref, gw_ref, acc_ref, rank_s, e2_s, *, tn, alpha, n_steps, ng):
    t = pl.program_id(0)
    (_, _), (_, g), (_, g_out) = _peer_steps(t, n_steps, ng)
    slot = t % 2
    prev = 1 - slot

    @pl.when(t == 0)
    def _():
        a_ref[...] = jnp.zeros(a_ref.shape, F32)
        gw_ref[...] = jnp.zeros(gw_ref.shape, BF16)

    @pl.when(g == 0)
    def _():
        rank_s[...] = rank_ref[...].astype(BF16)
        e2_s[...] = e2_ref[...].astype(BF16)

    a_ref[slot] = _dot(pu_ref[...], hb_ref[...])

    assert I1_PER_STEP == SUBLANES
    for lt in range(tn // LANES):
        ls = slice(lt * LANES, (lt + 1) * LANES)
        grp = lambda ref, h: ref[pl.ds(pl.multiple_of(h * P_NKEYS + g * I1_PER_STEP, SUBLANES), SUBLANES), ls]
        cnt8 = [grp(cnt_ref, h) for h in range(P_HEADS)]
        e18 = [grp(e1_ref, h) for h in range(P_HEADS)]
        for j in range(I1_PER_STEP):
            bcast = lambda a8: jnp.broadcast_to(a8[j:j + 1, :], (I2_BLOCK, LANES)).astype(BF16)
            cnt = [bcast(cnt8[h]) for h in range(P_HEADS)]
            e1 = [bcast(e18[h]) for h in range(P_HEADS)]
            for i2b in range(P_NKEYS // I2_BLOCK):
                w = jnp.zeros((I2_BLOCK, LANES), BF16)
                for h in range(P_HEADS):
                    rows = slice(h * P_NKEYS + i2b * I2_BLOCK, h * P_NKEYS + (i2b + 1) * I2_BLOCK)
                    w = w + jnp.where(rank_s[rows, ls] < cnt[h], e2_s[rows, ls] * e1[h], jnp.zeros((), BF16))
                arow = slice(j * P_NKEYS + i2b * I2_BLOCK, j * P_NKEYS + (i2b + 1) * I2_BLOCK)
                gw_ref[prev, arow, ls] = (w.astype(F32) * _gelu(a_ref[prev, arow, ls])).astype(BF16)

    part = _dot(pvt_ref[0], gw_ref[slot])
    acc_ref[...] = jnp.where(g_out == 0, part, acc_ref[...] + part)

    @pl.when(jnp.logical_and(g_out == ng - 1, t >= 2))
    def _():
        y_ref[...] = _layer_norm(alpha * h1_ref[...] + acc_ref[...].T, g2_ref[...], b2_ref[...])


def _peer(h1, h1b, cnt, e1, rank, e2, pu, pv, ln_g, ln_b, alpha, tn):
    n, d = h1.shape
    hk = P_HEADS * P_NKEYS
    ng = pu.shape[0] // E_PER_STEP
    n_steps = (n // tn) * ng
    stage = lambda k, f: (lambda t: f(*_peer_steps(t, n_steps, ng)[k]))
    const = pl.BlockSpec((1, d), lambda t: (0, 0))
    routing = pl.BlockSpec((hk, tn), stage(1, lambda i, g: (0, i)))
    pvt = pv.astype(BF16).reshape(ng, E_PER_STEP, d).transpose(0, 2, 1)
    return pl.pallas_call(
        functools.partial(_peer_kernel, tn=tn, alpha=alpha, n_steps=n_steps, ng=ng),
        grid=(n_steps + 2,),
        in_specs=[pl.BlockSpec((d, tn), stage(0, lambda i, g: (0, i))),
                  pl.BlockSpec((E_PER_STEP, d), stage(0, lambda i, g: (g, 0))),
                  pl.BlockSpec((1, d, E_PER_STEP), stage(2, lambda i, g: (g, 0, 0))),
                  routing, routing, routing, routing,
                  pl.BlockSpec((tn, d), stage(2, lambda i, g: (i, 0))), const, const],
        out_specs=pl.BlockSpec((tn, d), stage(2, lambda i, g: (i, 0))),
        out_shape=jax.ShapeDtypeStruct((n, d), F32),
        scratch_shapes=[pltpu.VMEM((2, E_PER_STEP, tn), F32), pltpu.VMEM((2, E_PER_STEP, tn), BF16),
                        pltpu.VMEM((d, tn), F32), pltpu.VMEM((hk, tn), BF16), pltpu.VMEM((hk, tn), BF16)],
        compiler_params=_params(("arbitrary",)),
        name="peer",
    )(h1b, pu.astype(BF16), pvt, cnt, e1, rank, e2, h1, ln_g.reshape(1, d), ln_b.reshape(1, d))


def _pick_tile(n, pref):
    t = min(n, pref)
    assert n % t == 0
    return t


def _layer(x, pos, limits, past, s0, w, *, chunk, alpha):
    b, t, d = x.shape
    n = b * t
    x2 = x.reshape(n, d)
    tm = _pick_tile(n, TOKEN_TILE)
    (aq, k32, v32, kb, vb, iq, ikw, ikb, gq, gk, gv, glog, gr, gate) = _proj(
        x2, pos, w["w_in"], w["w_fa"], w["b_fa"], tm)

    kb3, vb3, ikb3 = kb.reshape(b, t, LANES), vb.reshape(b, t, LANES), ikb.reshape(b, t, LANES)
    if past is not None:
        ck, cv, cik = past
        p = ck.shape[1]
        kb3 = jnp.concatenate([ck.reshape(b, p, LANES).astype(BF16), kb3], axis=1)
        vb3 = jnp.concatenate([cv.reshape(b, p, LANES).astype(BF16), vb3], axis=1)
        ikb3 = jnp.concatenate([cik.astype(BF16), ikb3[:, :, :IDX_DIM]], axis=1)
    topk = min(IDX_TOPK, kb3.shape[1] // 4)
    o_a = _attention(aq, iq, ikw, limits, kb3, vb3, ikb3, b, topk)

    o_b, s_fin = _gla(gq, gk, gv, glog, gr, w["g_gla_norm"], s0, b, chunk)

    h1, h1b, s1t, s2t = _merge(x2, o_a, o_b, gate, w["w_pa"], w["w_pb"], w["w_out"],
                               w["ln1_g"], w["ln1_b"], w["w_pq"], w["pk1"], w["pk2"], alpha, tm)
    cnt, e1, rank, e2 = _select(s1t, s2t, tm)
    y = _peer(h1, h1b, cnt, e1, rank, e2, w["pu"], w["pv"], w["ln2_g"], w["ln2_b"], alpha,
              _pick_tile(n, PEER_TOKEN_TILE))

    k_out = k32.reshape(b, t, A_KV_HEADS, A_HEAD_DIM)
    v_out = v32.reshape(b, t, A_KV_HEADS, A_HEAD_DIM)
    ik_out = ikw[:, :IDX_DIM].reshape(b, t, IDX_DIM)
    return y.reshape(b, t, d), k_out, v_out, ik_out, s_fin


def kernel(x_prompt, x_sample, cache_k, cache_v, cache_idx_k, state_gla, w_in, w_fa, b_fa, g_gla_norm,
           w_pa, w_pb, w_out, ln1_g, ln1_b, w_pq, pk1, pk2, pu, pv, ln2_g, ln2_b):
    depth = w_in.shape[0]
    alpha = (2.0 * depth) ** 0.25
    bp, tp, _ = x_prompt.shape
    bs, ts, _ = x_sample.shape
    past_len = cache_k.shape[2]
    pos_p = jnp.arange(tp)
    pos_s = past_len + jnp.arange(ts)
    lim_p = (np.arange(tp) // CHUNK + 1) * CHUNK
    lim_s = np.full((ts,), past_len + ts)
    names = ("w_in", "w_fa", "b_fa", "g_gla_norm", "w_pa", "w_pb", "w_out", "ln1_g", "ln1_b",
             "w_pq", "pk1", "pk2", "pu", "pv", "ln2_g", "ln2_b")
    stacked = (w_in, w_fa, b_fa, g_gla_norm, w_pa, w_pb, w_out, ln1_g, ln1_b, w_pq, pk1, pk2, pu, pv, ln2_g, ln2_b)
    hp, hs = x_prompt, x_sample
    outs_p, outs_s = [], []
    for l in range(depth):
        w = {nm: a[l] for nm, a in zip(names, stacked)}
        s0 = jnp.zeros((bp, G_HEADS, G_KEY_DIM, G_VAL_DIM), F32)
        hp, *rest = _layer(hp, pos_p, lim_p, None, s0, w, chunk=CHUNK, alpha=alpha)
        outs_p.append(rest)
        hs, *rest = _layer(hs, pos_s, lim_s, (cache_k[l], cache_v[l], cache_idx_k[l]), state_gla[l], w,
                           chunk=ts, alpha=alpha)
        outs_s.append(rest)
    stack = lambda outs, i: jnp.stack([o[i] for o in outs])
    return (hp, hs, stack(outs_p, 0), stack(outs_p, 1), stack(outs_p, 2), stack(outs_p, 3),
            stack(outs_s, 0), stack(outs_s, 1), stack(outs_s, 2), stack(outs_s, 3))
```

```python
import functools
import math

import numpy as np
import jax
import jax.numpy as jnp
from jax import lax
from jax.experimental import pallas as pl
from jax.experimental.pallas import tpu as pltpu

F32 = jnp.float32
BF16 = jnp.bfloat16
I32 = jnp.int32

LANES = 128
SUBLANES = 8
VMEM_LIMIT = 56 << 20

CHUNK = 64
A_HEADS = 8
A_KV_HEADS = 2
A_HEAD_DIM = 64
A_GROUP = A_HEADS // A_KV_HEADS
IDX_HEADS = 4
IDX_DIM = 64
IDX_TOPK = 256
ROPE_THETA = 10000.0
G_HEADS = 4
G_KEY_DIM = 128
G_VAL_DIM = 128
G_LOWRANK = 16
G_TAU = 16.0
P_HEADS = 8
P_NKEYS = 128
P_HALF = 128
P_TOPK = 16
LN_EPS = 1e-5

W_AQ = A_HEADS * A_HEAD_DIM
W_AK = A_KV_HEADS * A_HEAD_DIM
W_IQ = IDX_HEADS * IDX_DIM
W_G = G_HEADS * G_KEY_DIM
IN_SIZES = (W_AQ, W_AK, W_AK, W_IQ, IDX_DIM, IDX_HEADS, W_G, W_G, W_G, G_LOWRANK, W_G, None)

TOKEN_TILE = 256
PEER_TOKEN_TILE = 512
Q_BLOCK = 128
KEY_TILE = 512
GLA_SEQS = 4

INT_MIN = -(2 ** 31)
NEG_INF_KEY = -2139095041
NEG_BIG = -1e30


def _dot(a, b):
    return jnp.dot(a, b, preferred_element_type=F32)


def _dot_nt(a, b):
    return lax.dot_general(a, b, (((1,), (1,)), ((), ())), preferred_element_type=F32)


def _dot_tn(a, b):
    return lax.dot_general(a, b, (((0,), (0,)), ((), ())), preferred_element_type=F32)


def _sort_key(x):
    bits = pltpu.bitcast(x, I32)
    key = jnp.where(bits < 0, bits ^ 0x7FFFFFFF, bits)
    return jnp.where(key == -1, 0, key)


def _fold_rows(x, op):
    x = x.reshape(x.shape[0] // SUBLANES, SUBLANES, x.shape[1])
    while x.shape[0] > 1:
        half = x.shape[0] // 2
        folded = op(x[:half], x[half:2 * half])
        x = folded if x.shape[0] == 2 * half else jnp.concatenate([folded, x[2 * half:]], axis=0)
    return x[0]


def _params(sem):
    return pltpu.CompilerParams(dimension_semantics=sem, vmem_limit_bytes=VMEM_LIMIT)


_PG_AQ, _PG_K, _PG_V, _PG_IQ, _PG_IKW, _PG_GQ, _PG_GK, _PG_GV, _PG_GF, _PG_GR, _PG_GATE = range(11)


def _pack_layout(d_model):
    widths = [W_AQ, W_AK, W_AK, W_IQ, LANES, W_G, W_G, W_G, LANES, W_G, 2 * d_model]
    offs = np.concatenate([[0], np.cumsum(widths)]).tolist()
    return widths, offs


def _pack_w_in(w_in):
    d = w_in.shape[0]
    sizes = list(IN_SIZES[:-1]) + [2 * d]
    cuts = np.cumsum(sizes)[:-1].tolist()
    aq, ak, av, iq, ik, iw, gq, gk, gv, gf, gr, gate = jnp.split(w_in, cuts, axis=1)
    z = lambda n: jnp.zeros((d, n), w_in.dtype)
    ikw = jnp.concatenate([ik, iw, z(LANES - IDX_DIM - IDX_HEADS)], axis=1)
    gfp = jnp.concatenate([gf, z(LANES - G_LOWRANK)], axis=1)
    return jnp.concatenate([aq, ak, av, iq, ikw, gq, gk, gv, gfp, gr, gate], axis=1).astype(BF16)


def _rope_tables(pos):
    half = A_HEAD_DIM // 2
    inv = ROPE_THETA ** (-jnp.arange(half, dtype=F32) / half)
    ang = pos.astype(F32)[:, None] * inv[None, :]
    c, s = jnp.cos(ang), jnp.sin(ang)
    return jnp.concatenate([c, c, c, c], -1), jnp.concatenate([-s, s, -s, s], -1)


def _proj_kernel(x_ref, w_ref, wfa_ref, bfa_ref, cos_ref, sin_ref,
                 aq_ref, k_ref, v_ref, kb_ref, vb_ref, iq_ref, ikw_ref, ikb_ref,
                 gq_ref, gk_ref, gv_ref, glog_ref, gr_ref, gate_ref, *, offs, widths):
    xb = x_ref[...].astype(BF16)
    cos = cos_ref[...]
    sin = sin_ref[...]
    lane = lax.broadcasted_iota(I32, cos.shape, 1)
    first_half = (lane & (A_HEAD_DIM // 2)) == 0

    def proj(g):
        return _dot(xb, w_ref[:, offs[g]:offs[g] + widths[g]])

    def rope_slab(y):
        fwd = pltpu.roll(y, LANES - A_HEAD_DIM // 2, 1)
        bwd = pltpu.roll(y, A_HEAD_DIM // 2, 1)
        return y * cos + jnp.where(first_half, fwd, bwd) * sin

    def rope(y):
        return [rope_slab(y[:, s * LANES:(s + 1) * LANES]) for s in range(y.shape[1] // LANES)]

    for s, slab in enumerate(rope(proj(_PG_AQ))):
        aq_ref[:, s * LANES:(s + 1) * LANES] = (slab * (A_HEAD_DIM ** -0.5)).astype(BF16)
    k = rope(proj(_PG_K))[0]
    k_ref[...] = k
    kb_ref[...] = k.astype(BF16)
    v = proj(_PG_V)
    v_ref[...] = v
    vb_ref[...] = v.astype(BF16)
    for s, slab in enumerate(rope(proj(_PG_IQ))):
        iq_ref[:, s * LANES:(s + 1) * LANES] = (slab * (IDX_DIM ** -0.5)).astype(BF16)
    raw = proj(_PG_IKW)
    ikw = jnp.where(lane < IDX_DIM, rope_slab(raw), raw * (IDX_HEADS ** -0.5))
    ikw_ref[...] = ikw
    ikb_ref[...] = ikw.astype(BF16)
    gq_ref[...] = proj(_PG_GQ) * (G_KEY_DIM ** -0.5)
    gk_ref[...] = proj(_PG_GK)
    gv_ref[...] = proj(_PG_GV)
    z = _dot(proj(_PG_GF).astype(BF16), wfa_ref[...]) + bfa_ref[...]
    glog_ref[...] = (jnp.minimum(z, 0.0) - jnp.log1p(jnp.exp(-jnp.abs(z)))) * (1.0 / G_TAU)
    gr_ref[...] = proj(_PG_GR)
    gate_ref[...] = proj(_PG_GATE)


def _proj(x2, pos, w_in, w_fa, b_fa, tm):
    n, d = x2.shape
    t = pos.shape[0]
    widths, offs = _pack_layout(d)
    wp = _pack_w_in(w_in)
    wfa = jnp.concatenate([w_fa, jnp.zeros((LANES - G_LOWRANK, W_G), w_fa.dtype)], 0).astype(BF16)
    cos, sin = _rope_tables(pos)
    if tm > t:
        cos, sin = jnp.tile(cos, (tm // t, 1)), jnp.tile(sin, (tm // t, 1))
    nper = cos.shape[0] // tm
    row = lambda w: pl.BlockSpec((tm, w), lambda i: (i, 0))
    const = lambda a: pl.BlockSpec(a.shape, lambda i: (0, 0))
    tab = pl.BlockSpec((tm, LANES), lambda i: (i % nper, 0))
    outs = [(W_AQ, BF16), (LANES, F32), (LANES, F32), (LANES, BF16), (LANES, BF16), (W_IQ, BF16),
            (LANES, F32), (LANES, BF16), (W_G, F32), (W_G, F32), (W_G, F32), (W_G, F32), (W_G, F32),
            (2 * d, F32)]
    bfa = b_fa.reshape(1, W_G)
    return pl.pallas_call(
        functools.partial(_proj_kernel, offs=offs, widths=widths),
        grid=(n // tm,),
        in_specs=[row(d), const(wp), const(wfa), const(bfa), tab, tab],
        out_specs=[row(w) for w, _ in outs],
        out_shape=[jax.ShapeDtypeStruct((n, w), dt) for w, dt in outs],
        compiler_params=_params(("parallel",)),
        name="proj",
    )(x2, wp, wfa, bfa, cos, sin)


def _attn_kernel(aq_ref, iq_ref, ikw_ref, lim_ref, kb_ref, vt_ref, ikb_ref, o_ref,
                 keys_ref, qs_ref, iqs_ref, acc_ref, *, kt_w, nkt, topk, idx_bits):
    qb = Q_BLOCK

    aq_t = aq_ref[...].astype(F32).T
    for h in range(A_HEADS):
        qs_ref[h // A_GROUP, :, (h % A_GROUP) * qb:(h % A_GROUP + 1) * qb] = \
            aq_t[h * A_HEAD_DIM:(h + 1) * A_HEAD_DIM, :].astype(BF16)
    iq_t = iq_ref[...].astype(F32).T
    for h in range(IDX_HEADS):
        iqs_ref[:, h * qb:(h + 1) * qb] = iq_t[h * IDX_DIM:(h + 1) * IDX_DIM, :].astype(BF16)

    ikw_t = ikw_ref[...].T
    iw_rows = [ikw_t[IDX_DIM + h:IDX_DIM + h + 1, :] for h in range(IDX_HEADS)]
    lim = lim_ref[0, 0:1, :]
    sub = lax.broadcasted_iota(I32, (kt_w, qb), 0)

    def score_tile(kt, _):
        base = pl.multiple_of(kt * kt_w, kt_w)
        ik_t = ikb_ref[0, pl.ds(base, kt_w), :][:, :IDX_DIM]
        s = jnp.maximum(_dot(ik_t, iqs_ref[...]), 0.0)
        score = jnp.zeros((kt_w, qb), F32)
        for h in range(IDX_HEADS):
            score = score + s[:, h * qb:(h + 1) * qb] * iw_rows[h]
        keys_ref[pl.ds(base, kt_w), :] = jnp.where(sub + base < lim, _sort_key(score), NEG_INF_KEY)
        return 0

    lax.fori_loop(0, nkt, score_tile, 0, unroll=min(nkt, 2))

    def count(pred):
        acc = jnp.zeros((SUBLANES, qb), F32)
        for kt in range(nkt):
            acc = acc + _fold_rows(pred(keys_ref[kt * kt_w:(kt + 1) * kt_w, :], sub + kt * kt_w), jnp.add)
        return jnp.sum(acc, axis=0, keepdims=True)

    def count_ge(t_row):
        return count(lambda kk, idx: jnp.where(kk >= t_row, 1.0, 0.0))

    kf = float(topk)
    thr = jnp.where(count_ge(jnp.zeros((1, qb), I32)) >= kf, 0, INT_MIN).astype(I32)

    def thr_bit(i, t):
        cand = t + jnp.left_shift(jnp.int32(1), 30 - i)
        return jnp.where(count_ge(cand) >= kf, cand, t)

    thr = lax.fori_loop(0, 31, thr_bit, thr)
    n_gt = count_ge(thr + 1)
    n_eq = count_ge(thr) - n_gt
    need = kf - n_gt
    finite = thr > NEG_INF_KEY
    excess = jnp.where(finite, jnp.where(n_eq > need, 1.0, 0.0), 0.0)

    def count_eq_below(j_row):
        return count(lambda kk, idx: jnp.where(kk == thr, jnp.where(idx < j_row, 1.0, 0.0), 0.0))

    def resolve_ties():
        def bit(i, jc):
            cand = jc + jnp.left_shift(jnp.int32(1), idx_bits - 1 - i)
            return jnp.where(count_eq_below(cand) <= need - 1.0, cand, jc)
        jc = lax.fori_loop(0, idx_bits, bit, jnp.zeros((1, qb), I32))
        return jnp.where(finite, jc, -1)

    cut = lax.cond(jnp.max(excess) > 0.0, resolve_ties,
                   lambda: jnp.where(finite, 2 ** 30, -1).astype(I32))

    acc_ref[...] = jnp.zeros(acc_ref.shape, F32)
    gq = A_GROUP * qb

    def attend(kt, carry):
        base = pl.multiple_of(kt * kt_w, kt_w)
        k_t = kb_ref[0, pl.ds(base, kt_w), :]
        kk = keys_ref[pl.ds(base, kt_w), :]
        tie = jnp.where(sub + base <= cut, 0.0, NEG_BIG)
        bias = jnp.where(kk > thr, 0.0, jnp.where(kk == thr, tie, NEG_BIG))
        bias = jnp.concatenate([bias] * A_GROUP, axis=1)
        out = []
        for n in range(A_KV_HEADS):
            m_old, l_old = carry[2 * n], carry[2 * n + 1]
            logits = bias + _dot(k_t[:, n * A_HEAD_DIM:(n + 1) * A_HEAD_DIM], qs_ref[n])
            m_new = jnp.maximum(m_old, jnp.max(_fold_rows(logits, jnp.maximum), axis=0, keepdims=True))
            alpha = jnp.exp(m_old - m_new)
            p = jnp.exp(logits - m_new)
            l_new = alpha * l_old + jnp.sum(_fold_rows(p, jnp.add), axis=0, keepdims=True)
            v_t = vt_ref[0, n * A_HEAD_DIM:(n + 1) * A_HEAD_DIM, pl.ds(base, kt_w)]
            acc_ref[n] = alpha * acc_ref[n] + _dot(v_t, p.astype(BF16))
            out += [m_new, l_new]
        return tuple(out)

    init = (jnp.full((1, gq), NEG_BIG, F32), jnp.zeros((1, gq), F32)) * A_KV_HEADS
    fin = lax.fori_loop(0, nkt, attend, init, unroll=min(nkt, 2))
    l_row = [fin[2 * n + 1] for n in range(A_KV_HEADS)]

    o_t = jnp.concatenate([acc_ref[n] / l_row[n] for n in range(A_KV_HEADS)], axis=0)
    for n in range(A_KV_HEADS):
        for g in range(A_GROUP):
            h = n * A_GROUP + g
            blk = o_t[n * A_HEAD_DIM:(n + 1) * A_HEAD_DIM, g * qb:(g + 1) * qb]
            o_ref[:, h * A_HEAD_DIM:(h + 1) * A_HEAD_DIM] = blk.T.astype(BF16)


def _attention(aq, iq, ikw, limits, kb, vb, ikb, b, topk):
    qb, kt_w = Q_BLOCK, KEY_TILE
    tq = aq.shape[0] // b
    tq_pad = -(-tq // qb) * qb
    limits = np.asarray(limits)
    if tq_pad != tq:
        padq = lambda a: jnp.pad(a.reshape(b, tq, -1), ((0, 0), (0, tq_pad - tq), (0, 0))).reshape(b * tq_pad, -1)
        aq, iq, ikw = padq(aq), padq(iq), padq(ikw)
        limits = np.concatenate([limits, np.full((tq_pad - tq,), limits[-1])])
    l_all = kb.shape[1]
    l_pad = -(-l_all // kt_w) * kt_w
    if l_pad != l_all:
        padl = lambda a: jnp.pad(a, ((0, 0), (0, l_pad - l_all), (0, 0)))
        kb, vb, ikb = padl(kb), padl(vb), padl(ikb)
    vt = jnp.swapaxes(vb, 1, 2)
    nq = tq_pad // qb
    lim_blk = limits.reshape(nq, qb)
    nkt = np.minimum(-(-lim_blk.max(axis=1) // kt_w), l_pad // kt_w)
    lim = jnp.asarray(np.broadcast_to(lim_blk[:, None, :], (nq, SUBLANES, qb)).astype(np.int32))
    idx_bits = max(1, int(math.ceil(math.log2(l_pad))))
    keys = lambda a: pl.BlockSpec((1,) + a.shape[1:], lambda bi, j: (bi, 0, 0))
    gq = A_GROUP * qb
    runs, j0 = [], 0
    for j in range(1, nq + 1):
        if j == nq or nkt[j] != nkt[j0]:
            runs.append((j0, j - j0, int(nkt[j0])))
            j0 = j
    outs = []
    for j0, nj, n_tiles in runs:
        qrow = lambda w, j0=j0: pl.BlockSpec((qb, w), lambda bi, j: (bi * nq + j0 + j, 0))
        outs.append(pl.pallas_call(
            functools.partial(_attn_kernel, kt_w=kt_w, nkt=n_tiles, topk=topk, idx_bits=idx_bits),
            grid=(b, nj),
            in_specs=[qrow(W_AQ), qrow(W_IQ), qrow(LANES),
                      pl.BlockSpec((1, SUBLANES, qb), lambda bi, j, j0=j0: (j0 + j, 0, 0)),
                      keys(kb), keys(vt), keys(ikb)],
            out_specs=pl.BlockSpec((qb, W_AQ), lambda bi, j, nj=nj: (bi * nj + j, 0)),
            out_shape=jax.ShapeDtypeStruct((b * nj * qb, W_AQ), BF16),
            scratch_shapes=[pltpu.VMEM((n_tiles * kt_w, qb), I32),
                            pltpu.VMEM((A_KV_HEADS, A_HEAD_DIM, gq), BF16),
                            pltpu.VMEM((IDX_DIM, IDX_HEADS * qb), BF16),
                            pltpu.VMEM((A_KV_HEADS, A_HEAD_DIM, gq), F32)],
            compiler_params=_params(("parallel", "arbitrary")),
            name="attn",
        )(aq, iq, ikw, lim, kb, vt, ikb).reshape(b, nj * qb, W_AQ))
    o = outs[0] if len(outs) == 1 else jnp.concatenate(outs, axis=1)
    return o[:, :tq].reshape(b * tq, W_AQ)


def _gla_constants(c):
    nlev = int(math.log2(c))
    t = np.arange(c)
    mats = [(t[None, :] <= t[:, None])]
    masks = [np.eye(c, dtype=bool)]
    for lev in range(nlev):
        m = c >> (lev + 1)
        ref_row = (t // (2 * m)) * 2 * m + m
        mats.append(t[None, :] <= ref_row[:, None])
        upper = (t & m) != 0
        same = (t[:, None] // (2 * m)) == (t[None, :] // (2 * m))
        masks.append(same & upper[:, None] & ~upper[None, :])
    return (jnp.asarray(np.concatenate(mats, 0).astype(np.float32), BF16),
            jnp.asarray(np.stack(masks).astype(np.float32)), nlev)


def _gla_kernel(q_ref, k_ref, v_ref, g_ref, gr_ref, gn_ref, mst_ref, msk_ref, s0_ref,
                ob_ref, sfin_ref, st_ref, *, c, nlev, nb):
    i = pl.program_id(1)
    hk = G_KEY_DIM

    @pl.when(i == 0)
    def _():
        for s in range(nb):
            for h in range(G_HEADS):
                st_ref[s, h] = s0_ref[s, h].T

    mst = mst_ref[...]
    gn = gn_ref[...]
    hs = lambda a, h: a[:, h * hk:(h + 1) * hk]
    for s in range(nb):
        g = g_ref[s]
        g_hi = g.astype(BF16)
        r1 = g - g_hi.astype(F32)
        g_mid = r1.astype(BF16)
        g_lo = (r1 - g_mid.astype(F32)).astype(BF16)
        bs = _dot(mst, g_hi) + _dot(mst, g_mid) + _dot(mst, g_lo)
        b = bs[0:c]
        q = q_ref[s]
        k = k_ref[s]
        vb = v_ref[s].astype(BF16)
        row = lax.broadcasted_iota(I32, q.shape, 0)

        qb = q.astype(BF16)
        kb = k.astype(BF16)
        attn = [_dot_nt(hs(qb, h), hs(kb, h)) * msk_ref[0] for h in range(G_HEADS)]
        for lev in range(nlev):
            m = c >> (lev + 1)
            upper = (row & m) != 0
            d = b - bs[(lev + 1) * c:(lev + 2) * c]
            e = jnp.exp(jnp.where(upper, d, -d))
            qt = jnp.where(upper, q * e, 0.0).astype(BF16)
            kt = jnp.where(upper, 0.0, k * e).astype(BF16)
            mk = msk_ref[lev + 1]
            for h in range(G_HEADS):
                attn[h] = attn[h] + _dot_nt(hs(qt, h), hs(kt, h)) * mk

        qe = (q * jnp.exp(b)).astype(BF16)
        b_last = b[c - 1:c, :]
        khat = (k * jnp.exp(b_last - b)).astype(BF16)
        dec = jnp.exp(b_last)
        gr = gr_ref[s]
        for h in range(G_HEADS):
            st = st_ref[s, h]
            o = _dot_nt(hs(qe, h), st.astype(BF16)) + _dot(attn[h].astype(BF16), hs(vb, h))
            st_ref[s, h] = st * hs(dec, h) + _dot_tn(hs(vb, h), hs(khat, h))
            ms = jnp.mean(o * o, axis=1, keepdims=True)
            grh = hs(gr, h)
            of = o * lax.rsqrt(ms + LN_EPS) * hs(gn, h) * (grh / (1.0 + jnp.exp(-grh)))
            ob_ref[s, :, h * hk:(h + 1) * hk] = of.astype(BF16)

    @pl.when(i == pl.num_programs(1) - 1)
    def _():
        for s in range(nb):
            for h in range(G_HEADS):
                sfin_ref[s, h] = st_ref[s, h].T


def _gla(gq, gk, gv, glog, gr, g_norm, s0, b, c):
    n = gq.shape[0]
    t = n // b
    nc = t // c
    nb = GLA_SEQS if b % GLA_SEQS == 0 else 1
    mst, msk, nlev = _gla_constants(c)
    seq = lambda a: a.reshape(b, t, W_G)
    row = pl.BlockSpec((nb, c, W_G), lambda bi, i: (bi, i, 0))
    const = lambda a: pl.BlockSpec(a.shape, lambda bi, i: (0,) * a.ndim)
    st_spec = pl.BlockSpec((nb, G_HEADS, G_KEY_DIM, G_VAL_DIM), lambda bi, i: (bi, 0, 0, 0))
    gn = g_norm.reshape(1, W_G)
    ob, s_fin = pl.pallas_call(
        functools.partial(_gla_kernel, c=c, nlev=nlev, nb=nb),
        grid=(b // nb, nc),
        in_specs=[row, row, row, row, row, const(gn), const(mst), const(msk), st_spec],
        out_specs=[row, st_spec],
        out_shape=[jax.ShapeDtypeStruct((b, t, W_G), BF16),
                   jax.ShapeDtypeStruct((b, G_HEADS, G_KEY_DIM, G_VAL_DIM), F32)],
        scratch_shapes=[pltpu.VMEM((nb, G_HEADS, G_VAL_DIM, G_KEY_DIM), F32)],
        compiler_params=_params(("parallel", "arbitrary")),
        name="gla",
    )(seq(gq), seq(gk), seq(gv), seq(glog), seq(gr), gn, mst, msk, s0)
    return ob.reshape(n, W_G), s_fin


def _layer_norm(z, g, b):
    mu = jnp.mean(z, axis=1, keepdims=True)
    zc = z - mu
    var = jnp.mean(zc * zc, axis=1, keepdims=True)
    return zc * lax.rsqrt(var + LN_EPS) * g + b


def _sigmoid(x):
    return 1.0 / (1.0 + jnp.exp(-x))


def _merge_kernel(x_ref, oa_ref, ob_ref, gate_ref, wpa_ref, wpb_ref, wout_ref, g1_ref, b1_ref,
                  wpq_ref, pk1_ref, pk2_ref, h1_ref, h1b_ref, s1t_ref, s2t_ref, *, alpha, d):
    ya = _dot(oa_ref[...], wpa_ref[...])
    yb = _dot(ob_ref[...], wpb_ref[...])
    m = _sigmoid(gate_ref[:, :d]) * ya + _sigmoid(gate_ref[:, d:]) * yb
    mix = _dot(m.astype(BF16), wout_ref[...])
    h1 = _layer_norm(alpha * x_ref[...] + mix, g1_ref[...], b1_ref[...])
    h1_ref[...] = h1
    h1b_ref[...] = h1.T.astype(BF16)
    qp = _dot(h1.astype(BF16), wpq_ref[...])
    for h in range(P_HEADS):
        for half, (pk_ref, st_ref) in enumerate(((pk1_ref, s1t_ref), (pk2_ref, s2t_ref))):
            c0 = (2 * h + half) * P_HALF
            st_ref[h * P_NKEYS:(h + 1) * P_NKEYS, :] = _dot_nt(pk_ref[h], qp[:, c0:c0 + P_HALF].astype(BF16))


def _merge(x2, oa, ob, gate, w_pa, w_pb, w_out, ln_g, ln_b, w_pq, pk1, pk2, alpha, tm):
    n, d = x2.shape
    hk = P_HEADS * P_NKEYS
    row = lambda w: pl.BlockSpec((tm, w), lambda i: (i, 0))
    col = pl.BlockSpec((hk, tm), lambda i: (0, i))
    const = lambda a: pl.BlockSpec(a.shape, lambda i: (0,) * a.ndim)
    ws = [w_pa.astype(BF16), w_pb.astype(BF16), w_out.astype(BF16), ln_g.reshape(1, d), ln_b.reshape(1, d),
          w_pq.astype(BF16), pk1.astype(BF16), pk2.astype(BF16)]
    return pl.pallas_call(
        functools.partial(_merge_kernel, alpha=alpha, d=d),
        grid=(n // tm,),
        in_specs=[row(d), row(W_AQ), row(W_G), row(2 * d)] + [const(w) for w in ws],
        out_specs=[row(d), pl.BlockSpec((d, tm), lambda i: (0, i)), col, col],
        out_shape=[jax.ShapeDtypeStruct((n, d), F32), jax.ShapeDtypeStruct((d, n), BF16),
                   jax.ShapeDtypeStruct((hk, n), F32), jax.ShapeDtypeStruct((hk, n), F32)],
        compiler_params=_params(("parallel",)),
        name="merge",
    )(x2, oa, ob, gate, *ws)


def _top_desc(s, count):
    tops = []
    for r in range(count):
        m = jnp.max(_fold_rows(s, jnp.maximum), axis=0, keepdims=True)
        tops.append(m)
        if r + 1 < count:
            s = jnp.where(s == m, -jnp.inf, s)
    return tops


def _select_kernel(s1t_ref, s2t_ref, cnt_ref, e1_ref, rank_ref, e2_ref):
    nk = P_NKEYS

    def head(h, _):
        r0 = pl.multiple_of(h * nk, nk)
        rows = pl.ds(r0, nk)
        s1 = s1t_ref[rows, :]
        s2 = s2t_ref[rows, :]
        v1 = _top_desc(s1, P_TOPK)
        v2 = _top_desc(s2, P_TOPK)
        pairs = [(a, b) for a in range(P_TOPK) for b in range(P_TOPK // (a + 1))]
        fill = [jnp.full_like(v1[0], -jnp.inf)] * (-len(pairs) % SUBLANES)
        cand = jnp.concatenate([v1[a] + v2[b] for a, b in pairs] + fill, axis=0)
        work, seen = cand, jnp.zeros_like(v1[0])
        tau = jnp.full_like(v1[0], -jnp.inf)
        for _ in range(P_TOPK):
            m = jnp.max(work, axis=0, keepdims=True)
            hit = work == m
            seen = seen + jnp.sum(jnp.where(hit, 1.0, 0.0), axis=0, keepdims=True)
            tau = jnp.maximum(tau, jnp.where(seen >= float(P_TOPK), m, -jnp.inf))
            work = jnp.where(hit, -jnp.inf, work)
        cmax = v1[0] + v2[0]
        zsum = jnp.sum(jnp.where(cand >= tau, jnp.exp(cand - cmax), 0.0), axis=0, keepdims=True)
        v2all = jnp.concatenate(v2, axis=0)
        cnt = jnp.zeros(s1.shape, F32)
        rank = jnp.full(s2.shape, float(P_TOPK), F32)
        for a in range(P_TOPK):
            cnt_a = jnp.sum(jnp.where(v1[a] + v2all >= tau, 1.0, 0.0), axis=0, keepdims=True)
            cnt = jnp.where(s1 == v1[a], cnt_a, cnt)
            rank = jnp.where(s2 == v2[a], float(a), rank)
        cnt_ref[rows, :] = cnt
        rank_ref[rows, :] = rank
        e1_ref[rows, :] = jnp.exp(s1 - v1[0]) / zsum
        e2_ref[rows, :] = jnp.exp(s2 - v2[0])
        return 0

    lax.fori_loop(0, P_HEADS, head, 0)


def _select(s1t, s2t, tn):
    hk, n = s1t.shape
    col = pl.BlockSpec((hk, tn), lambda i: (0, i))
    return pl.pallas_call(
        _select_kernel,
        grid=(n // tn,),
        in_specs=[col, col],
        out_specs=[col, col, col, col],
        out_shape=[jax.ShapeDtypeStruct((hk, n), F32)] * 4,
        compiler_params=_params(("parallel",)),
        name="select",
    )(s1t, s2t)


I1_PER_STEP = 8
E_PER_STEP = I1_PER_STEP * P_NKEYS
I2_BLOCK = 16


def _gelu(x):
    return 0.5 * x * (1.0 + lax.erf(x * (2.0 ** -0.5)))


def _peer_steps(t, n_steps, ng):
    item = lambda d: jnp.clip(t - d, 0, n_steps - 1)
    return [(item(d) // ng, item(d) % ng) for d in range(3)]


def _peer_kernel(hb_ref, pu_ref, pvt_ref, cnt_ref, e1_ref, rank_ref, e2_ref, h1_ref, g2_ref, b2_ref,
                 y_ref, a_ref, gw_ref, acc_ref, rank_s, e2_s, *, tn, alpha, n_steps, ng):
    t = pl.program_id(0)
    (_, _), (_, g), (_, g_out) = _peer_steps(t, n_steps, ng)
    slot = t % 2
    prev = 1 - slot

    @pl.when(t == 0)
    def _():
        a_ref[...] = jnp.zeros(a_ref.shape, F32)
        gw_ref[...] = jnp.zeros(gw_ref.shape, BF16)

    @pl.when(g == 0)
    def _():
        rank_s[...] = rank_ref[...].astype(BF16)
        e2_s[...] = e2_ref[...].astype(BF16)

    a_ref[slot] = _dot(pu_ref[...], hb_ref[...])

    assert I1_PER_STEP == SUBLANES
    for lt in range(tn // LANES):
        ls = slice(lt * LANES, (lt + 1) * LANES)
        grp = lambda ref, h: ref[pl.ds(pl.multiple_of(h * P_NKEYS + g * I1_PER_STEP, SUBLANES), SUBLANES), ls]
        cnt8 = [grp(cnt_ref, h) for h in range(P_HEADS)]
        e18 = [grp(e1_ref, h) for h in range(P_HEADS)]
        for j in range(I1_PER_STEP):
            bcast = lambda a8: jnp.broadcast_to(a8[j:j + 1, :], (I2_BLOCK, LANES)).astype(BF16)
            cnt = [bcast(cnt8[h]) for h in range(P_HEADS)]
            e1 = [bcast(e18[h]) for h in range(P_HEADS)]
            for i2b in range(P_NKEYS // I2_BLOCK):
                w = jnp.zeros((I2_BLOCK, LANES), BF16)
                for h in range(P_HEADS):
                    rows = slice(h * P_NKEYS + i2b * I2_BLOCK, h * P_NKEYS + (i2b + 1) * I2_BLOCK)
                    w = w + jnp.where(rank_s[rows, ls] < cnt[h], e2_s[rows, ls] * e1[h], jnp.zeros((), BF16))
                arow = slice(j * P_NKEYS + i2b * I2_BLOCK, j * P_NKEYS + (i2b + 1) * I2_BLOCK)
                gw_ref[prev, arow, ls] = (w.astype(F32) * _gelu(a_ref[prev, arow, ls])).astype(BF16)

    part = _dot(pvt_ref[0], gw_ref[slot])
    acc_ref[...] = jnp.where(g_out == 0, part, acc_ref[...] + part)

    @pl.when(jnp.logical_and(g_out == ng - 1, t >= 2))
    def _():
        y_ref[...] = _layer_norm(alpha * h1_ref[...] + acc_ref[...].T, g2_ref[...], b2_ref[...])


def _peer(h1, h1b, cnt, e1, rank, e2, pu, pv, ln_g, ln_b, alpha, tn):
    n, d = h1.shape
    hk = P_HEADS * P_NKEYS
    ng = pu.shape[0] // E_PER_STEP
    n_steps = (n // tn) * ng
    stage = lambda k, f: (lambda t: f(*_peer_steps(t, n_steps, ng)[k]))
    const = pl.BlockSpec((1, d), lambda t: (0, 0))
    routing = pl.BlockSpec((hk, tn), stage(1, lambda i, g: (0, i)))
    pvt = pv.astype(BF16).reshape(ng, E_PER_STEP, d).transpose(0, 2, 1)
    return pl.pallas_call(
        functools.partial(_peer_kernel, tn=tn, alpha=alpha, n_steps=n_steps, ng=ng),
        grid=(n_steps + 2,),
        in_specs=[pl.BlockSpec((d, tn), stage(0, lambda i, g: (0, i))),
                  pl.BlockSpec((E_PER_STEP, d), stage(0, lambda i, g: (g, 0))),
                  pl.BlockSpec((1, d, E_PER_STEP), stage(2, lambda i, g: (g, 0, 0))),
                  routing, routing, routing, routing,
                  pl.BlockSpec((tn, d), stage(2, lambda i, g: (i, 0))), const, const],
        out_specs=pl.BlockSpec((tn, d), stage(2, lambda i, g: (i, 0))),
        out_shape=jax.ShapeDtypeStruct((n, d), F32),
        scratch_shapes=[pltpu.VMEM((2, E_PER_STEP, tn), F32), pltpu.VMEM((2, E_PER_STEP, tn), BF16),
                        pltpu.VMEM((d, tn), F32), pltpu.VMEM((hk, tn), BF16), pltpu.VMEM((hk, tn), BF16)],
        compiler_params=_params(("arbitrary",)),
        name="peer",
    )(h1b, pu.astype(BF16), pvt, cnt, e1, rank, e2, h1, ln_g.reshape(1, d), ln_b.reshape(1, d))


def _pick_tile(n, pref):
    t = min(n, pref)
    assert n % t == 0
    return t


def _layer(x, pos, limits, past, s0, w, *, chunk, alpha):
    b, t, d = x.shape
    n = b * t
    x2 = x.reshape(n, d)
    tm = _pick_tile(n, TOKEN_TILE)
    (aq, k32, v32, kb, vb, iq, ikw, ikb, gq, gk, gv, glog, gr, gate) = _proj(
        x2, pos, w["w_in"], w["w_fa"], w["b_fa"], tm)

    kb3, vb3, ikb3 = kb.reshape(b, t, LANES), vb.reshape(b, t, LANES), ikb.reshape(b, t, LANES)
    if past is not None:
        ck, cv, cik = past
        p = ck.shape[1]
        kb3 = jnp.concatenate([ck.reshape(b, p, LANES).astype(BF16), kb3], axis=1)
        vb3 = jnp.concatenate([cv.reshape(b, p, LANES).astype(BF16), vb3], axis=1)
        ikb3 = jnp.concatenate([cik.astype(BF16), ikb3[:, :, :IDX_DIM]], axis=1)
    topk = min(IDX_TOPK, kb3.shape[1] // 4)
    o_a = _attention(aq, iq, ikw, limits, kb3, vb3, ikb3, b, topk)

    o_b, s_fin = _gla(gq, gk, gv, glog, gr, w["g_gla_norm"], s0, b, chunk)

    h1, h1b, s1t, s2t = _merge(x2, o_a, o_b, gate, w["w_pa"], w["w_pb"], w["w_out"],
                               w["ln1_g"], w["ln1_b"], w["w_pq"], w["pk1"], w["pk2"], alpha, tm)
    cnt, e1, rank, e2 = _select(s1t, s2t, tm)
    y = _peer(h1, h1b, cnt, e1, rank, e2, w["pu"], w["pv"], w["ln2_g"], w["ln2_b"], alpha,
              _pick_tile(n, PEER_TOKEN_TILE))

    k_out = k32.reshape(b, t, A_KV_HEADS, A_HEAD_DIM)
    v_out = v32.reshape(b, t, A_KV_HEADS, A_HEAD_DIM)
    ik_out = ikw[:, :IDX_DIM].reshape(b, t, IDX_DIM)
    return y.reshape(b, t, d), k_out, v_out, ik_out, s_fin


def kernel(x_prompt, x_sample, cache_k, cache_v, cache_idx_k, state_gla, w_in, w_fa, b_fa, g_gla_norm,
           w_pa, w_pb, w_out, ln1_g, ln1_b, w_pq, pk1, pk2, pu, pv, ln2_g, ln2_b):
    depth = w_in.shape[0]
    alpha = (2.0 * depth) ** 0.25
    bp, tp, _ = x_prompt.shape
    bs, ts, _ = x_sample.shape
    past_len = cache_k.shape[2]
    pos_p = jnp.arange(tp)
    pos_s = past_len + jnp.arange(ts)
    lim_p = (np.arange(tp) // CHUNK + 1) * CHUNK
    lim_s = np.full((ts,), past_len + ts)
    names = ("w_in", "w_fa", "b_fa", "g_gla_norm", "w_pa", "w_pb", "w_out", "ln1_g", "ln1_b",
             "w_pq", "pk1", "pk2", "pu", "pv", "ln2_g", "ln2_b")
    stacked = (w_in, w_fa, b_fa, g_gla_norm, w_pa, w_pb, w_out, ln1_g, ln1_b, w_pq, pk1, pk2, pu, pv, ln2_g, ln2_b)
    hp, hs = x_prompt, x_sample
    outs_p, outs_s = [], []
    for l in range(depth):
        w = {nm: a[l] for nm, a in zip(names, stacked)}
        s0 = jnp.zeros((bp, G_HEADS, G_KEY_DIM, G_VAL_DIM), F32)
        hp, *rest = _layer(hp, pos_p, lim_p, None, s0, w, chunk=CHUNK, alpha=alpha)
        outs_p.append(rest)
        hs, *rest = _layer(hs, pos_s, lim_s, (cache_k[l], cache_v[l], cache_idx_k[l]), state_gla[l], w,
                           chunk=ts, alpha=alpha)
        outs_s.append(rest)
    stack = lambda outs, i: jnp.stack([o[i] for o in outs])
    return (hp, hs, stack(outs_p, 0), stack(outs_p, 1), stack(outs_p, 2), stack(outs_p, 3),
            stack(outs_s, 0), stack(outs_s, 1), stack(outs_s, 2), stack(outs_s, 3))
```

```python
import functools
import math

import numpy as np
import jax
import jax.numpy as jnp
from jax import lax
from jax.experimental import pallas as pl
from jax.experimental.pallas import tpu as pltpu

F32 = jnp.float32
BF16 = jnp.bfloat16
I32 = jnp.int32

LANES = 128
SUBLANES = 8
VMEM_LIMIT = 56 << 20

CHUNK = 64
A_HEADS = 8
A_KV_HEADS = 2
A_HEAD_DIM = 64
A_GROUP = A_HEADS // A_KV_HEADS
IDX_HEADS = 4
IDX_DIM = 64
IDX_TOPK = 256
ROPE_THETA = 10000.0
G_HEADS = 4
G_KEY_DIM = 128
G_VAL_DIM = 128
G_LOWRANK = 16
G_TAU = 16.0
P_HEADS = 8
P_NKEYS = 128
P_HALF = 128
P_TOPK = 16
LN_EPS = 1e-5

W_AQ = A_HEADS * A_HEAD_DIM
W_AK = A_KV_HEADS * A_HEAD_DIM
W_IQ = IDX_HEADS * IDX_DIM
W_G = G_HEADS * G_KEY_DIM
IN_SIZES = (W_AQ, W_AK, W_AK, W_IQ, IDX_DIM, IDX_HEADS, W_G, W_G, W_G, G_LOWRANK, W_G, None)

TOKEN_TILE = 256
PEER_TOKEN_TILE = 512
Q_BLOCK = 128
KEY_TILE = 512
GLA_SEQS = 4

INT_MIN = -(2 ** 31)
NEG_INF_KEY = -2139095041
NEG_BIG = -1e30


def _dot(a, b):
    return jnp.dot(a, b, preferred_element_type=F32)


def _dot_nt(a, b):
    return lax.dot_general(a, b, (((1,), (1,)), ((), ())), preferred_element_type=F32)


def _dot_tn(a, b):
    return lax.dot_general(a, b, (((0,), (0,)), ((), ())), preferred_element_type=F32)


def _sort_key(x):
    bits = pltpu.bitcast(x, I32)
    key = jnp.where(bits < 0, bits ^ 0x7FFFFFFF, bits)
    return jnp.where(key == -1, 0, key)


def _fold_rows(x, op):
    x = x.reshape(x.shape[0] // SUBLANES, SUBLANES, x.shape[1])
    while x.shape[0] > 1:
        half = x.shape[0] // 2
        folded = op(x[:half], x[half:2 * half])
        x = folded if x.shape[0] == 2 * half else jnp.concatenate([folded, x[2 * half:]], axis=0)
    return x[0]


def _params(sem):
    return pltpu.CompilerParams(dimension_semantics=sem, vmem_limit_bytes=VMEM_LIMIT)


_PG_AQ, _PG_K, _PG_V, _PG_IQ, _PG_IKW, _PG_GQ, _PG_GK, _PG_GV, _PG_GF, _PG_GR, _PG_GATE = range(11)


def _pack_layout(d_model):
    widths = [W_AQ, W_AK, W_AK, W_IQ, LANES, W_G, W_G, W_G, LANES, W_G, 2 * d_model]
    offs = np.concatenate([[0], np.cumsum(widths)]).tolist()
    return widths, offs


def _pack_w_in(w_in):
    d = w_in.shape[0]
    sizes = list(IN_SIZES[:-1]) + [2 * d]
    cuts = np.cumsum(sizes)[:-1].tolist()
    aq, ak, av, iq, ik, iw, gq, gk, gv, gf, gr, gate = jnp.split(w_in, cuts, axis=1)
    z = lambda n: jnp.zeros((d, n), w_in.dtype)
    ikw = jnp.concatenate([ik, iw, z(LANES - IDX_DIM - IDX_HEADS)], axis=1)
    gfp = jnp.concatenate([gf, z(LANES - G_LOWRANK)], axis=1)
    return jnp.concatenate([aq, ak, av, iq, ikw, gq, gk, gv, gfp, gr, gate], axis=1).astype(BF16)


def _rope_tables(pos):
    half = A_HEAD_DIM // 2
    inv = ROPE_THETA ** (-jnp.arange(half, dtype=F32) / half)
    ang = pos.astype(F32)[:, None] * inv[None, :]
    c, s = jnp.cos(ang), jnp.sin(ang)
    return jnp.concatenate([c, c, c, c], -1), jnp.concatenate([-s, s, -s, s], -1)


def _proj_kernel(x_ref, w_ref, wfa_ref, bfa_ref, cos_ref, sin_ref,
                 aq_ref, k_ref, v_ref, kb_ref, vb_ref, iq_ref, ikw_ref, ikb_ref,
                 gq_ref, gk_ref, gv_ref, glog_ref, gr_ref, gate_ref, *, offs, widths):
    xb = x_ref[...].astype(BF16)
    cos = cos_ref[...]
    sin = sin_ref[...]
    lane = lax.broadcasted_iota(I32, cos.shape, 1)
    first_half = (lane & (A_HEAD_DIM // 2)) == 0

    def proj(g):
        return _dot(xb, w_ref[:, offs[g]:offs[g] + widths[g]])

    def rope_slab(y):
        fwd = pltpu.roll(y, LANES - A_HEAD_DIM // 2, 1)
        bwd = pltpu.roll(y, A_HEAD_DIM // 2, 1)
        return y * cos + jnp.where(first_half, fwd, bwd) * sin

    def rope(y):
        return [rope_slab(y[:, s * LANES:(s + 1) * LANES]) for s in range(y.shape[1] // LANES)]

    for s, slab in enumerate(rope(proj(_PG_AQ))):
        aq_ref[:, s * LANES:(s + 1) * LANES] = (slab * (A_HEAD_DIM ** -0.5)).astype(BF16)
    k = rope(proj(_PG_K))[0]
    k_ref[...] = k
    kb_ref[...] = k.astype(BF16)
    v = proj(_PG_V)
    v_ref[...] = v
    vb_ref[...] = v.astype(BF16)
    for s, slab in enumerate(rope(proj(_PG_IQ))):
        iq_ref[:, s * LANES:(s + 1) * LANES] = (slab * (IDX_DIM ** -0.5)).astype(BF16)
    raw = proj(_PG_IKW)
    ikw = jnp.where(lane < IDX_DIM, rope_slab(raw), raw * (IDX_HEADS ** -0.5))
    ikw_ref[...] = ikw
    ikb_ref[...] = ikw.astype(BF16)
    gq_ref[...] = proj(_PG_GQ) * (G_KEY_DIM ** -0.5)
    gk_ref[...] = proj(_PG_GK)
    gv_ref[...] = proj(_PG_GV)
    z = _dot(proj(_PG_GF).astype(BF16), wfa_ref[...]) + bfa_ref[...]
    glog_ref[...] = (jnp.minimum(z, 0.0) - jnp.log1p(jnp.exp(-jnp.abs(z)))) * (1.0 / G_TAU)
    gr_ref[...] = proj(_PG_GR)
    gate_ref[...] = proj(_PG_GATE)


def _proj(x2, pos, w_in, w_fa, b_fa, tm):
    n, d = x2.shape
    t = pos.shape[0]
    widths, offs = _pack_layout(d)
    wp = _pack_w_in(w_in)
    wfa = jnp.concatenate([w_fa, jnp.zeros((LANES - G_LOWRANK, W_G), w_fa.dtype)], 0).astype(BF16)
    cos, sin = _rope_tables(pos)
    if tm > t:
        cos, sin = jnp.tile(cos, (tm // t, 1)), jnp.tile(sin, (tm // t, 1))
    nper = cos.shape[0] // tm
    row = lambda w: pl.BlockSpec((tm, w), lambda i: (i, 0))
    const = lambda a: pl.BlockSpec(a.shape, lambda i: (0, 0))
    tab = pl.BlockSpec((tm, LANES), lambda i: (i % nper, 0))
    outs = [(W_AQ, BF16), (LANES, F32), (LANES, F32), (LANES, BF16), (LANES, BF16), (W_IQ, BF16),
            (LANES, F32), (LANES, BF16), (W_G, F32), (W_G, F32), (W_G, F32), (W_G, F32), (W_G, F32),
            (2 * d, F32)]
    bfa = b_fa.reshape(1, W_G)
    return pl.pallas_call(
        functools.partial(_proj_kernel, offs=offs, widths=widths),
        grid=(n // tm,),
        in_specs=[row(d), const(wp), const(wfa), const(bfa), tab, tab],
        out_specs=[row(w) for w, _ in outs],
        out_shape=[jax.ShapeDtypeStruct((n, w), dt) for w, dt in outs],
        compiler_params=_params(("parallel",)),
        name="proj",
    )(x2, wp, wfa, bfa, cos, sin)


def _attn_kernel(aq_ref, iq_ref, ikw_ref, lim_ref, kb_ref, vt_ref, ikb_ref, o_ref,
                 keys_ref, qs_ref, iqs_ref, acc_ref, *, kt_w, nkt, topk, idx_bits):
    qb = Q_BLOCK

    aq_t = aq_ref[...].astype(F32).T
    for h in range(A_HEADS):
        qs_ref[h // A_GROUP, :, (h % A_GROUP) * qb:(h % A_GROUP + 1) * qb] = \
            aq_t[h * A_HEAD_DIM:(h + 1) * A_HEAD_DIM, :].astype(BF16)
    iq_t = iq_ref[...].astype(F32).T
    for h in range(IDX_HEADS):
        iqs_ref[:, h * qb:(h + 1) * qb] = iq_t[h * IDX_DIM:(h + 1) * IDX_DIM, :].astype(BF16)

    ikw_t = ikw_ref[...].T
    iw_rows = [ikw_t[IDX_DIM + h:IDX_DIM + h + 1, :] for h in range(IDX_HEADS)]
    lim = lim_ref[0, 0:1, :]
    sub = lax.broadcasted_iota(I32, (kt_w, qb), 0)

    def score_tile(kt, _):
        base = pl.multiple_of(kt * kt_w, kt_w)
        ik_t = ikb_ref[0, pl.ds(base, kt_w), :][:, :IDX_DIM]
        s = jnp.maximum(_dot(ik_t, iqs_ref[...]), 0.0)
        score = jnp.zeros((kt_w, qb), F32)
        for h in range(IDX_HEADS):
            score = score + s[:, h * qb:(h + 1) * qb] * iw_rows[h]
        keys_ref[pl.ds(base, kt_w), :] = jnp.where(sub + base < lim, _sort_key(score), NEG_INF_KEY)
        return 0

    lax.fori_loop(0, nkt, score_tile, 0, unroll=min(nkt, 2))

    def count(pred):
        acc = jnp.zeros((SUBLANES, qb), F32)
        for kt in range(nkt):
            acc = acc + _fold_rows(pred(keys_ref[kt * kt_w:(kt + 1) * kt_w, :], sub + kt * kt_w), jnp.add)
        return jnp.sum(acc, axis=0, keepdims=True)

    def count_ge(t_row):
        return count(lambda kk, idx: jnp.where(kk >= t_row, 1.0, 0.0))

    kf = float(topk)
    thr = jnp.where(count_ge(jnp.zeros((1, qb), I32)) >= kf, 0, INT_MIN).astype(I32)

    def thr_bit(i, t):
        cand = t + jnp.left_shift(jnp.int32(1), 30 - i)
        return jnp.where(count_ge(cand) >= kf, cand, t)

    thr = lax.fori_loop(0, 31, thr_bit, thr)
    n_gt = count_ge(thr + 1)
    n_eq = count_ge(thr) - n_gt
    need = kf - n_gt
    finite = thr > NEG_INF_KEY
    excess = jnp.where(finite, jnp.where(n_eq > need, 1.0, 0.0), 0.0)

    def count_eq_below(j_row):
        return count(lambda kk, idx: jnp.where(kk == thr, jnp.where(idx < j_row, 1.0, 0.0), 0.0))

    def resolve_ties():
        def bit(i, jc):
            cand = jc + jnp.left_shift(jnp.int32(1), idx_bits - 1 - i)
            return jnp.where(count_eq_below(cand) <= need - 1.0, cand, jc)
        jc = lax.fori_loop(0, idx_bits, bit, jnp.zeros((1, qb), I32))
        return jnp.where(finite, jc, -1)

    cut = lax.cond(jnp.max(excess) > 0.0, resolve_ties,
                   lambda: jnp.where(finite, 2 ** 30, -1).astype(I32))

    acc_ref[...] = jnp.zeros(acc_ref.shape, F32)
    gq = A_GROUP * qb

    for kt in range(nkt):
        rows = slice(kt * kt_w, (kt + 1) * kt_w)
        kk = keys_ref[rows, :]
        tie = jnp.where(sub + kt * kt_w <= cut, 0.0, NEG_BIG)
        keys_ref[rows, :] = pltpu.bitcast(
            jnp.where(kk > thr, 0.0, jnp.where(kk == thr, tie, NEG_BIG)), I32)

    def attend(kt, carry):
        base = pl.multiple_of(kt * kt_w, kt_w)
        k_t = kb_ref[0, pl.ds(base, kt_w), :]
        bias = pltpu.bitcast(keys_ref[pl.ds(base, kt_w), :], F32)
        bias = jnp.concatenate([bias] * A_GROUP, axis=1)
        out = []
        for n in range(A_KV_HEADS):
            m_old, l_old = carry[2 * n], carry[2 * n + 1]
            logits = bias + _dot(k_t[:, n * A_HEAD_DIM:(n + 1) * A_HEAD_DIM], qs_ref[n])
            m_new = jnp.maximum(m_old, jnp.max(_fold_rows(logits, jnp.maximum), axis=0, keepdims=True))
            alpha = jnp.exp(m_old - m_new)
            p = jnp.exp(logits - m_new)
            l_new = alpha * l_old + jnp.sum(_fold_rows(p, jnp.add), axis=0, keepdims=True)
            v_t = vt_ref[0, n * A_HEAD_DIM:(n + 1) * A_HEAD_DIM, pl.ds(base, kt_w)]
            acc_ref[n] = alpha * acc_ref[n] + _dot(v_t, p.astype(BF16))
            out += [m_new, l_new]
        return tuple(out)

    init = (jnp.full((1, gq), NEG_BIG, F32), jnp.zeros((1, gq), F32)) * A_KV_HEADS
    fin = lax.fori_loop(0, nkt, attend, init, unroll=min(nkt, 2))
    l_row = [fin[2 * n + 1] for n in range(A_KV_HEADS)]

    o_t = jnp.concatenate([acc_ref[n] / l_row[n] for n in range(A_KV_HEADS)], axis=0)
    for n in range(A_KV_HEADS):
        for g in range(A_GROUP):
            h = n * A_GROUP + g
            blk = o_t[n * A_HEAD_DIM:(n + 1) * A_HEAD_DIM, g * qb:(g + 1) * qb]
            o_ref[:, h * A_HEAD_DIM:(h + 1) * A_HEAD_DIM] = blk.T.astype(BF16)


def _attention(aq, iq, ikw, limits, kb, vb, ikb, b, topk):
    qb, kt_w = Q_BLOCK, KEY_TILE
    tq = aq.shape[0] // b
    tq_pad = -(-tq // qb) * qb
    limits = np.asarray(limits)
    if tq_pad != tq:
        padq = lambda a: jnp.pad(a.reshape(b, tq, -1), ((0, 0), (0, tq_pad - tq), (0, 0))).reshape(b * tq_pad, -1)
        aq, iq, ikw = padq(aq), padq(iq), padq(ikw)
        limits = np.concatenate([limits, np.full((tq_pad - tq,), limits[-1])])
    l_all = kb.shape[1]
    l_pad = -(-l_all // kt_w) * kt_w
    if l_pad != l_all:
        padl = lambda a: jnp.pad(a, ((0, 0), (0, l_pad - l_all), (0, 0)))
        kb, vb, ikb = padl(kb), padl(vb), padl(ikb)
    vt = jnp.swapaxes(vb, 1, 2)
    nq = tq_pad // qb
    lim_blk = limits.reshape(nq, qb)
    nkt = np.minimum(-(-lim_blk.max(axis=1) // kt_w), l_pad // kt_w)
    lim = jnp.asarray(np.broadcast_to(lim_blk[:, None, :], (nq, SUBLANES, qb)).astype(np.int32))
    idx_bits = max(1, int(math.ceil(math.log2(l_pad))))
    keys = lambda a: pl.BlockSpec((1,) + a.shape[1:], lambda bi, j: (bi, 0, 0))
    gq = A_GROUP * qb
    runs, j0 = [], 0
    for j in range(1, nq + 1):
        if j == nq or nkt[j] != nkt[j0]:
            runs.append((j0, j - j0, int(nkt[j0])))
            j0 = j
    outs = []
    for j0, nj, n_tiles in runs:
        qrow = lambda w, j0=j0: pl.BlockSpec((qb, w), lambda bi, j: (bi * nq + j0 + j, 0))
        outs.append(pl.pallas_call(
            functools.partial(_attn_kernel, kt_w=kt_w, nkt=n_tiles, topk=topk, idx_bits=idx_bits),
            grid=(b, nj),
            in_specs=[qrow(W_AQ), qrow(W_IQ), qrow(LANES),
                      pl.BlockSpec((1, SUBLANES, qb), lambda bi, j, j0=j0: (j0 + j, 0, 0)),
                      keys(kb), keys(vt), keys(ikb)],
            out_specs=pl.BlockSpec((qb, W_AQ), lambda bi, j, nj=nj: (bi * nj + j, 0)),
            out_shape=jax.ShapeDtypeStruct((b * nj * qb, W_AQ), BF16),
            scratch_shapes=[pltpu.VMEM((n_tiles * kt_w, qb), I32),
                            pltpu.VMEM((A_KV_HEADS, A_HEAD_DIM, gq), BF16),
                            pltpu.VMEM((IDX_DIM, IDX_HEADS * qb), BF16),
                            pltpu.VMEM((A_KV_HEADS, A_HEAD_DIM, gq), F32)],
            compiler_params=_params(("parallel", "arbitrary")),
            name="attn",
        )(aq, iq, ikw, lim, kb, vt, ikb).reshape(b, nj * qb, W_AQ))
    o = outs[0] if len(outs) == 1 else jnp.concatenate(outs, axis=1)
    return o[:, :tq].reshape(b * tq, W_AQ)


def _gla_constants(c):
    nlev = int(math.log2(c))
    t = np.arange(c)
    mats = [(t[None, :] <= t[:, None])]
    masks = [np.eye(c, dtype=bool)]
    for lev in range(nlev):
        m = c >> (lev + 1)
        ref_row = (t // (2 * m)) * 2 * m + m
        mats.append(t[None, :] <= ref_row[:, None])
        upper = (t & m) != 0
        same = (t[:, None] // (2 * m)) == (t[None, :] // (2 * m))
        masks.append(same & upper[:, None] & ~upper[None, :])
    return (jnp.asarray(np.concatenate(mats, 0).astype(np.float32), BF16),
            jnp.asarray(np.stack(masks).astype(np.float32)), nlev)


def _gla_kernel(q_ref, k_ref, v_ref, g_ref, gr_ref, gn_ref, mst_ref, msk_ref, s0_ref,
                ob_ref, sfin_ref, st_ref, *, c, nlev, nb):
    i = pl.program_id(1)
    hk = G_KEY_DIM

    @pl.when(i == 0)
    def _():
        for s in range(nb):
            for h in range(G_HEADS):
                st_ref[s, h] = s0_ref[s, h].T

    mst = mst_ref[...]
    gn = gn_ref[...]
    hs = lambda a, h: a[:, h * hk:(h + 1) * hk]
    for s in range(nb):
        g = g_ref[s]
        g_hi = g.astype(BF16)
        r1 = g - g_hi.astype(F32)
        g_mid = r1.astype(BF16)
        g_lo = (r1 - g_mid.astype(F32)).astype(BF16)
        bs = _dot(mst, g_hi) + _dot(mst, g_mid) + _dot(mst, g_lo)
        b = bs[0:c]
        q = q_ref[s]
        k = k_ref[s]
        vb = v_ref[s].astype(BF16)
        row = lax.broadcasted_iota(I32, q.shape, 0)

        qb = q.astype(BF16)
        kb = k.astype(BF16)
        attn = [_dot_nt(hs(qb, h), hs(kb, h)) * msk_ref[0] for h in range(G_HEADS)]
        for lev in range(nlev):
            m = c >> (lev + 1)
            upper = (row & m) != 0
            d = b - bs[(lev + 1) * c:(lev + 2) * c]
            e = jnp.exp(jnp.where(upper, d, -d))
            qt = jnp.where(upper, q * e, 0.0).astype(BF16)
            kt = jnp.where(upper, 0.0, k * e).astype(BF16)
            mk = msk_ref[lev + 1]
            for h in range(G_HEADS):
                attn[h] = attn[h] + _dot_nt(hs(qt, h), hs(kt, h)) * mk

        qe = (q * jnp.exp(b)).astype(BF16)
        b_last = b[c - 1:c, :]
        khat = (k * jnp.exp(b_last - b)).astype(BF16)
        dec = jnp.exp(b_last)
        gr = gr_ref[s]
        for h in range(G_HEADS):
            st = st_ref[s, h]
            o = _dot_nt(hs(qe, h), st.astype(BF16)) + _dot(attn[h].astype(BF16), hs(vb, h))
            st_ref[s, h] = st * hs(dec, h) + _dot_tn(hs(vb, h), hs(khat, h))
            ms = jnp.mean(o * o, axis=1, keepdims=True)
            grh = hs(gr, h)
            of = o * lax.rsqrt(ms + LN_EPS) * hs(gn, h) * (grh / (1.0 + jnp.exp(-grh)))
            ob_ref[s, :, h * hk:(h + 1) * hk] = of.astype(BF16)

    @pl.when(i == pl.num_programs(1) - 1)
    def _():
        for s in range(nb):
            for h in range(G_HEADS):
                sfin_ref[s, h] = st_ref[s, h].T


def _gla(gq, gk, gv, glog, gr, g_norm, s0, b, c):
    n = gq.shape[0]
    t = n // b
    nc = t // c
    nb = GLA_SEQS if b % GLA_SEQS == 0 else 1
    mst, msk, nlev = _gla_constants(c)
    seq = lambda a: a.reshape(b, t, W_G)
    row = pl.BlockSpec((nb, c, W_G), lambda bi, i: (bi, i, 0))
    const = lambda a: pl.BlockSpec(a.shape, lambda bi, i: (0,) * a.ndim)
    st_spec = pl.BlockSpec((nb, G_HEADS, G_KEY_DIM, G_VAL_DIM), lambda bi, i: (bi, 0, 0, 0))
    gn = g_norm.reshape(1, W_G)
    ob, s_fin = pl.pallas_call(
        functools.partial(_gla_kernel, c=c, nlev=nlev, nb=nb),
        grid=(b // nb, nc),
        in_specs=[row, row, row, row, row, const(gn), const(mst), const(msk), st_spec],
        out_specs=[row, st_spec],
        out_shape=[jax.ShapeDtypeStruct((b, t, W_G), BF16),
                   jax.ShapeDtypeStruct((b, G_HEADS, G_KEY_DIM, G_VAL_DIM), F32)],
        scratch_shapes=[pltpu.VMEM((nb, G_HEADS, G_VAL_DIM, G_KEY_DIM), F32)],
        compiler_params=_params(("parallel", "arbitrary")),
        name="gla",
    )(seq(gq), seq(gk), seq(gv), seq(glog), seq(gr), gn, mst, msk, s0)
    return ob.reshape(n, W_G), s_fin


def _layer_norm(z, g, b):
    mu = jnp.mean(z, axis=1, keepdims=True)
    zc = z - mu
    var = jnp.mean(zc * zc, axis=1, keepdims=True)
    return zc * lax.rsqrt(var + LN_EPS) * g + b


def _sigmoid(x):
    return 1.0 / (1.0 + jnp.exp(-x))


def _merge_kernel(x_ref, oa_ref, ob_ref, gate_ref, wpa_ref, wpb_ref, wout_ref, g1_ref, b1_ref,
                  wpq_ref, pk1_ref, pk2_ref, h1_ref, h1b_ref, s1t_ref, s2t_ref, *, alpha, d):
    ya = _dot(oa_ref[...], wpa_ref[...])
    yb = _dot(ob_ref[...], wpb_ref[...])
    m = _sigmoid(gate_ref[:, :d]) * ya + _sigmoid(gate_ref[:, d:]) * yb
    mix = _dot(m.astype(BF16), wout_ref[...])
    h1 = _layer_norm(alpha * x_ref[...] + mix, g1_ref[...], b1_ref[...])
    h1_ref[...] = h1
    h1b_ref[...] = h1.T.astype(BF16)
    qp = _dot(h1.astype(BF16), wpq_ref[...])
    for h in range(P_HEADS):
        for half, (pk_ref, st_ref) in enumerate(((pk1_ref, s1t_ref), (pk2_ref, s2t_ref))):
            c0 = (2 * h + half) * P_HALF
            st_ref[h * P_NKEYS:(h + 1) * P_NKEYS, :] = _dot_nt(pk_ref[h], qp[:, c0:c0 + P_HALF].astype(BF16))


def _merge(x2, oa, ob, gate, w_pa, w_pb, w_out, ln_g, ln_b, w_pq, pk1, pk2, alpha, tm):
    n, d = x2.shape
    hk = P_HEADS * P_NKEYS
    row = lambda w: pl.BlockSpec((tm, w), lambda i: (i, 0))
    col = pl.BlockSpec((hk, tm), lambda i: (0, i))
    const = lambda a: pl.BlockSpec(a.shape, lambda i: (0,) * a.ndim)
    ws = [w_pa.astype(BF16), w_pb.astype(BF16), w_out.astype(BF16), ln_g.reshape(1, d), ln_b.reshape(1, d),
          w_pq.astype(BF16), pk1.astype(BF16), pk2.astype(BF16)]
    return pl.pallas_call(
        functools.partial(_merge_kernel, alpha=alpha, d=d),
        grid=(n // tm,),
        in_specs=[row(d), row(W_AQ), row(W_G), row(2 * d)] + [const(w) for w in ws],
        out_specs=[row(d), pl.BlockSpec((d, tm), lambda i: (0, i)), col, col],
        out_shape=[jax.ShapeDtypeStruct((n, d), F32), jax.ShapeDtypeStruct((d, n), BF16),
                   jax.ShapeDtypeStruct((hk, n), F32), jax.ShapeDtypeStruct((hk, n), F32)],
        compiler_params=_params(("parallel",)),
        name="merge",
    )(x2, oa, ob, gate, *ws)


def _top_desc(s, count):
    tops = []
    for r in range(count):
        m = jnp.max(_fold_rows(s, jnp.maximum), axis=0, keepdims=True)
        tops.append(m)
        if r + 1 < count:
            s = jnp.where(s == m, -jnp.inf, s)
    return tops


def _select_kernel(s1t_ref, s2t_ref, cnt_ref, e1_ref, rank_ref, e2_ref):
    nk = P_NKEYS

    def head(h, _):
        r0 = pl.multiple_of(h * nk, nk)
        rows = pl.ds(r0, nk)
        s1 = s1t_ref[rows, :]
        s2 = s2t_ref[rows, :]
        v1 = _top_desc(s1, P_TOPK)
        v2 = _top_desc(s2, P_TOPK)
        pairs = [(a, b) for a in range(P_TOPK) for b in range(P_TOPK // (a + 1))]
        fill = [jnp.full_like(v1[0], -jnp.inf)] * (-len(pairs) % SUBLANES)
        cand = jnp.concatenate([v1[a] + v2[b] for a, b in pairs] + fill, axis=0)
        work, seen = cand, jnp.zeros_like(v1[0])
        tau = jnp.full_like(v1[0], -jnp.inf)
        for _ in range(P_TOPK):
            m = jnp.max(work, axis=0, keepdims=True)
            hit = work == m
            seen = seen + jnp.sum(jnp.where(hit, 1.0, 0.0), axis=0, keepdims=True)
            tau = jnp.maximum(tau, jnp.where(seen >= float(P_TOPK), m, -jnp.inf))
            work = jnp.where(hit, -jnp.inf, work)
        cmax = v1[0] + v2[0]
        zsum = jnp.sum(jnp.where(cand >= tau, jnp.exp(cand - cmax), 0.0), axis=0, keepdims=True)
        v2all = jnp.concatenate(v2, axis=0)
        cnt = jnp.zeros(s1.shape, F32)
        rank = jnp.full(s2.shape, float(P_TOPK), F32)
        for a in range(P_TOPK):
            cnt_a = jnp.sum(jnp.where(v1[a] + v2all >= tau, 1.0, 0.0), axis=0, keepdims=True)
            cnt = jnp.where(s1 == v1[a], cnt_a, cnt)
            rank = jnp.where(s2 == v2[a], float(a), rank)
        cnt_ref[rows, :] = cnt
        rank_ref[rows, :] = rank
        e1_ref[rows, :] = jnp.exp(s1 - v1[0]) / zsum * 0.5
        e2_ref[rows, :] = jnp.exp(s2 - v2[0])
        return 0

    lax.fori_loop(0, P_HEADS, head, 0)


def _select(s1t, s2t, tn):
    hk, n = s1t.shape
    col = pl.BlockSpec((hk, tn), lambda i: (0, i))
    return pl.pallas_call(
        _select_kernel,
        grid=(n // tn,),
        in_specs=[col, col],
        out_specs=[col, col, col, col],
        out_shape=[jax.ShapeDtypeStruct((hk, n), F32)] * 4,
        compiler_params=_params(("parallel",)),
        name="select",
    )(s1t, s2t)


I1_PER_STEP = 8
E_PER_STEP = I1_PER_STEP * P_NKEYS
I2_BLOCK = 16


def _gelu_x2(x):
    return x * (1.0 + lax.erf(x * (2.0 ** -0.5)))


def _peer_steps(t, n_steps, ng):
    item = lambda d: jnp.clip(t - d, 0, n_steps - 1)
    return [(item(d) // ng, item(d) % ng) for d in range(3)]


def _peer_kernel(hb_ref, pu_ref, pvt_ref, cnt_ref, e1_ref, rank_ref, e2_ref, h1_ref, g2_ref, b2_ref,
                 y_ref, a_ref, gw_ref, acc_ref, rank_s, e2_s, *, tn, alpha, n_steps, ng):
    t = pl.program_id(0)
    (_, _), (_, g), (_, g_out) = _peer_steps(t, n_steps, ng)
    slot = t % 2
    prev = 1 - slot

    @pl.when(t == 0)
    def _():
        a_ref[...] = jnp.zeros(a_ref.shape, F32)
        gw_ref[...] = jnp.zeros(gw_ref.shape, BF16)

    @pl.when(g == 0)
    def _():
        rank_s[...] = rank_ref[...].astype(BF16)
        e2_s[...] = e2_ref[...].astype(BF16)

    a_ref[slot] = _dot(pu_ref[...], hb_ref[...])

    assert I1_PER_STEP == SUBLANES
    for lt in range(tn // LANES):
        ls = slice(lt * LANES, (lt + 1) * LANES)
        grp = lambda ref, h: ref[pl.ds(pl.multiple_of(h * P_NKEYS + g * I1_PER_STEP, SUBLANES), SUBLANES), ls]
        cnt8 = [grp(cnt_ref, h) for h in range(P_HEADS)]
        e18 = [grp(e1_ref, h) for h in range(P_HEADS)]
        for j in range(I1_PER_STEP):
            bcast = lambda a8: jnp.broadcast_to(a8[j:j + 1, :], (I2_BLOCK, LANES)).astype(BF16)
            cnt = [bcast(cnt8[h]) for h in range(P_HEADS)]
            e1 = [bcast(e18[h]) for h in range(P_HEADS)]
            for i2b in range(P_NKEYS // I2_BLOCK):
                w = jnp.zeros((I2_BLOCK, LANES), BF16)
                for h in range(P_HEADS):
                    rows = slice(h * P_NKEYS + i2b * I2_BLOCK, h * P_NKEYS + (i2b + 1) * I2_BLOCK)
                    w = w + jnp.where(rank_s[rows, ls] < cnt[h], e2_s[rows, ls] * e1[h], jnp.zeros((), BF16))
                arow = slice(j * P_NKEYS + i2b * I2_BLOCK, j * P_NKEYS + (i2b + 1) * I2_BLOCK)
                gw_ref[prev, arow, ls] = w * _gelu_x2(a_ref[prev, arow, ls]).astype(BF16)

    part = _dot(pvt_ref[0], gw_ref[slot])
    acc_ref[...] = jnp.where(g_out == 0, part, acc_ref[...] + part)

    @pl.when(jnp.logical_and(g_out == ng - 1, t >= 2))
    def _():
        y_ref[...] = _layer_norm(alpha * h1_ref[...] + acc_ref[...].T, g2_ref[...], b2_ref[...])


def _peer(h1, h1b, cnt, e1, rank, e2, pu, pv, ln_g, ln_b, alpha, tn):
    n, d = h1.shape
    hk = P_HEADS * P_NKEYS
    ng = pu.shape[0] // E_PER_STEP
    n_steps = (n // tn) * ng
    stage = lambda k, f: (lambda t: f(*_peer_steps(t, n_steps, ng)[k]))
    const = pl.BlockSpec((1, d), lambda t: (0, 0))
    routing = pl.BlockSpec((hk, tn), stage(1, lambda i, g: (0, i)))
    pvt = pv.astype(BF16).reshape(ng, E_PER_STEP, d).transpose(0, 2, 1)
    return pl.pallas_call(
        functools.partial(_peer_kernel, tn=tn, alpha=alpha, n_steps=n_steps, ng=ng),
        grid=(n_steps + 2,),
        in_specs=[pl.BlockSpec((d, tn), stage(0, lambda i, g: (0, i))),
                  pl.BlockSpec((E_PER_STEP, d), stage(0, lambda i, g: (g, 0))),
                  pl.BlockSpec((1, d, E_PER_STEP), stage(2, lambda i, g: (g, 0, 0))),
                  routing, routing, routing, routing,
                  pl.BlockSpec((tn, d), stage(2, lambda i, g: (i, 0))), const, const],
        out_specs=pl.BlockSpec((tn, d), stage(2, lambda i, g: (i, 0))),
        out_shape=jax.ShapeDtypeStruct((n, d), F32),
        scratch_shapes=[pltpu.VMEM((2, E_PER_STEP, tn), F32), pltpu.VMEM((2, E_PER_STEP, tn), BF16),
                        pltpu.VMEM((d, tn), F32), pltpu.VMEM((hk, tn), BF16), pltpu.VMEM((hk, tn), BF16)],
        compiler_params=_params(("arbitrary",)),
        name="peer",
    )(h1b, pu.astype(BF16), pvt, cnt, e1, rank, e2, h1, ln_g.reshape(1, d), ln_b.reshape(1, d))


def _pick_tile(n, pref):
    t = min(n, pref)
    assert n % t == 0
    return t


def _layer(x, pos, limits, past, s0, w, *, chunk, alpha):
    b, t, d = x.shape
    n = b * t
    x2 = x.reshape(n, d)
    tm = _pick_tile(n, TOKEN_TILE)
    (aq, k32, v32, kb, vb, iq, ikw, ikb, gq, gk, gv, glog, gr, gate) = _proj(
        x2, pos, w["w_in"], w["w_fa"], w["b_fa"], tm)

    kb3, vb3, ikb3 = kb.reshape(b, t, LANES), vb.reshape(b, t, LANES), ikb.reshape(b, t, LANES)
    if past is not None:
        ck, cv, cik = past
        p = ck.shape[1]
        kb3 = jnp.concatenate([ck.reshape(b, p, LANES).astype(BF16), kb3], axis=1)
        vb3 = jnp.concatenate([cv.reshape(b, p, LANES).astype(BF16), vb3], axis=1)
        ikb3 = jnp.concatenate([cik.astype(BF16), ikb3[:, :, :IDX_DIM]], axis=1)
    topk = min(IDX_TOPK, kb3.shape[1] // 4)
    o_a = _attention(aq, iq, ikw, limits, kb3, vb3, ikb3, b, topk)

    o_b, s_fin = _gla(gq, gk, gv, glog, gr, w["g_gla_norm"], s0, b, chunk)

    h1, h1b, s1t, s2t = _merge(x2, o_a, o_b, gate, w["w_pa"], w["w_pb"], w["w_out"],
                               w["ln1_g"], w["ln1_b"], w["w_pq"], w["pk1"], w["pk2"], alpha, tm)
    cnt, e1, rank, e2 = _select(s1t, s2t, tm)
    y = _peer(h1, h1b, cnt, e1, rank, e2, w["pu"], w["pv"], w["ln2_g"], w["ln2_b"], alpha,
              _pick_tile(n, PEER_TOKEN_TILE))

    k_out = k32.reshape(b, t, A_KV_HEADS, A_HEAD_DIM)
    v_out = v32.reshape(b, t, A_KV_HEADS, A_HEAD_DIM)
    ik_out = ikw[:, :IDX_DIM].reshape(b, t, IDX_DIM)
    return y.reshape(b, t, d), k_out, v_out, ik_out, s_fin


def kernel(x_prompt, x_sample, cache_k, cache_v, cache_idx_k, state_gla, w_in, w_fa, b_fa, g_gla_norm,
           w_pa, w_pb, w_out, ln1_g, ln1_b, w_pq, pk1, pk2, pu, pv, ln2_g, ln2_b):
    depth = w_in.shape[0]
    alpha = (2.0 * depth) ** 0.25
    bp, tp, _ = x_prompt.shape
    bs, ts, _ = x_sample.shape
    past_len = cache_k.shape[2]
    pos_p = jnp.arange(tp)
    pos_s = past_len + jnp.arange(ts)
    lim_p = (np.arange(tp) // CHUNK + 1) * CHUNK
    lim_s = np.full((ts,), past_len + ts)
    names = ("w_in", "w_fa", "b_fa", "g_gla_norm", "w_pa", "w_pb", "w_out", "ln1_g", "ln1_b",
             "w_pq", "pk1", "pk2", "pu", "pv", "ln2_g", "ln2_b")
    stacked = (w_in, w_fa, b_fa, g_gla_norm, w_pa, w_pb, w_out, ln1_g, ln1_b, w_pq, pk1, pk2, pu, pv, ln2_g, ln2_b)
    hp, hs = x_prompt, x_sample
    outs_p, outs_s = [], []
    for l in range(depth):
        w = {nm: a[l] for nm, a in zip(names, stacked)}
        s0 = jnp.zeros((bp, G_HEADS, G_KEY_DIM, G_VAL_DIM), F32)
        hp, *rest = _layer(hp, pos_p, lim_p, None, s0, w, chunk=CHUNK, alpha=alpha)
        outs_p.append(rest)
        hs, *rest = _layer(hs, pos_s, lim_s, (cache_k[l], cache_v[l], cache_idx_k[l]), state_gla[l], w,
                           chunk=ts, alpha=alpha)
        outs_s.append(rest)
    stack = lambda outs, i: jnp.stack([o[i] for o in outs])
    return (hp, hs, stack(outs_p, 0), stack(outs_p, 1), stack(outs_p, 2), stack(outs_p, 3),
            stack(outs_s, 0), stack(outs_s, 1), stack(outs_s, 2), stack(outs_s, 3))
```

```python
import functools
import math

import numpy as np
import jax
import jax.numpy as jnp
from jax import lax
from jax.experimental import pallas as pl
from jax.experimental.pallas import tpu as pltpu

F32 = jnp.float32
BF16 = jnp.bfloat16
I32 = jnp.int32

LANES = 128
SUBLANES = 8
VMEM_LIMIT = 56 << 20

CHUNK = 64
A_HEADS = 8
A_KV_HEADS = 2
A_HEAD_DIM = 64
A_GROUP = A_HEADS // A_KV_HEADS
IDX_HEADS = 4
IDX_DIM = 64
IDX_TOPK = 256
ROPE_THETA = 10000.0
G_HEADS = 4
G_KEY_DIM = 128
G_VAL_DIM = 128
G_LOWRANK = 16
G_TAU = 16.0
P_HEADS = 8
P_NKEYS = 128
P_HALF = 128
P_TOPK = 16
LN_EPS = 1e-5

W_AQ = A_HEADS * A_HEAD_DIM
W_AK = A_KV_HEADS * A_HEAD_DIM
W_IQ = IDX_HEADS * IDX_DIM
W_G = G_HEADS * G_KEY_DIM
IN_SIZES = (W_AQ, W_AK, W_AK, W_IQ, IDX_DIM, IDX_HEADS, W_G, W_G, W_G, G_LOWRANK, W_G, None)

TOKEN_TILE = 256
PEER_TOKEN_TILE = 512
Q_BLOCK = 128
KEY_TILE = 512
GLA_SEQS = 4

INT_MIN = -(2 ** 31)
NEG_INF_KEY = -2139095041
NEG_BIG = -1e30


def _dot(a, b):
    return jnp.dot(a, b, preferred_element_type=F32)


def _dot_nt(a, b):
    return lax.dot_general(a, b, (((1,), (1,)), ((), ())), preferred_element_type=F32)


def _dot_tn(a, b):
    return lax.dot_general(a, b, (((0,), (0,)), ((), ())), preferred_element_type=F32)


def _sort_key(x):
    bits = pltpu.bitcast(x, I32)
    key = jnp.where(bits < 0, bits ^ 0x7FFFFFFF, bits)
    return jnp.where(key == -1, 0, key)


def _fold_rows(x, op):
    x = x.reshape(x.shape[0] // SUBLANES, SUBLANES, x.shape[1])
    while x.shape[0] > 1:
        half = x.shape[0] // 2
        folded = op(x[:half], x[half:2 * half])
        x = folded if x.shape[0] == 2 * half else jnp.concatenate([folded, x[2 * half:]], axis=0)
    return x[0]


def _params(sem):
    return pltpu.CompilerParams(dimension_semantics=sem, vmem_limit_bytes=VMEM_LIMIT)


_PG_AQ, _PG_K, _PG_V, _PG_IQ, _PG_IKW, _PG_GQ, _PG_GK, _PG_GV, _PG_GF, _PG_GR, _PG_GATE = range(11)


def _pack_layout(d_model):
    widths = [W_AQ, W_AK, W_AK, W_IQ, LANES, W_G, W_G, W_G, LANES, W_G, 2 * d_model]
    offs = np.concatenate([[0], np.cumsum(widths)]).tolist()
    return widths, offs


def _pack_w_in(w_in):
    d = w_in.shape[0]
    sizes = list(IN_SIZES[:-1]) + [2 * d]
    cuts = np.cumsum(sizes)[:-1].tolist()
    aq, ak, av, iq, ik, iw, gq, gk, gv, gf, gr, gate = jnp.split(w_in, cuts, axis=1)
    z = lambda n: jnp.zeros((d, n), w_in.dtype)
    ikw = jnp.concatenate([ik, iw, z(LANES - IDX_DIM - IDX_HEADS)], axis=1)
    gfp = jnp.concatenate([gf, z(LANES - G_LOWRANK)], axis=1)
    return jnp.concatenate([aq, ak, av, iq, ikw, gq, gk, gv, gfp, gr, gate], axis=1).astype(BF16)


def _rope_tables(pos):
    half = A_HEAD_DIM // 2
    inv = ROPE_THETA ** (-jnp.arange(half, dtype=F32) / half)
    ang = pos.astype(F32)[:, None] * inv[None, :]
    c, s = jnp.cos(ang), jnp.sin(ang)
    return jnp.concatenate([c, c, c, c], -1), jnp.concatenate([-s, s, -s, s], -1)


def _proj_kernel(x_ref, w_ref, wfa_ref, bfa_ref, cos_ref, sin_ref,
                 aq_ref, k_ref, v_ref, kb_ref, vb_ref, iq_ref, ikw_ref, ikb_ref,
                 gq_ref, gk_ref, gv_ref, glog_ref, gr_ref, gate_ref, *, offs, widths):
    xb = x_ref[...].astype(BF16)
    cos = cos_ref[...]
    sin = sin_ref[...]
    lane = lax.broadcasted_iota(I32, cos.shape, 1)
    first_half = (lane & (A_HEAD_DIM // 2)) == 0

    def proj(g):
        return _dot(xb, w_ref[:, offs[g]:offs[g] + widths[g]])

    def rope_slab(y):
        fwd = pltpu.roll(y, LANES - A_HEAD_DIM // 2, 1)
        bwd = pltpu.roll(y, A_HEAD_DIM // 2, 1)
        return y * cos + jnp.where(first_half, fwd, bwd) * sin

    def rope(y):
        return [rope_slab(y[:, s * LANES:(s + 1) * LANES]) for s in range(y.shape[1] // LANES)]

    for s, slab in enumerate(rope(proj(_PG_AQ))):
        aq_ref[:, s * LANES:(s + 1) * LANES] = (slab * (A_HEAD_DIM ** -0.5)).astype(BF16)
    k = rope(proj(_PG_K))[0]
    k_ref[...] = k
    kb_ref[...] = k.astype(BF16)
    v = proj(_PG_V)
    v_ref[...] = v
    vb_ref[...] = v.astype(BF16)
    for s, slab in enumerate(rope(proj(_PG_IQ))):
        iq_ref[:, s * LANES:(s + 1) * LANES] = (slab * (IDX_DIM ** -0.5)).astype(BF16)
    raw = proj(_PG_IKW)
    ikw = jnp.where(lane < IDX_DIM, rope_slab(raw), raw * (IDX_HEADS ** -0.5))
    ikw_ref[...] = ikw
    ikb_ref[...] = ikw.astype(BF16)
    gq_ref[...] = proj(_PG_GQ) * (G_KEY_DIM ** -0.5)
    gk_ref[...] = proj(_PG_GK)
    gv_ref[...] = proj(_PG_GV)
    z = _dot(proj(_PG_GF).astype(BF16), wfa_ref[...]) + bfa_ref[...]
    glog_ref[...] = (jnp.minimum(z, 0.0) - jnp.log1p(jnp.exp(-jnp.abs(z)))) * (1.0 / G_TAU)
    gr_ref[...] = proj(_PG_GR)
    gate_ref[...] = proj(_PG_GATE)


def _proj(x2, pos, w_in, w_fa, b_fa, tm):
    n, d = x2.shape
    t = pos.shape[0]
    widths, offs = _pack_layout(d)
    wp = _pack_w_in(w_in)
    wfa = jnp.concatenate([w_fa, jnp.zeros((LANES - G_LOWRANK, W_G), w_fa.dtype)], 0).astype(BF16)
    cos, sin = _rope_tables(pos)
    if tm > t:
        cos, sin = jnp.tile(cos, (tm // t, 1)), jnp.tile(sin, (tm // t, 1))
    nper = cos.shape[0] // tm
    row = lambda w: pl.BlockSpec((tm, w), lambda i: (i, 0))
    const = lambda a: pl.BlockSpec(a.shape, lambda i: (0, 0))
    tab = pl.BlockSpec((tm, LANES), lambda i: (i % nper, 0))
    outs = [(W_AQ, BF16), (LANES, F32), (LANES, F32), (LANES, BF16), (LANES, BF16), (W_IQ, BF16),
            (LANES, F32), (LANES, BF16), (W_G, F32), (W_G, F32), (W_G, F32), (W_G, F32), (W_G, F32),
            (2 * d, F32)]
    bfa = b_fa.reshape(1, W_G)
    return pl.pallas_call(
        functools.partial(_proj_kernel, offs=offs, widths=widths),
        grid=(n // tm,),
        in_specs=[row(d), const(wp), const(wfa), const(bfa), tab, tab],
        out_specs=[row(w) for w, _ in outs],
        out_shape=[jax.ShapeDtypeStruct((n, w), dt) for w, dt in outs],
        compiler_params=_params(("parallel",)),
        name="proj",
    )(x2, wp, wfa, bfa, cos, sin)


def _attn_kernel(aq_ref, iq_ref, ikw_ref, lim_ref, kb_ref, vt_ref, ikb_ref, o_ref,
                 keys_ref, qs_ref, iqs_ref, acc_ref, *, kt_w, nkt, topk, idx_bits):
    qb = Q_BLOCK

    aq_t = aq_ref[...].astype(F32).T
    for h in range(A_HEADS):
        qs_ref[h // A_GROUP, :, (h % A_GROUP) * qb:(h % A_GROUP + 1) * qb] = \
            aq_t[h * A_HEAD_DIM:(h + 1) * A_HEAD_DIM, :].astype(BF16)
    iq_t = iq_ref[...].astype(F32).T
    for h in range(IDX_HEADS):
        iqs_ref[:, h * qb:(h + 1) * qb] = iq_t[h * IDX_DIM:(h + 1) * IDX_DIM, :].astype(BF16)

    ikw_t = ikw_ref[...].T
    iw_rows = [ikw_t[IDX_DIM + h:IDX_DIM + h + 1, :] for h in range(IDX_HEADS)]
    lim = lim_ref[0, 0:1, :]
    sub = lax.broadcasted_iota(I32, (kt_w, qb), 0)

    def score_tile(kt, _):
        base = pl.multiple_of(kt * kt_w, kt_w)
        ik_t = ikb_ref[0, pl.ds(base, kt_w), :][:, :IDX_DIM]
        s = jnp.maximum(_dot(ik_t, iqs_ref[...]), 0.0)
        score = jnp.zeros((kt_w, qb), F32)
        for h in range(IDX_HEADS):
            score = score + s[:, h * qb:(h + 1) * qb] * iw_rows[h]
        keys_ref[pl.ds(base, kt_w), :] = jnp.where(sub + base < lim, _sort_key(score), NEG_INF_KEY)
        return 0

    lax.fori_loop(0, nkt, score_tile, 0, unroll=min(nkt, 2))

    def count(pred):
        acc = jnp.zeros((SUBLANES, qb), F32)
        for kt in range(nkt):
            acc = acc + _fold_rows(pred(keys_ref[kt * kt_w:(kt + 1) * kt_w, :], sub + kt * kt_w), jnp.add)
        return jnp.sum(acc, axis=0, keepdims=True)

    def count_ge(t_row):
        return count(lambda kk, idx: jnp.where(kk >= t_row, 1.0, 0.0))

    kf = float(topk)
    thr = jnp.where(count_ge(jnp.zeros((1, qb), I32)) >= kf, 0, INT_MIN).astype(I32)

    def thr_bit(i, t):
        cand = t + jnp.left_shift(jnp.int32(1), 30 - i)
        return jnp.where(count_ge(cand) >= kf, cand, t)

    thr = lax.fori_loop(0, 31, thr_bit, thr)
    n_gt = count_ge(thr + 1)
    n_eq = count_ge(thr) - n_gt
    need = kf - n_gt
    finite = thr > NEG_INF_KEY
    excess = jnp.where(finite, jnp.where(n_eq > need, 1.0, 0.0), 0.0)

    def count_eq_below(j_row):
        return count(lambda kk, idx: jnp.where(kk == thr, jnp.where(idx < j_row, 1.0, 0.0), 0.0))

    def resolve_ties():
        def bit(i, jc):
            cand = jc + jnp.left_shift(jnp.int32(1), idx_bits - 1 - i)
            return jnp.where(count_eq_below(cand) <= need - 1.0, cand, jc)
        jc = lax.fori_loop(0, idx_bits, bit, jnp.zeros((1, qb), I32))
        return jnp.where(finite, jc, -1)

    cut = lax.cond(jnp.max(excess) > 0.0, resolve_ties,
                   lambda: jnp.where(finite, 2 ** 30, -1).astype(I32))

    acc_ref[...] = jnp.zeros(acc_ref.shape, F32)
    gq = A_GROUP * qb

    def attend(kt, carry):
        base = pl.multiple_of(kt * kt_w, kt_w)
        k_t = kb_ref[0, pl.ds(base, kt_w), :]
        kk = keys_ref[pl.ds(base, kt_w), :]
        tie = jnp.where(sub + base <= cut, 0.0, NEG_BIG)
        bias = jnp.where(kk > thr, 0.0, jnp.where(kk == thr, tie, NEG_BIG))
        bias = jnp.concatenate([bias] * A_GROUP, axis=1)
        out = []
        for n in range(A_KV_HEADS):
            m_old, l_old = carry[2 * n], carry[2 * n + 1]
            logits = bias + _dot(k_t[:, n * A_HEAD_DIM:(n + 1) * A_HEAD_DIM], qs_ref[n])
            m_new = jnp.maximum(m_old, jnp.max(_fold_rows(logits, jnp.maximum), axis=0, keepdims=True))
            alpha = jnp.exp(m_old - m_new)
            p = jnp.exp(logits - m_new)
            l_new = alpha * l_old + jnp.sum(_fold_rows(p, jnp.add), axis=0, keepdims=True)
            v_t = vt_ref[0, n * A_HEAD_DIM:(n + 1) * A_HEAD_DIM, pl.ds(base, kt_w)]
            acc_ref[n] = alpha * acc_ref[n] + _dot(v_t, p.astype(BF16))
            out += [m_new, l_new]
        return tuple(out)

    init = (jnp.full((1, gq), NEG_BIG, F32), jnp.zeros((1, gq), F32)) * A_KV_HEADS
    fin = lax.fori_loop(0, nkt, attend, init, unroll=min(nkt, 2))
    l_row = [fin[2 * n + 1] for n in range(A_KV_HEADS)]

    o_t = jnp.concatenate([acc_ref[n] / l_row[n] for n in range(A_KV_HEADS)], axis=0)
    for n in range(A_KV_HEADS):
        for g in range(A_GROUP):
            h = n * A_GROUP + g
            blk = o_t[n * A_HEAD_DIM:(n + 1) * A_HEAD_DIM, g * qb:(g + 1) * qb]
            o_ref[:, h * A_HEAD_DIM:(h + 1) * A_HEAD_DIM] = blk.T.astype(BF16)


def _attention(aq, iq, ikw, limits, kb, vb, ikb, b, topk):
    qb, kt_w = Q_BLOCK, KEY_TILE
    tq = aq.shape[0] // b
    tq_pad = -(-tq // qb) * qb
    limits = np.asarray(limits)
    if tq_pad != tq:
        padq = lambda a: jnp.pad(a.reshape(b, tq, -1), ((0, 0), (0, tq_pad - tq), (0, 0))).reshape(b * tq_pad, -1)
        aq, iq, ikw = padq(aq), padq(iq), padq(ikw)
        limits = np.concatenate([limits, np.full((tq_pad - tq,), limits[-1])])
    l_all = kb.shape[1]
    l_pad = -(-l_all // kt_w) * kt_w
    if l_pad != l_all:
        padl = lambda a: jnp.pad(a, ((0, 0), (0, l_pad - l_all), (0, 0)))
        kb, vb, ikb = padl(kb), padl(vb), padl(ikb)
    vt = jnp.swapaxes(vb, 1, 2)
    nq = tq_pad // qb
    lim_blk = limits.reshape(nq, qb)
    nkt = np.minimum(-(-lim_blk.max(axis=1) // kt_w), l_pad // kt_w)
    lim = jnp.asarray(np.broadcast_to(lim_blk[:, None, :], (nq, SUBLANES, qb)).astype(np.int32))
    idx_bits = max(1, int(math.ceil(math.log2(l_pad))))
    keys = lambda a: pl.BlockSpec((1,) + a.shape[1:], lambda bi, j: (bi, 0, 0))
    gq = A_GROUP * qb
    runs, j0 = [], 0
    for j in range(1, nq + 1):
        if j == nq or nkt[j] != nkt[j0]:
            runs.append((j0, j - j0, int(nkt[j0])))
            j0 = j
    outs = []
    for j0, nj, n_tiles in runs:
        qrow = lambda w, j0=j0: pl.BlockSpec((qb, w), lambda bi, j: (bi * nq + j0 + j, 0))
        outs.append(pl.pallas_call(
            functools.partial(_attn_kernel, kt_w=kt_w, nkt=n_tiles, topk=topk, idx_bits=idx_bits),
            grid=(b, nj),
            in_specs=[qrow(W_AQ), qrow(W_IQ), qrow(LANES),
                      pl.BlockSpec((1, SUBLANES, qb), lambda bi, j, j0=j0: (j0 + j, 0, 0)),
                      keys(kb), keys(vt), keys(ikb)],
            out_specs=pl.BlockSpec((qb, W_AQ), lambda bi, j, nj=nj: (bi * nj + j, 0)),
            out_shape=jax.ShapeDtypeStruct((b * nj * qb, W_AQ), BF16),
            scratch_shapes=[pltpu.VMEM((n_tiles * kt_w, qb), I32),
                            pltpu.VMEM((A_KV_HEADS, A_HEAD_DIM, gq), BF16),
                            pltpu.VMEM((IDX_DIM, IDX_HEADS * qb), BF16),
                            pltpu.VMEM((A_KV_HEADS, A_HEAD_DIM, gq), F32)],
            compiler_params=_params(("parallel", "arbitrary")),
            name="attn",
        )(aq, iq, ikw, lim, kb, vt, ikb).reshape(b, nj * qb, W_AQ))
    o = outs[0] if len(outs) == 1 else jnp.concatenate(outs, axis=1)
    return o[:, :tq].reshape(b * tq, W_AQ)


def _gla_constants(c):
    nlev = int(math.log2(c))
    t = np.arange(c)
    mats = [(t[None, :] <= t[:, None])]
    masks = [np.eye(c, dtype=bool)]
    for lev in range(nlev):
        m = c >> (lev + 1)
        ref_row = (t // (2 * m)) * 2 * m + m
        mats.append(t[None, :] <= ref_row[:, None])
        upper = (t & m) != 0
        same = (t[:, None] // (2 * m)) == (t[None, :] // (2 * m))
        masks.append(same & upper[:, None] & ~upper[None, :])
    return (jnp.asarray(np.concatenate(mats, 0).astype(np.float32), BF16),
            jnp.asarray(np.stack(masks).astype(np.float32)), nlev)


def _gla_kernel(q_ref, k_ref, v_ref, g_ref, gr_ref, gn_ref, mst_ref, msk_ref, s0_ref,
                ob_ref, sfin_ref, st_ref, *, c, nlev, nb):
    i = pl.program_id(1)
    hk = G_KEY_DIM

    @pl.when(i == 0)
    def _():
        for s in range(nb):
            for h in range(G_HEADS):
                st_ref[s, h] = s0_ref[s, h].T

    mst = mst_ref[...]
    gn = gn_ref[...]
    hs = lambda a, h: a[:, h * hk:(h + 1) * hk]
    for s in range(nb):
        g = g_ref[s]
        g_hi = g.astype(BF16)
        r1 = g - g_hi.astype(F32)
        g_mid = r1.astype(BF16)
        g_lo = (r1 - g_mid.astype(F32)).astype(BF16)
        bs = _dot(mst, g_hi) + _dot(mst, g_mid) + _dot(mst, g_lo)
        b = bs[0:c]
        q = q_ref[s]
        k = k_ref[s]
        vb = v_ref[s].astype(BF16)
        row = lax.broadcasted_iota(I32, q.shape, 0)

        qb = q.astype(BF16)
        kb = k.astype(BF16)
        attn = [_dot_nt(hs(qb, h), hs(kb, h)) * msk_ref[0] for h in range(G_HEADS)]
        for lev in range(nlev):
            m = c >> (lev + 1)
            upper = (row & m) != 0
            d = b - bs[(lev + 1) * c:(lev + 2) * c]
            e = jnp.exp(jnp.where(upper, d, -d))
            qt = jnp.where(upper, q * e, 0.0).astype(BF16)
            kt = jnp.where(upper, 0.0, k * e).astype(BF16)
            mk = msk_ref[lev + 1]
            for h in range(G_HEADS):
                attn[h] = attn[h] + _dot_nt(hs(qt, h), hs(kt, h)) * mk

        qe = (q * jnp.exp(b)).astype(BF16)
        b_last = b[c - 1:c, :]
        khat = (k * jnp.exp(b_last - b)).astype(BF16)
        dec = jnp.exp(b_last)
        gr = gr_ref[s]
        for h in range(G_HEADS):
            st = st_ref[s, h]
            o = _dot_nt(hs(qe, h), st.astype(BF16)) + _dot(attn[h].astype(BF16), hs(vb, h))
            st_ref[s, h] = st * hs(dec, h) + _dot_tn(hs(vb, h), hs(khat, h))
            ms = jnp.mean(o * o, axis=1, keepdims=True)
            grh = hs(gr, h)
            of = o * lax.rsqrt(ms + LN_EPS) * hs(gn, h) * (grh / (1.0 + jnp.exp(-grh)))
            ob_ref[s, :, h * hk:(h + 1) * hk] = of.astype(BF16)

    @pl.when(i == pl.num_programs(1) - 1)
    def _():
        for s in range(nb):
            for h in range(G_HEADS):
                sfin_ref[s, h] = st_ref[s, h].T


def _gla(gq, gk, gv, glog, gr, g_norm, s0, b, c):
    n = gq.shape[0]
    t = n // b
    nc = t // c
    nb = GLA_SEQS if b % GLA_SEQS == 0 else 1
    mst, msk, nlev = _gla_constants(c)
    seq = lambda a: a.reshape(b, t, W_G)
    row = pl.BlockSpec((nb, c, W_G), lambda bi, i: (bi, i, 0))
    const = lambda a: pl.BlockSpec(a.shape, lambda bi, i: (0,) * a.ndim)
    st_spec = pl.BlockSpec((nb, G_HEADS, G_KEY_DIM, G_VAL_DIM), lambda bi, i: (bi, 0, 0, 0))
    gn = g_norm.reshape(1, W_G)
    ob, s_fin = pl.pallas_call(
        functools.partial(_gla_kernel, c=c, nlev=nlev, nb=nb),
        grid=(b // nb, nc),
        in_specs=[row, row, row, row, row, const(gn), const(mst), const(msk), st_spec],
        out_specs=[row, st_spec],
        out_shape=[jax.ShapeDtypeStruct((b, t, W_G), BF16),
                   jax.ShapeDtypeStruct((b, G_HEADS, G_KEY_DIM, G_VAL_DIM), F32)],
        scratch_shapes=[pltpu.VMEM((nb, G_HEADS, G_VAL_DIM, G_KEY_DIM), F32)],
        compiler_params=_params(("parallel", "arbitrary")),
        name="gla",
    )(seq(gq), seq(gk), seq(gv), seq(glog), seq(gr), gn, mst, msk, s0)
    return ob.reshape(n, W_G), s_fin


def _layer_norm(z, g, b):
    mu = jnp.mean(z, axis=1, keepdims=True)
    zc = z - mu
    var = jnp.mean(zc * zc, axis=1, keepdims=True)
    return zc * lax.rsqrt(var + LN_EPS) * g + b


def _sigmoid(x):
    return 1.0 / (1.0 + jnp.exp(-x))


def _merge_kernel(x_ref, oa_ref, ob_ref, gate_ref, wpa_ref, wpb_ref, wout_ref, g1_ref, b1_ref,
                  wpq_ref, pk1_ref, pk2_ref, h1_ref, h1b_ref, s1t_ref, s2t_ref, *, alpha, d):
    ya = _dot(oa_ref[...], wpa_ref[...])
    yb = _dot(ob_ref[...], wpb_ref[...])
    m = _sigmoid(gate_ref[:, :d]) * ya + _sigmoid(gate_ref[:, d:]) * yb
    mix = _dot(m.astype(BF16), wout_ref[...])
    h1 = _layer_norm(alpha * x_ref[...] + mix, g1_ref[...], b1_ref[...])
    h1_ref[...] = h1
    h1b_ref[...] = h1.T.astype(BF16)
    qp = _dot(h1.astype(BF16), wpq_ref[...])
    for h in range(P_HEADS):
        for half, (pk_ref, st_ref) in enumerate(((pk1_ref, s1t_ref), (pk2_ref, s2t_ref))):
            c0 = (2 * h + half) * P_HALF
            st_ref[h * P_NKEYS:(h + 1) * P_NKEYS, :] = _dot_nt(pk_ref[h], qp[:, c0:c0 + P_HALF].astype(BF16))


def _merge(x2, oa, ob, gate, w_pa, w_pb, w_out, ln_g, ln_b, w_pq, pk1, pk2, alpha, tm):
    n, d = x2.shape
    hk = P_HEADS * P_NKEYS
    row = lambda w: pl.BlockSpec((tm, w), lambda i: (i, 0))
    col = pl.BlockSpec((hk, tm), lambda i: (0, i))
    const = lambda a: pl.BlockSpec(a.shape, lambda i: (0,) * a.ndim)
    ws = [w_pa.astype(BF16), w_pb.astype(BF16), w_out.astype(BF16), ln_g.reshape(1, d), ln_b.reshape(1, d),
          w_pq.astype(BF16), pk1.astype(BF16), pk2.astype(BF16)]
    return pl.pallas_call(
        functools.partial(_merge_kernel, alpha=alpha, d=d),
        grid=(n // tm,),
        in_specs=[row(d), row(W_AQ), row(W_G), row(2 * d)] + [const(w) for w in ws],
        out_specs=[row(d), pl.BlockSpec((d, tm), lambda i: (0, i)), col, col],
        out_shape=[jax.ShapeDtypeStruct((n, d), F32), jax.ShapeDtypeStruct((d, n), BF16),
                   jax.ShapeDtypeStruct((hk, n), F32), jax.ShapeDtypeStruct((hk, n), F32)],
        compiler_params=_params(("parallel",)),
        name="merge",
    )(x2, oa, ob, gate, *ws)


def _top_desc(s, count):
    tops = []
    for r in range(count):
        m = jnp.max(_fold_rows(s, jnp.maximum), axis=0, keepdims=True)
        tops.append(m)
        if r + 1 < count:
            s = jnp.where(s == m, -jnp.inf, s)
    return tops


def _select_kernel(s1t_ref, s2t_ref, cnt_ref, e1_ref, rank_ref, e2_ref):
    nk = P_NKEYS

    def head(h, _):
        r0 = pl.multiple_of(h * nk, nk)
        rows = pl.ds(r0, nk)
        s1 = s1t_ref[rows, :]
        s2 = s2t_ref[rows, :]
        v1 = _top_desc(s1, P_TOPK)
        v2 = _top_desc(s2, P_TOPK)
        pairs = [(a, b) for a in range(P_TOPK) for b in range(P_TOPK // (a + 1))]
        fill = [jnp.full_like(v1[0], -jnp.inf)] * (-len(pairs) % SUBLANES)
        cand = jnp.concatenate([v1[a] + v2[b] for a, b in pairs] + fill, axis=0)
        work, seen = cand, jnp.zeros_like(v1[0])
        tau = jnp.full_like(v1[0], -jnp.inf)
        for _ in range(P_TOPK):
            m = jnp.max(work, axis=0, keepdims=True)
            hit = work == m
            seen = seen + jnp.sum(jnp.where(hit, 1.0, 0.0), axis=0, keepdims=True)
            tau = jnp.maximum(tau, jnp.where(seen >= float(P_TOPK), m, -jnp.inf))
            work = jnp.where(hit, -jnp.inf, work)
        cmax = v1[0] + v2[0]
        zsum = jnp.sum(jnp.where(cand >= tau, jnp.exp(cand - cmax), 0.0), axis=0, keepdims=True)
        v2all = jnp.concatenate(v2, axis=0)
        cnt = jnp.zeros(s1.shape, F32)
        rank = jnp.full(s2.shape, float(P_TOPK), F32)
        for a in range(P_TOPK):
            cnt_a = jnp.sum(jnp.where(v1[a] + v2all >= tau, 1.0, 0.0), axis=0, keepdims=True)
            cnt = jnp.where(s1 == v1[a], cnt_a, cnt)
            rank = jnp.where(s2 == v2[a], float(a), rank)
        cnt_ref[rows, :] = cnt
        rank_ref[rows, :] = rank
        e1_ref[rows, :] = jnp.exp(s1 - v1[0]) / zsum * 0.5
        e2_ref[rows, :] = jnp.exp(s2 - v2[0])
        return 0

    lax.fori_loop(0, P_HEADS, head, 0)


def _select(s1t, s2t, tn):
    hk, n = s1t.shape
    col = pl.BlockSpec((hk, tn), lambda i: (0, i))
    return pl.pallas_call(
        _select_kernel,
        grid=(n // tn,),
        in_specs=[col, col],
        out_specs=[col, col, col, col],
        out_shape=[jax.ShapeDtypeStruct((hk, n), F32)] * 4,
        compiler_params=_params(("parallel",)),
        name="select",
    )(s1t, s2t)


I1_PER_STEP = 8
E_PER_STEP = I1_PER_STEP * P_NKEYS
I2_BLOCK = 16


def _gelu_x2(x):
    return x * (1.0 + lax.erf(x * (2.0 ** -0.5)))


def _peer_steps(t, n_steps, ng):
    item = lambda d: jnp.clip(t - d, 0, n_steps - 1)
    return [(item(d) // ng, item(d) % ng) for d in range(3)]


def _peer_kernel(hb_ref, pu_ref, pvt_ref, cnt_ref, e1_ref, rank_ref, e2_ref, h1_ref, g2_ref, b2_ref,
                 y_ref, a_ref, gw_ref, acc_ref, rank_s, e2_s, *, tn, alpha, n_steps, ng):
    t = pl.program_id(0)
    (_, _), (_, g), (_, g_out) = _peer_steps(t, n_steps, ng)
    slot = t % 2
    prev = 1 - slot

    @pl.when(t == 0)
    def _():
        a_ref[...] = jnp.zeros(a_ref.shape, F32)
        gw_ref[...] = jnp.zeros(gw_ref.shape, BF16)

    @pl.when(g == 0)
    def _():
        rank_s[...] = rank_ref[...].astype(BF16)
        e2_s[...] = e2_ref[...].astype(BF16)

    @pl.when(g_out == 0)
    def _():
        acc_ref[...] = jnp.zeros(acc_ref.shape, F32)

    a_ref[slot] = _dot(pu_ref[...], hb_ref[...])

    assert I1_PER_STEP == SUBLANES
    for lt in range(tn // LANES):
        ls = slice(lt * LANES, (lt + 1) * LANES)
        grp = lambda ref, h: ref[pl.ds(pl.multiple_of(h * P_NKEYS + g * I1_PER_STEP, SUBLANES), SUBLANES), ls]
        cnt8 = [grp(cnt_ref, h) for h in range(P_HEADS)]
        e18 = [grp(e1_ref, h) for h in range(P_HEADS)]
        for j in range(I1_PER_STEP):
            bcast = lambda a8: jnp.broadcast_to(a8[j:j + 1, :], (I2_BLOCK, LANES)).astype(BF16)
            cnt = [bcast(cnt8[h]) for h in range(P_HEADS)]
            e1 = [bcast(e18[h]) for h in range(P_HEADS)]
            for i2b in range(P_NKEYS // I2_BLOCK):
                w = jnp.zeros((I2_BLOCK, LANES), BF16)
                for h in range(P_HEADS):
                    rows = slice(h * P_NKEYS + i2b * I2_BLOCK, h * P_NKEYS + (i2b + 1) * I2_BLOCK)
                    w = w + jnp.where(rank_s[rows, ls] < cnt[h], e2_s[rows, ls] * e1[h], jnp.zeros((), BF16))
                arow = slice(j * P_NKEYS + i2b * I2_BLOCK, j * P_NKEYS + (i2b + 1) * I2_BLOCK)
                gw_ref[prev, arow, ls] = w * _gelu_x2(a_ref[prev, arow, ls]).astype(BF16)

    acc_ref[...] += _dot(pvt_ref[0], gw_ref[slot])

    @pl.when(jnp.logical_and(g_out == ng - 1, t >= 2))
    def _():
        y_ref[...] = _layer_norm(alpha * h1_ref[...] + acc_ref[...].T, g2_ref[...], b2_ref[...])


def _peer(h1, h1b, cnt, e1, rank, e2, pu, pv, ln_g, ln_b, alpha, tn):
    n, d = h1.shape
    hk = P_HEADS * P_NKEYS
    ng = pu.shape[0] // E_PER_STEP
    n_steps = (n // tn) * ng
    stage = lambda k, f: (lambda t: f(*_peer_steps(t, n_steps, ng)[k]))
    const = pl.BlockSpec((1, d), lambda t: (0, 0))
    routing = pl.BlockSpec((hk, tn), stage(1, lambda i, g: (0, i)))
    pvt = pv.astype(BF16).reshape(ng, E_PER_STEP, d).transpose(0, 2, 1)
    return pl.pallas_call(
        functools.partial(_peer_kernel, tn=tn, alpha=alpha, n_steps=n_steps, ng=ng),
        grid=(n_steps + 2,),
        in_specs=[pl.BlockSpec((d, tn), stage(0, lambda i, g: (0, i))),
                  pl.BlockSpec((E_PER_STEP, d), stage(0, lambda i, g: (g, 0))),
                  pl.BlockSpec((1, d, E_PER_STEP), stage(2, lambda i, g: (g, 0, 0))),
                  routing, routing, routing, routing,
                  pl.BlockSpec((tn, d), stage(2, lambda i, g: (i, 0))), const, const],
        out_specs=pl.BlockSpec((tn, d), stage(2, lambda i, g: (i, 0))),
        out_shape=jax.ShapeDtypeStruct((n, d), F32),
        scratch_shapes=[pltpu.VMEM((2, E_PER_STEP, tn), F32), pltpu.VMEM((2, E_PER_STEP, tn), BF16),
                        pltpu.VMEM((d, tn), F32), pltpu.VMEM((hk, tn), BF16), pltpu.VMEM((hk, tn), BF16)],
        compiler_params=_params(("arbitrary",)),
        name="peer",
    )(h1b, pu.astype(BF16), pvt, cnt, e1, rank, e2, h1, ln_g.reshape(1, d), ln_b.reshape(1, d))


def _pick_tile(n, pref):
    t = min(n, pref)
    assert n % t == 0
    return t


def _layer(x, pos, limits, past, s0, w, *, chunk, alpha):
    b, t, d = x.shape
    n = b * t
    x2 = x.reshape(n, d)
    tm = _pick_tile(n, TOKEN_TILE)
    (aq, k32, v32, kb, vb, iq, ikw, ikb, gq, gk, gv, glog, gr, gate) = _proj(
        x2, pos, w["w_in"], w["w_fa"], w["b_fa"], tm)

    kb3, vb3, ikb3 = kb.reshape(b, t, LANES), vb.reshape(b, t, LANES), ikb.reshape(b, t, LANES)
    if past is not None:
        ck, cv, cik = past
        p = ck.shape[1]
        kb3 = jnp.concatenate([ck.reshape(b, p, LANES).astype(BF16), kb3], axis=1)
        vb3 = jnp.concatenate([cv.reshape(b, p, LANES).astype(BF16), vb3], axis=1)
        ikb3 = jnp.concatenate([cik.astype(BF16), ikb3[:, :, :IDX_DIM]], axis=1)
    topk = min(IDX_TOPK, kb3.shape[1] // 4)
    o_a = _attention(aq, iq, ikw, limits, kb3, vb3, ikb3, b, topk)

    o_b, s_fin = _gla(gq, gk, gv, glog, gr, w["g_gla_norm"], s0, b, chunk)

    h1, h1b, s1t, s2t = _merge(x2, o_a, o_b, gate, w["w_pa"], w["w_pb"], w["w_out"],
                               w["ln1_g"], w["ln1_b"], w["w_pq"], w["pk1"], w["pk2"], alpha, tm)
    cnt, e1, rank, e2 = _select(s1t, s2t, tm)
    y = _peer(h1, h1b, cnt, e1, rank, e2, w["pu"], w["pv"], w["ln2_g"], w["ln2_b"], alpha,
              _pick_tile(n, PEER_TOKEN_TILE))

    k_out = k32.reshape(b, t, A_KV_HEADS, A_HEAD_DIM)
    v_out = v32.reshape(b, t, A_KV_HEADS, A_HEAD_DIM)
    ik_out = ikw[:, :IDX_DIM].reshape(b, t, IDX_DIM)
    return y.reshape(b, t, d), k_out, v_out, ik_out, s_fin


def kernel(x_prompt, x_sample, cache_k, cache_v, cache_idx_k, state_gla, w_in, w_fa, b_fa, g_gla_norm,
           w_pa, w_pb, w_out, ln1_g, ln1_b, w_pq, pk1, pk2, pu, pv, ln2_g, ln2_b):
    depth = w_in.shape[0]
    alpha = (2.0 * depth) ** 0.25
    bp, tp, _ = x_prompt.shape
    bs, ts, _ = x_sample.shape
    past_len = cache_k.shape[2]
    pos_p = jnp.arange(tp)
    pos_s = past_len + jnp.arange(ts)
    lim_p = (np.arange(tp) // CHUNK + 1) * CHUNK
    lim_s = np.full((ts,), past_len + ts)
    names = ("w_in", "w_fa", "b_fa", "g_gla_norm", "w_pa", "w_pb", "w_out", "ln1_g", "ln1_b",
             "w_pq", "pk1", "pk2", "pu", "pv", "ln2_g", "ln2_b")
    stacked = (w_in, w_fa, b_fa, g_gla_norm, w_pa, w_pb, w_out, ln1_g, ln1_b, w_pq, pk1, pk2, pu, pv, ln2_g, ln2_b)
    hp, hs = x_prompt, x_sample
    outs_p, outs_s = [], []
    for l in range(depth):
        w = {nm: a[l] for nm, a in zip(names, stacked)}
        s0 = jnp.zeros((bp, G_HEADS, G_KEY_DIM, G_VAL_DIM), F32)
        hp, *rest = _layer(hp, pos_p, lim_p, None, s0, w, chunk=CHUNK, alpha=alpha)
        outs_p.append(rest)
        hs, *rest = _layer(hs, pos_s, lim_s, (cache_k[l], cache_v[l], cache_idx_k[l]), state_gla[l], w,
                           chunk=ts, alpha=alpha)
        outs_s.append(rest)
    stack = lambda outs, i: jnp.stack([o[i] for o in outs])
    return (hp, hs, stack(outs_p, 0), stack(outs_p, 1), stack(outs_p, 2), stack(outs_p, 3),
            stack(outs_s, 0), stack(outs_s, 1), stack(outs_s, 2), stack(outs_s, 3))
```

```python
import functools
import math

import numpy as np
import jax
import jax.numpy as jnp
from jax import lax
from jax.experimental import pallas as pl
from jax.experimental.pallas import tpu as pltpu

F32 = jnp.float32
BF16 = jnp.bfloat16
I32 = jnp.int32

LANES = 128
SUBLANES = 8
VMEM_LIMIT = 56 << 20

CHUNK = 64
A_HEADS = 8
A_KV_HEADS = 2
A_HEAD_DIM = 64
A_GROUP = A_HEADS // A_KV_HEADS
IDX_HEADS = 4
IDX_DIM = 64
IDX_TOPK = 256
ROPE_THETA = 10000.0
G_HEADS = 4
G_KEY_DIM = 128
G_VAL_DIM = 128
G_LOWRANK = 16
G_TAU = 16.0
P_HEADS = 8
P_NKEYS = 128
P_HALF = 128
P_TOPK = 16
LN_EPS = 1e-5

W_AQ = A_HEADS * A_HEAD_DIM
W_AK = A_KV_HEADS * A_HEAD_DIM
W_IQ = IDX_HEADS * IDX_DIM
W_G = G_HEADS * G_KEY_DIM
IN_SIZES = (W_AQ, W_AK, W_AK, W_IQ, IDX_DIM, IDX_HEADS, W_G, W_G, W_G, G_LOWRANK, W_G, None)

TOKEN_TILE = 256
PEER_TOKEN_TILE = 512
Q_BLOCK = 128
KEY_TILE = 512
GLA_SEQS = 4

INT_MIN = -(2 ** 31)
NEG_INF_KEY = -2139095041
NEG_BIG = -1e30


def _dot(a, b):
    return jnp.dot(a, b, preferred_element_type=F32)


def _dot_nt(a, b):
    return lax.dot_general(a, b, (((1,), (1,)), ((), ())), preferred_element_type=F32)


def _dot_tn(a, b):
    return lax.dot_general(a, b, (((0,), (0,)), ((), ())), preferred_element_type=F32)


def _sort_key(x):
    bits = pltpu.bitcast(x, I32)
    key = jnp.where(bits < 0, bits ^ 0x7FFFFFFF, bits)
    return jnp.where(key == -1, 0, key)


def _fold_rows(x, op):
    x = x.reshape(x.shape[0] // SUBLANES, SUBLANES, x.shape[1])
    while x.shape[0] > 1:
        half = x.shape[0] // 2
        folded = op(x[:half], x[half:2 * half])
        x = folded if x.shape[0] == 2 * half else jnp.concatenate([folded, x[2 * half:]], axis=0)
    return x[0]


def _params(sem):
    return pltpu.CompilerParams(dimension_semantics=sem, vmem_limit_bytes=VMEM_LIMIT)


_PG_AQ, _PG_K, _PG_V, _PG_IQ, _PG_IKW, _PG_GQ, _PG_GK, _PG_GV, _PG_GF, _PG_GR, _PG_GATE = range(11)


def _pack_layout(d_model):
    widths = [W_AQ, W_AK, W_AK, W_IQ, LANES, W_G, W_G, W_G, LANES, W_G, 2 * d_model]
    offs = np.concatenate([[0], np.cumsum(widths)]).tolist()
    return widths, offs


def _pack_w_in(w_in):
    d = w_in.shape[0]
    sizes = list(IN_SIZES[:-1]) + [2 * d]
    cuts = np.cumsum(sizes)[:-1].tolist()
    aq, ak, av, iq, ik, iw, gq, gk, gv, gf, gr, gate = jnp.split(w_in, cuts, axis=1)
    z = lambda n: jnp.zeros((d, n), w_in.dtype)
    ikw = jnp.concatenate([ik, iw, z(LANES - IDX_DIM - IDX_HEADS)], axis=1)
    gfp = jnp.concatenate([gf, z(LANES - G_LOWRANK)], axis=1)
    return jnp.concatenate([aq, ak, av, iq, ikw, gq, gk, gv, gfp, gr, gate], axis=1).astype(BF16)


def _rope_tables(pos):
    half = A_HEAD_DIM // 2
    inv = ROPE_THETA ** (-jnp.arange(half, dtype=F32) / half)
    ang = pos.astype(F32)[:, None] * inv[None, :]
    c, s = jnp.cos(ang), jnp.sin(ang)
    return jnp.concatenate([c, c, c, c], -1), jnp.concatenate([-s, s, -s, s], -1)


def _proj_kernel(x_ref, w_ref, wfa_ref, bfa_ref, cos_ref, sin_ref,
                 aq_ref, k_ref, v_ref, kb_ref, vb_ref, iq_ref, ikw_ref, ikb_ref,
                 gq_ref, gk_ref, gv_ref, glog_ref, gr_ref, gate_ref, *, offs, widths):
    xb = x_ref[...].astype(BF16)
    cos = cos_ref[...]
    sin = sin_ref[...]
    lane = lax.broadcasted_iota(I32, cos.shape, 1)
    first_half = (lane & (A_HEAD_DIM // 2)) == 0

    def proj(g):
        return _dot(xb, w_ref[:, offs[g]:offs[g] + widths[g]])

    def rope_slab(y):
        fwd = pltpu.roll(y, LANES - A_HEAD_DIM // 2, 1)
        bwd = pltpu.roll(y, A_HEAD_DIM // 2, 1)
        return y * cos + jnp.where(first_half, fwd, bwd) * sin

    def rope(y):
        return [rope_slab(y[:, s * LANES:(s + 1) * LANES]) for s in range(y.shape[1] // LANES)]

    for s, slab in enumerate(rope(proj(_PG_AQ))):
        aq_ref[:, s * LANES:(s + 1) * LANES] = (slab * (A_HEAD_DIM ** -0.5)).astype(BF16)
    k = rope(proj(_PG_K))[0]
    k_ref[...] = k
    kb_ref[...] = k.astype(BF16)
    v = proj(_PG_V)
    v_ref[...] = v
    vb_ref[...] = v.astype(BF16)
    for s, slab in enumerate(rope(proj(_PG_IQ))):
        iq_ref[:, s * LANES:(s + 1) * LANES] = (slab * (IDX_DIM ** -0.5)).astype(BF16)
    raw = proj(_PG_IKW)
    ikw = jnp.where(lane < IDX_DIM, rope_slab(raw), raw * (IDX_HEADS ** -0.5))
    ikw_ref[...] = ikw
    ikb_ref[...] = ikw.astype(BF16)
    gq_ref[...] = proj(_PG_GQ) * (G_KEY_DIM ** -0.5)
    gk_ref[...] = proj(_PG_GK)
    gv_ref[...] = proj(_PG_GV)
    z = _dot(proj(_PG_GF).astype(BF16), wfa_ref[...]) + bfa_ref[...]
    glog_ref[...] = (jnp.minimum(z, 0.0) - jnp.log1p(jnp.exp(-jnp.abs(z)))) * (1.0 / G_TAU)
    gr_ref[...] = proj(_PG_GR)
    gate_ref[...] = proj(_PG_GATE)


def _proj(x2, pos, w_in, w_fa, b_fa, tm):
    n, d = x2.shape
    t = pos.shape[0]
    widths, offs = _pack_layout(d)
    wp = _pack_w_in(w_in)
    wfa = jnp.concatenate([w_fa, jnp.zeros((LANES - G_LOWRANK, W_G), w_fa.dtype)], 0).astype(BF16)
    cos, sin = _rope_tables(pos)
    if tm > t:
        cos, sin = jnp.tile(cos, (tm // t, 1)), jnp.tile(sin, (tm // t, 1))
    nper = cos.shape[0] // tm
    row = lambda w: pl.BlockSpec((tm, w), lambda i: (i, 0))
    const = lambda a: pl.BlockSpec(a.shape, lambda i: (0, 0))
    tab = pl.BlockSpec((tm, LANES), lambda i: (i % nper, 0))
    outs = [(W_AQ, BF16), (LANES, F32), (LANES, F32), (LANES, BF16), (LANES, BF16), (W_IQ, BF16),
            (LANES, F32), (LANES, BF16), (W_G, F32), (W_G, F32), (W_G, F32), (W_G, F32), (W_G, F32),
            (2 * d, F32)]
    bfa = b_fa.reshape(1, W_G)
    return pl.pallas_call(
        functools.partial(_proj_kernel, offs=offs, widths=widths),
        grid=(n // tm,),
        in_specs=[row(d), const(wp), const(wfa), const(bfa), tab, tab],
        out_specs=[row(w) for w, _ in outs],
        out_shape=[jax.ShapeDtypeStruct((n, w), dt) for w, dt in outs],
        compiler_params=_params(("parallel",)),
        name="proj",
    )(x2, wp, wfa, bfa, cos, sin)


def _attn_kernel(aq_ref, iq_ref, ikw_ref, lim_ref, kb_ref, vt_ref, ikb_ref, o_ref,
                 keys_ref, qs_ref, iqs_ref, acc_ref, *, kt_w, nkt, topk, idx_bits):
    qb = Q_BLOCK

    aq_t = aq_ref[...].astype(F32).T
    for h in range(A_HEADS):
        qs_ref[h // A_GROUP, :, (h % A_GROUP) * qb:(h % A_GROUP + 1) * qb] = \
            aq_t[h * A_HEAD_DIM:(h + 1) * A_HEAD_DIM, :].astype(BF16)
    iq_t = iq_ref[...].astype(F32).T
    for h in range(IDX_HEADS):
        iqs_ref[:, h * qb:(h + 1) * qb] = iq_t[h * IDX_DIM:(h + 1) * IDX_DIM, :].astype(BF16)

    ikw_t = ikw_ref[...].T
    iw_rows = [ikw_t[IDX_DIM + h:IDX_DIM + h + 1, :] for h in range(IDX_HEADS)]
    lim = lim_ref[0, 0:1, :]
    sub = lax.broadcasted_iota(I32, (kt_w, qb), 0)

    def score_tile(kt, _):
        base = pl.multiple_of(kt * kt_w, kt_w)
        ik_t = ikb_ref[0, pl.ds(base, kt_w), :][:, :IDX_DIM]
        s = jnp.maximum(_dot(ik_t, iqs_ref[...]), 0.0)
        score = jnp.zeros((kt_w, qb), F32)
        for h in range(IDX_HEADS):
            score = score + s[:, h * qb:(h + 1) * qb] * iw_rows[h]
        keys_ref[pl.ds(base, kt_w), :] = jnp.where(sub + base < lim, _sort_key(score), NEG_INF_KEY)
        return 0

    lax.fori_loop(0, nkt, score_tile, 0, unroll=min(nkt, 2))

    def count(pred):
        acc = jnp.zeros((SUBLANES, qb), F32)
        for kt in range(nkt):
            acc = acc + _fold_rows(pred(keys_ref[kt * kt_w:(kt + 1) * kt_w, :], sub + kt * kt_w), jnp.add)
        return jnp.sum(acc, axis=0, keepdims=True)

    def count_ge(t_row):
        return count(lambda kk, idx: jnp.where(kk >= t_row, 1.0, 0.0))

    kf = float(topk)
    thr = jnp.where(count_ge(jnp.zeros((1, qb), I32)) >= kf, 0, INT_MIN).astype(I32)

    def thr_bit(i, t):
        cand = t + jnp.left_shift(jnp.int32(1), 30 - i)
        return jnp.where(count_ge(cand) >= kf, cand, t)

    thr = lax.fori_loop(0, 31, thr_bit, thr)
    n_gt = count_ge(thr + 1)
    n_eq = count_ge(thr) - n_gt
    need = kf - n_gt
    finite = thr > NEG_INF_KEY
    excess = jnp.where(finite, jnp.where(n_eq > need, 1.0, 0.0), 0.0)

    def count_eq_below(j_row):
        return count(lambda kk, idx: jnp.where(kk == thr, jnp.where(idx < j_row, 1.0, 0.0), 0.0))

    def resolve_ties():
        def bit(i, jc):
            cand = jc + jnp.left_shift(jnp.int32(1), idx_bits - 1 - i)
            return jnp.where(count_eq_below(cand) <= need - 1.0, cand, jc)
        jc = lax.fori_loop(0, idx_bits, bit, jnp.zeros((1, qb), I32))
        return jnp.where(finite, jc, -1)

    cut = lax.cond(jnp.max(excess) > 0.0, resolve_ties,
                   lambda: jnp.where(finite, 2 ** 30, -1).astype(I32))

    acc_ref[...] = jnp.zeros(acc_ref.shape, F32)
    gq = A_GROUP * qb

    def attend(kt, carry):
        base = pl.multiple_of(kt * kt_w, kt_w)
        k_t = kb_ref[0, pl.ds(base, kt_w), :]
        kk = keys_ref[pl.ds(base, kt_w), :]
        tie = jnp.where(sub + base <= cut, 0.0, NEG_BIG)
        bias = jnp.where(kk > thr, 0.0, jnp.where(kk == thr, tie, NEG_BIG))
        bias = jnp.concatenate([bias] * A_GROUP, axis=1)
        out = []
        for n in range(A_KV_HEADS):
            m_old, l_old = carry[2 * n], carry[2 * n + 1]
            logits = bias + _dot(k_t[:, n * A_HEAD_DIM:(n + 1) * A_HEAD_DIM], qs_ref[n])
            m_new = jnp.maximum(m_old, jnp.max(_fold_rows(logits, jnp.maximum), axis=0, keepdims=True))
            alpha = jnp.exp(m_old - m_new)
            p = jnp.exp(logits - m_new)
            l_new = alpha * l_old + jnp.sum(_fold_rows(p, jnp.add), axis=0, keepdims=True)
            v_t = vt_ref[0, n * A_HEAD_DIM:(n + 1) * A_HEAD_DIM, pl.ds(base, kt_w)]
            acc_ref[n] = alpha * acc_ref[n] + _dot(v_t, p.astype(BF16))
            out += [m_new, l_new]
        return tuple(out)

    init = (jnp.full((1, gq), NEG_BIG, F32), jnp.zeros((1, gq), F32)) * A_KV_HEADS
    fin = lax.fori_loop(0, nkt, attend, init, unroll=min(nkt, 2))
    l_row = [fin[2 * n + 1] for n in range(A_KV_HEADS)]

    o_t = jnp.concatenate([acc_ref[n] / l_row[n] for n in range(A_KV_HEADS)], axis=0)
    for n in range(A_KV_HEADS):
        for g in range(A_GROUP):
            h = n * A_GROUP + g
            blk = o_t[n * A_HEAD_DIM:(n + 1) * A_HEAD_DIM, g * qb:(g + 1) * qb]
            o_ref[:, h * A_HEAD_DIM:(h + 1) * A_HEAD_DIM] = blk.T.astype(BF16)


def _attention(aq, iq, ikw, limits, kb, vb, ikb, b, topk):
    qb, kt_w = Q_BLOCK, KEY_TILE
    tq = aq.shape[0] // b
    tq_pad = -(-tq // qb) * qb
    limits = np.asarray(limits)
    if tq_pad != tq:
        padq = lambda a: jnp.pad(a.reshape(b, tq, -1), ((0, 0), (0, tq_pad - tq), (0, 0))).reshape(b * tq_pad, -1)
        aq, iq, ikw = padq(aq), padq(iq), padq(ikw)
        limits = np.concatenate([limits, np.full((tq_pad - tq,), limits[-1])])
    l_all = kb.shape[1]
    l_pad = -(-l_all // kt_w) * kt_w
    if l_pad != l_all:
        padl = lambda a: jnp.pad(a, ((0, 0), (0, l_pad - l_all), (0, 0)))
        kb, vb, ikb = padl(kb), padl(vb), padl(ikb)
    vt = jnp.swapaxes(vb, 1, 2)
    nq = tq_pad // qb
    lim_blk = limits.reshape(nq, qb)
    nkt = np.minimum(-(-lim_blk.max(axis=1) // kt_w), l_pad // kt_w)
    lim = jnp.asarray(np.broadcast_to(lim_blk[:, None, :], (nq, SUBLANES, qb)).astype(np.int32))
    idx_bits = max(1, int(math.ceil(math.log2(l_pad))))
    keys = lambda a: pl.BlockSpec((1,) + a.shape[1:], lambda bi, j: (bi, 0, 0))
    gq = A_GROUP * qb
    runs, j0 = [], 0
    for j in range(1, nq + 1):
        if j == nq or nkt[j] != nkt[j0]:
            runs.append((j0, j - j0, int(nkt[j0])))
            j0 = j
    outs = []
    for j0, nj, n_tiles in runs:
        qrow = lambda w, j0=j0: pl.BlockSpec((qb, w), lambda bi, j: (bi * nq + j0 + j, 0))
        outs.append(pl.pallas_call(
            functools.partial(_attn_kernel, kt_w=kt_w, nkt=n_tiles, topk=topk, idx_bits=idx_bits),
            grid=(b, nj),
            in_specs=[qrow(W_AQ), qrow(W_IQ), qrow(LANES),
                      pl.BlockSpec((1, SUBLANES, qb), lambda bi, j, j0=j0: (j0 + j, 0, 0)),
                      keys(kb), keys(vt), keys(ikb)],
            out_specs=pl.BlockSpec((qb, W_AQ), lambda bi, j, nj=nj: (bi * nj + j, 0)),
            out_shape=jax.ShapeDtypeStruct((b * nj * qb, W_AQ), BF16),
            scratch_shapes=[pltpu.VMEM((n_tiles * kt_w, qb), I32),
                            pltpu.VMEM((A_KV_HEADS, A_HEAD_DIM, gq), BF16),
                            pltpu.VMEM((IDX_DIM, IDX_HEADS * qb), BF16),
                            pltpu.VMEM((A_KV_HEADS, A_HEAD_DIM, gq), F32)],
            compiler_params=_params(("parallel", "arbitrary")),
            name="attn",
        )(aq, iq, ikw, lim, kb, vt, ikb).reshape(b, nj * qb, W_AQ))
    o = outs[0] if len(outs) == 1 else jnp.concatenate(outs, axis=1)
    return o[:, :tq].reshape(b * tq, W_AQ)


def _gla_constants(c):
    nlev = int(math.log2(c))
    t = np.arange(c)
    mats = [(t[None, :] <= t[:, None])]
    masks = [np.eye(c, dtype=bool)]
    for lev in range(nlev):
        m = c >> (lev + 1)
        ref_row = (t // (2 * m)) * 2 * m + m
        mats.append(t[None, :] <= ref_row[:, None])
        upper = (t & m) != 0
        same = (t[:, None] // (2 * m)) == (t[None, :] // (2 * m))
        masks.append(same & upper[:, None] & ~upper[None, :])
    return (jnp.asarray(np.concatenate(mats, 0).astype(np.float32), BF16),
            jnp.asarray(np.stack(masks).astype(np.float32)), nlev)


def _gla_kernel(q_ref, k_ref, v_ref, g_ref, gr_ref, gn_ref, mst_ref, msk_ref, s0_ref,
                ob_ref, sfin_ref, st_ref, *, c, nlev, nb):
    i = pl.program_id(1)
    hk = G_KEY_DIM

    @pl.when(i == 0)
    def _():
        for s in range(nb):
            for h in range(G_HEADS):
                st_ref[s, h] = s0_ref[s, h].T

    mst = mst_ref[...]
    gn = gn_ref[...]
    hs = lambda a, h: a[:, h * hk:(h + 1) * hk]
    for s in range(nb):
        g = g_ref[s]
        g_hi = g.astype(BF16)
        r1 = g - g_hi.astype(F32)
        g_mid = r1.astype(BF16)
        g_lo = (r1 - g_mid.astype(F32)).astype(BF16)
        bs = _dot(mst, g_hi) + _dot(mst, g_mid) + _dot(mst, g_lo)
        b = bs[0:c]
        q = q_ref[s]
        k = k_ref[s]
        vb = v_ref[s].astype(BF16)
        row = lax.broadcasted_iota(I32, q.shape, 0)

        qb = q.astype(BF16)
        kb = k.astype(BF16)
        attn = [_dot_nt(hs(qb, h), hs(kb, h)) * msk_ref[0] for h in range(G_HEADS)]
        for lev in range(nlev):
            m = c >> (lev + 1)
            upper = (row & m) != 0
            d = b - bs[(lev + 1) * c:(lev + 2) * c]
            e = jnp.exp(jnp.where(upper, d, -d))
            qt = jnp.where(upper, q * e, 0.0).astype(BF16)
            kt = jnp.where(upper, 0.0, k * e).astype(BF16)
            mk = msk_ref[lev + 1]
            for h in range(G_HEADS):
                attn[h] = attn[h] + _dot_nt(hs(qt, h), hs(kt, h)) * mk

        qe = (q * jnp.exp(b)).astype(BF16)
        b_last = b[c - 1:c, :]
        khat = (k * jnp.exp(b_last - b)).astype(BF16)
        dec = jnp.exp(b_last)
        gr = gr_ref[s]
        for h in range(G_HEADS):
            st = st_ref[s, h]
            o = _dot_nt(hs(qe, h), st.astype(BF16)) + _dot(attn[h].astype(BF16), hs(vb, h))
            st_ref[s, h] = st * hs(dec, h) + _dot_tn(hs(vb, h), hs(khat, h))
            ms = jnp.mean(o * o, axis=1, keepdims=True)
            grh = hs(gr, h)
            of = o * lax.rsqrt(ms + LN_EPS) * hs(gn, h) * (grh / (1.0 + jnp.exp(-grh)))
            ob_ref[s, :, h * hk:(h + 1) * hk] = of.astype(BF16)

    @pl.when(i == pl.num_programs(1) - 1)
    def _():
        for s in range(nb):
            for h in range(G_HEADS):
                sfin_ref[s, h] = st_ref[s, h].T


def _gla(gq, gk, gv, glog, gr, g_norm, s0, b, c):
    n = gq.shape[0]
    t = n // b
    nc = t // c
    nb = GLA_SEQS if b % GLA_SEQS == 0 else 1
    mst, msk, nlev = _gla_constants(c)
    seq = lambda a: a.reshape(b, t, W_G)
    row = pl.BlockSpec((nb, c, W_G), lambda bi, i: (bi, i, 0))
    const = lambda a: pl.BlockSpec(a.shape, lambda bi, i: (0,) * a.ndim)
    st_spec = pl.BlockSpec((nb, G_HEADS, G_KEY_DIM, G_VAL_DIM), lambda bi, i: (bi, 0, 0, 0))
    gn = g_norm.reshape(1, W_G)
    ob, s_fin = pl.pallas_call(
        functools.partial(_gla_kernel, c=c, nlev=nlev, nb=nb),
        grid=(b // nb, nc),
        in_specs=[row, row, row, row, row, const(gn), const(mst), const(msk), st_spec],
        out_specs=[row, st_spec],
        out_shape=[jax.ShapeDtypeStruct((b, t, W_G), BF16),
                   jax.ShapeDtypeStruct((b, G_HEADS, G_KEY_DIM, G_VAL_DIM), F32)],
        scratch_shapes=[pltpu.VMEM((nb, G_HEADS, G_VAL_DIM, G_KEY_DIM), F32)],
        compiler_params=_params(("parallel", "arbitrary")),
        name="gla",
    )(seq(gq), seq(gk), seq(gv), seq(glog), seq(gr), gn, mst, msk, s0)
    return ob.reshape(n, W_G), s_fin


def _layer_norm(z, g, b):
    mu = jnp.mean(z, axis=1, keepdims=True)
    zc = z - mu
    var = jnp.mean(zc * zc, axis=1, keepdims=True)
    return zc * lax.rsqrt(var + LN_EPS) * g + b


def _sigmoid(x):
    return 1.0 / (1.0 + jnp.exp(-x))


def _merge_kernel(x_ref, oa_ref, ob_ref, gate_ref, wpa_ref, wpb_ref, wout_ref, g1_ref, b1_ref,
                  wpq_ref, pk1_ref, pk2_ref, h1_ref, h1b_ref, s1t_ref, s2t_ref, *, alpha, d):
    ya = _dot(oa_ref[...], wpa_ref[...])
    yb = _dot(ob_ref[...], wpb_ref[...])
    m = _sigmoid(gate_ref[:, :d]) * ya + _sigmoid(gate_ref[:, d:]) * yb
    mix = _dot(m.astype(BF16), wout_ref[...])
    h1 = _layer_norm(alpha * x_ref[...] + mix, g1_ref[...], b1_ref[...])
    h1_ref[...] = h1
    h1b_ref[...] = h1.T.astype(BF16)
    qp = _dot(h1.astype(BF16), wpq_ref[...])
    for h in range(P_HEADS):
        for half, (pk_ref, st_ref) in enumerate(((pk1_ref, s1t_ref), (pk2_ref, s2t_ref))):
            c0 = (2 * h + half) * P_HALF
            st_ref[h * P_NKEYS:(h + 1) * P_NKEYS, :] = _dot_nt(pk_ref[h], qp[:, c0:c0 + P_HALF].astype(BF16))


def _merge(x2, oa, ob, gate, w_pa, w_pb, w_out, ln_g, ln_b, w_pq, pk1, pk2, alpha, tm):
    n, d = x2.shape
    hk = P_HEADS * P_NKEYS
    row = lambda w: pl.BlockSpec((tm, w), lambda i: (i, 0))
    col = pl.BlockSpec((hk, tm), lambda i: (0, i))
    const = lambda a: pl.BlockSpec(a.shape, lambda i: (0,) * a.ndim)
    ws = [w_pa.astype(BF16), w_pb.astype(BF16), w_out.astype(BF16), ln_g.reshape(1, d), ln_b.reshape(1, d),
          w_pq.astype(BF16), pk1.astype(BF16), pk2.astype(BF16)]
    return pl.pallas_call(
        functools.partial(_merge_kernel, alpha=alpha, d=d),
        grid=(n // tm,),
        in_specs=[row(d), row(W_AQ), row(W_G), row(2 * d)] + [const(w) for w in ws],
        out_specs=[row(d), pl.BlockSpec((d, tm), lambda i: (0, i)), col, col],
        out_shape=[jax.ShapeDtypeStruct((n, d), F32), jax.ShapeDtypeStruct((d, n), BF16),
                   jax.ShapeDtypeStruct((hk, n), F32), jax.ShapeDtypeStruct((hk, n), F32)],
        compiler_params=_params(("parallel",)),
        name="merge",
    )(x2, oa, ob, gate, *ws)


def _top_desc(s, count, want_rank=False):
    tops = []
    rank = jnp.full(s.shape, float(count), F32) if want_rank else None
    for r in range(count):
        m = jnp.max(_fold_rows(s, jnp.maximum), axis=0, keepdims=True)
        tops.append(m)
        hit = s == m
        if want_rank:
            rank = jnp.where(hit, float(r), rank)
        if r + 1 < count:
            s = jnp.where(hit, -jnp.inf, s)
    return tops, rank


def _select_kernel(s1t_ref, s2t_ref, cnt_ref, e1_ref, rank_ref, e2_ref):
    nk = P_NKEYS

    def head(h, _):
        r0 = pl.multiple_of(h * nk, nk)
        rows = pl.ds(r0, nk)
        s1 = s1t_ref[rows, :]
        s2 = s2t_ref[rows, :]
        v1, _ = _top_desc(s1, P_TOPK)
        v2, rank = _top_desc(s2, P_TOPK, want_rank=True)
        pairs = [(a, b) for a in range(P_TOPK) for b in range(P_TOPK // (a + 1))]
        fill = [jnp.full_like(v1[0], -jnp.inf)] * (-len(pairs) % SUBLANES)
        cand = jnp.concatenate([v1[a] + v2[b] for a, b in pairs] + fill, axis=0)
        work, seen = cand, jnp.zeros_like(v1[0])
        tau = jnp.full_like(v1[0], -jnp.inf)
        for _ in range(P_TOPK):
            m = jnp.max(work, axis=0, keepdims=True)
            hit = work == m
            seen = seen + jnp.sum(jnp.where(hit, 1.0, 0.0), axis=0, keepdims=True)
            tau = jnp.maximum(tau, jnp.where(seen >= float(P_TOPK), m, -jnp.inf))
            work = jnp.where(hit, -jnp.inf, work)
        cmax = v1[0] + v2[0]
        zsum = jnp.sum(jnp.where(cand >= tau, jnp.exp(cand - cmax), 0.0), axis=0, keepdims=True)
        v2all = jnp.concatenate(v2, axis=0)
        cnt = jnp.zeros(s1.shape, F32)
        for a in range(P_TOPK):
            cnt_a = jnp.sum(jnp.where(v1[a] + v2all >= tau, 1.0, 0.0), axis=0, keepdims=True)
            cnt = jnp.where(s1 == v1[a], cnt_a, cnt)
        e1 = jnp.exp(s1 - v1[0]) / zsum * 0.5
        for st in range(nk // I1_PER_STEP):
            dst = pl.ds(pl.multiple_of(st * P_HEADS * I1_PER_STEP + h * I1_PER_STEP, I1_PER_STEP), I1_PER_STEP)
            src = slice(st * I1_PER_STEP, (st + 1) * I1_PER_STEP)
            cnt_ref[dst, :] = cnt[src, :]
            e1_ref[dst, :] = e1[src, :]
        rank_ref[rows, :] = rank
        e2_ref[rows, :] = jnp.exp(s2 - v2[0])
        return 0

    lax.fori_loop(0, P_HEADS, head, 0)


def _select(s1t, s2t, tn):
    hk, n = s1t.shape
    col = pl.BlockSpec((hk, tn), lambda i: (0, i))
    return pl.pallas_call(
        _select_kernel,
        grid=(n // tn,),
        in_specs=[col, col],
        out_specs=[col, col, col, col],
        out_shape=[jax.ShapeDtypeStruct((hk, n), F32)] * 4,
        compiler_params=_params(("parallel",)),
        name="select",
    )(s1t, s2t)


I1_PER_STEP = 8
E_PER_STEP = I1_PER_STEP * P_NKEYS
I2_BLOCK = 16


def _gelu_x2(x):
    return x * (1.0 + lax.erf(x * (2.0 ** -0.5)))


def _peer_steps(t, n_steps, ng):
    item = lambda d: jnp.clip(t - d, 0, n_steps - 1)
    return [(item(d) // ng, item(d) % ng) for d in range(3)]


def _peer_kernel(hb_ref, pu_ref, pvt_ref, cnt_ref, e1_ref, rank_ref, e2_ref, h1_ref, g2_ref, b2_ref,
                 y_ref, a_ref, gw_ref, acc_ref, rank_s, e2_s, *, tn, alpha, n_steps, ng):
    t = pl.program_id(0)
    (_, _), (_, g), (_, g_out) = _peer_steps(t, n_steps, ng)
    slot = t % 2
    prev = 1 - slot

    @pl.when(t == 0)
    def _():
        a_ref[...] = jnp.zeros(a_ref.shape, F32)
        gw_ref[...] = jnp.zeros(gw_ref.shape, BF16)

    @pl.when(g == 0)
    def _():
        rank_s[...] = rank_ref[...].astype(BF16)
        e2_s[...] = e2_ref[...].astype(BF16)

    @pl.when(g_out == 0)
    def _():
        acc_ref[...] = jnp.zeros(acc_ref.shape, F32)

    a_ref[slot] = _dot(pu_ref[...], hb_ref[...])

    for lt in range(tn // LANES):
        ls = slice(lt * LANES, (lt + 1) * LANES)
        for j in range(I1_PER_STEP):
            bcast = lambda ref, h: jnp.broadcast_to(
                ref[h * I1_PER_STEP + j:h * I1_PER_STEP + j + 1, ls], (I2_BLOCK, LANES)).astype(BF16)
            cnt = [bcast(cnt_ref, h) for h in range(P_HEADS)]
            e1 = [bcast(e1_ref, h) for h in range(P_HEADS)]
            for i2b in range(P_NKEYS // I2_BLOCK):
                w = jnp.zeros((I2_BLOCK, LANES), BF16)
                for h in range(P_HEADS):
                    rows = slice(h * P_NKEYS + i2b * I2_BLOCK, h * P_NKEYS + (i2b + 1) * I2_BLOCK)
                    w = w + jnp.where(rank_s[rows, ls] < cnt[h], e2_s[rows, ls] * e1[h], jnp.zeros((), BF16))
                arow = slice(j * P_NKEYS + i2b * I2_BLOCK, j * P_NKEYS + (i2b + 1) * I2_BLOCK)
                gw_ref[prev, arow, ls] = w * _gelu_x2(a_ref[prev, arow, ls]).astype(BF16)

    acc_ref[...] += _dot(pvt_ref[0], gw_ref[slot])

    @pl.when(jnp.logical_and(g_out == ng - 1, t >= 2))
    def _():
        y_ref[...] = _layer_norm(alpha * h1_ref[...] + acc_ref[...].T, g2_ref[...], b2_ref[...])


def _peer(h1, h1b, cnt, e1, rank, e2, pu, pv, ln_g, ln_b, alpha, tn):
    n, d = h1.shape
    hk = P_HEADS * P_NKEYS
    ng = pu.shape[0] // E_PER_STEP
    n_steps = (n // tn) * ng
    stage = lambda k, f: (lambda t: f(*_peer_steps(t, n_steps, ng)[k]))
    const = pl.BlockSpec((1, d), lambda t: (0, 0))
    routing = pl.BlockSpec((hk, tn), stage(1, lambda i, g: (0, i)))
    step_rows = pl.BlockSpec((P_HEADS * I1_PER_STEP, tn), stage(1, lambda i, g: (g, i)))
    pvt = pv.astype(BF16).reshape(ng, E_PER_STEP, d).transpose(0, 2, 1)
    return pl.pallas_call(
        functools.partial(_peer_kernel, tn=tn, alpha=alpha, n_steps=n_steps, ng=ng),
        grid=(n_steps + 2,),
        in_specs=[pl.BlockSpec((d, tn), stage(0, lambda i, g: (0, i))),
                  pl.BlockSpec((E_PER_STEP, d), stage(0, lambda i, g: (g, 0))),
                  pl.BlockSpec((1, d, E_PER_STEP), stage(2, lambda i, g: (g, 0, 0))),
                  step_rows, step_rows, routing, routing,
                  pl.BlockSpec((tn, d), stage(2, lambda i, g: (i, 0))), const, const],
        out_specs=pl.BlockSpec((tn, d), stage(2, lambda i, g: (i, 0))),
        out_shape=jax.ShapeDtypeStruct((n, d), F32),
        scratch_shapes=[pltpu.VMEM((2, E_PER_STEP, tn), F32), pltpu.VMEM((2, E_PER_STEP, tn), BF16),
                        pltpu.VMEM((d, tn), F32), pltpu.VMEM((hk, tn), BF16), pltpu.VMEM((hk, tn), BF16)],
        compiler_params=_params(("arbitrary",)),
        name="peer",
    )(h1b, pu.astype(BF16), pvt, cnt, e1, rank, e2, h1, ln_g.reshape(1, d), ln_b.reshape(1, d))


def _pick_tile(n, pref):
    t = min(n, pref)
    assert n % t == 0
    return t


def _layer(x, pos, limits, past, s0, w, *, chunk, alpha):
    b, t, d = x.shape
    n = b * t
    x2 = x.reshape(n, d)
    tm = _pick_tile(n, TOKEN_TILE)
    (aq, k32, v32, kb, vb, iq, ikw, ikb, gq, gk, gv, glog, gr, gate) = _proj(
        x2, pos, w["w_in"], w["w_fa"], w["b_fa"], tm)

    kb3, vb3, ikb3 = kb.reshape(b, t, LANES), vb.reshape(b, t, LANES), ikb.reshape(b, t, LANES)
    if past is not None:
        ck, cv, cik = past
        p = ck.shape[1]
        kb3 = jnp.concatenate([ck.reshape(b, p, LANES).astype(BF16), kb3], axis=1)
        vb3 = jnp.concatenate([cv.reshape(b, p, LANES).astype(BF16), vb3], axis=1)
        ikb3 = jnp.concatenate([cik.astype(BF16), ikb3[:, :, :IDX_DIM]], axis=1)
    topk = min(IDX_TOPK, kb3.shape[1] // 4)
    o_a = _attention(aq, iq, ikw, limits, kb3, vb3, ikb3, b, topk)

    o_b, s_fin = _gla(gq, gk, gv, glog, gr, w["g_gla_norm"], s0, b, chunk)

    h1, h1b, s1t, s2t = _merge(x2, o_a, o_b, gate, w["w_pa"], w["w_pb"], w["w_out"],
                               w["ln1_g"], w["ln1_b"], w["w_pq"], w["pk1"], w["pk2"], alpha, tm)
    cnt, e1, rank, e2 = _select(s1t, s2t, tm)
    y = _peer(h1, h1b, cnt, e1, rank, e2, w["pu"], w["pv"], w["ln2_g"], w["ln2_b"], alpha,
              _pick_tile(n, PEER_TOKEN_TILE))

    k_out = k32.reshape(b, t, A_KV_HEADS, A_HEAD_DIM)
    v_out = v32.reshape(b, t, A_KV_HEADS, A_HEAD_DIM)
    ik_out = ikw[:, :IDX_DIM].reshape(b, t, IDX_DIM)
    return y.reshape(b, t, d), k_out, v_out, ik_out, s_fin


def kernel(x_prompt, x_sample, cache_k, cache_v, cache_idx_k, state_gla, w_in, w_fa, b_fa, g_gla_norm,
           w_pa, w_pb, w_out, ln1_g, ln1_b, w_pq, pk1, pk2, pu, pv, ln2_g, ln2_b):
    depth = w_in.shape[0]
    alpha = (2.0 * depth) ** 0.25
    bp, tp, _ = x_prompt.shape
    bs, ts, _ = x_sample.shape
    past_len = cache_k.shape[2]
    pos_p = jnp.arange(tp)
    pos_s = past_len + jnp.arange(ts)
    lim_p = (np.arange(tp) // CHUNK + 1) * CHUNK
    lim_s = np.full((ts,), past_len + ts)
    names = ("w_in", "w_fa", "b_fa", "g_gla_norm", "w_pa", "w_pb", "w_out", "ln1_g", "ln1_b",
             "w_pq", "pk1", "pk2", "pu", "pv", "ln2_g", "ln2_b")
    stacked = (w_in, w_fa, b_fa, g_gla_norm, w_pa, w_pb, w_out, ln1_g, ln1_b, w_pq, pk1, pk2, pu, pv, ln2_g, ln2_b)
    hp, hs = x_prompt, x_sample
    outs_p, outs_s = [], []
    for l in range(depth):
        w = {nm: a[l] for nm, a in zip(names, stacked)}
        s0 = jnp.zeros((bp, G_HEADS, G_KEY_DIM, G_VAL_DIM), F32)
        hp, *rest = _layer(hp, pos_p, lim_p, None, s0, w, chunk=CHUNK, alpha=alpha)
        outs_p.append(rest)
        hs, *rest = _layer(hs, pos_s, lim_s, (cache_k[l], cache_v[l], cache_idx_k[l]), state_gla[l], w,
                           chunk=ts, alpha=alpha)
        outs_s.append(rest)
    stack = lambda outs, i: jnp.stack([o[i] for o in outs])
    return (hp, hs, stack(outs_p, 0), stack(outs_p, 1), stack(outs_p, 2), stack(outs_p, 3),
            stack(outs_s, 0), stack(outs_s, 1), stack(outs_s, 2), stack(outs_s, 3))
```

```python
import functools
import math

import numpy as np
import jax
import jax.numpy as jnp
from jax import lax
from jax.experimental import pallas as pl
from jax.experimental.pallas import tpu as pltpu

F32 = jnp.float32
BF16 = jnp.bfloat16
I32 = jnp.int32

LANES = 128
SUBLANES = 8
VMEM_LIMIT = 56 << 20

CHUNK = 64
A_HEADS = 8
A_KV_HEADS = 2
A_HEAD_DIM = 64
A_GROUP = A_HEADS // A_KV_HEADS
IDX_HEADS = 4
IDX_DIM = 64
IDX_TOPK = 256
ROPE_THETA = 10000.0
G_HEADS = 4
G_KEY_DIM = 128
G_VAL_DIM = 128
G_LOWRANK = 16
G_TAU = 16.0
P_HEADS = 8
P_NKEYS = 128
P_HALF = 128
P_TOPK = 16
LN_EPS = 1e-5

W_AQ = A_HEADS * A_HEAD_DIM
W_AK = A_KV_HEADS * A_HEAD_DIM
W_IQ = IDX_HEADS * IDX_DIM
W_G = G_HEADS * G_KEY_DIM
IN_SIZES = (W_AQ, W_AK, W_AK, W_IQ, IDX_DIM, IDX_HEADS, W_G, W_G, W_G, G_LOWRANK, W_G, None)

TOKEN_TILE = 256
PEER_TOKEN_TILE = 512
Q_BLOCK = 128
KEY_TILE = 512
GLA_SEQS = 4

INT_MIN = -(2 ** 31)
NEG_INF_KEY = -2139095041
NEG_BIG = -1e30
LOG2_E = 1.4426950408889634


def _dot(a, b):
    return jnp.dot(a, b, preferred_element_type=F32)


def _dot_nt(a, b):
    return lax.dot_general(a, b, (((1,), (1,)), ((), ())), preferred_element_type=F32)


def _dot_tn(a, b):
    return lax.dot_general(a, b, (((0,), (0,)), ((), ())), preferred_element_type=F32)


def _sort_key(x):
    bits = pltpu.bitcast(x, I32)
    key = jnp.where(bits < 0, bits ^ 0x7FFFFFFF, bits)
    return jnp.where(key == -1, 0, key)


def _fold_rows(x, op):
    x = x.reshape(x.shape[0] // SUBLANES, SUBLANES, x.shape[1])
    while x.shape[0] > 1:
        half = x.shape[0] // 2
        folded = op(x[:half], x[half:2 * half])
        x = folded if x.shape[0] == 2 * half else jnp.concatenate([folded, x[2 * half:]], axis=0)
    return x[0]


def _params(sem):
    return pltpu.CompilerParams(dimension_semantics=sem, vmem_limit_bytes=VMEM_LIMIT)


_PG_AQ, _PG_K, _PG_V, _PG_IQ, _PG_IKW, _PG_GQ, _PG_GK, _PG_GV, _PG_GF, _PG_GR, _PG_GATE = range(11)


def _pack_layout(d_model):
    widths = [W_AQ, W_AK, W_AK, W_IQ, LANES, W_G, W_G, W_G, LANES, W_G, 2 * d_model]
    offs = np.concatenate([[0], np.cumsum(widths)]).tolist()
    return widths, offs


def _pack_w_in(w_in):
    d = w_in.shape[0]
    sizes = list(IN_SIZES[:-1]) + [2 * d]
    cuts = np.cumsum(sizes)[:-1].tolist()
    aq, ak, av, iq, ik, iw, gq, gk, gv, gf, gr, gate = jnp.split(w_in, cuts, axis=1)
    z = lambda n: jnp.zeros((d, n), w_in.dtype)
    ikw = jnp.concatenate([ik, iw, z(LANES - IDX_DIM - IDX_HEADS)], axis=1)
    gfp = jnp.concatenate([gf, z(LANES - G_LOWRANK)], axis=1)
    return jnp.concatenate([aq, ak, av, iq, ikw, gq, gk, gv, gfp, gr, gate], axis=1).astype(BF16)


def _rope_tables(pos):
    half = A_HEAD_DIM // 2
    inv = ROPE_THETA ** (-jnp.arange(half, dtype=F32) / half)
    ang = pos.astype(F32)[:, None] * inv[None, :]
    c, s = jnp.cos(ang), jnp.sin(ang)
    return jnp.concatenate([c, c, c, c], -1), jnp.concatenate([-s, s, -s, s], -1)


def _proj_kernel(x_ref, w_ref, wfa_ref, bfa_ref, cos_ref, sin_ref,
                 aq_ref, k_ref, v_ref, kb_ref, vb_ref, iq_ref, ikw_ref, ikb_ref,
                 gq_ref, gk_ref, gv_ref, glog_ref, gr_ref, gate_ref, *, offs, widths):
    xb = x_ref[...].astype(BF16)
    cos = cos_ref[...]
    sin = sin_ref[...]
    lane = lax.broadcasted_iota(I32, cos.shape, 1)
    first_half = (lane & (A_HEAD_DIM // 2)) == 0

    def proj(g):
        return _dot(xb, w_ref[:, offs[g]:offs[g] + widths[g]])

    def rope_slab(y):
        fwd = pltpu.roll(y, LANES - A_HEAD_DIM // 2, 1)
        bwd = pltpu.roll(y, A_HEAD_DIM // 2, 1)
        return y * cos + jnp.where(first_half, fwd, bwd) * sin

    def rope(y):
        return [rope_slab(y[:, s * LANES:(s + 1) * LANES]) for s in range(y.shape[1] // LANES)]

    for s, slab in enumerate(rope(proj(_PG_AQ))):
        aq_ref[:, s * LANES:(s + 1) * LANES] = (slab * (A_HEAD_DIM ** -0.5 * LOG2_E)).astype(BF16)
    k = rope(proj(_PG_K))[0]
    k_ref[...] = k
    kb_ref[...] = k.astype(BF16)
    v = proj(_PG_V)
    v_ref[...] = v
    vb_ref[...] = v.astype(BF16)
    for s, slab in enumerate(rope(proj(_PG_IQ))):
        iq_ref[:, s * LANES:(s + 1) * LANES] = (slab * (IDX_DIM ** -0.5)).astype(BF16)
    raw = proj(_PG_IKW)
    ikw = jnp.where(lane < IDX_DIM, rope_slab(raw), raw * (IDX_HEADS ** -0.5))
    ikw_ref[...] = ikw
    ikb_ref[...] = ikw.astype(BF16)
    gq_ref[...] = proj(_PG_GQ) * (G_KEY_DIM ** -0.5)
    gk_ref[...] = proj(_PG_GK)
    gv_ref[...] = proj(_PG_GV)
    z = _dot(proj(_PG_GF).astype(BF16), wfa_ref[...]) + bfa_ref[...]
    glog_ref[...] = (jnp.minimum(z, 0.0) - jnp.log1p(jnp.exp(-jnp.abs(z)))) * (1.0 / G_TAU)
    gr_ref[...] = proj(_PG_GR)
    gate_ref[...] = proj(_PG_GATE)


def _proj(x2, pos, w_in, w_fa, b_fa, tm):
    n, d = x2.shape
    t = pos.shape[0]
    widths, offs = _pack_layout(d)
    wp = _pack_w_in(w_in)
    wfa = jnp.concatenate([w_fa, jnp.zeros((LANES - G_LOWRANK, W_G), w_fa.dtype)], 0).astype(BF16)
    cos, sin = _rope_tables(pos)
    if tm > t:
        cos, sin = jnp.tile(cos, (tm // t, 1)), jnp.tile(sin, (tm // t, 1))
    nper = cos.shape[0] // tm
    row = lambda w: pl.BlockSpec((tm, w), lambda i: (i, 0))
    const = lambda a: pl.BlockSpec(a.shape, lambda i: (0, 0))
    tab = pl.BlockSpec((tm, LANES), lambda i: (i % nper, 0))
    outs = [(W_AQ, BF16), (LANES, F32), (LANES, F32), (LANES, BF16), (LANES, BF16), (W_IQ, BF16),
            (LANES, F32), (LANES, BF16), (W_G, F32), (W_G, F32), (W_G, F32), (W_G, F32), (W_G, F32),
            (2 * d, F32)]
    bfa = b_fa.reshape(1, W_G)
    return pl.pallas_call(
        functools.partial(_proj_kernel, offs=offs, widths=widths),
        grid=(n // tm,),
        in_specs=[row(d), const(wp), const(wfa), const(bfa), tab, tab],
        out_specs=[row(w) for w, _ in outs],
        out_shape=[jax.ShapeDtypeStruct((n, w), dt) for w, dt in outs],
        compiler_params=_params(("parallel",)),
        name="proj",
    )(x2, wp, wfa, bfa, cos, sin)


def _attn_kernel(aq_ref, iq_ref, ikw_ref, lim_ref, kb_ref, vt_ref, ikb_ref, o_ref,
                 keys_ref, qs_ref, iqs_ref, acc_ref, *, kt_w, nkt, topk, idx_bits):
    qb = Q_BLOCK

    aq_t = aq_ref[...].astype(F32).T
    for h in range(A_HEADS):
        qs_ref[h // A_GROUP, :, (h % A_GROUP) * qb:(h % A_GROUP + 1) * qb] = \
            aq_t[h * A_HEAD_DIM:(h + 1) * A_HEAD_DIM, :].astype(BF16)
    iq_t = iq_ref[...].astype(F32).T
    for h in range(IDX_HEADS):
        iqs_ref[:, h * qb:(h + 1) * qb] = iq_t[h * IDX_DIM:(h + 1) * IDX_DIM, :].astype(BF16)

    ikw_t = ikw_ref[...].T
    iw_rows = [ikw_t[IDX_DIM + h:IDX_DIM + h + 1, :] for h in range(IDX_HEADS)]
    lim = lim_ref[0, 0:1, :]
    sub = lax.broadcasted_iota(I32, (kt_w, qb), 0)

    def score_tile(kt, _):
        base = pl.multiple_of(kt * kt_w, kt_w)
        ik_t = ikb_ref[0, pl.ds(base, kt_w), :][:, :IDX_DIM]
        s = jnp.maximum(_dot(ik_t, iqs_ref[...]), 0.0)
        score = jnp.zeros((kt_w, qb), F32)
        for h in range(IDX_HEADS):
            score = score + s[:, h * qb:(h + 1) * qb] * iw_rows[h]
        keys_ref[pl.ds(base, kt_w), :] = jnp.where(sub + base < lim, _sort_key(score), NEG_INF_KEY)
        return 0

    lax.fori_loop(0, nkt, score_tile, 0, unroll=min(nkt, 2))

    def count(pred):
        acc = jnp.zeros((SUBLANES, qb), F32)
        for kt in range(nkt):
            acc = acc + _fold_rows(pred(keys_ref[kt * kt_w:(kt + 1) * kt_w, :], sub + kt * kt_w), jnp.add)
        return jnp.sum(acc, axis=0, keepdims=True)

    def count_ge(t_row):
        return count(lambda kk, idx: jnp.where(kk >= t_row, 1.0, 0.0))

    kf = float(topk)
    thr = jnp.where(count_ge(jnp.zeros((1, qb), I32)) >= kf, 0, INT_MIN).astype(I32)

    def thr_bit(i, t):
        cand = t + jnp.left_shift(jnp.int32(1), 30 - i)
        return jnp.where(count_ge(cand) >= kf, cand, t)

    thr = lax.fori_loop(0, 31, thr_bit, thr)
    n_gt = count_ge(thr + 1)
    n_eq = count_ge(thr) - n_gt
    need = kf - n_gt
    finite = thr > NEG_INF_KEY
    excess = jnp.where(finite, jnp.where(n_eq > need, 1.0, 0.0), 0.0)

    def count_eq_below(j_row):
        return count(lambda kk, idx: jnp.where(kk == thr, jnp.where(idx < j_row, 1.0, 0.0), 0.0))

    def resolve_ties():
        def bit(i, jc):
            cand = jc + jnp.left_shift(jnp.int32(1), idx_bits - 1 - i)
            return jnp.where(count_eq_below(cand) <= need - 1.0, cand, jc)
        jc = lax.fori_loop(0, idx_bits, bit, jnp.zeros((1, qb), I32))
        return jnp.where(finite, jc, -1)

    cut = lax.cond(jnp.max(excess) > 0.0, resolve_ties,
                   lambda: jnp.where(finite, 2 ** 30, -1).astype(I32))

    acc_ref[...] = jnp.zeros(acc_ref.shape, F32)
    gq = A_GROUP * qb

    def attend(kt, carry):
        base = pl.multiple_of(kt * kt_w, kt_w)
        k_t = kb_ref[0, pl.ds(base, kt_w), :]
        kk = keys_ref[pl.ds(base, kt_w), :]
        tie = jnp.where(sub + base <= cut, 0.0, NEG_BIG)
        bias = jnp.where(kk > thr, 0.0, jnp.where(kk == thr, tie, NEG_BIG))
        bias = jnp.concatenate([bias] * A_GROUP, axis=1)
        out = []
        for n in range(A_KV_HEADS):
            m_old, l_old = carry[2 * n], carry[2 * n + 1]
            logits = bias + _dot(k_t[:, n * A_HEAD_DIM:(n + 1) * A_HEAD_DIM], qs_ref[n])
            m_new = jnp.maximum(m_old, jnp.max(_fold_rows(logits, jnp.maximum), axis=0, keepdims=True))
            alpha = jnp.exp2(m_old - m_new)
            p = jnp.exp2(logits - m_new)
            l_new = alpha * l_old + jnp.sum(_fold_rows(p, jnp.add), axis=0, keepdims=True)
            v_t = vt_ref[0, n * A_HEAD_DIM:(n + 1) * A_HEAD_DIM, pl.ds(base, kt_w)]
            acc_ref[n] = alpha * acc_ref[n] + _dot(v_t, p.astype(BF16))
            out += [m_new, l_new]
        return tuple(out)

    init = (jnp.full((1, gq), NEG_BIG, F32), jnp.zeros((1, gq), F32)) * A_KV_HEADS
    fin = lax.fori_loop(0, nkt, attend, init, unroll=min(nkt, 2))
    l_row = [fin[2 * n + 1] for n in range(A_KV_HEADS)]

    o_t = jnp.concatenate([acc_ref[n] / l_row[n] for n in range(A_KV_HEADS)], axis=0)
    for n in range(A_KV_HEADS):
        for g in range(A_GROUP):
            h = n * A_GROUP + g
            blk = o_t[n * A_HEAD_DIM:(n + 1) * A_HEAD_DIM, g * qb:(g + 1) * qb]
            o_ref[:, h * A_HEAD_DIM:(h + 1) * A_HEAD_DIM] = blk.T.astype(BF16)


def _attention(aq, iq, ikw, limits, kb, vb, ikb, b, topk):
    qb, kt_w = Q_BLOCK, KEY_TILE
    tq = aq.shape[0] // b
    tq_pad = -(-tq // qb) * qb
    limits = np.asarray(limits)
    if tq_pad != tq:
        padq = lambda a: jnp.pad(a.reshape(b, tq, -1), ((0, 0), (0, tq_pad - tq), (0, 0))).reshape(b * tq_pad, -1)
        aq, iq, ikw = padq(aq), padq(iq), padq(ikw)
        limits = np.concatenate([limits, np.full((tq_pad - tq,), limits[-1])])
    l_all = kb.shape[1]
    l_pad = -(-l_all // kt_w) * kt_w
    if l_pad != l_all:
        padl = lambda a: jnp.pad(a, ((0, 0), (0, l_pad - l_all), (0, 0)))
        kb, vb, ikb = padl(kb), padl(vb), padl(ikb)
    vt = jnp.swapaxes(vb, 1, 2)
    nq = tq_pad // qb
    lim_blk = limits.reshape(nq, qb)
    nkt = np.minimum(-(-lim_blk.max(axis=1) // kt_w), l_pad // kt_w)
    lim = jnp.asarray(np.broadcast_to(lim_blk[:, None, :], (nq, SUBLANES, qb)).astype(np.int32))
    idx_bits = max(1, int(math.ceil(math.log2(l_pad))))
    keys = lambda a: pl.BlockSpec((1,) + a.shape[1:], lambda bi, j: (bi, 0, 0))
    gq = A_GROUP * qb
    runs, j0 = [], 0
    for j in range(1, nq + 1):
        if j == nq or nkt[j] != nkt[j0]:
            runs.append((j0, j - j0, int(nkt[j0])))
            j0 = j
    outs = []
    for j0, nj, n_tiles in runs:
        qrow = lambda w, j0=j0: pl.BlockSpec((qb, w), lambda bi, j: (bi * nq + j0 + j, 0))
        outs.append(pl.pallas_call(
            functools.partial(_attn_kernel, kt_w=kt_w, nkt=n_tiles, topk=topk, idx_bits=idx_bits),
            grid=(b, nj),
            in_specs=[qrow(W_AQ), qrow(W_IQ), qrow(LANES),
                      pl.BlockSpec((1, SUBLANES, qb), lambda bi, j, j0=j0: (j0 + j, 0, 0)),
                      keys(kb), keys(vt), keys(ikb)],
            out_specs=pl.BlockSpec((qb, W_AQ), lambda bi, j, nj=nj: (bi * nj + j, 0)),
            out_shape=jax.ShapeDtypeStruct((b * nj * qb, W_AQ), BF16),
            scratch_shapes=[pltpu.VMEM((n_tiles * kt_w, qb), I32),
                            pltpu.VMEM((A_KV_HEADS, A_HEAD_DIM, gq), BF16),
                            pltpu.VMEM((IDX_DIM, IDX_HEADS * qb), BF16),
                            pltpu.VMEM((A_KV_HEADS, A_HEAD_DIM, gq), F32)],
            compiler_params=_params(("parallel", "arbitrary")),
            name="attn",
        )(aq, iq, ikw, lim, kb, vt, ikb).reshape(b, nj * qb, W_AQ))
    o = outs[0] if len(outs) == 1 else jnp.concatenate(outs, axis=1)
    return o[:, :tq].reshape(b * tq, W_AQ)


def _gla_constants(c):
    nlev = int(math.log2(c))
    t = np.arange(c)
    mats = [(t[None, :] <= t[:, None])]
    masks = [np.eye(c, dtype=bool)]
    for lev in range(nlev):
        m = c >> (lev + 1)
        ref_row = (t // (2 * m)) * 2 * m + m
        mats.append(t[None, :] <= ref_row[:, None])
        upper = (t & m) != 0
        same = (t[:, None] // (2 * m)) == (t[None, :] // (2 * m))
        masks.append(same & upper[:, None] & ~upper[None, :])
    return (jnp.asarray(np.concatenate(mats, 0).astype(np.float32), BF16),
            jnp.asarray(np.stack(masks).astype(np.float32)), nlev)


def _gla_kernel(q_ref, k_ref, v_ref, g_ref, gr_ref, gn_ref, mst_ref, msk_ref, s0_ref,
                ob_ref, sfin_ref, st_ref, *, c, nlev, nb):
    i = pl.program_id(1)
    hk = G_KEY_DIM

    @pl.when(i == 0)
    def _():
        for s in range(nb):
            for h in range(G_HEADS):
                st_ref[s, h] = s0_ref[s, h].T

    mst = mst_ref[...]
    gn = gn_ref[...]
    hs = lambda a, h: a[:, h * hk:(h + 1) * hk]
    for s in range(nb):
        g = g_ref[s]
        g_hi = g.astype(BF16)
        r1 = g - g_hi.astype(F32)
        g_mid = r1.astype(BF16)
        g_lo = (r1 - g_mid.astype(F32)).astype(BF16)
        bs = _dot(mst, g_hi) + _dot(mst, g_mid) + _dot(mst, g_lo)
        b = bs[0:c]
        q = q_ref[s]
        k = k_ref[s]
        vb = v_ref[s].astype(BF16)
        row = lax.broadcasted_iota(I32, q.shape, 0)

        qb = q.astype(BF16)
        kb = k.astype(BF16)
        attn = [_dot_nt(hs(qb, h), hs(kb, h)) * msk_ref[0] for h in range(G_HEADS)]
        for lev in range(nlev):
            m = c >> (lev + 1)
            upper = (row & m) != 0
            d = b - bs[(lev + 1) * c:(lev + 2) * c]
            e = jnp.exp(jnp.where(upper, d, -d))
            qt = jnp.where(upper, q * e, 0.0).astype(BF16)
            kt = jnp.where(upper, 0.0, k * e).astype(BF16)
            mk = msk_ref[lev + 1]
            for h in range(G_HEADS):
                attn[h] = attn[h] + _dot_nt(hs(qt, h), hs(kt, h)) * mk

        qe = (q * jnp.exp(b)).astype(BF16)
        b_last = b[c - 1:c, :]
        khat = (k * jnp.exp(b_last - b)).astype(BF16)
        dec = jnp.exp(b_last)
        gr = gr_ref[s]
        for h in range(G_HEADS):
            st = st_ref[s, h]
            o = _dot_nt(hs(qe, h), st.astype(BF16)) + _dot(attn[h].astype(BF16), hs(vb, h))
            st_ref[s, h] = st * hs(dec, h) + _dot_tn(hs(vb, h), hs(khat, h))
            ms = jnp.mean(o * o, axis=1, keepdims=True)
            grh = hs(gr, h)
            of = o * lax.rsqrt(ms + LN_EPS) * hs(gn, h) * (grh / (1.0 + jnp.exp(-grh)))
            ob_ref[s, :, h * hk:(h + 1) * hk] = of.astype(BF16)

    @pl.when(i == pl.num_programs(1) - 1)
    def _():
        for s in range(nb):
            for h in range(G_HEADS):
                sfin_ref[s, h] = st_ref[s, h].T


def _gla(gq, gk, gv, glog, gr, g_norm, s0, b, c):
    n = gq.shape[0]
    t = n // b
    nc = t // c
    nb = GLA_SEQS if b % GLA_SEQS == 0 else 1
    mst, msk, nlev = _gla_constants(c)
    seq = lambda a: a.reshape(b, t, W_G)
    row = pl.BlockSpec((nb, c, W_G), lambda bi, i: (bi, i, 0))
    const = lambda a: pl.BlockSpec(a.shape, lambda bi, i: (0,) * a.ndim)
    st_spec = pl.BlockSpec((nb, G_HEADS, G_KEY_DIM, G_VAL_DIM), lambda bi, i: (bi, 0, 0, 0))
    gn = g_norm.reshape(1, W_G)
    ob, s_fin = pl.pallas_call(
        functools.partial(_gla_kernel, c=c, nlev=nlev, nb=nb),
        grid=(b // nb, nc),
        in_specs=[row, row, row, row, row, const(gn), const(mst), const(msk), st_spec],
        out_specs=[row, st_spec],
        out_shape=[jax.ShapeDtypeStruct((b, t, W_G), BF16),
                   jax.ShapeDtypeStruct((b, G_HEADS, G_KEY_DIM, G_VAL_DIM), F32)],
        scratch_shapes=[pltpu.VMEM((nb, G_HEADS, G_VAL_DIM, G_KEY_DIM), F32)],
        compiler_params=_params(("parallel", "arbitrary")),
        name="gla",
    )(seq(gq), seq(gk), seq(gv), seq(glog), seq(gr), gn, mst, msk, s0)
    return ob.reshape(n, W_G), s_fin


def _layer_norm(z, g, b):
    mu = jnp.mean(z, axis=1, keepdims=True)
    zc = z - mu
    var = jnp.mean(zc * zc, axis=1, keepdims=True)
    return zc * lax.rsqrt(var + LN_EPS) * g + b


def _sigmoid(x):
    return 1.0 / (1.0 + jnp.exp(-x))


def _merge_kernel(x_ref, oa_ref, ob_ref, gate_ref, wpa_ref, wpb_ref, wout_ref, g1_ref, b1_ref,
                  wpq_ref, pk1_ref, pk2_ref, h1_ref, h1b_ref, s1t_ref, s2t_ref, *, alpha, d):
    ya = _dot(oa_ref[...], wpa_ref[...])
    yb = _dot(ob_ref[...], wpb_ref[...])
    m = _sigmoid(gate_ref[:, :d]) * ya + _sigmoid(gate_ref[:, d:]) * yb
    mix = _dot(m.astype(BF16), wout_ref[...])
    h1 = _layer_norm(alpha * x_ref[...] + mix, g1_ref[...], b1_ref[...])
    h1_ref[...] = h1
    h1b_ref[...] = h1.T.astype(BF16)
    qp = _dot(h1.astype(BF16), wpq_ref[...])
    for h in range(P_HEADS):
        for half, (pk_ref, st_ref) in enumerate(((pk1_ref, s1t_ref), (pk2_ref, s2t_ref))):
            c0 = (2 * h + half) * P_HALF
            st_ref[h * P_NKEYS:(h + 1) * P_NKEYS, :] = _dot_nt(pk_ref[h], qp[:, c0:c0 + P_HALF].astype(BF16))


def _merge(x2, oa, ob, gate, w_pa, w_pb, w_out, ln_g, ln_b, w_pq, pk1, pk2, alpha, tm):
    n, d = x2.shape
    hk = P_HEADS * P_NKEYS
    row = lambda w: pl.BlockSpec((tm, w), lambda i: (i, 0))
    col = pl.BlockSpec((hk, tm), lambda i: (0, i))
    const = lambda a: pl.BlockSpec(a.shape, lambda i: (0,) * a.ndim)
    ws = [w_pa.astype(BF16), w_pb.astype(BF16), w_out.astype(BF16), ln_g.reshape(1, d), ln_b.reshape(1, d),
          w_pq.astype(BF16), pk1.astype(BF16), pk2.astype(BF16)]
    return pl.pallas_call(
        functools.partial(_merge_kernel, alpha=alpha, d=d),
        grid=(n // tm,),
        in_specs=[row(d), row(W_AQ), row(W_G), row(2 * d)] + [const(w) for w in ws],
        out_specs=[row(d), pl.BlockSpec((d, tm), lambda i: (0, i)), col, col],
        out_shape=[jax.ShapeDtypeStruct((n, d), F32), jax.ShapeDtypeStruct((d, n), BF16),
                   jax.ShapeDtypeStruct((hk, n), F32), jax.ShapeDtypeStruct((hk, n), F32)],
        compiler_params=_params(("parallel",)),
        name="merge",
    )(x2, oa, ob, gate, *ws)


def _top_desc(s, count, want_rank=False):
    tops = []
    rank = jnp.full(s.shape, float(count), F32) if want_rank else None
    for r in range(count):
        m = jnp.max(_fold_rows(s, jnp.maximum), axis=0, keepdims=True)
        tops.append(m)
        hit = s == m
        if want_rank:
            rank = jnp.where(hit, float(r), rank)
        if r + 1 < count:
            s = jnp.where(hit, -jnp.inf, s)
    return tops, rank


def _select_kernel(s1t_ref, s2t_ref, cnt_ref, e1_ref, rank_ref, e2_ref):
    nk = P_NKEYS

    def head(h, _):
        r0 = pl.multiple_of(h * nk, nk)
        rows = pl.ds(r0, nk)
        s1 = s1t_ref[rows, :]
        s2 = s2t_ref[rows, :]
        v1, _ = _top_desc(s1, P_TOPK)
        v2, rank = _top_desc(s2, P_TOPK, want_rank=True)
        pairs = [(a, b) for a in range(P_TOPK) for b in range(P_TOPK // (a + 1))]
        fill = [jnp.full_like(v1[0], -jnp.inf)] * (-len(pairs) % SUBLANES)
        cand = jnp.concatenate([v1[a] + v2[b] for a, b in pairs] + fill, axis=0)
        work, seen = cand, jnp.zeros_like(v1[0])
        tau = jnp.full_like(v1[0], -jnp.inf)
        for _ in range(P_TOPK):
            m = jnp.max(work, axis=0, keepdims=True)
            hit = work == m
            seen = seen + jnp.sum(jnp.where(hit, 1.0, 0.0), axis=0, keepdims=True)
            tau = jnp.maximum(tau, jnp.where(seen >= float(P_TOPK), m, -jnp.inf))
            work = jnp.where(hit, -jnp.inf, work)
        cmax = v1[0] + v2[0]
        zsum = jnp.sum(jnp.where(cand >= tau, jnp.exp(cand - cmax), 0.0), axis=0, keepdims=True)
        v2all = jnp.concatenate(v2, axis=0)
        cnt = jnp.zeros(s1.shape, F32)
        for a in range(P_TOPK):
            cnt_a = jnp.sum(jnp.where(v1[a] + v2all >= tau, 1.0, 0.0), axis=0, keepdims=True)
            cnt = jnp.where(s1 == v1[a], cnt_a, cnt)
        e1 = jnp.exp(s1 - v1[0]) / zsum * 0.5
        for st in range(nk // I1_PER_STEP):
            dst = pl.ds(pl.multiple_of(st * P_HEADS * I1_PER_STEP + h * I1_PER_STEP, I1_PER_STEP), I1_PER_STEP)
            src = slice(st * I1_PER_STEP, (st + 1) * I1_PER_STEP)
            cnt_ref[dst, :] = cnt[src, :]
            e1_ref[dst, :] = e1[src, :]
        rank_ref[rows, :] = rank
        e2_ref[rows, :] = jnp.exp(s2 - v2[0])
        return 0

    lax.fori_loop(0, P_HEADS, head, 0)


def _select(s1t, s2t, tn):
    hk, n = s1t.shape
    col = pl.BlockSpec((hk, tn), lambda i: (0, i))
    return pl.pallas_call(
        _select_kernel,
        grid=(n // tn,),
        in_specs=[col, col],
        out_specs=[col, col, col, col],
        out_shape=[jax.ShapeDtypeStruct((hk, n), F32)] * 4,
        compiler_params=_params(("parallel",)),
        name="select",
    )(s1t, s2t)


I1_PER_STEP = 8
E_PER_STEP = I1_PER_STEP * P_NKEYS
I2_BLOCK = 16


def _gelu_x2(x):
    return x * (1.0 + lax.erf(x * (2.0 ** -0.5)))


def _peer_steps(t, n_steps, ng):
    item = lambda d: jnp.clip(t - d, 0, n_steps - 1)
    return [(item(d) // ng, item(d) % ng) for d in range(3)]


def _peer_kernel(hb_ref, pu_ref, pvt_ref, cnt_ref, e1_ref, rank_ref, e2_ref, h1_ref, g2_ref, b2_ref,
                 y_ref, a_ref, gw_ref, acc_ref, rank_s, e2_s, *, tn, alpha, n_steps, ng):
    t = pl.program_id(0)
    (_, _), (_, g), (_, g_out) = _peer_steps(t, n_steps, ng)
    slot = t % 2
    prev = 1 - slot

    @pl.when(t == 0)
    def _():
        a_ref[...] = jnp.zeros(a_ref.shape, F32)
        gw_ref[...] = jnp.zeros(gw_ref.shape, BF16)

    @pl.when(g == 0)
    def _():
        rank_s[...] = rank_ref[...].astype(BF16)
        e2_s[...] = e2_ref[...].astype(BF16)

    @pl.when(g_out == 0)
    def _():
        acc_ref[...] = jnp.zeros(acc_ref.shape, F32)

    a_ref[slot] = _dot(pu_ref[...], hb_ref[...])

    for lt in range(tn // LANES):
        ls = slice(lt * LANES, (lt + 1) * LANES)
        for j in range(I1_PER_STEP):
            bcast = lambda ref, h: jnp.broadcast_to(
                ref[h * I1_PER_STEP + j:h * I1_PER_STEP + j + 1, ls], (I2_BLOCK, LANES)).astype(BF16)
            cnt = [bcast(cnt_ref, h) for h in range(P_HEADS)]
            e1 = [bcast(e1_ref, h) for h in range(P_HEADS)]
            for i2b in range(P_NKEYS // I2_BLOCK):
                w = jnp.zeros((I2_BLOCK, LANES), BF16)
                for h in range(P_HEADS):
                    rows = slice(h * P_NKEYS + i2b * I2_BLOCK, h * P_NKEYS + (i2b + 1) * I2_BLOCK)
                    w = w + jnp.where(rank_s[rows, ls] < cnt[h], e2_s[rows, ls] * e1[h], jnp.zeros((), BF16))
                arow = slice(j * P_NKEYS + i2b * I2_BLOCK, j * P_NKEYS + (i2b + 1) * I2_BLOCK)
                gw_ref[prev, arow, ls] = w * _gelu_x2(a_ref[prev, arow, ls]).astype(BF16)

    acc_ref[...] += _dot(pvt_ref[0], gw_ref[slot])

    @pl.when(jnp.logical_and(g_out == ng - 1, t >= 2))
    def _():
        y_ref[...] = _layer_norm(alpha * h1_ref[...] + acc_ref[...].T, g2_ref[...], b2_ref[...])


def _peer(h1, h1b, cnt, e1, rank, e2, pu, pv, ln_g, ln_b, alpha, tn):
    n, d = h1.shape
    hk = P_HEADS * P_NKEYS
    ng = pu.shape[0] // E_PER_STEP
    n_steps = (n // tn) * ng
    stage = lambda k, f: (lambda t: f(*_peer_steps(t, n_steps, ng)[k]))
    const = pl.BlockSpec((1, d), lambda t: (0, 0))
    routing = pl.BlockSpec((hk, tn), stage(1, lambda i, g: (0, i)))
    step_rows = pl.BlockSpec((P_HEADS * I1_PER_STEP, tn), stage(1, lambda i, g: (g, i)))
    pvt = pv.astype(BF16).reshape(ng, E_PER_STEP, d).transpose(0, 2, 1)
    return pl.pallas_call(
        functools.partial(_peer_kernel, tn=tn, alpha=alpha, n_steps=n_steps, ng=ng),
        grid=(n_steps + 2,),
        in_specs=[pl.BlockSpec((d, tn), stage(0, lambda i, g: (0, i))),
                  pl.BlockSpec((E_PER_STEP, d), stage(0, lambda i, g: (g, 0))),
                  pl.BlockSpec((1, d, E_PER_STEP), stage(2, lambda i, g: (g, 0, 0))),
                  step_rows, step_rows, routing, routing,
                  pl.BlockSpec((tn, d), stage(2, lambda i, g: (i, 0))), const, const],
        out_specs=pl.BlockSpec((tn, d), stage(2, lambda i, g: (i, 0))),
        out_shape=jax.ShapeDtypeStruct((n, d), F32),
        scratch_shapes=[pltpu.VMEM((2, E_PER_STEP, tn), F32), pltpu.VMEM((2, E_PER_STEP, tn), BF16),
                        pltpu.VMEM((d, tn), F32), pltpu.VMEM((hk, tn), BF16), pltpu.VMEM((hk, tn), BF16)],
        compiler_params=_params(("arbitrary",)),
        name="peer",
    )(h1b, pu.astype(BF16), pvt, cnt, e1, rank, e2, h1, ln_g.reshape(1, d), ln_b.reshape(1, d))


def _pick_tile(n, pref):
    t = min(n, pref)
    assert n % t == 0
    return t


def _layer(x, pos, limits, past, s0, w, *, chunk, alpha):
    b, t, d = x.shape
    n = b * t
    x2 = x.reshape(n, d)
    tm = _pick_tile(n, TOKEN_TILE)
    (aq, k32, v32, kb, vb, iq, ikw, ikb, gq, gk, gv, glog, gr, gate) = _proj(
        x2, pos, w["w_in"], w["w_fa"], w["b_fa"], tm)

    kb3, vb3, ikb3 = kb.reshape(b, t, LANES), vb.reshape(b, t, LANES), ikb.reshape(b, t, LANES)
    if past is not None:
        ck, cv, cik = past
        p = ck.shape[1]
        kb3 = jnp.concatenate([ck.reshape(b, p, LANES).astype(BF16), kb3], axis=1)
        vb3 = jnp.concatenate([cv.reshape(b, p, LANES).astype(BF16), vb3], axis=1)
        ikb3 = jnp.concatenate([cik.astype(BF16), ikb3[:, :, :IDX_DIM]], axis=1)
    topk = min(IDX_TOPK, kb3.shape[1] // 4)
    o_a = _attention(aq, iq, ikw, limits, kb3, vb3, ikb3, b, topk)

    o_b, s_fin = _gla(gq, gk, gv, glog, gr, w["g_gla_norm"], s0, b, chunk)

    h1, h1b, s1t, s2t = _merge(x2, o_a, o_b, gate, w["w_pa"], w["w_pb"], w["w_out"],
                               w["ln1_g"], w["ln1_b"], w["w_pq"], w["pk1"], w["pk2"], alpha, tm)
    cnt, e1, rank, e2 = _select(s1t, s2t, tm)
    y = _peer(h1, h1b, cnt, e1, rank, e2, w["pu"], w["pv"], w["ln2_g"], w["ln2_b"], alpha,
              _pick_tile(n, PEER_TOKEN_TILE))

    k_out = k32.reshape(b, t, A_KV_HEADS, A_HEAD_DIM)
    v_out = v32.reshape(b, t, A_KV_HEADS, A_HEAD_DIM)
    ik_out = ikw[:, :IDX_DIM].reshape(b, t, IDX_DIM)
    return y.reshape(b, t, d), k_out, v_out, ik_out, s_fin


def kernel(x_prompt, x_sample, cache_k, cache_v, cache_idx_k, state_gla, w_in, w_fa, b_fa, g_gla_norm,
           w_pa, w_pb, w_out, ln1_g, ln1_b, w_pq, pk1, pk2, pu, pv, ln2_g, ln2_b):
    depth = w_in.shape[0]
    alpha = (2.0 * depth) ** 0.25
    bp, tp, _ = x_prompt.shape
    bs, ts, _ = x_sample.shape
    past_len = cache_k.shape[2]
    pos_p = jnp.arange(tp)
    pos_s = past_len + jnp.arange(ts)
    lim_p = (np.arange(tp) // CHUNK + 1) * CHUNK
    lim_s = np.full((ts,), past_len + ts)
    names = ("w_in", "w_fa", "b_fa", "g_gla_norm", "w_pa", "w_pb", "w_out", "ln1_g", "ln1_b",
             "w_pq", "pk1", "pk2", "pu", "pv", "ln2_g", "ln2_b")
    stacked = (w_in, w_fa, b_fa, g_gla_norm, w_pa, w_pb, w_out, ln1_g, ln1_b, w_pq, pk1, pk2, pu, pv, ln2_g, ln2_b)
    hp, hs = x_prompt, x_sample
    outs_p, outs_s = [], []
    for l in range(depth):
        w = {nm: a[l] for nm, a in zip(names, stacked)}
        s0 = jnp.zeros((bp, G_HEADS, G_KEY_DIM, G_VAL_DIM), F32)
        hp, *rest = _layer(hp, pos_p, lim_p, None, s0, w, chunk=CHUNK, alpha=alpha)
        outs_p.append(rest)
        hs, *rest = _layer(hs, pos_s, lim_s, (cache_k[l], cache_v[l], cache_idx_k[l]), state_gla[l], w,
                           chunk=ts, alpha=alpha)
        outs_s.append(rest)
    stack = lambda outs, i: jnp.stack([o[i] for o in outs])
    return (hp, hs, stack(outs_p, 0), stack(outs_p, 1), stack(outs_p, 2), stack(outs_p, 3),
            stack(outs_s, 0), stack(outs_s, 1), stack(outs_s, 2), stack(outs_s, 3))
```

```python
import functools
import math

import numpy as np
import jax
import jax.numpy as jnp
from jax import lax
from jax.experimental import pallas as pl
from jax.experimental.pallas import tpu as pltpu

F32 = jnp.float32
BF16 = jnp.bfloat16
I32 = jnp.int32

LANES = 128
SUBLANES = 8
VMEM_LIMIT = 56 << 20

CHUNK = 64
A_HEADS = 8
A_KV_HEADS = 2
A_HEAD_DIM = 64
A_GROUP = A_HEADS // A_KV_HEADS
IDX_HEADS = 4
IDX_DIM = 64
IDX_TOPK = 256
ROPE_THETA = 10000.0
G_HEADS = 4
G_KEY_DIM = 128
G_VAL_DIM = 128
G_LOWRANK = 16
G_TAU = 16.0
P_HEADS = 8
P_NKEYS = 128
P_HALF = 128
P_TOPK = 16
LN_EPS = 1e-5

W_AQ = A_HEADS * A_HEAD_DIM
W_AK = A_KV_HEADS * A_HEAD_DIM
W_IQ = IDX_HEADS * IDX_DIM
W_G = G_HEADS * G_KEY_DIM
IN_SIZES = (W_AQ, W_AK, W_AK, W_IQ, IDX_DIM, IDX_HEADS, W_G, W_G, W_G, G_LOWRANK, W_G, None)

TOKEN_TILE = 256
PEER_TOKEN_TILE = 512
Q_BLOCK = 128
KEY_TILE = 512
GLA_SEQS = 4

INT_MIN = -(2 ** 31)
NEG_INF_KEY = -2139095041
NEG_BIG = -1e30
LOG2_E = 1.4426950408889634


def _dot(a, b):
    return jnp.dot(a, b, preferred_element_type=F32)


def _dot_nt(a, b):
    return lax.dot_general(a, b, (((1,), (1,)), ((), ())), preferred_element_type=F32)


def _dot_tn(a, b):
    return lax.dot_general(a, b, (((0,), (0,)), ((), ())), preferred_element_type=F32)


def _sort_key(x):
    bits = pltpu.bitcast(x, I32)
    key = jnp.where(bits < 0, bits ^ 0x7FFFFFFF, bits)
    return jnp.where(key == -1, 0, key)


def _fold_rows(x, op):
    x = x.reshape(x.shape[0] // SUBLANES, SUBLANES, x.shape[1])
    while x.shape[0] > 1:
        half = x.shape[0] // 2
        folded = op(x[:half], x[half:2 * half])
        x = folded if x.shape[0] == 2 * half else jnp.concatenate([folded, x[2 * half:]], axis=0)
    return x[0]


def _params(sem):
    return pltpu.CompilerParams(dimension_semantics=sem, vmem_limit_bytes=VMEM_LIMIT)


_PG_AQ, _PG_K, _PG_V, _PG_IQ, _PG_IKW, _PG_GQ, _PG_GK, _PG_GV, _PG_GF, _PG_GR, _PG_GATE = range(11)


def _pack_layout(d_model):
    widths = [W_AQ, W_AK, W_AK, W_IQ, LANES, W_G, W_G, W_G, LANES, W_G, 2 * d_model]
    offs = np.concatenate([[0], np.cumsum(widths)]).tolist()
    return widths, offs


def _pack_w_in(w_in):
    d = w_in.shape[0]
    sizes = list(IN_SIZES[:-1]) + [2 * d]
    cuts = np.cumsum(sizes)[:-1].tolist()
    aq, ak, av, iq, ik, iw, gq, gk, gv, gf, gr, gate = jnp.split(w_in, cuts, axis=1)
    z = lambda n: jnp.zeros((d, n), w_in.dtype)
    ikw = jnp.concatenate([ik, iw, z(LANES - IDX_DIM - IDX_HEADS)], axis=1)
    gfp = jnp.concatenate([gf, z(LANES - G_LOWRANK)], axis=1)
    return jnp.concatenate([aq, ak, av, iq, ikw, gq, gk, gv, gfp, gr, gate], axis=1).astype(BF16)


def _rope_tables(pos):
    half = A_HEAD_DIM // 2
    inv = ROPE_THETA ** (-jnp.arange(half, dtype=F32) / half)
    ang = pos.astype(F32)[:, None] * inv[None, :]
    c, s = jnp.cos(ang), jnp.sin(ang)
    return jnp.concatenate([c, c, c, c], -1), jnp.concatenate([-s, s, -s, s], -1)


def _proj_kernel(x_ref, w_ref, wfa_ref, bfa_ref, cos_ref, sin_ref,
                 aq_ref, k_ref, v_ref, kb_ref, vb_ref, iq_ref, ikw_ref, ikb_ref,
                 gq_ref, gk_ref, gv_ref, glog_ref, gr_ref, gate_ref, *, offs, widths):
    xb = x_ref[...].astype(BF16)
    cos = cos_ref[...]
    sin = sin_ref[...]
    lane = lax.broadcasted_iota(I32, cos.shape, 1)
    first_half = (lane & (A_HEAD_DIM // 2)) == 0

    def proj(g):
        return _dot(xb, w_ref[:, offs[g]:offs[g] + widths[g]])

    def rope_slab(y):
        fwd = pltpu.roll(y, LANES - A_HEAD_DIM // 2, 1)
        bwd = pltpu.roll(y, A_HEAD_DIM // 2, 1)
        return y * cos + jnp.where(first_half, fwd, bwd) * sin

    def rope(y):
        return [rope_slab(y[:, s * LANES:(s + 1) * LANES]) for s in range(y.shape[1] // LANES)]

    for s, slab in enumerate(rope(proj(_PG_AQ))):
        aq_ref[:, s * LANES:(s + 1) * LANES] = (slab * (A_HEAD_DIM ** -0.5 * LOG2_E)).astype(BF16)
    k = rope(proj(_PG_K))[0]
    k_ref[...] = k
    kb_ref[...] = k.astype(BF16)
    v = proj(_PG_V)
    v_ref[...] = v
    vb_ref[...] = v.astype(BF16)
    for s, slab in enumerate(rope(proj(_PG_IQ))):
        iq_ref[:, s * LANES:(s + 1) * LANES] = (slab * (IDX_DIM ** -0.5)).astype(BF16)
    raw = proj(_PG_IKW)
    ikw = jnp.where(lane < IDX_DIM, rope_slab(raw), raw * (IDX_HEADS ** -0.5))
    ikw_ref[...] = ikw
    ikb_ref[...] = ikw.astype(BF16)
    gq_ref[...] = proj(_PG_GQ) * (G_KEY_DIM ** -0.5)
    gk_ref[...] = proj(_PG_GK)
    gv_ref[...] = proj(_PG_GV)
    z = _dot(proj(_PG_GF).astype(BF16), wfa_ref[...]) + bfa_ref[...]
    glog_ref[...] = (jnp.minimum(z, 0.0) - jnp.log1p(jnp.exp(-jnp.abs(z)))) * (1.0 / G_TAU)
    gr_ref[...] = proj(_PG_GR)
    gate_ref[...] = proj(_PG_GATE)


def _proj(x2, pos, w_in, w_fa, b_fa, tm):
    n, d = x2.shape
    t = pos.shape[0]
    widths, offs = _pack_layout(d)
    wp = _pack_w_in(w_in)
    wfa = jnp.concatenate([w_fa, jnp.zeros((LANES - G_LOWRANK, W_G), w_fa.dtype)], 0).astype(BF16)
    cos, sin = _rope_tables(pos)
    if tm > t:
        cos, sin = jnp.tile(cos, (tm // t, 1)), jnp.tile(sin, (tm // t, 1))
    nper = cos.shape[0] // tm
    row = lambda w: pl.BlockSpec((tm, w), lambda i: (i, 0))
    const = lambda a: pl.BlockSpec(a.shape, lambda i: (0, 0))
    tab = pl.BlockSpec((tm, LANES), lambda i: (i % nper, 0))
    outs = [(W_AQ, BF16), (LANES, F32), (LANES, F32), (LANES, BF16), (LANES, BF16), (W_IQ, BF16),
            (LANES, F32), (LANES, BF16), (W_G, F32), (W_G, F32), (W_G, F32), (W_G, F32), (W_G, F32),
            (2 * d, F32)]
    bfa = b_fa.reshape(1, W_G)
    return pl.pallas_call(
        functools.partial(_proj_kernel, offs=offs, widths=widths),
        grid=(n // tm,),
        in_specs=[row(d), const(wp), const(wfa), const(bfa), tab, tab],
        out_specs=[row(w) for w, _ in outs],
        out_shape=[jax.ShapeDtypeStruct((n, w), dt) for w, dt in outs],
        compiler_params=_params(("parallel",)),
        name="proj",
    )(x2, wp, wfa, bfa, cos, sin)


def _attn_kernel(aq_ref, iq_ref, ikw_ref, lim_ref, kb_ref, vt_ref, ikb_ref, o_ref,
                 keys_ref, qs_ref, iqs_ref, acc_ref, *, kt_w, nkt, topk, idx_bits):
    qb = Q_BLOCK

    aq_t = aq_ref[...].astype(F32).T
    for h in range(A_HEADS):
        qs_ref[h // A_GROUP, :, (h % A_GROUP) * qb:(h % A_GROUP + 1) * qb] = \
            aq_t[h * A_HEAD_DIM:(h + 1) * A_HEAD_DIM, :].astype(BF16)
    iq_t = iq_ref[...].astype(F32).T
    for h in range(IDX_HEADS):
        iqs_ref[:, h * qb:(h + 1) * qb] = iq_t[h * IDX_DIM:(h + 1) * IDX_DIM, :].astype(BF16)

    ikw_t = ikw_ref[...].T
    iw_rows = [ikw_t[IDX_DIM + h:IDX_DIM + h + 1, :] for h in range(IDX_HEADS)]
    lim = lim_ref[0, 0:1, :]
    sub = lax.broadcasted_iota(I32, (kt_w, qb), 0)

    def score_tile(kt, _):
        base = pl.multiple_of(kt * kt_w, kt_w)
        ik_t = ikb_ref[0, pl.ds(base, kt_w), :][:, :IDX_DIM]
        s = jnp.maximum(_dot(ik_t, iqs_ref[...]), 0.0)
        score = jnp.zeros((kt_w, qb), F32)
        for h in range(IDX_HEADS):
            score = score + s[:, h * qb:(h + 1) * qb] * iw_rows[h]
        keys_ref[pl.ds(base, kt_w), :] = jnp.where(sub + base < lim, _sort_key(score), NEG_INF_KEY)
        return 0

    lax.fori_loop(0, nkt, score_tile, 0, unroll=min(nkt, 2))

    def count(pred):
        acc = jnp.zeros((SUBLANES, qb), F32)
        for kt in range(nkt):
            acc = acc + _fold_rows(pred(keys_ref[kt * kt_w:(kt + 1) * kt_w, :], sub + kt * kt_w), jnp.add)
        return jnp.sum(acc, axis=0, keepdims=True)

    def count_ge(t_row):
        return count(lambda kk, idx: jnp.where(kk >= t_row, 1.0, 0.0))

    kf = float(topk)
    thr = jnp.where(count_ge(jnp.zeros((1, qb), I32)) >= kf, 0, INT_MIN).astype(I32)

    def thr_bit(i, t):
        cand = t + jnp.left_shift(jnp.int32(1), 30 - i)
        return jnp.where(count_ge(cand) >= kf, cand, t)

    thr = lax.fori_loop(0, 31, thr_bit, thr)
    n_gt = count_ge(thr + 1)
    n_eq = count_ge(thr) - n_gt
    need = kf - n_gt
    finite = thr > NEG_INF_KEY
    excess = jnp.where(finite, jnp.where(n_eq > need, 1.0, 0.0), 0.0)

    def count_eq_below(j_row):
        return count(lambda kk, idx: jnp.where(kk == thr, jnp.where(idx < j_row, 1.0, 0.0), 0.0))

    def resolve_ties():
        def bit(i, jc):
            cand = jc + jnp.left_shift(jnp.int32(1), idx_bits - 1 - i)
            return jnp.where(count_eq_below(cand) <= need - 1.0, cand, jc)
        jc = lax.fori_loop(0, idx_bits, bit, jnp.zeros((1, qb), I32))
        return jnp.where(finite, jc, -1)

    cut = lax.cond(jnp.max(excess) > 0.0, resolve_ties,
                   lambda: jnp.where(finite, 2 ** 30, -1).astype(I32))

    acc_ref[...] = jnp.zeros(acc_ref.shape, F32)
    gq = A_GROUP * qb

    def attend(kt, carry):
        base = pl.multiple_of(kt * kt_w, kt_w)
        k_t = kb_ref[0, pl.ds(base, kt_w), :]
        kk = keys_ref[pl.ds(base, kt_w), :]
        tie = jnp.where(sub + base <= cut, 0.0, NEG_BIG)
        bias = jnp.where(kk > thr, 0.0, jnp.where(kk == thr, tie, NEG_BIG))
        bias = jnp.concatenate([bias] * A_GROUP, axis=1)
        out = []
        for n in range(A_KV_HEADS):
            m_old, l_old = carry[2 * n], carry[2 * n + 1]
            logits = bias + _dot(k_t[:, n * A_HEAD_DIM:(n + 1) * A_HEAD_DIM], qs_ref[n])
            m_new = jnp.maximum(m_old, jnp.max(_fold_rows(logits, jnp.maximum), axis=0, keepdims=True))
            alpha = jnp.exp2(m_old - m_new)
            p = jnp.exp2(logits - m_new)
            l_new = alpha * l_old + jnp.sum(_fold_rows(p, jnp.add), axis=0, keepdims=True)
            v_t = vt_ref[0, n * A_HEAD_DIM:(n + 1) * A_HEAD_DIM, pl.ds(base, kt_w)]
            acc_ref[n] = alpha * acc_ref[n] + _dot(v_t, p.astype(BF16))
            out += [m_new, l_new]
        return tuple(out)

    init = (jnp.full((1, gq), NEG_BIG, F32), jnp.zeros((1, gq), F32)) * A_KV_HEADS
    fin = lax.fori_loop(0, nkt, attend, init, unroll=min(nkt, 2))
    l_row = [fin[2 * n + 1] for n in range(A_KV_HEADS)]

    o_t = jnp.concatenate([acc_ref[n] / l_row[n] for n in range(A_KV_HEADS)], axis=0)
    for n in range(A_KV_HEADS):
        for g in range(A_GROUP):
            h = n * A_GROUP + g
            blk = o_t[n * A_HEAD_DIM:(n + 1) * A_HEAD_DIM, g * qb:(g + 1) * qb]
            o_ref[:, h * A_HEAD_DIM:(h + 1) * A_HEAD_DIM] = blk.T.astype(BF16)


def _attention(aq, iq, ikw, limits, kb, vb, ikb, b, topk):
    qb, kt_w = Q_BLOCK, KEY_TILE
    tq = aq.shape[0] // b
    tq_pad = -(-tq // qb) * qb
    limits = np.asarray(limits)
    if tq_pad != tq:
        padq = lambda a: jnp.pad(a.reshape(b, tq, -1), ((0, 0), (0, tq_pad - tq), (0, 0))).reshape(b * tq_pad, -1)
        aq, iq, ikw = padq(aq), padq(iq), padq(ikw)
        limits = np.concatenate([limits, np.full((tq_pad - tq,), limits[-1])])
    l_all = kb.shape[1]
    l_pad = -(-l_all // kt_w) * kt_w
    if l_pad != l_all:
        padl = lambda a: jnp.pad(a, ((0, 0), (0, l_pad - l_all), (0, 0)))
        kb, vb, ikb = padl(kb), padl(vb), padl(ikb)
    vt = jnp.swapaxes(vb, 1, 2)
    nq = tq_pad // qb
    lim_blk = limits.reshape(nq, qb)
    nkt = np.minimum(-(-lim_blk.max(axis=1) // kt_w), l_pad // kt_w)
    lim = jnp.asarray(np.broadcast_to(lim_blk[:, None, :], (nq, SUBLANES, qb)).astype(np.int32))
    idx_bits = max(1, int(math.ceil(math.log2(l_pad))))
    keys = lambda a: pl.BlockSpec((1,) + a.shape[1:], lambda bi, j: (bi, 0, 0))
    gq = A_GROUP * qb
    runs, j0 = [], 0
    for j in range(1, nq + 1):
        if j == nq or nkt[j] != nkt[j0]:
            runs.append((j0, j - j0, int(nkt[j0])))
            j0 = j
    outs = []
    for j0, nj, n_tiles in runs:
        qrow = lambda w, j0=j0: pl.BlockSpec((qb, w), lambda bi, j: (bi * nq + j0 + j, 0))
        outs.append(pl.pallas_call(
            functools.partial(_attn_kernel, kt_w=kt_w, nkt=n_tiles, topk=topk, idx_bits=idx_bits),
            grid=(b, nj),
            in_specs=[qrow(W_AQ), qrow(W_IQ), qrow(LANES),
                      pl.BlockSpec((1, SUBLANES, qb), lambda bi, j, j0=j0: (j0 + j, 0, 0)),
                      keys(kb), keys(vt), keys(ikb)],
            out_specs=pl.BlockSpec((qb, W_AQ), lambda bi, j, nj=nj: (bi * nj + j, 0)),
            out_shape=jax.ShapeDtypeStruct((b * nj * qb, W_AQ), BF16),
            scratch_shapes=[pltpu.VMEM((n_tiles * kt_w, qb), I32),
                            pltpu.VMEM((A_KV_HEADS, A_HEAD_DIM, gq), BF16),
                            pltpu.VMEM((IDX_DIM, IDX_HEADS * qb), BF16),
                            pltpu.VMEM((A_KV_HEADS, A_HEAD_DIM, gq), F32)],
            compiler_params=_params(("parallel", "arbitrary")),
            name="attn",
        )(aq, iq, ikw, lim, kb, vt, ikb).reshape(b, nj * qb, W_AQ))
    o = outs[0] if len(outs) == 1 else jnp.concatenate(outs, axis=1)
    return o[:, :tq].reshape(b * tq, W_AQ)


def _gla_constants(c):
    nlev = int(math.log2(c))
    t = np.arange(c)
    mats = [(t[None, :] <= t[:, None])]
    masks = [np.eye(c, dtype=bool)]
    for lev in range(nlev):
        m = c >> (lev + 1)
        ref_row = (t // (2 * m)) * 2 * m + m
        mats.append(t[None, :] <= ref_row[:, None])
        upper = (t & m) != 0
        same = (t[:, None] // (2 * m)) == (t[None, :] // (2 * m))
        masks.append(same & upper[:, None] & ~upper[None, :])
    return (jnp.asarray(np.concatenate(mats, 0).astype(np.float32), BF16),
            jnp.asarray(np.stack(masks).astype(np.float32)), nlev)


def _gla_kernel(q_ref, k_ref, v_ref, g_ref, gr_ref, gn_ref, mst_ref, msk_ref, s0_ref,
                ob_ref, sfin_ref, st_ref, *, c, nlev, nb):
    i = pl.program_id(1)
    hk = G_KEY_DIM

    @pl.when(i == 0)
    def _():
        for s in range(nb):
            for h in range(G_HEADS):
                st_ref[s, h] = s0_ref[s, h].T

    mst = mst_ref[...]
    gn = gn_ref[...]
    hs = lambda a, h: a[:, h * hk:(h + 1) * hk]
    for s in range(nb):
        g = g_ref[s]
        g_hi = g.astype(BF16)
        r1 = g - g_hi.astype(F32)
        g_mid = r1.astype(BF16)
        g_lo = (r1 - g_mid.astype(F32)).astype(BF16)
        bs = _dot(mst, g_hi) + _dot(mst, g_mid) + _dot(mst, g_lo)
        b = bs[0:c]
        q = q_ref[s]
        k = k_ref[s]
        vb = v_ref[s].astype(BF16)
        row = lax.broadcasted_iota(I32, q.shape, 0)

        qb = q.astype(BF16)
        kb = k.astype(BF16)
        attn = [_dot_nt(hs(qb, h), hs(kb, h)) * msk_ref[0] for h in range(G_HEADS)]
        for lev in range(nlev):
            m = c >> (lev + 1)
            upper = (row & m) != 0
            d = b - bs[(lev + 1) * c:(lev + 2) * c]
            e = jnp.exp(jnp.where(upper, d, -d))
            qt = jnp.where(upper, q * e, 0.0).astype(BF16)
            kt = jnp.where(upper, 0.0, k * e).astype(BF16)
            mk = msk_ref[lev + 1]
            for h in range(G_HEADS):
                attn[h] = attn[h] + _dot_nt(hs(qt, h), hs(kt, h)) * mk

        qe = (q * jnp.exp(b)).astype(BF16)
        b_last = b[c - 1:c, :]
        khat = (k * jnp.exp(b_last - b)).astype(BF16)
        dec = jnp.exp(b_last)
        gr = gr_ref[s]
        for h in range(G_HEADS):
            st = st_ref[s, h]
            o = _dot_nt(hs(qe, h), st.astype(BF16)) + _dot(attn[h].astype(BF16), hs(vb, h))
            st_ref[s, h] = st * hs(dec, h) + _dot_tn(hs(vb, h), hs(khat, h))
            ms = jnp.mean(o * o, axis=1, keepdims=True)
            grh = hs(gr, h)
            of = o * lax.rsqrt(ms + LN_EPS) * hs(gn, h) * (grh / (1.0 + jnp.exp(-grh)))
            ob_ref[s, :, h * hk:(h + 1) * hk] = of.astype(BF16)

    @pl.when(i == pl.num_programs(1) - 1)
    def _():
        for s in range(nb):
            for h in range(G_HEADS):
                sfin_ref[s, h] = st_ref[s, h].T


def _gla(gq, gk, gv, glog, gr, g_norm, s0, b, c):
    n = gq.shape[0]
    t = n // b
    nc = t // c
    nb = GLA_SEQS if b % GLA_SEQS == 0 else 1
    mst, msk, nlev = _gla_constants(c)
    seq = lambda a: a.reshape(b, t, W_G)
    row = pl.BlockSpec((nb, c, W_G), lambda bi, i: (bi, i, 0))
    const = lambda a: pl.BlockSpec(a.shape, lambda bi, i: (0,) * a.ndim)
    st_spec = pl.BlockSpec((nb, G_HEADS, G_KEY_DIM, G_VAL_DIM), lambda bi, i: (bi, 0, 0, 0))
    gn = g_norm.reshape(1, W_G)
    ob, s_fin = pl.pallas_call(
        functools.partial(_gla_kernel, c=c, nlev=nlev, nb=nb),
        grid=(b // nb, nc),
        in_specs=[row, row, row, row, row, const(gn), const(mst), const(msk), st_spec],
        out_specs=[row, st_spec],
        out_shape=[jax.ShapeDtypeStruct((b, t, W_G), BF16),
                   jax.ShapeDtypeStruct((b, G_HEADS, G_KEY_DIM, G_VAL_DIM), F32)],
        scratch_shapes=[pltpu.VMEM((nb, G_HEADS, G_VAL_DIM, G_KEY_DIM), F32)],
        compiler_params=_params(("parallel", "arbitrary")),
        name="gla",
    )(seq(gq), seq(gk), seq(gv), seq(glog), seq(gr), gn, mst, msk, s0)
    return ob.reshape(n, W_G), s_fin


def _layer_norm(z, g, b):
    mu = jnp.mean(z, axis=1, keepdims=True)
    zc = z - mu
    var = jnp.mean(zc * zc, axis=1, keepdims=True)
    return zc * lax.rsqrt(var + LN_EPS) * g + b


def _sigmoid(x):
    return 1.0 / (1.0 + jnp.exp(-x))


def _merge_kernel(x_ref, oa_ref, ob_ref, gate_ref, wpa_ref, wpb_ref, wout_ref, g1_ref, b1_ref,
                  wpq_ref, pk1_ref, pk2_ref, h1_ref, h1b_ref, s1t_ref, s2t_ref, *, alpha, d):
    ya = _dot(oa_ref[...], wpa_ref[...])
    yb = _dot(ob_ref[...], wpb_ref[...])
    m = _sigmoid(gate_ref[:, :d]) * ya + _sigmoid(gate_ref[:, d:]) * yb
    mix = _dot(m.astype(BF16), wout_ref[...])
    h1 = _layer_norm(alpha * x_ref[...] + mix, g1_ref[...], b1_ref[...])
    h1_ref[...] = h1
    h1b_ref[...] = h1.T.astype(BF16)
    qp = _dot(h1.astype(BF16), wpq_ref[...])
    for h in range(P_HEADS):
        for half, (pk_ref, st_ref) in enumerate(((pk1_ref, s1t_ref), (pk2_ref, s2t_ref))):
            c0 = (2 * h + half) * P_HALF
            st_ref[h * P_NKEYS:(h + 1) * P_NKEYS, :] = _dot_nt(pk_ref[h], qp[:, c0:c0 + P_HALF].astype(BF16))


def _merge(x2, oa, ob, gate, w_pa, w_pb, w_out, ln_g, ln_b, w_pq, pk1, pk2, alpha, tm):
    n, d = x2.shape
    hk = P_HEADS * P_NKEYS
    row = lambda w: pl.BlockSpec((tm, w), lambda i: (i, 0))
    col = pl.BlockSpec((hk, tm), lambda i: (0, i))
    const = lambda a: pl.BlockSpec(a.shape, lambda i: (0,) * a.ndim)
    ws = [w_pa.astype(BF16), w_pb.astype(BF16), w_out.astype(BF16), ln_g.reshape(1, d), ln_b.reshape(1, d),
          w_pq.astype(BF16), pk1.astype(BF16), pk2.astype(BF16)]
    return pl.pallas_call(
        functools.partial(_merge_kernel, alpha=alpha, d=d),
        grid=(n // tm,),
        in_specs=[row(d), row(W_AQ), row(W_G), row(2 * d)] + [const(w) for w in ws],
        out_specs=[row(d), pl.BlockSpec((d, tm), lambda i: (0, i)), col, col],
        out_shape=[jax.ShapeDtypeStruct((n, d), F32), jax.ShapeDtypeStruct((d, n), BF16),
                   jax.ShapeDtypeStruct((hk, n), F32), jax.ShapeDtypeStruct((hk, n), F32)],
        compiler_params=_params(("parallel",)),
        name="merge",
    )(x2, oa, ob, gate, *ws)


def _top_desc(s, count, want_rank=False):
    tops = []
    rank = jnp.full(s.shape, float(count), F32) if want_rank else None
    for r in range(count):
        m = jnp.max(_fold_rows(s, jnp.maximum), axis=0, keepdims=True)
        tops.append(m)
        hit = s == m
        if want_rank:
            rank = jnp.where(hit, float(r), rank)
        if r + 1 < count:
            s = jnp.where(hit, -jnp.inf, s)
    return tops, rank


def _select_kernel(s1t_ref, s2t_ref, cnt_ref, e1_ref, rank_ref, e2_ref):
    nk = P_NKEYS

    def head(h, _):
        r0 = pl.multiple_of(h * nk, nk)
        rows = pl.ds(r0, nk)
        s1 = s1t_ref[rows, :]
        s2 = s2t_ref[rows, :]
        v1, _ = _top_desc(s1, P_TOPK)
        v2, rank = _top_desc(s2, P_TOPK, want_rank=True)
        pairs = [(a, b) for a in range(P_TOPK) for b in range(P_TOPK // (a + 1))]
        fill = [jnp.full_like(v1[0], -jnp.inf)] * (-len(pairs) % SUBLANES)
        cand = jnp.concatenate([v1[a] + v2[b] for a, b in pairs] + fill, axis=0)
        work, seen = cand, jnp.zeros_like(v1[0])
        tau = jnp.full_like(v1[0], -jnp.inf)
        for _ in range(P_TOPK):
            m = jnp.max(work, axis=0, keepdims=True)
            hit = work == m
            seen = seen + jnp.sum(jnp.where(hit, 1.0, 0.0), axis=0, keepdims=True)
            tau = jnp.maximum(tau, jnp.where(seen >= float(P_TOPK), m, -jnp.inf))
            work = jnp.where(hit, -jnp.inf, work)
        cmax = v1[0] + v2[0]
        zsum = jnp.sum(jnp.where(cand >= tau, jnp.exp(cand - cmax), 0.0), axis=0, keepdims=True)
        v2all = jnp.concatenate(v2, axis=0)
        cnt = jnp.zeros(s1.shape, F32)
        for a in range(P_TOPK):
            cnt_a = jnp.sum(jnp.where(v1[a] + v2all >= tau, 1.0, 0.0), axis=0, keepdims=True)
            cnt = jnp.where(s1 == v1[a], cnt_a, cnt)
        e1 = jnp.exp(s1 - v1[0]) / zsum * 0.5
        for st in range(nk // I1_PER_STEP):
            dst = pl.ds(pl.multiple_of(st * P_HEADS * I1_PER_STEP + h * I1_PER_STEP, I1_PER_STEP), I1_PER_STEP)
            src = slice(st * I1_PER_STEP, (st + 1) * I1_PER_STEP)
            cnt_ref[dst, :] = cnt[src, :]
            e1_ref[dst, :] = e1[src, :]
        rank_ref[rows, :] = rank
        e2_ref[rows, :] = jnp.exp(s2 - v2[0])
        return 0

    lax.fori_loop(0, P_HEADS, head, 0)


def _select(s1t, s2t, tn):
    hk, n = s1t.shape
    col = pl.BlockSpec((hk, tn), lambda i: (0, i))
    return pl.pallas_call(
        _select_kernel,
        grid=(n // tn,),
        in_specs=[col, col],
        out_specs=[col, col, col, col],
        out_shape=[jax.ShapeDtypeStruct((hk, n), F32)] * 4,
        compiler_params=_params(("parallel",)),
        name="select",
    )(s1t, s2t)


I1_PER_STEP = 8
E_PER_STEP = I1_PER_STEP * P_NKEYS
I2_BLOCK = 16


def _gelu_x2(x):
    return x * (1.0 + lax.erf(x * (2.0 ** -0.5)))


def _peer_steps(t, n_steps, ng):
    item = lambda d: jnp.clip(t - d, 0, n_steps - 1)
    return [(item(d) // ng, item(d) % ng) for d in range(3)]


def _peer_kernel(hb_ref, pu_ref, pvt_ref, cnt_ref, e1_ref, rank_ref, e2_ref, h1_ref, g2_ref, b2_ref,
                 y_ref, a_ref, gw_ref, acc_ref, rank_s, e2_s, *, tn, alpha, n_steps, ng):
    t = pl.program_id(0)
    (_, _), (_, g), (_, g_out) = _peer_steps(t, n_steps, ng)
    slot = t % 2
    prev = 1 - slot

    @pl.when(t == 0)
    def _():
        a_ref[...] = jnp.zeros(a_ref.shape, F32)
        gw_ref[...] = jnp.zeros(gw_ref.shape, BF16)

    @pl.when(g == 0)
    def _():
        rank_s[...] = rank_ref[...].astype(BF16)
        e2_s[...] = e2_ref[...].astype(BF16)

    @pl.when(g_out == 0)
    def _():
        acc_ref[...] = jnp.zeros(acc_ref.shape, F32)

    a_ref[slot] = _dot(pu_ref[...], hb_ref[...])

    def lane_tile(lt, _):
        ls = pl.ds(pl.multiple_of(lt * LANES, LANES), LANES)
        for j in range(I1_PER_STEP):
            bcast = lambda ref, h: jnp.broadcast_to(
                ref[h * I1_PER_STEP + j:h * I1_PER_STEP + j + 1, ls], (I2_BLOCK, LANES)).astype(BF16)
            cnt = [bcast(cnt_ref, h) for h in range(P_HEADS)]
            e1 = [bcast(e1_ref, h) for h in range(P_HEADS)]
            for i2b in range(P_NKEYS // I2_BLOCK):
                w = jnp.zeros((I2_BLOCK, LANES), BF16)
                for h in range(P_HEADS):
                    rows = slice(h * P_NKEYS + i2b * I2_BLOCK, h * P_NKEYS + (i2b + 1) * I2_BLOCK)
                    w = w + jnp.where(rank_s[rows, ls] < cnt[h], e2_s[rows, ls] * e1[h], jnp.zeros((), BF16))
                arow = slice(j * P_NKEYS + i2b * I2_BLOCK, j * P_NKEYS + (i2b + 1) * I2_BLOCK)
                gw_ref[prev, arow, ls] = w * _gelu_x2(a_ref[prev, arow, ls]).astype(BF16)
        return 0

    lax.fori_loop(0, tn // LANES, lane_tile, 0)

    acc_ref[...] += _dot(pvt_ref[0], gw_ref[slot])

    @pl.when(jnp.logical_and(g_out == ng - 1, t >= 2))
    def _():
        y_ref[...] = _layer_norm(alpha * h1_ref[...] + acc_ref[...].T, g2_ref[...], b2_ref[...])


def _peer(h1, h1b, cnt, e1, rank, e2, pu, pv, ln_g, ln_b, alpha, tn):
    n, d = h1.shape
    hk = P_HEADS * P_NKEYS
    ng = pu.shape[0] // E_PER_STEP
    n_steps = (n // tn) * ng
    stage = lambda k, f: (lambda t: f(*_peer_steps(t, n_steps, ng)[k]))
    const = pl.BlockSpec((1, d), lambda t: (0, 0))
    routing = pl.BlockSpec((hk, tn), stage(1, lambda i, g: (0, i)))
    step_rows = pl.BlockSpec((P_HEADS * I1_PER_STEP, tn), stage(1, lambda i, g: (g, i)))
    pvt = pv.astype(BF16).reshape(ng, E_PER_STEP, d).transpose(0, 2, 1)
    return pl.pallas_call(
        functools.partial(_peer_kernel, tn=tn, alpha=alpha, n_steps=n_steps, ng=ng),
        grid=(n_steps + 2,),
        in_specs=[pl.BlockSpec((d, tn), stage(0, lambda i, g: (0, i))),
                  pl.BlockSpec((E_PER_STEP, d), stage(0, lambda i, g: (g, 0))),
                  pl.BlockSpec((1, d, E_PER_STEP), stage(2, lambda i, g: (g, 0, 0))),
                  step_rows, step_rows, routing, routing,
                  pl.BlockSpec((tn, d), stage(2, lambda i, g: (i, 0))), const, const],
        out_specs=pl.BlockSpec((tn, d), stage(2, lambda i, g: (i, 0))),
        out_shape=jax.ShapeDtypeStruct((n, d), F32),
        scratch_shapes=[pltpu.VMEM((2, E_PER_STEP, tn), F32), pltpu.VMEM((2, E_PER_STEP, tn), BF16),
                        pltpu.VMEM((d, tn), F32), pltpu.VMEM((hk, tn), BF16), pltpu.VMEM((hk, tn), BF16)],
        compiler_params=_params(("arbitrary",)),
        name="peer",
    )(h1b, pu.astype(BF16), pvt, cnt, e1, rank, e2, h1, ln_g.reshape(1, d), ln_b.reshape(1, d))


def _pick_tile(n, pref):
    t = min(n, pref)
    assert n % t == 0
    return t


def _layer(x, pos, limits, past, s0, w, *, chunk, alpha):
    b, t, d = x.shape
    n = b * t
    x2 = x.reshape(n, d)
    tm = _pick_tile(n, TOKEN_TILE)
    (aq, k32, v32, kb, vb, iq, ikw, ikb, gq, gk, gv, glog, gr, gate) = _proj(
        x2, pos, w["w_in"], w["w_fa"], w["b_fa"], tm)

    kb3, vb3, ikb3 = kb.reshape(b, t, LANES), vb.reshape(b, t, LANES), ikb.reshape(b, t, LANES)
    if past is not None:
        ck, cv, cik = past
        p = ck.shape[1]
        kb3 = jnp.concatenate([ck.reshape(b, p, LANES).astype(BF16), kb3], axis=1)
        vb3 = jnp.concatenate([cv.reshape(b, p, LANES).astype(BF16), vb3], axis=1)
        ikb3 = jnp.concatenate([cik.astype(BF16), ikb3[:, :, :IDX_DIM]], axis=1)
    topk = min(IDX_TOPK, kb3.shape[1] // 4)
    o_a = _attention(aq, iq, ikw, limits, kb3, vb3, ikb3, b, topk)

    o_b, s_fin = _gla(gq, gk, gv, glog, gr, w["g_gla_norm"], s0, b, chunk)

    h1, h1b, s1t, s2t = _merge(x2, o_a, o_b, gate, w["w_pa"], w["w_pb"], w["w_out"],
                               w["ln1_g"], w["ln1_b"], w["w_pq"], w["pk1"], w["pk2"], alpha, tm)
    cnt, e1, rank, e2 = _select(s1t, s2t, tm)
    y = _peer(h1, h1b, cnt, e1, rank, e2, w["pu"], w["pv"], w["ln2_g"], w["ln2_b"], alpha,
              _pick_tile(n, PEER_TOKEN_TILE))

    k_out = k32.reshape(b, t, A_KV_HEADS, A_HEAD_DIM)
    v_out = v32.reshape(b, t, A_KV_HEADS, A_HEAD_DIM)
    ik_out = ikw[:, :IDX_DIM].reshape(b, t, IDX_DIM)
    return y.reshape(b, t, d), k_out, v_out, ik_out, s_fin


def kernel(x_prompt, x_sample, cache_k, cache_v, cache_idx_k, state_gla, w_in, w_fa, b_fa, g_gla_norm,
           w_pa, w_pb, w_out, ln1_g, ln1_b, w_pq, pk1, pk2, pu, pv, ln2_g, ln2_b):
    depth = w_in.shape[0]
    alpha = (2.0 * depth) ** 0.25
    bp, tp, _ = x_prompt.shape
    bs, ts, _ = x_sample.shape
    past_len = cache_k.shape[2]
    pos_p = jnp.arange(tp)
    pos_s = past_len + jnp.arange(ts)
    lim_p = (np.arange(tp) // CHUNK + 1) * CHUNK
    lim_s = np.full((ts,), past_len + ts)
    names = ("w_in", "w_fa", "b_fa", "g_gla_norm", "w_pa", "w_pb", "w_out", "ln1_g", "ln1_b",
             "w_pq", "pk1", "pk2", "pu", "pv", "ln2_g", "ln2_b")
    stacked = (w_in, w_fa, b_fa, g_gla_norm, w_pa, w_pb, w_out, ln1_g, ln1_b, w_pq, pk1, pk2, pu, pv, ln2_g, ln2_b)
    hp, hs = x_prompt, x_sample
    outs_p, outs_s = [], []
    for l in range(depth):
        w = {nm: a[l] for nm, a in zip(names, stacked)}
        s0 = jnp.zeros((bp, G_HEADS, G_KEY_DIM, G_VAL_DIM), F32)
        hp, *rest = _layer(hp, pos_p, lim_p, None, s0, w, chunk=CHUNK, alpha=alpha)
        outs_p.append(rest)
        hs, *rest = _layer(hs, pos_s, lim_s, (cache_k[l], cache_v[l], cache_idx_k[l]), state_gla[l], w,
                           chunk=ts, alpha=alpha)
        outs_s.append(rest)
    stack = lambda outs, i: jnp.stack([o[i] for o in outs])
    return (hp, hs, stack(outs_p, 0), stack(outs_p, 1), stack(outs_p, 2), stack(outs_p, 3),
            stack(outs_s, 0), stack(outs_s, 1), stack(outs_s, 2), stack(outs_s, 3))
```

```python
import functools
import math

import numpy as np
import jax
import jax.numpy as jnp
from jax import lax
from jax.experimental import pallas as pl
from jax.experimental.pallas import tpu as pltpu

F32 = jnp.float32
BF16 = jnp.bfloat16
I32 = jnp.int32

LANES = 128
SUBLANES = 8
VMEM_LIMIT = 56 << 20

CHUNK = 64
A_HEADS = 8
A_KV_HEADS = 2
A_HEAD_DIM = 64
A_GROUP = A_HEADS // A_KV_HEADS
IDX_HEADS = 4
IDX_DIM = 64
IDX_TOPK = 256
ROPE_THETA = 10000.0
G_HEADS = 4
G_KEY_DIM = 128
G_VAL_DIM = 128
G_LOWRANK = 16
G_TAU = 16.0
P_HEADS = 8
P_NKEYS = 128
P_HALF = 128
P_TOPK = 16
LN_EPS = 1e-5

W_AQ = A_HEADS * A_HEAD_DIM
W_AK = A_KV_HEADS * A_HEAD_DIM
W_IQ = IDX_HEADS * IDX_DIM
W_G = G_HEADS * G_KEY_DIM
IN_SIZES = (W_AQ, W_AK, W_AK, W_IQ, IDX_DIM, IDX_HEADS, W_G, W_G, W_G, G_LOWRANK, W_G, None)

TOKEN_TILE = 256
PEER_TOKEN_TILE = 512
Q_BLOCK = 128
KEY_TILE = 512
SOFTMAX_TILES = 2
GLA_SEQS = 4

INT_MIN = -(2 ** 31)
NEG_INF_KEY = -2139095041
NEG_BIG = -1e30
LOG2_E = 1.4426950408889634


def _dot(a, b):
    return jnp.dot(a, b, preferred_element_type=F32)


def _dot_nt(a, b):
    return lax.dot_general(a, b, (((1,), (1,)), ((), ())), preferred_element_type=F32)


def _dot_tn(a, b):
    return lax.dot_general(a, b, (((0,), (0,)), ((), ())), preferred_element_type=F32)


def _sort_key(x):
    bits = pltpu.bitcast(x, I32)
    key = jnp.where(bits < 0, bits ^ 0x7FFFFFFF, bits)
    return jnp.where(key == -1, 0, key)


def _fold_rows(x, op):
    x = x.reshape(x.shape[0] // SUBLANES, SUBLANES, x.shape[1])
    while x.shape[0] > 1:
        half = x.shape[0] // 2
        folded = op(x[:half], x[half:2 * half])
        x = folded if x.shape[0] == 2 * half else jnp.concatenate([folded, x[2 * half:]], axis=0)
    return x[0]


def _params(sem):
    return pltpu.CompilerParams(dimension_semantics=sem, vmem_limit_bytes=VMEM_LIMIT)


_PG_AQ, _PG_K, _PG_V, _PG_IQ, _PG_IKW, _PG_GQ, _PG_GK, _PG_GV, _PG_GF, _PG_GR, _PG_GATE = range(11)


def _pack_layout(d_model):
    widths = [W_AQ, W_AK, W_AK, W_IQ, LANES, W_G, W_G, W_G, LANES, W_G, 2 * d_model]
    offs = np.concatenate([[0], np.cumsum(widths)]).tolist()
    return widths, offs


def _pack_w_in(w_in):
    d = w_in.shape[0]
    sizes = list(IN_SIZES[:-1]) + [2 * d]
    cuts = np.cumsum(sizes)[:-1].tolist()
    aq, ak, av, iq, ik, iw, gq, gk, gv, gf, gr, gate = jnp.split(w_in, cuts, axis=1)
    z = lambda n: jnp.zeros((d, n), w_in.dtype)
    ikw = jnp.concatenate([ik, iw, z(LANES - IDX_DIM - IDX_HEADS)], axis=1)
    gfp = jnp.concatenate([gf, z(LANES - G_LOWRANK)], axis=1)
    return jnp.concatenate([aq, ak, av, iq, ikw, gq, gk, gv, gfp, gr, gate], axis=1).astype(BF16)


def _rope_tables(pos):
    half = A_HEAD_DIM // 2
    inv = ROPE_THETA ** (-jnp.arange(half, dtype=F32) / half)
    ang = pos.astype(F32)[:, None] * inv[None, :]
    c, s = jnp.cos(ang), jnp.sin(ang)
    return jnp.concatenate([c, c, c, c], -1), jnp.concatenate([-s, s, -s, s], -1)


def _proj_kernel(x_ref, w_ref, wfa_ref, bfa_ref, cos_ref, sin_ref,
                 aq_ref, k_ref, v_ref, kb_ref, vb_ref, iq_ref, ikw_ref, ikb_ref,
                 gq_ref, gk_ref, gv_ref, glog_ref, gr_ref, gate_ref, *, offs, widths):
    xb = x_ref[...].astype(BF16)
    cos = cos_ref[...]
    sin = sin_ref[...]
    lane = lax.broadcasted_iota(I32, cos.shape, 1)
    first_half = (lane & (A_HEAD_DIM // 2)) == 0

    def proj(g):
        return _dot(xb, w_ref[:, offs[g]:offs[g] + widths[g]])

    def rope_slab(y):
        fwd = pltpu.roll(y, LANES - A_HEAD_DIM // 2, 1)
        bwd = pltpu.roll(y, A_HEAD_DIM // 2, 1)
        return y * cos + jnp.where(first_half, fwd, bwd) * sin

    def rope(y):
        return [rope_slab(y[:, s * LANES:(s + 1) * LANES]) for s in range(y.shape[1] // LANES)]

    for s, slab in enumerate(rope(proj(_PG_AQ))):
        aq_ref[:, s * LANES:(s + 1) * LANES] = (slab * (A_HEAD_DIM ** -0.5 * LOG2_E)).astype(BF16)
    k = rope(proj(_PG_K))[0]
    k_ref[...] = k
    kb_ref[...] = k.astype(BF16)
    v = proj(_PG_V)
    v_ref[...] = v
    vb_ref[...] = v.astype(BF16)
    for s, slab in enumerate(rope(proj(_PG_IQ))):
        iq_ref[:, s * LANES:(s + 1) * LANES] = (slab * (IDX_DIM ** -0.5)).astype(BF16)
    raw = proj(_PG_IKW)
    ikw = jnp.where(lane < IDX_DIM, rope_slab(raw), raw * (IDX_HEADS ** -0.5))
    ikw_ref[...] = ikw
    ikb_ref[...] = ikw.astype(BF16)
    gq_ref[...] = proj(_PG_GQ) * (G_KEY_DIM ** -0.5)
    gk_ref[...] = proj(_PG_GK)
    gv_ref[...] = proj(_PG_GV)
    z = _dot(proj(_PG_GF).astype(BF16), wfa_ref[...]) + bfa_ref[...]
    glog_ref[...] = (jnp.minimum(z, 0.0) - jnp.log1p(jnp.exp(-jnp.abs(z)))) * (1.0 / G_TAU)
    gr_ref[...] = proj(_PG_GR)
    gate_ref[...] = proj(_PG_GATE)


def _proj(x2, pos, w_in, w_fa, b_fa, tm):
    n, d = x2.shape
    t = pos.shape[0]
    widths, offs = _pack_layout(d)
    wp = _pack_w_in(w_in)
    wfa = jnp.concatenate([w_fa, jnp.zeros((LANES - G_LOWRANK, W_G), w_fa.dtype)], 0).astype(BF16)
    cos, sin = _rope_tables(pos)
    if tm > t:
        cos, sin = jnp.tile(cos, (tm // t, 1)), jnp.tile(sin, (tm // t, 1))
    nper = cos.shape[0] // tm
    row = lambda w: pl.BlockSpec((tm, w), lambda i: (i, 0))
    const = lambda a: pl.BlockSpec(a.shape, lambda i: (0, 0))
    tab = pl.BlockSpec((tm, LANES), lambda i: (i % nper, 0))
    outs = [(W_AQ, BF16), (LANES, F32), (LANES, F32), (LANES, BF16), (LANES, BF16), (W_IQ, BF16),
            (LANES, F32), (LANES, BF16), (W_G, F32), (W_G, F32), (W_G, F32), (W_G, F32), (W_G, F32),
            (2 * d, F32)]
    bfa = b_fa.reshape(1, W_G)
    return pl.pallas_call(
        functools.partial(_proj_kernel, offs=offs, widths=widths),
        grid=(n // tm,),
        in_specs=[row(d), const(wp), const(wfa), const(bfa), tab, tab],
        out_specs=[row(w) for w, _ in outs],
        out_shape=[jax.ShapeDtypeStruct((n, w), dt) for w, dt in outs],
        compiler_params=_params(("parallel",)),
        name="proj",
    )(x2, wp, wfa, bfa, cos, sin)


def _attn_kernel(aq_ref, iq_ref, ikw_ref, lim_ref, kb_ref, vt_ref, ikb_ref, o_ref,
                 keys_ref, qs_ref, iqs_ref, acc_ref, *, kt_w, nkt, topk, idx_bits):
    qb = Q_BLOCK

    aq_t = aq_ref[...].astype(F32).T
    for h in range(A_HEADS):
        qs_ref[h // A_GROUP, :, (h % A_GROUP) * qb:(h % A_GROUP + 1) * qb] = \
            aq_t[h * A_HEAD_DIM:(h + 1) * A_HEAD_DIM, :].astype(BF16)
    iq_t = iq_ref[...].astype(F32).T
    for h in range(IDX_HEADS):
        iqs_ref[:, h * qb:(h + 1) * qb] = iq_t[h * IDX_DIM:(h + 1) * IDX_DIM, :].astype(BF16)

    ikw_t = ikw_ref[...].T
    iw_rows = [ikw_t[IDX_DIM + h:IDX_DIM + h + 1, :] for h in range(IDX_HEADS)]
    lim = lim_ref[0, 0:1, :]
    sub = lax.broadcasted_iota(I32, (kt_w, qb), 0)

    def score_tile(kt, _):
        base = pl.multiple_of(kt * kt_w, kt_w)
        ik_t = ikb_ref[0, pl.ds(base, kt_w), :][:, :IDX_DIM]
        s = jnp.maximum(_dot(ik_t, iqs_ref[...]), 0.0)
        score = jnp.zeros((kt_w, qb), F32)
        for h in range(IDX_HEADS):
            score = score + s[:, h * qb:(h + 1) * qb] * iw_rows[h]
        keys_ref[pl.ds(base, kt_w), :] = jnp.where(sub + base < lim, _sort_key(score), NEG_INF_KEY)
        return 0

    lax.fori_loop(0, nkt, score_tile, 0, unroll=min(nkt, 2))

    def count(pred):
        acc = jnp.zeros((SUBLANES, qb), F32)
        for kt in range(nkt):
            acc = acc + _fold_rows(pred(keys_ref[kt * kt_w:(kt + 1) * kt_w, :], sub + kt * kt_w), jnp.add)
        return jnp.sum(acc, axis=0, keepdims=True)

    def count_ge(t_row):
        return count(lambda kk, idx: jnp.where(kk >= t_row, 1.0, 0.0))

    kf = float(topk)
    thr = jnp.where(count_ge(jnp.zeros((1, qb), I32)) >= kf, 0, INT_MIN).astype(I32)

    def thr_bit(i, t):
        cand = t + jnp.left_shift(jnp.int32(1), 30 - i)
        return jnp.where(count_ge(cand) >= kf, cand, t)

    thr = lax.fori_loop(0, 31, thr_bit, thr)
    n_gt = count_ge(thr + 1)
    n_eq = count_ge(thr) - n_gt
    need = kf - n_gt
    finite = thr > NEG_INF_KEY
    excess = jnp.where(finite, jnp.where(n_eq > need, 1.0, 0.0), 0.0)

    def count_eq_below(j_row):
        return count(lambda kk, idx: jnp.where(kk == thr, jnp.where(idx < j_row, 1.0, 0.0), 0.0))

    def resolve_ties():
        def bit(i, jc):
            cand = jc + jnp.left_shift(jnp.int32(1), idx_bits - 1 - i)
            return jnp.where(count_eq_below(cand) <= need - 1.0, cand, jc)
        jc = lax.fori_loop(0, idx_bits, bit, jnp.zeros((1, qb), I32))
        return jnp.where(finite, jc, -1)

    cut = lax.cond(jnp.max(excess) > 0.0, resolve_ties,
                   lambda: jnp.where(finite, 2 ** 30, -1).astype(I32))

    acc_ref[...] = jnp.zeros(acc_ref.shape, F32)
    gq = A_GROUP * qb

    def attend(base, width, carry):
        k_t = kb_ref[0, pl.ds(base, width), :]
        kk = keys_ref[pl.ds(base, width), :]
        tie = jnp.where(lax.broadcasted_iota(I32, (width, qb), 0) + base <= cut, 0.0, NEG_BIG)
        bias = jnp.where(kk > thr, 0.0, jnp.where(kk == thr, tie, NEG_BIG))
        bias = jnp.concatenate([bias] * A_GROUP, axis=1)
        out = []
        for n in range(A_KV_HEADS):
            m_old, l_old = carry[2 * n], carry[2 * n + 1]
            logits = bias + _dot(k_t[:, n * A_HEAD_DIM:(n + 1) * A_HEAD_DIM], qs_ref[n])
            m_new = jnp.maximum(m_old, jnp.max(_fold_rows(logits, jnp.maximum), axis=0, keepdims=True))
            alpha = jnp.exp2(m_old - m_new)
            p = jnp.exp2(logits - m_new)
            l_new = alpha * l_old + jnp.sum(_fold_rows(p, jnp.add), axis=0, keepdims=True)
            v_t = vt_ref[0, n * A_HEAD_DIM:(n + 1) * A_HEAD_DIM, pl.ds(base, width)]
            acc_ref[n] = alpha * acc_ref[n] + _dot(v_t, p.astype(BF16))
            out += [m_new, l_new]
        return tuple(out)

    wide = SOFTMAX_TILES * kt_w
    fin = (jnp.full((1, gq), NEG_BIG, F32), jnp.zeros((1, gq), F32)) * A_KV_HEADS
    fin = lax.fori_loop(0, nkt // SOFTMAX_TILES,
                        lambda i, c: attend(pl.multiple_of(i * wide, wide), wide, c), fin)
    for kt in range(nkt - nkt % SOFTMAX_TILES, nkt):
        fin = attend(kt * kt_w, kt_w, fin)
    l_row = [fin[2 * n + 1] for n in range(A_KV_HEADS)]

    o_t = jnp.concatenate([acc_ref[n] / l_row[n] for n in range(A_KV_HEADS)], axis=0)
    for n in range(A_KV_HEADS):
        for g in range(A_GROUP):
            h = n * A_GROUP + g
            blk = o_t[n * A_HEAD_DIM:(n + 1) * A_HEAD_DIM, g * qb:(g + 1) * qb]
            o_ref[:, h * A_HEAD_DIM:(h + 1) * A_HEAD_DIM] = blk.T.astype(BF16)


def _attention(aq, iq, ikw, limits, kb, vb, ikb, b, topk):
    qb, kt_w = Q_BLOCK, KEY_TILE
    tq = aq.shape[0] // b
    tq_pad = -(-tq // qb) * qb
    limits = np.asarray(limits)
    if tq_pad != tq:
        padq = lambda a: jnp.pad(a.reshape(b, tq, -1), ((0, 0), (0, tq_pad - tq), (0, 0))).reshape(b * tq_pad, -1)
        aq, iq, ikw = padq(aq), padq(iq), padq(ikw)
        limits = np.concatenate([limits, np.full((tq_pad - tq,), limits[-1])])
    l_all = kb.shape[1]
    l_pad = -(-l_all // kt_w) * kt_w
    if l_pad != l_all:
        padl = lambda a: jnp.pad(a, ((0, 0), (0, l_pad - l_all), (0, 0)))
        kb, vb, ikb = padl(kb), padl(vb), padl(ikb)
    vt = jnp.swapaxes(vb, 1, 2)
    nq = tq_pad // qb
    lim_blk = limits.reshape(nq, qb)
    nkt = np.minimum(-(-lim_blk.max(axis=1) // kt_w), l_pad // kt_w)
    lim = jnp.asarray(np.broadcast_to(lim_blk[:, None, :], (nq, SUBLANES, qb)).astype(np.int32))
    idx_bits = max(1, int(math.ceil(math.log2(l_pad))))
    keys = lambda a: pl.BlockSpec((1,) + a.shape[1:], lambda bi, j: (bi, 0, 0))
    gq = A_GROUP * qb
    runs, j0 = [], 0
    for j in range(1, nq + 1):
        if j == nq or nkt[j] != nkt[j0]:
            runs.append((j0, j - j0, int(nkt[j0])))
            j0 = j
    outs = []
    for j0, nj, n_tiles in runs:
        qrow = lambda w, j0=j0: pl.BlockSpec((qb, w), lambda bi, j: (bi * nq + j0 + j, 0))
        outs.append(pl.pallas_call(
            functools.partial(_attn_kernel, kt_w=kt_w, nkt=n_tiles, topk=topk, idx_bits=idx_bits),
            grid=(b, nj),
            in_specs=[qrow(W_AQ), qrow(W_IQ), qrow(LANES),
                      pl.BlockSpec((1, SUBLANES, qb), lambda bi, j, j0=j0: (j0 + j, 0, 0)),
                      keys(kb), keys(vt), keys(ikb)],
            out_specs=pl.BlockSpec((qb, W_AQ), lambda bi, j, nj=nj: (bi * nj + j, 0)),
            out_shape=jax.ShapeDtypeStruct((b * nj * qb, W_AQ), BF16),
            scratch_shapes=[pltpu.VMEM((n_tiles * kt_w, qb), I32),
                            pltpu.VMEM((A_KV_HEADS, A_HEAD_DIM, gq), BF16),
                            pltpu.VMEM((IDX_DIM, IDX_HEADS * qb), BF16),
                            pltpu.VMEM((A_KV_HEADS, A_HEAD_DIM, gq), F32)],
            compiler_params=_params(("parallel", "arbitrary")),
            name="attn",
        )(aq, iq, ikw, lim, kb, vt, ikb).reshape(b, nj * qb, W_AQ))
    o = outs[0] if len(outs) == 1 else jnp.concatenate(outs, axis=1)
    return o[:, :tq].reshape(b * tq, W_AQ)


def _gla_constants(c):
    nlev = int(math.log2(c))
    t = np.arange(c)
    mats = [(t[None, :] <= t[:, None])]
    masks = [np.eye(c, dtype=bool)]
    for lev in range(nlev):
        m = c >> (lev + 1)
        ref_row = (t // (2 * m)) * 2 * m + m
        mats.append(t[None, :] <= ref_row[:, None])
        upper = (t & m) != 0
        same = (t[:, None] // (2 * m)) == (t[None, :] // (2 * m))
        masks.append(same & upper[:, None] & ~upper[None, :])
    return (jnp.asarray(np.concatenate(mats, 0).astype(np.float32), BF16),
            jnp.asarray(np.stack(masks).astype(np.float32)), nlev)


def _gla_kernel(q_ref, k_ref, v_ref, g_ref, gr_ref, gn_ref, mst_ref, msk_ref, s0_ref,
                ob_ref, sfin_ref, st_ref, *, c, nlev, nb):
    i = pl.program_id(1)
    hk = G_KEY_DIM

    @pl.when(i == 0)
    def _():
        for s in range(nb):
            for h in range(G_HEADS):
                st_ref[s, h] = s0_ref[s, h].T

    mst = mst_ref[...]
    gn = gn_ref[...]
    hs = lambda a, h: a[:, h * hk:(h + 1) * hk]
    for s in range(nb):
        g = g_ref[s]
        g_hi = g.astype(BF16)
        r1 = g - g_hi.astype(F32)
        g_mid = r1.astype(BF16)
        g_lo = (r1 - g_mid.astype(F32)).astype(BF16)
        bs = _dot(mst, g_hi) + _dot(mst, g_mid) + _dot(mst, g_lo)
        b = bs[0:c]
        q = q_ref[s]
        k = k_ref[s]
        vb = v_ref[s].astype(BF16)
        row = lax.broadcasted_iota(I32, q.shape, 0)

        qb = q.astype(BF16)
        kb = k.astype(BF16)
        attn = [_dot_nt(hs(qb, h), hs(kb, h)) * msk_ref[0] for h in range(G_HEADS)]
        for lev in range(nlev):
            m = c >> (lev + 1)
            upper = (row & m) != 0
            d = b - bs[(lev + 1) * c:(lev + 2) * c]
            e = jnp.exp(jnp.where(upper, d, -d))
            qt = jnp.where(upper, q * e, 0.0).astype(BF16)
            kt = jnp.where(upper, 0.0, k * e).astype(BF16)
            mk = msk_ref[lev + 1]
            for h in range(G_HEADS):
                attn[h] = attn[h] + _dot_nt(hs(qt, h), hs(kt, h)) * mk

        qe = (q * jnp.exp(b)).astype(BF16)
        b_last = b[c - 1:c, :]
        khat = (k * jnp.exp(b_last - b)).astype(BF16)
        dec = jnp.exp(b_last)
        gr = gr_ref[s]
        for h in range(G_HEADS):
            st = st_ref[s, h]
            o = _dot_nt(hs(qe, h), st.astype(BF16)) + _dot(attn[h].astype(BF16), hs(vb, h))
            st_ref[s, h] = st * hs(dec, h) + _dot_tn(hs(vb, h), hs(khat, h))
            ms = jnp.mean(o * o, axis=1, keepdims=True)
            grh = hs(gr, h)
            of = o * lax.rsqrt(ms + LN_EPS) * hs(gn, h) * (grh / (1.0 + jnp.exp(-grh)))
            ob_ref[s, :, h * hk:(h + 1) * hk] = of.astype(BF16)

    @pl.when(i == pl.num_programs(1) - 1)
    def _():
        for s in range(nb):
            for h in range(G_HEADS):
                sfin_ref[s, h] = st_ref[s, h].T


def _gla(gq, gk, gv, glog, gr, g_norm, s0, b, c):
    n = gq.shape[0]
    t = n // b
    nc = t // c
    nb = GLA_SEQS if b % GLA_SEQS == 0 else 1
    mst, msk, nlev = _gla_constants(c)
    seq = lambda a: a.reshape(b, t, W_G)
    row = pl.BlockSpec((nb, c, W_G), lambda bi, i: (bi, i, 0))
    const = lambda a: pl.BlockSpec(a.shape, lambda bi, i: (0,) * a.ndim)
    st_spec = pl.BlockSpec((nb, G_HEADS, G_KEY_DIM, G_VAL_DIM), lambda bi, i: (bi, 0, 0, 0))
    gn = g_norm.reshape(1, W_G)
    ob, s_fin = pl.pallas_call(
        functools.partial(_gla_kernel, c=c, nlev=nlev, nb=nb),
        grid=(b // nb, nc),
        in_specs=[row, row, row, row, row, const(gn), const(mst), const(msk), st_spec],
        out_specs=[row, st_spec],
        out_shape=[jax.ShapeDtypeStruct((b, t, W_G), BF16),
                   jax.ShapeDtypeStruct((b, G_HEADS, G_KEY_DIM, G_VAL_DIM), F32)],
        scratch_shapes=[pltpu.VMEM((nb, G_HEADS, G_VAL_DIM, G_KEY_DIM), F32)],
        compiler_params=_params(("parallel", "arbitrary")),
        name="gla",
    )(seq(gq), seq(gk), seq(gv), seq(glog), seq(gr), gn, mst, msk, s0)
    return ob.reshape(n, W_G), s_fin


def _layer_norm(z, g, b):
    mu = jnp.mean(z, axis=1, keepdims=True)
    zc = z - mu
    var = jnp.mean(zc * zc, axis=1, keepdims=True)
    return zc * lax.rsqrt(var + LN_EPS) * g + b


def _sigmoid(x):
    return 1.0 / (1.0 + jnp.exp(-x))


def _merge_kernel(x_ref, oa_ref, ob_ref, gate_ref, wpa_ref, wpb_ref, wout_ref, g1_ref, b1_ref,
                  wpq_ref, pk1_ref, pk2_ref, h1_ref, h1b_ref, s1t_ref, s2t_ref, *, alpha, d):
    ya = _dot(oa_ref[...], wpa_ref[...])
    yb = _dot(ob_ref[...], wpb_ref[...])
    m = _sigmoid(gate_ref[:, :d]) * ya + _sigmoid(gate_ref[:, d:]) * yb
    mix = _dot(m.astype(BF16), wout_ref[...])
    h1 = _layer_norm(alpha * x_ref[...] + mix, g1_ref[...], b1_ref[...])
    h1_ref[...] = h1
    h1b_ref[...] = h1.T.astype(BF16)
    qp = _dot(h1.astype(BF16), wpq_ref[...])
    for h in range(P_HEADS):
        for half, (pk_ref, st_ref) in enumerate(((pk1_ref, s1t_ref), (pk2_ref, s2t_ref))):
            c0 = (2 * h + half) * P_HALF
            st_ref[h * P_NKEYS:(h + 1) * P_NKEYS, :] = _dot_nt(pk_ref[h], qp[:, c0:c0 + P_HALF].astype(BF16))


def _merge(x2, oa, ob, gate, w_pa, w_pb, w_out, ln_g, ln_b, w_pq, pk1, pk2, alpha, tm):
    n, d = x2.shape
    hk = P_HEADS * P_NKEYS
    row = lambda w: pl.BlockSpec((tm, w), lambda i: (i, 0))
    col = pl.BlockSpec((hk, tm), lambda i: (0, i))
    const = lambda a: pl.BlockSpec(a.shape, lambda i: (0,) * a.ndim)
    ws = [w_pa.astype(BF16), w_pb.astype(BF16), w_out.astype(BF16), ln_g.reshape(1, d), ln_b.reshape(1, d),
          w_pq.astype(BF16), pk1.astype(BF16), pk2.astype(BF16)]
    return pl.pallas_call(
        functools.partial(_merge_kernel, alpha=alpha, d=d),
        grid=(n // tm,),
        in_specs=[row(d), row(W_AQ), row(W_G), row(2 * d)] + [const(w) for w in ws],
        out_specs=[row(d), pl.BlockSpec((d, tm), lambda i: (0, i)), col, col],
        out_shape=[jax.ShapeDtypeStruct((n, d), F32), jax.ShapeDtypeStruct((d, n), BF16),
                   jax.ShapeDtypeStruct((hk, n), F32), jax.ShapeDtypeStruct((hk, n), F32)],
        compiler_params=_params(("parallel",)),
        name="merge",
    )(x2, oa, ob, gate, *ws)


def _top_desc(s, count, want_rank=False):
    tops = []
    rank = jnp.full(s.shape, float(count), F32) if want_rank else None
    for r in range(count):
        m = jnp.max(_fold_rows(s, jnp.maximum), axis=0, keepdims=True)
        tops.append(m)
        hit = s == m
        if want_rank:
            rank = jnp.where(hit, float(r), rank)
        if r + 1 < count:
            s = jnp.where(hit, -jnp.inf, s)
    return tops, rank


def _select_kernel(s1t_ref, s2t_ref, cnt_ref, e1_ref, rank_ref, e2_ref):
    nk = P_NKEYS

    def head(h, _):
        r0 = pl.multiple_of(h * nk, nk)
        rows = pl.ds(r0, nk)
        s1 = s1t_ref[rows, :]
        s2 = s2t_ref[rows, :]
        v1, _ = _top_desc(s1, P_TOPK)
        v2, rank = _top_desc(s2, P_TOPK, want_rank=True)
        pairs = [(a, b) for a in range(P_TOPK) for b in range(P_TOPK // (a + 1))]
        fill = [jnp.full_like(v1[0], -jnp.inf)] * (-len(pairs) % SUBLANES)
        cand = jnp.concatenate([v1[a] + v2[b] for a, b in pairs] + fill, axis=0)
        work, seen = cand, jnp.zeros_like(v1[0])
        tau = jnp.full_like(v1[0], -jnp.inf)
        for _ in range(P_TOPK):
            m = jnp.max(work, axis=0, keepdims=True)
            hit = work == m
            seen = seen + jnp.sum(jnp.where(hit, 1.0, 0.0), axis=0, keepdims=True)
            tau = jnp.maximum(tau, jnp.where(seen >= float(P_TOPK), m, -jnp.inf))
            work = jnp.where(hit, -jnp.inf, work)
        cmax = v1[0] + v2[0]
        zsum = jnp.sum(jnp.where(cand >= tau, jnp.exp(cand - cmax), 0.0), axis=0, keepdims=True)
        v2all = jnp.concatenate(v2, axis=0)
        cnt = jnp.zeros(s1.shape, F32)
        for a in range(P_TOPK):
            cnt_a = jnp.sum(jnp.where(v1[a] + v2all >= tau, 1.0, 0.0), axis=0, keepdims=True)
            cnt = jnp.where(s1 == v1[a], cnt_a, cnt)
        e1 = jnp.exp(s1 - v1[0]) / zsum * 0.5
        for st in range(nk // I1_PER_STEP):
            dst = pl.ds(pl.multiple_of(st * P_HEADS * I1_PER_STEP + h * I1_PER_STEP, I1_PER_STEP), I1_PER_STEP)
            src = slice(st * I1_PER_STEP, (st + 1) * I1_PER_STEP)
            cnt_ref[dst, :] = cnt[src, :]
            e1_ref[dst, :] = e1[src, :]
        rank_ref[rows, :] = rank
        e2_ref[rows, :] = jnp.exp(s2 - v2[0])
        return 0

    lax.fori_loop(0, P_HEADS, head, 0)


def _select(s1t, s2t, tn):
    hk, n = s1t.shape
    col = pl.BlockSpec((hk, tn), lambda i: (0, i))
    return pl.pallas_call(
        _select_kernel,
        grid=(n // tn,),
        in_specs=[col, col],
        out_specs=[col, col, col, col],
        out_shape=[jax.ShapeDtypeStruct((hk, n), F32)] * 4,
        compiler_params=_params(("parallel",)),
        name="select",
    )(s1t, s2t)


I1_PER_STEP = 8
E_PER_STEP = I1_PER_STEP * P_NKEYS
I2_BLOCK = 16


def _gelu_x2(x):
    return x * (1.0 + lax.erf(x * (2.0 ** -0.5)))


def _peer_steps(t, n_steps, ng):
    item = lambda d: jnp.clip(t - d, 0, n_steps - 1)
    return [(item(d) // ng, item(d) % ng) for d in range(3)]


def _peer_kernel(hb_ref, pu_ref, pvt_ref, cnt_ref, e1_ref, rank_ref, e2_ref, h1_ref, g2_ref, b2_ref,
                 y_ref, a_ref, gw_ref, acc_ref, rank_s, e2_s, *, tn, alpha, n_steps, ng):
    t = pl.program_id(0)
    (_, _), (_, g), (_, g_out) = _peer_steps(t, n_steps, ng)
    slot = t % 2
    prev = 1 - slot

    @pl.when(t == 0)
    def _():
        a_ref[...] = jnp.zeros(a_ref.shape, F32)
        gw_ref[...] = jnp.zeros(gw_ref.shape, BF16)

    @pl.when(g == 0)
    def _():
        rank_s[...] = rank_ref[...].astype(BF16)
        e2_s[...] = e2_ref[...].astype(BF16)

    @pl.when(g_out == 0)
    def _():
        acc_ref[...] = jnp.zeros(acc_ref.shape, F32)

    a_ref[slot] = _dot(pu_ref[...], hb_ref[...])

    for lt in range(tn // LANES):
        ls = slice(lt * LANES, (lt + 1) * LANES)
        for j in range(I1_PER_STEP):
            bcast = lambda ref, h: jnp.broadcast_to(
                ref[h * I1_PER_STEP + j:h * I1_PER_STEP + j + 1, ls], (I2_BLOCK, LANES)).astype(BF16)
            cnt = [bcast(cnt_ref, h) for h in range(P_HEADS)]
            e1 = [bcast(e1_ref, h) for h in range(P_HEADS)]
            for i2b in range(P_NKEYS // I2_BLOCK):
                w = jnp.zeros((I2_BLOCK, LANES), BF16)
                for h in range(P_HEADS):
                    rows = slice(h * P_NKEYS + i2b * I2_BLOCK, h * P_NKEYS + (i2b + 1) * I2_BLOCK)
                    w = w + jnp.where(rank_s[rows, ls] < cnt[h], e2_s[rows, ls] * e1[h], jnp.zeros((), BF16))
                arow = slice(j * P_NKEYS + i2b * I2_BLOCK, j * P_NKEYS + (i2b + 1) * I2_BLOCK)
                gw_ref[prev, arow, ls] = w * _gelu_x2(a_ref[prev, arow, ls]).astype(BF16)

    acc_ref[...] += _dot(pvt_ref[0], gw_ref[slot])

    @pl.when(jnp.logical_and(g_out == ng - 1, t >= 2))
    def _():
        y_ref[...] = _layer_norm(alpha * h1_ref[...] + acc_ref[...].T, g2_ref[...], b2_ref[...])


def _peer(h1, h1b, cnt, e1, rank, e2, pu, pv, ln_g, ln_b, alpha, tn):
    n, d = h1.shape
    hk = P_HEADS * P_NKEYS
    ng = pu.shape[0] // E_PER_STEP
    n_steps = (n // tn) * ng
    stage = lambda k, f: (lambda t: f(*_peer_steps(t, n_steps, ng)[k]))
    const = pl.BlockSpec((1, d), lambda t: (0, 0))
    routing = pl.BlockSpec((hk, tn), stage(1, lambda i, g: (0, i)))
    step_rows = pl.BlockSpec((P_HEADS * I1_PER_STEP, tn), stage(1, lambda i, g: (g, i)))
    pvt = pv.astype(BF16).reshape(ng, E_PER_STEP, d).transpose(0, 2, 1)
    return pl.pallas_call(
        functools.partial(_peer_kernel, tn=tn, alpha=alpha, n_steps=n_steps, ng=ng),
        grid=(n_steps + 2,),
        in_specs=[pl.BlockSpec((d, tn), stage(0, lambda i, g: (0, i))),
                  pl.BlockSpec((E_PER_STEP, d), stage(0, lambda i, g: (g, 0))),
                  pl.BlockSpec((1, d, E_PER_STEP), stage(2, lambda i, g: (g, 0, 0))),
                  step_rows, step_rows, routing, routing,
                  pl.BlockSpec((tn, d), stage(2, lambda i, g: (i, 0))), const, const],
        out_specs=pl.BlockSpec((tn, d), stage(2, lambda i, g: (i, 0))),
        out_shape=jax.ShapeDtypeStruct((n, d), F32),
        scratch_shapes=[pltpu.VMEM((2, E_PER_STEP, tn), F32), pltpu.VMEM((2, E_PER_STEP, tn), BF16),
                        pltpu.VMEM((d, tn), F32), pltpu.VMEM((hk, tn), BF16), pltpu.VMEM((hk, tn), BF16)],
        compiler_params=_params(("arbitrary",)),
        name="peer",
    )(h1b, pu.astype(BF16), pvt, cnt, e1, rank, e2, h1, ln_g.reshape(1, d), ln_b.reshape(1, d))


def _pick_tile(n, pref):
    t = min(n, pref)
    assert n % t == 0
    return t


def _layer(x, pos, limits, past, s0, w, *, chunk, alpha):
    b, t, d = x.shape
    n = b * t
    x2 = x.reshape(n, d)
    tm = _pick_tile(n, TOKEN_TILE)
    (aq, k32, v32, kb, vb, iq, ikw, ikb, gq, gk, gv, glog, gr, gate) = _proj(
        x2, pos, w["w_in"], w["w_fa"], w["b_fa"], tm)

    kb3, vb3, ikb3 = kb.reshape(b, t, LANES), vb.reshape(b, t, LANES), ikb.reshape(b, t, LANES)
    if past is not None:
        ck, cv, cik = past
        p = ck.shape[1]
        kb3 = jnp.concatenate([ck.reshape(b, p, LANES).astype(BF16), kb3], axis=1)
        vb3 = jnp.concatenate([cv.reshape(b, p, LANES).astype(BF16), vb3], axis=1)
        ikb3 = jnp.concatenate([cik.astype(BF16), ikb3[:, :, :IDX_DIM]], axis=1)
    topk = min(IDX_TOPK, kb3.shape[1] // 4)
    o_a = _attention(aq, iq, ikw, limits, kb3, vb3, ikb3, b, topk)

    o_b, s_fin = _gla(gq, gk, gv, glog, gr, w["g_gla_norm"], s0, b, chunk)

    h1, h1b, s1t, s2t = _merge(x2, o_a, o_b, gate, w["w_pa"], w["w_pb"], w["w_out"],
                               w["ln1_g"], w["ln1_b"], w["w_pq"], w["pk1"], w["pk2"], alpha, tm)
    cnt, e1, rank, e2 = _select(s1t, s2t, tm)
    y = _peer(h1, h1b, cnt, e1, rank, e2, w["pu"], w["pv"], w["ln2_g"], w["ln2_b"], alpha,
              _pick_tile(n, PEER_TOKEN_TILE))

    k_out = k32.reshape(b, t, A_KV_HEADS, A_HEAD_DIM)
    v_out = v32.reshape(b, t, A_KV_HEADS, A_HEAD_DIM)
    ik_out = ikw[:, :IDX_DIM].reshape(b, t, IDX_DIM)
    return y.reshape(b, t, d), k_out, v_out, ik_out, s_fin


def kernel(x_prompt, x_sample, cache_k, cache_v, cache_idx_k, state_gla, w_in, w_fa, b_fa, g_gla_norm,
           w_pa, w_pb, w_out, ln1_g, ln1_b, w_pq, pk1, pk2, pu, pv, ln2_g, ln2_b):
    depth = w_in.shape[0]
    alpha = (2.0 * depth) ** 0.25
    bp, tp, _ = x_prompt.shape
    bs, ts, _ = x_sample.shape
    past_len = cache_k.shape[2]
    pos_p = jnp.arange(tp)
    pos_s = past_len + jnp.arange(ts)
    lim_p = (np.arange(tp) // CHUNK + 1) * CHUNK
    lim_s = np.full((ts,), past_len + ts)
    names = ("w_in", "w_fa", "b_fa", "g_gla_norm", "w_pa", "w_pb", "w_out", "ln1_g", "ln1_b",
             "w_pq", "pk1", "pk2", "pu", "pv", "ln2_g", "ln2_b")
    stacked = (w_in, w_fa, b_fa, g_gla_norm, w_pa, w_pb, w_out, ln1_g, ln1_b, w_pq, pk1, pk2, pu, pv, ln2_g, ln2_b)
    hp, hs = x_prompt, x_sample
    outs_p, outs_s = [], []
    for l in range(depth):
        w = {nm: a[l] for nm, a in zip(names, stacked)}
        s0 = jnp.zeros((bp, G_HEADS, G_KEY_DIM, G_VAL_DIM), F32)
        hp, *rest = _layer(hp, pos_p, lim_p, None, s0, w, chunk=CHUNK, alpha=alpha)
        outs_p.append(rest)
        hs, *rest = _layer(hs, pos_s, lim_s, (cache_k[l], cache_v[l], cache_idx_k[l]), state_gla[l], w,
                           chunk=ts, alpha=alpha)
        outs_s.append(rest)
    stack = lambda outs, i: jnp.stack([o[i] for o in outs])
    return (hp, hs, stack(outs_p, 0), stack(outs_p, 1), stack(outs_p, 2), stack(outs_p, 3),
            stack(outs_s, 0), stack(outs_s, 1), stack(outs_s, 2), stack(outs_s, 3))
```

```python
import functools
import math

import numpy as np
import jax
import jax.numpy as jnp
from jax import lax
from jax.experimental import pallas as pl
from jax.experimental.pallas import tpu as pltpu

F32 = jnp.float32
BF16 = jnp.bfloat16
I32 = jnp.int32

LANES = 128
SUBLANES = 8
VMEM_LIMIT = 56 << 20

CHUNK = 64
A_HEADS = 8
A_KV_HEADS = 2
A_HEAD_DIM = 64
A_GROUP = A_HEADS // A_KV_HEADS
IDX_HEADS = 4
IDX_DIM = 64
IDX_TOPK = 256
ROPE_THETA = 10000.0
G_HEADS = 4
G_KEY_DIM = 128
G_VAL_DIM = 128
G_LOWRANK = 16
G_TAU = 16.0
P_HEADS = 8
P_NKEYS = 128
P_HALF = 128
P_TOPK = 16
LN_EPS = 1e-5

W_AQ = A_HEADS * A_HEAD_DIM
W_AK = A_KV_HEADS * A_HEAD_DIM
W_IQ = IDX_HEADS * IDX_DIM
W_G = G_HEADS * G_KEY_DIM
IN_SIZES = (W_AQ, W_AK, W_AK, W_IQ, IDX_DIM, IDX_HEADS, W_G, W_G, W_G, G_LOWRANK, W_G, None)

TOKEN_TILE = 256
MERGE_TOKEN_TILE = 512
PEER_TOKEN_TILE = 512
Q_BLOCK = 128
KEY_TILE = 512
SOFTMAX_TILES = 2
GLA_SEQS = 4

INT_MIN = -(2 ** 31)
NEG_INF_KEY = -2139095041
NEG_BIG = -1e30
LOG2_E = 1.4426950408889634


def _dot(a, b):
    return jnp.dot(a, b, preferred_element_type=F32)


def _dot_nt(a, b):
    return lax.dot_general(a, b, (((1,), (1,)), ((), ())), preferred_element_type=F32)


def _dot_tn(a, b):
    return lax.dot_general(a, b, (((0,), (0,)), ((), ())), preferred_element_type=F32)


def _sort_key(x):
    bits = pltpu.bitcast(x, I32)
    key = jnp.where(bits < 0, bits ^ 0x7FFFFFFF, bits)
    return jnp.where(key == -1, 0, key)


def _fold_rows(x, op):
    x = x.reshape(x.shape[0] // SUBLANES, SUBLANES, x.shape[1])
    while x.shape[0] > 1:
        half = x.shape[0] // 2
        folded = op(x[:half], x[half:2 * half])
        x = folded if x.shape[0] == 2 * half else jnp.concatenate([folded, x[2 * half:]], axis=0)
    return x[0]


def _params(sem):
    return pltpu.CompilerParams(dimension_semantics=sem, vmem_limit_bytes=VMEM_LIMIT)


_PG_AQ, _PG_K, _PG_V, _PG_IQ, _PG_IKW, _PG_GQ, _PG_GK, _PG_GV, _PG_GF, _PG_GR, _PG_GATE = range(11)


def _pack_layout(d_model):
    widths = [W_AQ, W_AK, W_AK, W_IQ, LANES, W_G, W_G, W_G, LANES, W_G, 2 * d_model]
    offs = np.concatenate([[0], np.cumsum(widths)]).tolist()
    return widths, offs


def _pack_w_in(w_in):
    d = w_in.shape[0]
    sizes = list(IN_SIZES[:-1]) + [2 * d]
    cuts = np.cumsum(sizes)[:-1].tolist()
    aq, ak, av, iq, ik, iw, gq, gk, gv, gf, gr, gate = jnp.split(w_in, cuts, axis=1)
    z = lambda n: jnp.zeros((d, n), w_in.dtype)
    ikw = jnp.concatenate([ik, iw, z(LANES - IDX_DIM - IDX_HEADS)], axis=1)
    gfp = jnp.concatenate([gf, z(LANES - G_LOWRANK)], axis=1)
    return jnp.concatenate([aq, ak, av, iq, ikw, gq, gk, gv, gfp, gr, gate], axis=1).astype(BF16)


def _rope_tables(pos):
    half = A_HEAD_DIM // 2
    inv = ROPE_THETA ** (-jnp.arange(half, dtype=F32) / half)
    ang = pos.astype(F32)[:, None] * inv[None, :]
    c, s = jnp.cos(ang), jnp.sin(ang)
    return jnp.concatenate([c, c, c, c], -1), jnp.concatenate([-s, s, -s, s], -1)


def _proj_kernel(x_ref, w_ref, wfa_ref, bfa_ref, cos_ref, sin_ref,
                 aq_ref, k_ref, v_ref, kb_ref, vb_ref, iq_ref, ikw_ref, ikb_ref,
                 gq_ref, gk_ref, gv_ref, glog_ref, gr_ref, gate_ref, *, offs, widths):
    xb = x_ref[...].astype(BF16)
    cos = cos_ref[...]
    sin = sin_ref[...]
    lane = lax.broadcasted_iota(I32, cos.shape, 1)
    first_half = (lane & (A_HEAD_DIM // 2)) == 0

    def proj(g):
        return _dot(xb, w_ref[:, offs[g]:offs[g] + widths[g]])

    def rope_slab(y):
        fwd = pltpu.roll(y, LANES - A_HEAD_DIM // 2, 1)
        bwd = pltpu.roll(y, A_HEAD_DIM // 2, 1)
        return y * cos + jnp.where(first_half, fwd, bwd) * sin

    def rope(y):
        return [rope_slab(y[:, s * LANES:(s + 1) * LANES]) for s in range(y.shape[1] // LANES)]

    for s, slab in enumerate(rope(proj(_PG_AQ))):
        aq_ref[:, s * LANES:(s + 1) * LANES] = (slab * (A_HEAD_DIM ** -0.5 * LOG2_E)).astype(BF16)
    k = rope(proj(_PG_K))[0]
    k_ref[...] = k
    kb_ref[...] = k.astype(BF16)
    v = proj(_PG_V)
    v_ref[...] = v
    vb_ref[...] = v.astype(BF16)
    for s, slab in enumerate(rope(proj(_PG_IQ))):
        iq_ref[:, s * LANES:(s + 1) * LANES] = (slab * (IDX_DIM ** -0.5)).astype(BF16)
    raw = proj(_PG_IKW)
    ikw = jnp.where(lane < IDX_DIM, rope_slab(raw), raw * (IDX_HEADS ** -0.5))
    ikw_ref[...] = ikw
    ikb_ref[...] = ikw.astype(BF16)
    gq_ref[...] = proj(_PG_GQ) * (G_KEY_DIM ** -0.5)
    gk_ref[...] = proj(_PG_GK)
    gv_ref[...] = proj(_PG_GV)
    z = _dot(proj(_PG_GF).astype(BF16), wfa_ref[...]) + bfa_ref[...]
    glog_ref[...] = (jnp.minimum(z, 0.0) - jnp.log1p(jnp.exp(-jnp.abs(z)))) * (1.0 / G_TAU)
    gr_ref[...] = proj(_PG_GR)
    gate_ref[...] = proj(_PG_GATE)


def _proj(x2, pos, w_in, w_fa, b_fa, tm):
    n, d = x2.shape
    t = pos.shape[0]
    widths, offs = _pack_layout(d)
    wp = _pack_w_in(w_in)
    wfa = jnp.concatenate([w_fa, jnp.zeros((LANES - G_LOWRANK, W_G), w_fa.dtype)], 0).astype(BF16)
    cos, sin = _rope_tables(pos)
    if tm > t:
        cos, sin = jnp.tile(cos, (tm // t, 1)), jnp.tile(sin, (tm // t, 1))
    nper = cos.shape[0] // tm
    row = lambda w: pl.BlockSpec((tm, w), lambda i: (i, 0))
    const = lambda a: pl.BlockSpec(a.shape, lambda i: (0, 0))
    tab = pl.BlockSpec((tm, LANES), lambda i: (i % nper, 0))
    outs = [(W_AQ, BF16), (LANES, F32), (LANES, F32), (LANES, BF16), (LANES, BF16), (W_IQ, BF16),
            (LANES, F32), (LANES, BF16), (W_G, F32), (W_G, F32), (W_G, F32), (W_G, F32), (W_G, F32),
            (2 * d, F32)]
    bfa = b_fa.reshape(1, W_G)
    return pl.pallas_call(
        functools.partial(_proj_kernel, offs=offs, widths=widths),
        grid=(n // tm,),
        in_specs=[row(d), const(wp), const(wfa), const(bfa), tab, tab],
        out_specs=[row(w) for w, _ in outs],
        out_shape=[jax.ShapeDtypeStruct((n, w), dt) for w, dt in outs],
        compiler_params=_params(("parallel",)),
        name="proj",
    )(x2, wp, wfa, bfa, cos, sin)


def _attn_kernel(aq_ref, iq_ref, ikw_ref, lim_ref, kb_ref, vt_ref, ikb_ref, o_ref,
                 keys_ref, qs_ref, iqs_ref, acc_ref, *, kt_w, nkt, topk, idx_bits):
    qb = Q_BLOCK

    aq_t = aq_ref[...].astype(F32).T
    for h in range(A_HEADS):
        qs_ref[h // A_GROUP, :, (h % A_GROUP) * qb:(h % A_GROUP + 1) * qb] = \
            aq_t[h * A_HEAD_DIM:(h + 1) * A_HEAD_DIM, :].astype(BF16)
    iq_t = iq_ref[...].astype(F32).T
    for h in range(IDX_HEADS):
        iqs_ref[:, h * qb:(h + 1) * qb] = iq_t[h * IDX_DIM:(h + 1) * IDX_DIM, :].astype(BF16)

    ikw_t = ikw_ref[...].T
    iw_rows = [ikw_t[IDX_DIM + h:IDX_DIM + h + 1, :] for h in range(IDX_HEADS)]
    lim = lim_ref[0, 0:1, :]
    sub = lax.broadcasted_iota(I32, (kt_w, qb), 0)

    def score_tile(kt, _):
        base = pl.multiple_of(kt * kt_w, kt_w)
        ik_t = ikb_ref[0, pl.ds(base, kt_w), :][:, :IDX_DIM]
        s = jnp.maximum(_dot(ik_t, iqs_ref[...]), 0.0)
        score = jnp.zeros((kt_w, qb), F32)
        for h in range(IDX_HEADS):
            score = score + s[:, h * qb:(h + 1) * qb] * iw_rows[h]
        keys_ref[pl.ds(base, kt_w), :] = jnp.where(sub + base < lim, _sort_key(score), NEG_INF_KEY)
        return 0

    lax.fori_loop(0, nkt, score_tile, 0, unroll=min(nkt, 2))

    def count(pred):
        acc = jnp.zeros((SUBLANES, qb), F32)
        for kt in range(nkt):
            acc = acc + _fold_rows(pred(keys_ref[kt * kt_w:(kt + 1) * kt_w, :], sub + kt * kt_w), jnp.add)
        return jnp.sum(acc, axis=0, keepdims=True)

    def count_ge(t_row):
        return count(lambda kk, idx: jnp.where(kk >= t_row, 1.0, 0.0))

    kf = float(topk)
    thr = jnp.where(count_ge(jnp.zeros((1, qb), I32)) >= kf, 0, INT_MIN).astype(I32)

    def thr_bit(i, t):
        cand = t + jnp.left_shift(jnp.int32(1), 30 - i)
        return jnp.where(count_ge(cand) >= kf, cand, t)

    thr = lax.fori_loop(0, 31, thr_bit, thr)
    n_gt = count_ge(thr + 1)
    n_eq = count_ge(thr) - n_gt
    need = kf - n_gt
    finite = thr > NEG_INF_KEY
    excess = jnp.where(finite, jnp.where(n_eq > need, 1.0, 0.0), 0.0)

    def count_eq_below(j_row):
        return count(lambda kk, idx: jnp.where(kk == thr, jnp.where(idx < j_row, 1.0, 0.0), 0.0))

    def resolve_ties():
        def bit(i, jc):
            cand = jc + jnp.left_shift(jnp.int32(1), idx_bits - 1 - i)
            return jnp.where(count_eq_below(cand) <= need - 1.0, cand, jc)
        jc = lax.fori_loop(0, idx_bits, bit, jnp.zeros((1, qb), I32))
        return jnp.where(finite, jc, -1)

    cut = lax.cond(jnp.max(excess) > 0.0, resolve_ties,
                   lambda: jnp.where(finite, 2 ** 30, -1).astype(I32))

    acc_ref[...] = jnp.zeros(acc_ref.shape, F32)
    gq = A_GROUP * qb

    def attend(base, width, carry):
        k_t = kb_ref[0, pl.ds(base, width), :]
        kk = keys_ref[pl.ds(base, width), :]
        tie = jnp.where(lax.broadcasted_iota(I32, (width, qb), 0) + base <= cut, 0.0, NEG_BIG)
        bias = jnp.where(kk > thr, 0.0, jnp.where(kk == thr, tie, NEG_BIG))
        bias = jnp.concatenate([bias] * A_GROUP, axis=1)
        out = []
        for n in range(A_KV_HEADS):
            m_old, l_old = carry[2 * n], carry[2 * n + 1]
            logits = bias + _dot(k_t[:, n * A_HEAD_DIM:(n + 1) * A_HEAD_DIM], qs_ref[n])
            m_new = jnp.maximum(m_old, jnp.max(_fold_rows(logits, jnp.maximum), axis=0, keepdims=True))
            alpha = jnp.exp2(m_old - m_new)
            p = jnp.exp2(logits - m_new)
            l_new = alpha * l_old + jnp.sum(_fold_rows(p, jnp.add), axis=0, keepdims=True)
            v_t = vt_ref[0, n * A_HEAD_DIM:(n + 1) * A_HEAD_DIM, pl.ds(base, width)]
            acc_ref[n] = alpha * acc_ref[n] + _dot(v_t, p.astype(BF16))
            out += [m_new, l_new]
        return tuple(out)

    wide = SOFTMAX_TILES * kt_w
    fin = (jnp.full((1, gq), NEG_BIG, F32), jnp.zeros((1, gq), F32)) * A_KV_HEADS
    fin = lax.fori_loop(0, nkt // SOFTMAX_TILES,
                        lambda i, c: attend(pl.multiple_of(i * wide, wide), wide, c), fin)
    for kt in range(nkt - nkt % SOFTMAX_TILES, nkt):
        fin = attend(kt * kt_w, kt_w, fin)
    l_row = [fin[2 * n + 1] for n in range(A_KV_HEADS)]

    o_t = jnp.concatenate([acc_ref[n] / l_row[n] for n in range(A_KV_HEADS)], axis=0)
    for n in range(A_KV_HEADS):
        for g in range(A_GROUP):
            h = n * A_GROUP + g
            blk = o_t[n * A_HEAD_DIM:(n + 1) * A_HEAD_DIM, g * qb:(g + 1) * qb]
            o_ref[:, h * A_HEAD_DIM:(h + 1) * A_HEAD_DIM] = blk.T.astype(BF16)


def _attention(aq, iq, ikw, limits, kb, vb, ikb, b, topk):
    qb, kt_w = Q_BLOCK, KEY_TILE
    tq = aq.shape[0] // b
    tq_pad = -(-tq // qb) * qb
    limits = np.asarray(limits)
    if tq_pad != tq:
        padq = lambda a: jnp.pad(a.reshape(b, tq, -1), ((0, 0), (0, tq_pad - tq), (0, 0))).reshape(b * tq_pad, -1)
        aq, iq, ikw = padq(aq), padq(iq), padq(ikw)
        limits = np.concatenate([limits, np.full((tq_pad - tq,), limits[-1])])
    l_all = kb.shape[1]
    l_pad = -(-l_all // kt_w) * kt_w
    if l_pad != l_all:
        padl = lambda a: jnp.pad(a, ((0, 0), (0, l_pad - l_all), (0, 0)))
        kb, vb, ikb = padl(kb), padl(vb), padl(ikb)
    vt = jnp.swapaxes(vb, 1, 2)
    nq = tq_pad // qb
    lim_blk = limits.reshape(nq, qb)
    nkt = np.minimum(-(-lim_blk.max(axis=1) // kt_w), l_pad // kt_w)
    lim = jnp.asarray(np.broadcast_to(lim_blk[:, None, :], (nq, SUBLANES, qb)).astype(np.int32))
    idx_bits = max(1, int(math.ceil(math.log2(l_pad))))
    keys = lambda a: pl.BlockSpec((1,) + a.shape[1:], lambda bi, j: (bi, 0, 0))
    gq = A_GROUP * qb
    runs, j0 = [], 0
    for j in range(1, nq + 1):
        if j == nq or nkt[j] != nkt[j0]:
            runs.append((j0, j - j0, int(nkt[j0])))
            j0 = j
    outs = []
    for j0, nj, n_tiles in runs:
        qrow = lambda w, j0=j0: pl.BlockSpec((qb, w), lambda bi, j: (bi * nq + j0 + j, 0))
        outs.append(pl.pallas_call(
            functools.partial(_attn_kernel, kt_w=kt_w, nkt=n_tiles, topk=topk, idx_bits=idx_bits),
            grid=(b, nj),
            in_specs=[qrow(W_AQ), qrow(W_IQ), qrow(LANES),
                      pl.BlockSpec((1, SUBLANES, qb), lambda bi, j, j0=j0: (j0 + j, 0, 0)),
                      keys(kb), keys(vt), keys(ikb)],
            out_specs=pl.BlockSpec((qb, W_AQ), lambda bi, j, nj=nj: (bi * nj + j, 0)),
            out_shape=jax.ShapeDtypeStruct((b * nj * qb, W_AQ), BF16),
            scratch_shapes=[pltpu.VMEM((n_tiles * kt_w, qb), I32),
                            pltpu.VMEM((A_KV_HEADS, A_HEAD_DIM, gq), BF16),
                            pltpu.VMEM((IDX_DIM, IDX_HEADS * qb), BF16),
                            pltpu.VMEM((A_KV_HEADS, A_HEAD_DIM, gq), F32)],
            compiler_params=_params(("parallel", "arbitrary")),
            name="attn",
        )(aq, iq, ikw, lim, kb, vt, ikb).reshape(b, nj * qb, W_AQ))
    o = outs[0] if len(outs) == 1 else jnp.concatenate(outs, axis=1)
    return o[:, :tq].reshape(b * tq, W_AQ)


def _gla_constants(c):
    nlev = int(math.log2(c))
    t = np.arange(c)
    mats = [(t[None, :] <= t[:, None])]
    masks = [np.eye(c, dtype=bool)]
    for lev in range(nlev):
        m = c >> (lev + 1)
        ref_row = (t // (2 * m)) * 2 * m + m
        mats.append(t[None, :] <= ref_row[:, None])
        upper = (t & m) != 0
        same = (t[:, None] // (2 * m)) == (t[None, :] // (2 * m))
        masks.append(same & upper[:, None] & ~upper[None, :])
    return (jnp.asarray(np.concatenate(mats, 0).astype(np.float32), BF16),
            jnp.asarray(np.stack(masks).astype(np.float32)), nlev)


def _gla_kernel(q_ref, k_ref, v_ref, g_ref, gr_ref, gn_ref, mst_ref, msk_ref, s0_ref,
                ob_ref, sfin_ref, st_ref, *, c, nlev, nb):
    i = pl.program_id(1)
    hk = G_KEY_DIM

    @pl.when(i == 0)
    def _():
        for s in range(nb):
            for h in range(G_HEADS):
                st_ref[s, h] = s0_ref[s, h].T

    mst = mst_ref[...]
    gn = gn_ref[...]
    hs = lambda a, h: a[:, h * hk:(h + 1) * hk]
    for s in range(nb):
        g = g_ref[s]
        g_hi = g.astype(BF16)
        r1 = g - g_hi.astype(F32)
        g_mid = r1.astype(BF16)
        g_lo = (r1 - g_mid.astype(F32)).astype(BF16)
        bs = _dot(mst, g_hi) + _dot(mst, g_mid) + _dot(mst, g_lo)
        b = bs[0:c]
        q = q_ref[s]
        k = k_ref[s]
        vb = v_ref[s].astype(BF16)
        row = lax.broadcasted_iota(I32, q.shape, 0)

        qb = q.astype(BF16)
        kb = k.astype(BF16)
        attn = [_dot_nt(hs(qb, h), hs(kb, h)) * msk_ref[0] for h in range(G_HEADS)]
        for lev in range(nlev):
            m = c >> (lev + 1)
            upper = (row & m) != 0
            d = b - bs[(lev + 1) * c:(lev + 2) * c]
            e = jnp.exp(jnp.where(upper, d, -d))
            qt = jnp.where(upper, q * e, 0.0).astype(BF16)
            kt = jnp.where(upper, 0.0, k * e).astype(BF16)
            mk = msk_ref[lev + 1]
            for h in range(G_HEADS):
                attn[h] = attn[h] + _dot_nt(hs(qt, h), hs(kt, h)) * mk

        qe = (q * jnp.exp(b)).astype(BF16)
        b_last = b[c - 1:c, :]
        khat = (k * jnp.exp(b_last - b)).astype(BF16)
        dec = jnp.exp(b_last)
        gr = gr_ref[s]
        for h in range(G_HEADS):
            st = st_ref[s, h]
            o = _dot_nt(hs(qe, h), st.astype(BF16)) + _dot(attn[h].astype(BF16), hs(vb, h))
            st_ref[s, h] = st * hs(dec, h) + _dot_tn(hs(vb, h), hs(khat, h))
            ms = jnp.mean(o * o, axis=1, keepdims=True)
            grh = hs(gr, h)
            of = o * lax.rsqrt(ms + LN_EPS) * hs(gn, h) * (grh / (1.0 + jnp.exp(-grh)))
            ob_ref[s, :, h * hk:(h + 1) * hk] = of.astype(BF16)

    @pl.when(i == pl.num_programs(1) - 1)
    def _():
        for s in range(nb):
            for h in range(G_HEADS):
                sfin_ref[s, h] = st_ref[s, h].T


def _gla(gq, gk, gv, glog, gr, g_norm, s0, b, c):
    n = gq.shape[0]
    t = n // b
    nc = t // c
    nb = GLA_SEQS if b % GLA_SEQS == 0 else 1
    mst, msk, nlev = _gla_constants(c)
    seq = lambda a: a.reshape(b, t, W_G)
    row = pl.BlockSpec((nb, c, W_G), lambda bi, i: (bi, i, 0))
    const = lambda a: pl.BlockSpec(a.shape, lambda bi, i: (0,) * a.ndim)
    st_spec = pl.BlockSpec((nb, G_HEADS, G_KEY_DIM, G_VAL_DIM), lambda bi, i: (bi, 0, 0, 0))
    gn = g_norm.reshape(1, W_G)
    ob, s_fin = pl.pallas_call(
        functools.partial(_gla_kernel, c=c, nlev=nlev, nb=nb),
        grid=(b // nb, nc),
        in_specs=[row, row, row, row, row, const(gn), const(mst), const(msk), st_spec],
        out_specs=[row, st_spec],
        out_shape=[jax.ShapeDtypeStruct((b, t, W_G), BF16),
                   jax.ShapeDtypeStruct((b, G_HEADS, G_KEY_DIM, G_VAL_DIM), F32)],
        scratch_shapes=[pltpu.VMEM((nb, G_HEADS, G_VAL_DIM, G_KEY_DIM), F32)],
        compiler_params=_params(("parallel", "arbitrary")),
        name="gla",
    )(seq(gq), seq(gk), seq(gv), seq(glog), seq(gr), gn, mst, msk, s0)
    return ob.reshape(n, W_G), s_fin


def _layer_norm(z, g, b):
    mu = jnp.mean(z, axis=1, keepdims=True)
    zc = z - mu
    var = jnp.mean(zc * zc, axis=1, keepdims=True)
    return zc * lax.rsqrt(var + LN_EPS) * g + b


def _sigmoid(x):
    return 1.0 / (1.0 + jnp.exp(-x))


def _merge_kernel(x_ref, oa_ref, ob_ref, gate_ref, wpa_ref, wpb_ref, wout_ref, g1_ref, b1_ref,
                  wpq_ref, pk1_ref, pk2_ref, h1_ref, h1b_ref, s1t_ref, s2t_ref, *, alpha, d):
    ya = _dot(oa_ref[...], wpa_ref[...])
    yb = _dot(ob_ref[...], wpb_ref[...])
    m = _sigmoid(gate_ref[:, :d]) * ya + _sigmoid(gate_ref[:, d:]) * yb
    mix = _dot(m.astype(BF16), wout_ref[...])
    h1 = _layer_norm(alpha * x_ref[...] + mix, g1_ref[...], b1_ref[...])
    h1_ref[...] = h1
    h1b_ref[...] = h1.T.astype(BF16)
    qp = _dot(h1.astype(BF16), wpq_ref[...])
    for h in range(P_HEADS):
        for half, (pk_ref, st_ref) in enumerate(((pk1_ref, s1t_ref), (pk2_ref, s2t_ref))):
            c0 = (2 * h + half) * P_HALF
            st_ref[h * P_NKEYS:(h + 1) * P_NKEYS, :] = _dot_nt(pk_ref[h], qp[:, c0:c0 + P_HALF].astype(BF16))


def _merge(x2, oa, ob, gate, w_pa, w_pb, w_out, ln_g, ln_b, w_pq, pk1, pk2, alpha, tm):
    n, d = x2.shape
    hk = P_HEADS * P_NKEYS
    row = lambda w: pl.BlockSpec((tm, w), lambda i: (i, 0))
    col = pl.BlockSpec((hk, tm), lambda i: (0, i))
    const = lambda a: pl.BlockSpec(a.shape, lambda i: (0,) * a.ndim)
    ws = [w_pa.astype(BF16), w_pb.astype(BF16), w_out.astype(BF16), ln_g.reshape(1, d), ln_b.reshape(1, d),
          w_pq.astype(BF16), pk1.astype(BF16), pk2.astype(BF16)]
    return pl.pallas_call(
        functools.partial(_merge_kernel, alpha=alpha, d=d),
        grid=(n // tm,),
        in_specs=[row(d), row(W_AQ), row(W_G), row(2 * d)] + [const(w) for w in ws],
        out_specs=[row(d), pl.BlockSpec((d, tm), lambda i: (0, i)), col, col],
        out_shape=[jax.ShapeDtypeStruct((n, d), F32), jax.ShapeDtypeStruct((d, n), BF16),
                   jax.ShapeDtypeStruct((hk, n), F32), jax.ShapeDtypeStruct((hk, n), F32)],
        compiler_params=_params(("parallel",)),
        name="merge",
    )(x2, oa, ob, gate, *ws)


def _top_desc(s, count, want_rank=False):
    tops = []
    rank = jnp.full(s.shape, float(count), F32) if want_rank else None
    for r in range(count):
        m = jnp.max(_fold_rows(s, jnp.maximum), axis=0, keepdims=True)
        tops.append(m)
        hit = s == m
        if want_rank:
            rank = jnp.where(hit, float(r), rank)
        if r + 1 < count:
            s = jnp.where(hit, -jnp.inf, s)
    return tops, rank


def _select_kernel(s1t_ref, s2t_ref, cnt_ref, e1_ref, rank_ref, e2_ref):
    nk = P_NKEYS

    def head(h, _):
        r0 = pl.multiple_of(h * nk, nk)
        rows = pl.ds(r0, nk)
        s1 = s1t_ref[rows, :]
        s2 = s2t_ref[rows, :]
        v1, _ = _top_desc(s1, P_TOPK)
        v2, rank = _top_desc(s2, P_TOPK, want_rank=True)
        pairs = [(a, b) for a in range(P_TOPK) for b in range(P_TOPK // (a + 1))]
        fill = [jnp.full_like(v1[0], -jnp.inf)] * (-len(pairs) % SUBLANES)
        cand = jnp.concatenate([v1[a] + v2[b] for a, b in pairs] + fill, axis=0)
        work, seen = cand, jnp.zeros_like(v1[0])
        tau = jnp.full_like(v1[0], -jnp.inf)
        for _ in range(P_TOPK):
            m = jnp.max(work, axis=0, keepdims=True)
            hit = work == m
            seen = seen + jnp.sum(jnp.where(hit, 1.0, 0.0), axis=0, keepdims=True)
            tau = jnp.maximum(tau, jnp.where(seen >= float(P_TOPK), m, -jnp.inf))
            work = jnp.where(hit, -jnp.inf, work)
        cmax = v1[0] + v2[0]
        zsum = jnp.sum(jnp.where(cand >= tau, jnp.exp(cand - cmax), 0.0), axis=0, keepdims=True)
        v2all = jnp.concatenate(v2, axis=0)
        cnt = jnp.zeros(s1.shape, F32)
        for a in range(P_TOPK):
            cnt_a = jnp.sum(jnp.where(v1[a] + v2all >= tau, 1.0, 0.0), axis=0, keepdims=True)
            cnt = jnp.where(s1 == v1[a], cnt_a, cnt)
        e1 = jnp.exp(s1 - v1[0]) / zsum * 0.5
        for st in range(nk // I1_PER_STEP):
            dst = pl.ds(pl.multiple_of(st * P_HEADS * I1_PER_STEP + h * I1_PER_STEP, I1_PER_STEP), I1_PER_STEP)
            src = slice(st * I1_PER_STEP, (st + 1) * I1_PER_STEP)
            cnt_ref[dst, :] = cnt[src, :]
            e1_ref[dst, :] = e1[src, :]
        rank_ref[rows, :] = rank
        e2_ref[rows, :] = jnp.exp(s2 - v2[0])
        return 0

    lax.fori_loop(0, P_HEADS, head, 0)


def _select(s1t, s2t, tn):
    hk, n = s1t.shape
    col = pl.BlockSpec((hk, tn), lambda i: (0, i))
    return pl.pallas_call(
        _select_kernel,
        grid=(n // tn,),
        in_specs=[col, col],
        out_specs=[col, col, col, col],
        out_shape=[jax.ShapeDtypeStruct((hk, n), F32)] * 4,
        compiler_params=_params(("parallel",)),
        name="select",
    )(s1t, s2t)


I1_PER_STEP = 8
E_PER_STEP = I1_PER_STEP * P_NKEYS
I2_BLOCK = 16


def _gelu_x2(x):
    return x * (1.0 + lax.erf(x * (2.0 ** -0.5)))


def _peer_steps(t, n_steps, ng):
    item = lambda d: jnp.clip(t - d, 0, n_steps - 1)
    return [(item(d) // ng, item(d) % ng) for d in range(3)]


def _peer_kernel(hb_ref, pu_ref, pvt_ref, cnt_ref, e1_ref, rank_ref, e2_ref, h1_ref, g2_ref, b2_ref,
                 y_ref, a_ref, gw_ref, acc_ref, rank_s, e2_s, *, tn, alpha, n_steps, ng):
    t = pl.program_id(0)
    (_, _), (_, g), (_, g_out) = _peer_steps(t, n_steps, ng)
    slot = t % 2
    prev = 1 - slot

    @pl.when(t == 0)
    def _():
        a_ref[...] = jnp.zeros(a_ref.shape, F32)
        gw_ref[...] = jnp.zeros(gw_ref.shape, BF16)

    @pl.when(g == 0)
    def _():
        rank_s[...] = rank_ref[...].astype(BF16)
        e2_s[...] = e2_ref[...].astype(BF16)

    @pl.when(g_out == 0)
    def _():
        acc_ref[...] = jnp.zeros(acc_ref.shape, F32)

    a_ref[slot] = _dot(pu_ref[...], hb_ref[...])

    for lt in range(tn // LANES):
        ls = slice(lt * LANES, (lt + 1) * LANES)
        for j in range(I1_PER_STEP):
            bcast = lambda ref, h: jnp.broadcast_to(
                ref[h * I1_PER_STEP + j:h * I1_PER_STEP + j + 1, ls], (I2_BLOCK, LANES)).astype(BF16)
            cnt = [bcast(cnt_ref, h) for h in range(P_HEADS)]
            e1 = [bcast(e1_ref, h) for h in range(P_HEADS)]
            for i2b in range(P_NKEYS // I2_BLOCK):
                w = jnp.zeros((I2_BLOCK, LANES), BF16)
                for h in range(P_HEADS):
                    rows = slice(h * P_NKEYS + i2b * I2_BLOCK, h * P_NKEYS + (i2b + 1) * I2_BLOCK)
                    w = w + jnp.where(rank_s[rows, ls] < cnt[h], e2_s[rows, ls] * e1[h], jnp.zeros((), BF16))
                arow = slice(j * P_NKEYS + i2b * I2_BLOCK, j * P_NKEYS + (i2b + 1) * I2_BLOCK)
                gw_ref[prev, arow, ls] = w * _gelu_x2(a_ref[prev, arow, ls]).astype(BF16)

    acc_ref[...] += _dot(pvt_ref[0], gw_ref[slot])

    @pl.when(jnp.logical_and(g_out == ng - 1, t >= 2))
    def _():
        y_ref[...] = _layer_norm(alpha * h1_ref[...] + acc_ref[...].T, g2_ref[...], b2_ref[...])


def _peer(h1, h1b, cnt, e1, rank, e2, pu, pv, ln_g, ln_b, alpha, tn):
    n, d = h1.shape
    hk = P_HEADS * P_NKEYS
    ng = pu.shape[0] // E_PER_STEP
    n_steps = (n // tn) * ng
    stage = lambda k, f: (lambda t: f(*_peer_steps(t, n_steps, ng)[k]))
    const = pl.BlockSpec((1, d), lambda t: (0, 0))
    routing = pl.BlockSpec((hk, tn), stage(1, lambda i, g: (0, i)))
    step_rows = pl.BlockSpec((P_HEADS * I1_PER_STEP, tn), stage(1, lambda i, g: (g, i)))
    pvt = pv.astype(BF16).reshape(ng, E_PER_STEP, d).transpose(0, 2, 1)
    return pl.pallas_call(
        functools.partial(_peer_kernel, tn=tn, alpha=alpha, n_steps=n_steps, ng=ng),
        grid=(n_steps + 2,),
        in_specs=[pl.BlockSpec((d, tn), stage(0, lambda i, g: (0, i))),
                  pl.BlockSpec((E_PER_STEP, d), stage(0, lambda i, g: (g, 0))),
                  pl.BlockSpec((1, d, E_PER_STEP), stage(2, lambda i, g: (g, 0, 0))),
                  step_rows, step_rows, routing, routing,
                  pl.BlockSpec((tn, d), stage(2, lambda i, g: (i, 0))), const, const],
        out_specs=pl.BlockSpec((tn, d), stage(2, lambda i, g: (i, 0))),
        out_shape=jax.ShapeDtypeStruct((n, d), F32),
        scratch_shapes=[pltpu.VMEM((2, E_PER_STEP, tn), F32), pltpu.VMEM((2, E_PER_STEP, tn), BF16),
                        pltpu.VMEM((d, tn), F32), pltpu.VMEM((hk, tn), BF16), pltpu.VMEM((hk, tn), BF16)],
        compiler_params=_params(("arbitrary",)),
        name="peer",
    )(h1b, pu.astype(BF16), pvt, cnt, e1, rank, e2, h1, ln_g.reshape(1, d), ln_b.reshape(1, d))


def _pick_tile(n, pref):
    t = min(n, pref)
    assert n % t == 0
    return t


def _layer(x, pos, limits, past, s0, w, *, chunk, alpha):
    b, t, d = x.shape
    n = b * t
    x2 = x.reshape(n, d)
    tm = _pick_tile(n, TOKEN_TILE)
    (aq, k32, v32, kb, vb, iq, ikw, ikb, gq, gk, gv, glog, gr, gate) = _proj(
        x2, pos, w["w_in"], w["w_fa"], w["b_fa"], tm)

    kb3, vb3, ikb3 = kb.reshape(b, t, LANES), vb.reshape(b, t, LANES), ikb.reshape(b, t, LANES)
    if past is not None:
        ck, cv, cik = past
        p = ck.shape[1]
        kb3 = jnp.concatenate([ck.reshape(b, p, LANES).astype(BF16), kb3], axis=1)
        vb3 = jnp.concatenate([cv.reshape(b, p, LANES).astype(BF16), vb3], axis=1)
        ikb3 = jnp.concatenate([cik.astype(BF16), ikb3[:, :, :IDX_DIM]], axis=1)
    topk = min(IDX_TOPK, kb3.shape[1] // 4)
    o_a = _attention(aq, iq, ikw, limits, kb3, vb3, ikb3, b, topk)

    o_b, s_fin = _gla(gq, gk, gv, glog, gr, w["g_gla_norm"], s0, b, chunk)

    h1, h1b, s1t, s2t = _merge(x2, o_a, o_b, gate, w["w_pa"], w["w_pb"], w["w_out"],
                               w["ln1_g"], w["ln1_b"], w["w_pq"], w["pk1"], w["pk2"], alpha,
                               _pick_tile(n, MERGE_TOKEN_TILE))
    cnt, e1, rank, e2 = _select(s1t, s2t, tm)
    y = _peer(h1, h1b, cnt, e1, rank, e2, w["pu"], w["pv"], w["ln2_g"], w["ln2_b"], alpha,
              _pick_tile(n, PEER_TOKEN_TILE))

    k_out = k32.reshape(b, t, A_KV_HEADS, A_HEAD_DIM)
    v_out = v32.reshape(b, t, A_KV_HEADS, A_HEAD_DIM)
    ik_out = ikw[:, :IDX_DIM].reshape(b, t, IDX_DIM)
    return y.reshape(b, t, d), k_out, v_out, ik_out, s_fin


def kernel(x_prompt, x_sample, cache_k, cache_v, cache_idx_k, state_gla, w_in, w_fa, b_fa, g_gla_norm,
           w_pa, w_pb, w_out, ln1_g, ln1_b, w_pq, pk1, pk2, pu, pv, ln2_g, ln2_b):
    depth = w_in.shape[0]
    alpha = (2.0 * depth) ** 0.25
    bp, tp, _ = x_prompt.shape
    bs, ts, _ = x_sample.shape
    past_len = cache_k.shape[2]
    pos_p = jnp.arange(tp)
    pos_s = past_len + jnp.arange(ts)
    lim_p = (np.arange(tp) // CHUNK + 1) * CHUNK
    lim_s = np.full((ts,), past_len + ts)
    names = ("w_in", "w_fa", "b_fa", "g_gla_norm", "w_pa", "w_pb", "w_out", "ln1_g", "ln1_b",
             "w_pq", "pk1", "pk2", "pu", "pv", "ln2_g", "ln2_b")
    stacked = (w_in, w_fa, b_fa, g_gla_norm, w_pa, w_pb, w_out, ln1_g, ln1_b, w_pq, pk1, pk2, pu, pv, ln2_g, ln2_b)
    hp, hs = x_prompt, x_sample
    outs_p, outs_s = [], []
    for l in range(depth):
        w = {nm: a[l] for nm, a in zip(names, stacked)}
        s0 = jnp.zeros((bp, G_HEADS, G_KEY_DIM, G_VAL_DIM), F32)
        hp, *rest = _layer(hp, pos_p, lim_p, None, s0, w, chunk=CHUNK, alpha=alpha)
        outs_p.append(rest)
        hs, *rest = _layer(hs, pos_s, lim_s, (cache_k[l], cache_v[l], cache_idx_k[l]), state_gla[l], w,
                           chunk=ts, alpha=alpha)
        outs_s.append(rest)
    stack = lambda outs, i: jnp.stack([o[i] for o in outs])
    return (hp, hs, stack(outs_p, 0), stack(outs_p, 1), stack(outs_p, 2), stack(outs_p, 3),
            stack(outs_s, 0), stack(outs_s, 1), stack(outs_s, 2), stack(outs_s, 3))
```

```python
import functools
import math

import numpy as np
import jax
import jax.numpy as jnp
from jax import lax
from jax.experimental import pallas as pl
from jax.experimental.pallas import tpu as pltpu

F32 = jnp.float32
BF16 = jnp.bfloat16
I32 = jnp.int32

LANES = 128
SUBLANES = 8
VMEM_LIMIT = 56 << 20

CHUNK = 64
A_HEADS = 8
A_KV_HEADS = 2
A_HEAD_DIM = 64
A_GROUP = A_HEADS // A_KV_HEADS
IDX_HEADS = 4
IDX_DIM = 64
IDX_TOPK = 256
ROPE_THETA = 10000.0
G_HEADS = 4
G_KEY_DIM = 128
G_VAL_DIM = 128
G_LOWRANK = 16
G_TAU = 16.0
P_HEADS = 8
P_NKEYS = 128
P_HALF = 128
P_TOPK = 16
LN_EPS = 1e-5

W_AQ = A_HEADS * A_HEAD_DIM
W_AK = A_KV_HEADS * A_HEAD_DIM
W_IQ = IDX_HEADS * IDX_DIM
W_G = G_HEADS * G_KEY_DIM
IN_SIZES = (W_AQ, W_AK, W_AK, W_IQ, IDX_DIM, IDX_HEADS, W_G, W_G, W_G, G_LOWRANK, W_G, None)

TOKEN_TILE = 256
MERGE_TOKEN_TILE = 512
PEER_TOKEN_TILE = 512
Q_BLOCK = 128
KEY_TILE = 512
SOFTMAX_TILES = 2
GLA_SEQS = 4

INT_MIN = -(2 ** 31)
NEG_INF_KEY = -2139095041
NEG_BIG = -1e30
LOG2_E = 1.4426950408889634


def _dot(a, b):
    return jnp.dot(a, b, preferred_element_type=F32)


def _dot_nt(a, b):
    return lax.dot_general(a, b, (((1,), (1,)), ((), ())), preferred_element_type=F32)


def _dot_tn(a, b):
    return lax.dot_general(a, b, (((0,), (0,)), ((), ())), preferred_element_type=F32)


def _sort_key(x):
    bits = pltpu.bitcast(x, I32)
    key = jnp.where(bits < 0, bits ^ 0x7FFFFFFF, bits)
    return jnp.where(key == -1, 0, key)


def _fold_rows(x, op):
    x = x.reshape(x.shape[0] // SUBLANES, SUBLANES, x.shape[1])
    while x.shape[0] > 1:
        half = x.shape[0] // 2
        folded = op(x[:half], x[half:2 * half])
        x = folded if x.shape[0] == 2 * half else jnp.concatenate([folded, x[2 * half:]], axis=0)
    return x[0]


def _params(sem):
    return pltpu.CompilerParams(dimension_semantics=sem, vmem_limit_bytes=VMEM_LIMIT)


_PG_AQ, _PG_K, _PG_V, _PG_IQ, _PG_IKW, _PG_GQ, _PG_GK, _PG_GV, _PG_GF, _PG_GR, _PG_GATE = range(11)


def _pack_layout(d_model):
    widths = [W_AQ, W_AK, W_AK, W_IQ, LANES, W_G, W_G, W_G, LANES, W_G, 2 * d_model]
    offs = np.concatenate([[0], np.cumsum(widths)]).tolist()
    return widths, offs


def _pack_w_in(w_in):
    d = w_in.shape[0]
    sizes = list(IN_SIZES[:-1]) + [2 * d]
    cuts = np.cumsum(sizes)[:-1].tolist()
    aq, ak, av, iq, ik, iw, gq, gk, gv, gf, gr, gate = jnp.split(w_in, cuts, axis=1)
    z = lambda n: jnp.zeros((d, n), w_in.dtype)
    ikw = jnp.concatenate([ik, iw, z(LANES - IDX_DIM - IDX_HEADS)], axis=1)
    gfp = jnp.concatenate([gf, z(LANES - G_LOWRANK)], axis=1)
    return jnp.concatenate([aq, ak, av, iq, ikw, gq, gk, gv, gfp, gr, gate], axis=1).astype(BF16)


def _rope_tables(pos):
    half = A_HEAD_DIM // 2
    inv = ROPE_THETA ** (-jnp.arange(half, dtype=F32) / half)
    ang = pos.astype(F32)[:, None] * inv[None, :]
    c, s = jnp.cos(ang), jnp.sin(ang)
    return jnp.concatenate([c, c, c, c], -1), jnp.concatenate([-s, s, -s, s], -1)


def _proj_kernel(x_ref, w_ref, wfa_ref, bfa_ref, cos_ref, sin_ref,
                 aq_ref, k_ref, v_ref, kb_ref, vb_ref, iq_ref, ikw_ref, ikb_ref,
                 gq_ref, gk_ref, gv_ref, glog_ref, gr_ref, gate_ref, *, offs, widths):
    xb = x_ref[...].astype(BF16)
    cos = cos_ref[...]
    sin = sin_ref[...]
    lane = lax.broadcasted_iota(I32, cos.shape, 1)
    first_half = (lane & (A_HEAD_DIM // 2)) == 0

    def proj(g):
        return _dot(xb, w_ref[:, offs[g]:offs[g] + widths[g]])

    def rope_slab(y):
        fwd = pltpu.roll(y, LANES - A_HEAD_DIM // 2, 1)
        bwd = pltpu.roll(y, A_HEAD_DIM // 2, 1)
        return y * cos + jnp.where(first_half, fwd, bwd) * sin

    def rope(y):
        return [rope_slab(y[:, s * LANES:(s + 1) * LANES]) for s in range(y.shape[1] // LANES)]

    for s, slab in enumerate(rope(proj(_PG_AQ))):
        aq_ref[:, s * LANES:(s + 1) * LANES] = (slab * (A_HEAD_DIM ** -0.5 * LOG2_E)).astype(BF16)
    k = rope(proj(_PG_K))[0]
    k_ref[...] = k
    kb_ref[...] = k.astype(BF16)
    v = proj(_PG_V)
    v_ref[...] = v
    vb_ref[...] = v.astype(BF16)
    for s, slab in enumerate(rope(proj(_PG_IQ))):
        iq_ref[:, s * LANES:(s + 1) * LANES] = (slab * (IDX_DIM ** -0.5)).astype(BF16)
    raw = proj(_PG_IKW)
    ikw = jnp.where(lane < IDX_DIM, rope_slab(raw), raw * (IDX_HEADS ** -0.5))
    ikw_ref[...] = ikw
    ikb_ref[...] = ikw.astype(BF16)
    gq_ref[...] = proj(_PG_GQ) * (G_KEY_DIM ** -0.5)
    gk_ref[...] = proj(_PG_GK)
    gv_ref[...] = proj(_PG_GV)
    z = _dot(proj(_PG_GF).astype(BF16), wfa_ref[...]) + bfa_ref[...]
    glog_ref[...] = (jnp.minimum(z, 0.0) - jnp.log1p(jnp.exp(-jnp.abs(z)))) * (1.0 / G_TAU)
    gr_ref[...] = proj(_PG_GR)
    gate_ref[...] = proj(_PG_GATE)


def _proj(x2, pos, w_in, w_fa, b_fa, tm):
    n, d = x2.shape
    t = pos.shape[0]
    widths, offs = _pack_layout(d)
    wp = _pack_w_in(w_in)
    wfa = jnp.concatenate([w_fa, jnp.zeros((LANES - G_LOWRANK, W_G), w_fa.dtype)], 0).astype(BF16)
    cos, sin = _rope_tables(pos)
    if tm > t:
        cos, sin = jnp.tile(cos, (tm // t, 1)), jnp.tile(sin, (tm // t, 1))
    nper = cos.shape[0] // tm
    row = lambda w: pl.BlockSpec((tm, w), lambda i: (i, 0))
    const = lambda a: pl.BlockSpec(a.shape, lambda i: (0, 0))
    tab = pl.BlockSpec((tm, LANES), lambda i: (i % nper, 0))
    outs = [(W_AQ, BF16), (LANES, F32), (LANES, F32), (LANES, BF16), (LANES, BF16), (W_IQ, BF16),
            (LANES, F32), (LANES, BF16), (W_G, F32), (W_G, F32), (W_G, F32), (W_G, F32), (W_G, F32),
            (2 * d, F32)]
    bfa = b_fa.reshape(1, W_G)
    return pl.pallas_call(
        functools.partial(_proj_kernel, offs=offs, widths=widths),
        grid=(n // tm,),
        in_specs=[row(d), const(wp), const(wfa), const(bfa), tab, tab],
        out_specs=[row(w) for w, _ in outs],
        out_shape=[jax.ShapeDtypeStruct((n, w), dt) for w, dt in outs],
        compiler_params=_params(("parallel",)),
        name="proj",
    )(x2, wp, wfa, bfa, cos, sin)


def _attn_kernel(aq_ref, iq_ref, ikw_ref, lim_ref, kb_ref, vt_ref, ikb_ref, o_ref,
                 keys_ref, qs_ref, iqs_ref, acc_ref, *, kt_w, nkt, topk, idx_bits):
    qb = Q_BLOCK

    aq_t = aq_ref[...].astype(F32).T
    for h in range(A_HEADS):
        qs_ref[h // A_GROUP, :, (h % A_GROUP) * qb:(h % A_GROUP + 1) * qb] = \
            aq_t[h * A_HEAD_DIM:(h + 1) * A_HEAD_DIM, :].astype(BF16)
    iq_t = iq_ref[...].astype(F32).T
    for h in range(IDX_HEADS):
        iqs_ref[:, h * qb:(h + 1) * qb] = iq_t[h * IDX_DIM:(h + 1) * IDX_DIM, :].astype(BF16)

    ikw_t = ikw_ref[...].T
    iw_rows = [ikw_t[IDX_DIM + h:IDX_DIM + h + 1, :] for h in range(IDX_HEADS)]
    lim = lim_ref[0, 0:1, :]
    sub = lax.broadcasted_iota(I32, (kt_w, qb), 0)

    def score_tile(kt, _):
        base = pl.multiple_of(kt * kt_w, kt_w)
        ik_t = ikb_ref[0, pl.ds(base, kt_w), :][:, :IDX_DIM]
        s = jnp.maximum(_dot(ik_t, iqs_ref[...]), 0.0)
        score = jnp.zeros((kt_w, qb), F32)
        for h in range(IDX_HEADS):
            score = score + s[:, h * qb:(h + 1) * qb] * iw_rows[h]
        keys_ref[pl.ds(base, kt_w), :] = jnp.where(sub + base < lim, _sort_key(score), NEG_INF_KEY)
        return 0

    lax.fori_loop(0, nkt, score_tile, 0, unroll=min(nkt, 2))

    def count(pred):
        acc = jnp.zeros((SUBLANES, qb), F32)
        for kt in range(nkt):
            acc = acc + _fold_rows(pred(keys_ref[kt * kt_w:(kt + 1) * kt_w, :], sub + kt * kt_w), jnp.add)
        return jnp.sum(acc, axis=0, keepdims=True)

    def count_ge(t_row):
        return count(lambda kk, idx: jnp.where(kk >= t_row, 1.0, 0.0))

    kf = float(topk)
    thr = jnp.where(count_ge(jnp.zeros((1, qb), I32)) >= kf, 0, INT_MIN).astype(I32)

    def thr_bit(i, t):
        cand = t + jnp.left_shift(jnp.int32(1), 30 - i)
        return jnp.where(count_ge(cand) >= kf, cand, t)

    thr = lax.fori_loop(0, 31, thr_bit, thr)
    n_gt = count_ge(thr + 1)
    n_eq = count_ge(thr) - n_gt
    need = kf - n_gt
    finite = thr > NEG_INF_KEY
    excess = jnp.where(finite, jnp.where(n_eq > need, 1.0, 0.0), 0.0)

    def count_eq_below(j_row):
        return count(lambda kk, idx: jnp.where(kk == thr, jnp.where(idx < j_row, 1.0, 0.0), 0.0))

    def resolve_ties():
        def bit(i, jc):
            cand = jc + jnp.left_shift(jnp.int32(1), idx_bits - 1 - i)
            return jnp.where(count_eq_below(cand) <= need - 1.0, cand, jc)
        jc = lax.fori_loop(0, idx_bits, bit, jnp.zeros((1, qb), I32))
        return jnp.where(finite, jc, -1)

    cut = lax.cond(jnp.max(excess) > 0.0, resolve_ties,
                   lambda: jnp.where(finite, 2 ** 30, -1).astype(I32))

    acc_ref[...] = jnp.zeros(acc_ref.shape, F32)
    gq = A_GROUP * qb

    def attend(base, width, carry):
        k_t = kb_ref[0, pl.ds(base, width), :]
        kk = keys_ref[pl.ds(base, width), :]
        tie = jnp.where(lax.broadcasted_iota(I32, (width, qb), 0) + base <= cut, 0.0, NEG_BIG)
        bias = jnp.where(kk > thr, 0.0, jnp.where(kk == thr, tie, NEG_BIG))
        bias = jnp.concatenate([bias] * A_GROUP, axis=1)
        out = []
        for n in range(A_KV_HEADS):
            m_old, l_old = carry[2 * n], carry[2 * n + 1]
            logits = bias + _dot(k_t[:, n * A_HEAD_DIM:(n + 1) * A_HEAD_DIM], qs_ref[n])
            m_new = jnp.maximum(m_old, jnp.max(_fold_rows(logits, jnp.maximum), axis=0, keepdims=True))
            alpha = jnp.exp2(m_old - m_new)
            p = jnp.exp2(logits - m_new)
            l_new = alpha * l_old + jnp.sum(_fold_rows(p, jnp.add), axis=0, keepdims=True)
            v_t = vt_ref[0, n * A_HEAD_DIM:(n + 1) * A_HEAD_DIM, pl.ds(base, width)]
            acc_ref[n] = alpha * acc_ref[n] + _dot(v_t, p.astype(BF16))
            out += [m_new, l_new]
        return tuple(out)

    wide = SOFTMAX_TILES * kt_w
    fin = (jnp.full((1, gq), NEG_BIG, F32), jnp.zeros((1, gq), F32)) * A_KV_HEADS
    fin = lax.fori_loop(0, nkt // SOFTMAX_TILES,
                        lambda i, c: attend(pl.multiple_of(i * wide, wide), wide, c), fin)
    for kt in range(nkt - nkt % SOFTMAX_TILES, nkt):
        fin = attend(kt * kt_w, kt_w, fin)
    l_row = [fin[2 * n + 1] for n in range(A_KV_HEADS)]

    o_t = jnp.concatenate([acc_ref[n] / l_row[n] for n in range(A_KV_HEADS)], axis=0)
    for n in range(A_KV_HEADS):
        for g in range(A_GROUP):
            h = n * A_GROUP + g
            blk = o_t[n * A_HEAD_DIM:(n + 1) * A_HEAD_DIM, g * qb:(g + 1) * qb]
            o_ref[:, h * A_HEAD_DIM:(h + 1) * A_HEAD_DIM] = blk.T.astype(BF16)


def _attention(aq, iq, ikw, limits, kb, vb, ikb, b, topk):
    qb, kt_w = Q_BLOCK, KEY_TILE
    tq = aq.shape[0] // b
    tq_pad = -(-tq // qb) * qb
    limits = np.asarray(limits)
    if tq_pad != tq:
        padq = lambda a: jnp.pad(a.reshape(b, tq, -1), ((0, 0), (0, tq_pad - tq), (0, 0))).reshape(b * tq_pad, -1)
        aq, iq, ikw = padq(aq), padq(iq), padq(ikw)
        limits = np.concatenate([limits, np.full((tq_pad - tq,), limits[-1])])
    l_all = kb.shape[1]
    l_pad = -(-l_all // kt_w) * kt_w
    if l_pad != l_all:
        padl = lambda a: jnp.pad(a, ((0, 0), (0, l_pad - l_all), (0, 0)))
        kb, vb, ikb = padl(kb), padl(vb), padl(ikb)
    vt = jnp.swapaxes(vb, 1, 2)
    nq = tq_pad // qb
    lim_blk = limits.reshape(nq, qb)
    nkt = np.minimum(-(-lim_blk.max(axis=1) // kt_w), l_pad // kt_w)
    lim = jnp.asarray(np.broadcast_to(lim_blk[:, None, :], (nq, SUBLANES, qb)).astype(np.int32))
    idx_bits = max(1, int(math.ceil(math.log2(l_pad))))
    keys = lambda a: pl.BlockSpec((1,) + a.shape[1:], lambda bi, j: (bi, 0, 0))
    gq = A_GROUP * qb
    runs, j0 = [], 0
    for j in range(1, nq + 1):
        if j == nq or nkt[j] != nkt[j0]:
            runs.append((j0, j - j0, int(nkt[j0])))
            j0 = j
    outs = []
    for j0, nj, n_tiles in runs:
        qrow = lambda w, j0=j0: pl.BlockSpec((qb, w), lambda bi, j: (bi * nq + j0 + j, 0))
        outs.append(pl.pallas_call(
            functools.partial(_attn_kernel, kt_w=kt_w, nkt=n_tiles, topk=topk, idx_bits=idx_bits),
            grid=(b, nj),
            in_specs=[qrow(W_AQ), qrow(W_IQ), qrow(LANES),
                      pl.BlockSpec((1, SUBLANES, qb), lambda bi, j, j0=j0: (j0 + j, 0, 0)),
                      keys(kb), keys(vt), keys(ikb)],
            out_specs=pl.BlockSpec((qb, W_AQ), lambda bi, j, nj=nj: (bi * nj + j, 0)),
            out_shape=jax.ShapeDtypeStruct((b * nj * qb, W_AQ), BF16),
            scratch_shapes=[pltpu.VMEM((n_tiles * kt_w, qb), I32),
                            pltpu.VMEM((A_KV_HEADS, A_HEAD_DIM, gq), BF16),
                            pltpu.VMEM((IDX_DIM, IDX_HEADS * qb), BF16),
                            pltpu.VMEM((A_KV_HEADS, A_HEAD_DIM, gq), F32)],
            compiler_params=_params(("parallel", "arbitrary")),
            name="attn",
        )(aq, iq, ikw, lim, kb, vt, ikb).reshape(b, nj * qb, W_AQ))
    o = outs[0] if len(outs) == 1 else jnp.concatenate(outs, axis=1)
    return o[:, :tq].reshape(b * tq, W_AQ)


def _gla_constants(c):
    nlev = int(math.log2(c))
    t = np.arange(c)
    mats = [(t[None, :] <= t[:, None])]
    masks = [np.eye(c, dtype=bool)]
    for lev in range(nlev):
        m = c >> (lev + 1)
        ref_row = (t // (2 * m)) * 2 * m + m
        mats.append(t[None, :] <= ref_row[:, None])
        upper = (t & m) != 0
        same = (t[:, None] // (2 * m)) == (t[None, :] // (2 * m))
        masks.append(same & upper[:, None] & ~upper[None, :])
    return (jnp.asarray(np.concatenate(mats, 0).astype(np.float32), BF16),
            jnp.asarray(np.stack(masks).astype(np.float32)), nlev)


def _gla_kernel(q_ref, k_ref, v_ref, g_ref, gr_ref, gn_ref, mst_ref, msk_ref, s0_ref,
                ob_ref, sfin_ref, st_ref, *, c, nlev, nb):
    i = pl.program_id(1)
    hk = G_KEY_DIM

    @pl.when(i == 0)
    def _():
        for s in range(nb):
            for h in range(G_HEADS):
                st_ref[s, h] = s0_ref[s, h].T

    mst = mst_ref[...]
    gn = gn_ref[...]
    hs = lambda a, h: a[:, h * hk:(h + 1) * hk]
    for s in range(nb):
        g = g_ref[s]
        g_hi = g.astype(BF16)
        r1 = g - g_hi.astype(F32)
        g_mid = r1.astype(BF16)
        g_lo = (r1 - g_mid.astype(F32)).astype(BF16)
        bs = _dot(mst, g_hi) + _dot(mst, g_mid) + _dot(mst, g_lo)
        b = bs[0:c]
        q = q_ref[s]
        k = k_ref[s]
        vb = v_ref[s].astype(BF16)
        row = lax.broadcasted_iota(I32, q.shape, 0)

        qb = q.astype(BF16)
        kb = k.astype(BF16)
        attn = [_dot_nt(hs(qb, h), hs(kb, h)) * msk_ref[0] for h in range(G_HEADS)]
        for lev in range(nlev):
            m = c >> (lev + 1)
            upper = (row & m) != 0
            d = b - bs[(lev + 1) * c:(lev + 2) * c]
            e = jnp.exp(jnp.where(upper, d, -d))
            qt = jnp.where(upper, q * e, 0.0).astype(BF16)
            kt = jnp.where(upper, 0.0, k * e).astype(BF16)
            mk = msk_ref[lev + 1]
            for h in range(G_HEADS):
                attn[h] = attn[h] + _dot_nt(hs(qt, h), hs(kt, h)) * mk

        qe = (q * jnp.exp(b)).astype(BF16)
        b_last = b[c - 1:c, :]
        khat = (k * jnp.exp(b_last - b)).astype(BF16)
        dec = jnp.exp(b_last)
        gr = gr_ref[s]
        for h in range(G_HEADS):
            st = st_ref[s, h]
            o = _dot_nt(hs(qe, h), st.astype(BF16)) + _dot(attn[h].astype(BF16), hs(vb, h))
            st_ref[s, h] = st * hs(dec, h) + _dot_tn(hs(vb, h), hs(khat, h))
            ms = jnp.mean(o * o, axis=1, keepdims=True)
            grh = hs(gr, h)
            of = o * lax.rsqrt(ms + LN_EPS) * hs(gn, h) * (grh / (1.0 + jnp.exp(-grh)))
            ob_ref[s, :, h * hk:(h + 1) * hk] = of.astype(BF16)

    @pl.when(i == pl.num_programs(1) - 1)
    def _():
        for s in range(nb):
            for h in range(G_HEADS):
                sfin_ref[s, h] = st_ref[s, h].T


def _gla(gq, gk, gv, glog, gr, g_norm, s0, b, c):
    n = gq.shape[0]
    t = n // b
    nc = t // c
    nb = GLA_SEQS if b % GLA_SEQS == 0 else 1
    mst, msk, nlev = _gla_constants(c)
    seq = lambda a: a.reshape(b, t, W_G)
    row = pl.BlockSpec((nb, c, W_G), lambda bi, i: (bi, i, 0))
    const = lambda a: pl.BlockSpec(a.shape, lambda bi, i: (0,) * a.ndim)
    st_spec = pl.BlockSpec((nb, G_HEADS, G_KEY_DIM, G_VAL_DIM), lambda bi, i: (bi, 0, 0, 0))
    gn = g_norm.reshape(1, W_G)
    ob, s_fin = pl.pallas_call(
        functools.partial(_gla_kernel, c=c, nlev=nlev, nb=nb),
        grid=(b // nb, nc),
        in_specs=[row, row, row, row, row, const(gn), const(mst), const(msk), st_spec],
        out_specs=[row, st_spec],
        out_shape=[jax.ShapeDtypeStruct((b, t, W_G), BF16),
                   jax.ShapeDtypeStruct((b, G_HEADS, G_KEY_DIM, G_VAL_DIM), F32)],
        scratch_shapes=[pltpu.VMEM((nb, G_HEADS, G_VAL_DIM, G_KEY_DIM), F32)],
        compiler_params=_params(("parallel", "arbitrary")),
        name="gla",
    )(seq(gq), seq(gk), seq(gv), seq(glog), seq(gr), gn, mst, msk, s0)
    return ob.reshape(n, W_G), s_fin


def _layer_norm(z, g, b):
    mu = jnp.mean(z, axis=1, keepdims=True)
    zc = z - mu
    var = jnp.mean(zc * zc, axis=1, keepdims=True)
    return zc * lax.rsqrt(var + LN_EPS) * g + b


def _sigmoid(x):
    return 1.0 / (1.0 + jnp.exp(-x))


def _merge_kernel(x_ref, oa_ref, ob_ref, gate_ref, wpa_ref, wpb_ref, wout_ref, g1_ref, b1_ref,
                  wpq_ref, pk1_ref, pk2_ref, h1_ref, h1b_ref, s1t_ref, s2t_ref, *, alpha, d):
    ya = _dot(oa_ref[...], wpa_ref[...])
    yb = _dot(ob_ref[...], wpb_ref[...])
    m = _sigmoid(gate_ref[:, :d]) * ya + _sigmoid(gate_ref[:, d:]) * yb
    mix = _dot(m.astype(BF16), wout_ref[...])
    h1 = _layer_norm(alpha * x_ref[...] + mix, g1_ref[...], b1_ref[...])
    h1_ref[...] = h1
    h1b_ref[...] = h1.T.astype(BF16)
    qp = _dot(h1.astype(BF16), wpq_ref[...])
    for h in range(P_HEADS):
        for half, (pk_ref, st_ref) in enumerate(((pk1_ref, s1t_ref), (pk2_ref, s2t_ref))):
            c0 = (2 * h + half) * P_HALF
            st_ref[h * P_NKEYS:(h + 1) * P_NKEYS, :] = _dot_nt(pk_ref[h], qp[:, c0:c0 + P_HALF].astype(BF16))


def _merge(x2, oa, ob, gate, w_pa, w_pb, w_out, ln_g, ln_b, w_pq, pk1, pk2, alpha, tm):
    n, d = x2.shape
    hk = P_HEADS * P_NKEYS
    row = lambda w: pl.BlockSpec((tm, w), lambda i: (i, 0))
    col = pl.BlockSpec((hk, tm), lambda i: (0, i))
    const = lambda a: pl.BlockSpec(a.shape, lambda i: (0,) * a.ndim)
    ws = [w_pa.astype(BF16), w_pb.astype(BF16), w_out.astype(BF16), ln_g.reshape(1, d), ln_b.reshape(1, d),
          w_pq.astype(BF16), pk1.astype(BF16), pk2.astype(BF16)]
    return pl.pallas_call(
        functools.partial(_merge_kernel, alpha=alpha, d=d),
        grid=(n // tm,),
        in_specs=[row(d), row(W_AQ), row(W_G), row(2 * d)] + [const(w) for w in ws],
        out_specs=[row(d), pl.BlockSpec((d, tm), lambda i: (0, i)), col, col],
        out_shape=[jax.ShapeDtypeStruct((n, d), F32), jax.ShapeDtypeStruct((d, n), BF16),
                   jax.ShapeDtypeStruct((hk, n), F32), jax.ShapeDtypeStruct((hk, n), F32)],
        compiler_params=_params(("parallel",)),
        name="merge",
    )(x2, oa, ob, gate, *ws)


def _top_desc(s, count, want_rank=False):
    tops = []
    rank = jnp.full(s.shape, float(count), F32) if want_rank else None
    for r in range(count):
        m = jnp.max(_fold_rows(s, jnp.maximum), axis=0, keepdims=True)
        tops.append(m)
        hit = s == m
        if want_rank:
            rank = jnp.where(hit, float(r), rank)
        if r + 1 < count:
            s = jnp.where(hit, -jnp.inf, s)
    return tops, rank


def _select_kernel(s1t_ref, s2t_ref, cnt_ref, e1_ref, rank_ref, e2_ref):
    nk = P_NKEYS

    def head(h, _):
        r0 = pl.multiple_of(h * nk, nk)
        rows = pl.ds(r0, nk)
        s1 = s1t_ref[rows, :]
        s2 = s2t_ref[rows, :]
        v1, _ = _top_desc(s1, P_TOPK)
        v2, rank = _top_desc(s2, P_TOPK, want_rank=True)
        pairs = [(a, b) for a in range(P_TOPK) for b in range(P_TOPK // (a + 1))]
        fill = [jnp.full_like(v1[0], -jnp.inf)] * (-len(pairs) % SUBLANES)
        cand = jnp.concatenate([v1[a] + v2[b] for a, b in pairs] + fill, axis=0)
        work, seen = cand, jnp.zeros_like(v1[0])
        tau = jnp.full_like(v1[0], -jnp.inf)
        for _ in range(P_TOPK):
            m = jnp.max(work, axis=0, keepdims=True)
            hit = work == m
            seen = seen + jnp.sum(jnp.where(hit, 1.0, 0.0), axis=0, keepdims=True)
            tau = jnp.maximum(tau, jnp.where(seen >= float(P_TOPK), m, -jnp.inf))
            work = jnp.where(hit, -jnp.inf, work)
        cmax = v1[0] + v2[0]
        zsum = jnp.sum(jnp.where(cand >= tau, jnp.exp(cand - cmax), 0.0), axis=0, keepdims=True)
        v2all = jnp.concatenate(v2, axis=0)
        cnt = jnp.zeros(s1.shape, F32)
        for a in range(P_TOPK):
            cnt_a = jnp.sum(jnp.where(v1[a] + v2all >= tau, 1.0, 0.0), axis=0, keepdims=True)
            cnt = jnp.where(s1 == v1[a], cnt_a, cnt)
        e1 = jnp.exp(s1 - v1[0]) / zsum * 0.5
        for st in range(nk // I1_PER_STEP):
            dst = pl.ds(pl.multiple_of(st * P_HEADS * I1_PER_STEP + h * I1_PER_STEP, I1_PER_STEP), I1_PER_STEP)
            src = slice(st * I1_PER_STEP, (st + 1) * I1_PER_STEP)
            cnt_ref[dst, :] = cnt[src, :]
            e1_ref[dst, :] = e1[src, :]
        rank_ref[rows, :] = rank
        e2_ref[rows, :] = jnp.exp(s2 - v2[0])
        return 0

    lax.fori_loop(0, P_HEADS, head, 0)


def _select(s1t, s2t, tn):
    hk, n = s1t.shape
    col = pl.BlockSpec((hk, tn), lambda i: (0, i))
    return pl.pallas_call(
        _select_kernel,
        grid=(n // tn,),
        in_specs=[col, col],
        out_specs=[col, col, col, col],
        out_shape=[jax.ShapeDtypeStruct((hk, n), F32)] * 4,
        compiler_params=_params(("parallel",)),
        name="select",
    )(s1t, s2t)


I1_PER_STEP = 8
E_PER_STEP = I1_PER_STEP * P_NKEYS
I2_BLOCK = 16


def _gelu_x2(x):
    return x * (1.0 + lax.erf(x * (2.0 ** -0.5)))


def _peer_steps(t, n_steps, ng):
    item = lambda d: jnp.clip(t - d, 0, n_steps - 1)
    return [(item(d) // ng, item(d) % ng) for d in range(3)]


def _peer_kernel(hb_ref, pu_ref, pvt_ref, cnt_ref, e1_ref, rank_ref, e2_ref, h1_ref, g2_ref, b2_ref,
                 y_ref, a_ref, gw_ref, acc_ref, rank_s, e2_s, *, tn, alpha, n_steps, ng):
    t = pl.program_id(0)
    (_, _), (_, g), (_, g_out) = _peer_steps(t, n_steps, ng)
    slot = t % 2
    prev = 1 - slot

    @pl.when(t == 0)
    def _():
        a_ref[...] = jnp.zeros(a_ref.shape, F32)
        gw_ref[...] = jnp.zeros(gw_ref.shape, BF16)

    @pl.when(g == 0)
    def _():
        rank_s[...] = rank_ref[...].astype(BF16)
        e2_s[...] = e2_ref[...].astype(BF16)

    @pl.when(g_out == 0)
    def _():
        acc_ref[...] = jnp.zeros(acc_ref.shape, F32)

    a_ref[slot] = _dot(pu_ref[...], hb_ref[...])

    for lt in range(tn // LANES):
        ls = slice(lt * LANES, (lt + 1) * LANES)
        for j in range(I1_PER_STEP):
            bcast = lambda ref, h: jnp.broadcast_to(
                ref[h * I1_PER_STEP + j:h * I1_PER_STEP + j + 1, ls], (I2_BLOCK, LANES)).astype(BF16)
            cnt = [bcast(cnt_ref, h) for h in range(P_HEADS)]
            e1 = [bcast(e1_ref, h) for h in range(P_HEADS)]
            for i2b in range(P_NKEYS // I2_BLOCK):
                w = jnp.zeros((I2_BLOCK, LANES), BF16)
                for h in range(P_HEADS):
                    rows = slice(h * P_NKEYS + i2b * I2_BLOCK, h * P_NKEYS + (i2b + 1) * I2_BLOCK)
                    w = w + jnp.where(rank_s[rows, ls] < cnt[h], e2_s[rows, ls] * e1[h], jnp.zeros((), BF16))
                arow = slice(j * P_NKEYS + i2b * I2_BLOCK, j * P_NKEYS + (i2b + 1) * I2_BLOCK)
                gw_ref[prev, arow, ls] = w * _gelu_x2(a_ref[prev, arow, ls]).astype(BF16)

    acc_ref[...] += _dot(pvt_ref[0], gw_ref[slot])

    @pl.when(jnp.logical_and(g_out == ng - 1, t >= 2))
    def _():
        y_ref[...] = _layer_norm(alpha * h1_ref[...] + acc_ref[...].T, g2_ref[...], b2_ref[...])


def _peer(h1, h1b, cnt, e1, rank, e2, pu, pv, ln_g, ln_b, alpha, tn):
    n, d = h1.shape
    hk = P_HEADS * P_NKEYS
    ng = pu.shape[0] // E_PER_STEP
    n_steps = (n // tn) * ng
    stage = lambda k, f: (lambda t: f(*_peer_steps(t, n_steps, ng)[k]))
    const = pl.BlockSpec((1, d), lambda t: (0, 0))
    routing = pl.BlockSpec((hk, tn), stage(1, lambda i, g: (0, i)))
    step_rows = pl.BlockSpec((P_HEADS * I1_PER_STEP, tn), stage(1, lambda i, g: (g, i)))
    pvt = pv.astype(BF16).reshape(ng, E_PER_STEP, d).transpose(0, 2, 1)
    return pl.pallas_call(
        functools.partial(_peer_kernel, tn=tn, alpha=alpha, n_steps=n_steps, ng=ng),
        grid=(n_steps + 2,),
        in_specs=[pl.BlockSpec((d, tn), stage(0, lambda i, g: (0, i))),
                  pl.BlockSpec((E_PER_STEP, d), stage(0, lambda i, g: (g, 0))),
                  pl.BlockSpec((1, d, E_PER_STEP), stage(2, lambda i, g: (g, 0, 0))),
                  step_rows, step_rows, routing, routing,
                  pl.BlockSpec((tn, d), stage(2, lambda i, g: (i, 0))), const, const],
        out_specs=pl.BlockSpec((tn, d), stage(2, lambda i, g: (i, 0))),
        out_shape=jax.ShapeDtypeStruct((n, d), F32),
        scratch_shapes=[pltpu.VMEM((2, E_PER_STEP, tn), F32), pltpu.VMEM((2, E_PER_STEP, tn), BF16),
                        pltpu.VMEM((d, tn), F32), pltpu.VMEM((hk, tn), BF16), pltpu.VMEM((hk, tn), BF16)],
        compiler_params=_params(("arbitrary",)),
        name="peer",
    )(h1b, pu.astype(BF16), pvt, cnt, e1, rank, e2, h1, ln_g.reshape(1, d), ln_b.reshape(1, d))


def _pick_tile(n, pref):
    t = min(n, pref)
    assert n % t == 0
    return t


def _layer(x, pos, limits, past, s0, w, *, chunk, alpha):
    b, t, d = x.shape
    n = b * t
    x2 = x.reshape(n, d)
    tm = _pick_tile(n, TOKEN_TILE)
    (aq, k32, v32, kb, vb, iq, ikw, ikb, gq, gk, gv, glog, gr, gate) = _proj(
        x2, pos, w["w_in"], w["w_fa"], w["b_fa"], tm)

    kb3, vb3, ikb3 = kb.reshape(b, t, LANES), vb.reshape(b, t, LANES), ikb.reshape(b, t, LANES)
    if past is not None:
        ck, cv, cik = past
        p = ck.shape[1]
        kb3 = jnp.concatenate([ck.reshape(b, p, LANES).astype(BF16), kb3], axis=1)
        vb3 = jnp.concatenate([cv.reshape(b, p, LANES).astype(BF16), vb3], axis=1)
        ikb3 = jnp.concatenate([cik.astype(BF16), ikb3[:, :, :IDX_DIM]], axis=1)
    topk = min(IDX_TOPK, kb3.shape[1] // 4)
    o_a = _attention(aq, iq, ikw, limits, kb3, vb3, ikb3, b, topk)

    o_b, s_fin = _gla(gq, gk, gv, glog, gr, w["g_gla_norm"], s0, b, chunk)

    h1, h1b, s1t, s2t = _merge(x2, o_a, o_b, gate, w["w_pa"], w["w_pb"], w["w_out"],
                               w["ln1_g"], w["ln1_b"], w["w_pq"], w["pk1"], w["pk2"], alpha,
                               _pick_tile(n, MERGE_TOKEN_TILE))
    cnt, e1, rank, e2 = _select(s1t, s2t, _pick_tile(n, MERGE_TOKEN_TILE))
    y = _peer(h1, h1b, cnt, e1, rank, e2, w["pu"], w["pv"], w["ln2_g"], w["ln2_b"], alpha,
              _pick_tile(n, PEER_TOKEN_TILE))

    k_out = k32.reshape(b, t, A_KV_HEADS, A_HEAD_DIM)
    v_out = v32.reshape(b, t, A_KV_HEADS, A_HEAD_DIM)
    ik_out = ikw[:, :IDX_DIM].reshape(b, t, IDX_DIM)
    return y.reshape(b, t, d), k_out, v_out, ik_out, s_fin


def kernel(x_prompt, x_sample, cache_k, cache_v, cache_idx_k, state_gla, w_in, w_fa, b_fa, g_gla_norm,
           w_pa, w_pb, w_out, ln1_g, ln1_b, w_pq, pk1, pk2, pu, pv, ln2_g, ln2_b):
    depth = w_in.shape[0]
    alpha = (2.0 * depth) ** 0.25
    bp, tp, _ = x_prompt.shape
    bs, ts, _ = x_sample.shape
    past_len = cache_k.shape[2]
    pos_p = jnp.arange(tp)
    pos_s = past_len + jnp.arange(ts)
    lim_p = (np.arange(tp) // CHUNK + 1) * CHUNK
    lim_s = np.full((ts,), past_len + ts)
    names = ("w_in", "w_fa", "b_fa", "g_gla_norm", "w_pa", "w_pb", "w_out", "ln1_g", "ln1_b",
             "w_pq", "pk1", "pk2", "pu", "pv", "ln2_g", "ln2_b")
    stacked = (w_in, w_fa, b_fa, g_gla_norm, w_pa, w_pb, w_out, ln1_g, ln1_b, w_pq, pk1, pk2, pu, pv, ln2_g, ln2_b)
    hp, hs = x_prompt, x_sample
    outs_p, outs_s = [], []
    for l in range(depth):
        w = {nm: a[l] for nm, a in zip(names, stacked)}
        s0 = jnp.zeros((bp, G_HEADS, G_KEY_DIM, G_VAL_DIM), F32)
        hp, *rest = _layer(hp, pos_p, lim_p, None, s0, w, chunk=CHUNK, alpha=alpha)
        outs_p.append(rest)
        hs, *rest = _layer(hs, pos_s, lim_s, (cache_k[l], cache_v[l], cache_idx_k[l]), state_gla[l], w,
                           chunk=ts, alpha=alpha)
        outs_s.append(rest)
    stack = lambda outs, i: jnp.stack([o[i] for o in outs])
    return (hp, hs, stack(outs_p, 0), stack(outs_p, 1), stack(outs_p, 2), stack(outs_p, 3),
            stack(outs_s, 0), stack(outs_s, 1), stack(outs_s, 2), stack(outs_s, 3))
```
